```python
import jax, jax.numpy as jnp
from jax import lax
import numpy as np

D_MODEL = 2048
BATCH = 1
SEQ = 8192
DEPTH = 1
DEC_BATCH = 16
DEC_SEQ = 16
PAST_LEN = 2048

CHUNK = 64
EPS = 1e-6
MLA_HEADS = 16
MLA_Q_RANK = 512
MLA_KV_RANK = 512
MLA_NOPE = 128
MLA_ROPE = 64
MLA_V = 128
MLA_QK_DIM = MLA_NOPE + MLA_ROPE
MLA_SCALE = MLA_QK_DIM ** -0.5
ROPE_THETA = 10000.0
Q_BLOCK = 128
GLA_HEADS = 4
GLA_DK = 256
GLA_DV = 512
GLA_GATE_RANK = 16
GLA_TAU = 16.0
GLA_BLOCK = 16
D_FF = 4 * D_MODEL
D_IN = (MLA_Q_RANK + MLA_KV_RANK + MLA_ROPE + 2 * GLA_HEADS * GLA_DK + 2 * GLA_HEADS * GLA_DV
        + GLA_GATE_RANK + 2 * D_MODEL)

kernel_name = "hybrid_mla_gla_streaming_step"


def rmsnorm(x, g):
    xf = x.astype(jnp.float32)
    inv = lax.rsqrt(jnp.mean(xf * xf, axis=-1, keepdims=True) + EPS)
    return (xf * inv * g.astype(jnp.float32)).astype(x.dtype)


def rope(x, pos):
    half = MLA_ROPE // 2
    freqs = jnp.power(ROPE_THETA, -jnp.arange(half, dtype=jnp.float32) / half)
    ang = pos[:, None] * freqs[None, :]
    shape = (ang.shape[0],) + (1,) * (x.ndim - 3) + (half,)
    cos = jnp.cos(ang).reshape(shape)
    sin = jnp.sin(ang).reshape(shape)
    xf = x.astype(jnp.float32)
    x1, x2 = xf[..., :half], xf[..., half:]
    return jnp.concatenate([x1 * cos - x2 * sin, x2 * cos + x1 * sin], axis=-1).astype(x.dtype)


def _front(x, pos, norm_g, w_in, q_norm_g, w_uq, kv_norm_g, gq_n, gq_r, gk_n, gk_r, w_a2, b_a):
    B, T, _ = x.shape
    f32 = jnp.float32
    h = rmsnorm(x, norm_g)
    z = h @ w_in
    gk_w, gv_w = GLA_HEADS * GLA_DK, GLA_HEADS * GLA_DV
    points = [int(p) for p in np.cumsum([MLA_Q_RANK, MLA_KV_RANK, MLA_ROPE, gk_w, gk_w, gv_w,
                                         GLA_GATE_RANK, gv_w, D_MODEL])]
    (q_lat, kv_lat, k_rope_raw, g_q, g_k, g_v, a_lr, out_gate, gate_mla, gate_gla) = jnp.split(z, points, axis=-1)
    q = jnp.einsum('btc,chd->bthd', rmsnorm(q_lat, q_norm_g), w_uq)
    q_nope = q[..., :MLA_NOPE].astype(f32)
    q_rope = rope(q[..., MLA_NOPE:], pos).astype(f32)
    ss = jnp.sum(q_nope * q_nope, -1, keepdims=True) + jnp.sum(q_rope * q_rope, -1, keepdims=True)
    inv_q = lax.rsqrt(ss / MLA_QK_DIM + EPS)
    qn = (q_nope * inv_q * (gq_n * gk_n).astype(f32) * MLA_SCALE).astype(x.dtype)
    qr = (q_rope * inv_q * jnp.tile((gq_r * gk_r).astype(f32), 2) * MLA_SCALE).astype(x.dtype)
    ckv = rmsnorm(kv_lat, kv_norm_g)
    krope = rope(k_rope_raw, pos)
    gla_q = g_q.reshape(B, T, GLA_HEADS, GLA_DK) * (GLA_DK ** -0.5)
    gla_k = g_k.reshape(B, T, GLA_HEADS, GLA_DK)
    gla_v = g_v.reshape(B, T, GLA_HEADS, GLA_DV)
    log_a = (jax.nn.log_sigmoid((a_lr @ w_a2 + b_a).astype(f32)) / GLA_TAU).reshape(B, T, GLA_HEADS, GLA_DK)
    return (qn, qr, ckv, krope, gla_q, gla_k, gla_v, log_a, out_gate, gate_mla, gate_gla)


def _mla_keys(ckv, krope, w_ukv):
    kv = jnp.einsum('bsc,chd->bshd', ckv, w_ukv)
    kn, v = kv[..., :MLA_NOPE], kv[..., MLA_NOPE:]
    knf = kn.astype(jnp.float32)
    krf = krope.astype(jnp.float32)
    ss = jnp.sum(knf * knf, -1) + jnp.sum(krf * krf, -1)[..., None]
    inv_k = lax.rsqrt(ss / MLA_QK_DIM + EPS)
    return kn, v, jnp.swapaxes(inv_k, 1, 2)


def _attend(qn, qr, kn, kr, v, inv_k, mask):
    s = (jnp.einsum('bqhd,bkhd->bhqk', qn, kn, preferred_element_type=jnp.float32)
         + jnp.einsum('bqhr,bkr->bhqk', qr, kr, preferred_element_type=jnp.float32))
    s = s * inv_k[:, :, None, :].astype(jnp.float32)
    if mask is not None:
        s = jnp.where(mask, s, -jnp.inf)
    p = jax.nn.softmax(s, axis=-1).astype(v.dtype)
    return jnp.einsum('bhqk,bkhd->bqhd', p, v)


def _mla_prompt_attention(qn, qr, kn, kr, v, inv_k):
    B, T, H, _ = qn.shape
    key_chunk = jnp.arange(T) // CHUNK

    def block(i):
        q0 = i * Q_BLOCK
        qn_b = lax.dynamic_slice_in_dim(qn, q0, Q_BLOCK, axis=1)
        qr_b = lax.dynamic_slice_in_dim(qr, q0, Q_BLOCK, axis=1)
        q_chunk = (q0 + jnp.arange(Q_BLOCK)) // CHUNK
        mask = key_chunk[None, :] <= q_chunk[:, None]
        return _attend(qn_b, qr_b, kn, kr, v, inv_k, mask)

    out = lax.map(block, jnp.arange(T // Q_BLOCK))
    return jnp.moveaxis(out, 0, 1).reshape(B, T, H * MLA_V)


def _gla_chunked(q, k, v, log_a, s0):
    B, T, H, DK = q.shape
    DV = v.shape[-1]
    L = GLA_BLOCK
    n = -(-T // L)
    pad = n * L - T
    f32 = jnp.float32

    def prep(a):
        a = jnp.pad(a.astype(f32), ((0, 0), (0, pad), (0, 0), (0, 0)))
        return a.reshape(B, n, L, H, a.shape[-1])

    qb, kb, vb, lab = prep(q), prep(k), prep(v), prep(log_a)
    b = jnp.cumsum(lab, axis=2)
    q_t = qb * jnp.exp(b)
    k_t = kb * jnp.exp(-b)
    b_end = b[:, :, -1]
    k_end = kb * jnp.exp(b_end[:, :, None] - b)
    tril = jnp.tril(jnp.ones((L, L), dtype=bool))
    A = jnp.where(tril, jnp.einsum('bnqhd,bnkhd->bnhqk', q_t, k_t), 0.0)
    o_intra = jnp.einsum('bnhqk,bnkhe->bnqhe', A, vb)

    def step(S, xs):
        q_n, k_n, v_n, be_n = xs
        o = jnp.einsum('blhd,bhde->blhe', q_n, S)
        S = S * jnp.exp(be_n)[..., None] + jnp.einsum('blhd,blhe->bhde', k_n, v_n)
        return S, o

    xs = (jnp.moveaxis(q_t, 1, 0), jnp.moveaxis(k_end, 1, 0), jnp.moveaxis(vb, 1, 0), jnp.moveaxis(b_end, 1, 0))
    s_fin, o_inter = lax.scan(step, s0.astype(f32), xs)
    o = o_intra + jnp.moveaxis(o_inter, 0, 1)
    return o.reshape(B, n * L, H, DV)[:, :T], s_fin


def _back(x, o_mla, o_gla, out_gate, gate_mla, gate_gla, gla_norm_g, w_o, norm_ffn_g, w_up, w_down):
    B, T, _ = x.shape
    o_gla = rmsnorm(o_gla, gla_norm_g).astype(x.dtype).reshape(B, T, GLA_HEADS * GLA_DV) * jax.nn.silu(out_gate)
    mixed = jax.nn.sigmoid(gate_mla) * o_mla + jax.nn.sigmoid(gate_gla) * o_gla
    x = x + mixed @ w_o
    h = rmsnorm(x, norm_ffn_g)
    return x + jnp.square(jax.nn.relu(h @ w_up)) @ w_down


def setup_inputs(seed: int = 0) -> dict:
    key = jax.random.key(seed)
    ks = jax.random.split(key, 22)
    f32 = jnp.float32

    def nrm(k, shape, scale=1.0):
        return jax.random.normal(k, shape, f32) * scale

    def gain(k, n):
        return 1.0 + 0.01 * jax.random.normal(k, (DEPTH, n), f32)

    return {
        "x_prompt": nrm(ks[0], (BATCH, SEQ, D_MODEL)),
        "x_sample": nrm(ks[1], (DEC_BATCH, DEC_SEQ, D_MODEL)),
        "cache_mla_ckv": nrm(ks[2], (DEPTH, DEC_BATCH, PAST_LEN, MLA_KV_RANK)),
        "cache_mla_krope": nrm(ks[3], (DEPTH, DEC_BATCH, PAST_LEN, MLA_ROPE)),
        "state_gla": nrm(ks[4], (DEPTH, DEC_BATCH, GLA_HEADS, GLA_DK, GLA_DV)),
        "norm_mix_g": gain(ks[5], D_MODEL),
        "w_in": nrm(ks[6], (DEPTH, D_MODEL, D_IN), D_MODEL ** -0.5),
        "mla_q_norm_g": gain(ks[7], MLA_Q_RANK),
        "mla_w_uq": nrm(ks[8], (DEPTH, MLA_Q_RANK, MLA_HEADS, MLA_QK_DIM), MLA_Q_RANK ** -0.5),
        "mla_kv_norm_g": gain(ks[9], MLA_KV_RANK),
        "mla_w_ukv": nrm(ks[10], (DEPTH, MLA_KV_RANK, MLA_HEADS, MLA_NOPE + MLA_V), MLA_KV_RANK ** -0.5),
        "mla_q_gain_nope": gain(ks[11], MLA_NOPE),
        "mla_q_gain_rope": gain(ks[12], MLA_ROPE // 2),
        "mla_k_gain_nope": gain(ks[13], MLA_NOPE),
        "mla_k_gain_rope": gain(ks[14], MLA_ROPE // 2),
        "gla_w_a2": nrm(ks[15], (DEPTH, GLA_GATE_RANK, GLA_HEADS * GLA_DK), GLA_GATE_RANK ** -0.5),
        "gla_b_a": nrm(ks[16], (DEPTH, GLA_HEADS * GLA_DK), 0.1),
        "gla_norm_g": gain(ks[17], GLA_DV),
        "w_o": nrm(ks[18], (DEPTH, D_MODEL, D_MODEL), D_MODEL ** -0.5),
        "norm_ffn_g": gain(ks[19], D_MODEL),
        "ffn_w_up": nrm(ks[20], (DEPTH, D_MODEL, D_FF), D_MODEL ** -0.5),
        "ffn_w_down": nrm(ks[21], (DEPTH, D_FF, D_MODEL), D_FF ** -0.5),
    }


def reference(x_prompt, x_sample, cache_mla_ckv, cache_mla_krope, state_gla,
              norm_mix_g, w_in, mla_q_norm_g, mla_w_uq, mla_kv_norm_g, mla_w_ukv,
              mla_q_gain_nope, mla_q_gain_rope, mla_k_gain_nope, mla_k_gain_rope,
              gla_w_a2, gla_b_a, gla_norm_g, w_o, norm_ffn_g, ffn_w_up, ffn_w_down):
    past_len = cache_mla_ckv.shape[2]
    pos_p = jnp.arange(x_prompt.shape[1], dtype=jnp.float32)
    pos_s = past_len + jnp.arange(x_sample.shape[1], dtype=jnp.float32)
    xp, xs = x_prompt, x_sample
    ckv_p, kr_p, st_p, ckv_s, kr_s, st_s = [], [], [], [], [], []
    for l in range(DEPTH):
        front_w = (norm_mix_g[l], w_in[l], mla_q_norm_g[l], mla_w_uq[l], mla_kv_norm_g[l],
                   mla_q_gain_nope[l], mla_q_gain_rope[l], mla_k_gain_nope[l], mla_k_gain_rope[l],
                   gla_w_a2[l], gla_b_a[l])
        back_w = (gla_norm_g[l], w_o[l], norm_ffn_g[l], ffn_w_up[l], ffn_w_down[l])

        qn, qr, ckv, kr, gq, gk, gv, la, og, gm, gg = _front(xp, pos_p, *front_w)
        kn, v, inv_k = _mla_keys(ckv, kr, mla_w_ukv[l])
        o_mla = _mla_prompt_attention(qn, qr, kn, kr, v, inv_k)
        s0 = jnp.zeros((xp.shape[0], GLA_HEADS, GLA_DK, GLA_DV), jnp.float32)
        o_gla, s_new = _gla_chunked(gq, gk, gv, la, s0)
        xp_next = _back(xp, o_mla, o_gla, og, gm, gg, *back_w)
        ckv_p.append(ckv)
        kr_p.append(kr)
        st_p.append(s_new.astype(xp.dtype))
        xp = xp_next

        qn, qr, ckv, kr, gq, gk, gv, la, og, gm, gg = _front(xs, pos_s, *front_w)
        ckv_all = jnp.concatenate([cache_mla_ckv[l], ckv], axis=1)
        kr_all = jnp.concatenate([cache_mla_krope[l], kr], axis=1)
        kn, v, inv_k = _mla_keys(ckv_all, kr_all, mla_w_ukv[l])
        o_mla = _attend(qn, qr, kn, kr_all, v, inv_k, None).reshape(xs.shape[0], xs.shape[1], MLA_HEADS * MLA_V)
        o_gla, s_new = _gla_chunked(gq, gk, gv, la, state_gla[l])
        xs_next = _back(xs, o_mla, o_gla, og, gm, gg, *back_w)
        ckv_s.append(ckv)
        kr_s.append(kr)
        st_s.append(s_new.astype(state_gla.dtype))
        xs = xs_next
    return (xp, xs, jnp.stack(ckv_p), jnp.stack(kr_p), jnp.stack(st_p),
            jnp.stack(ckv_s), jnp.stack(kr_s), jnp.stack(st_s))
```

```python
import functools

import jax
import jax.numpy as jnp
import numpy as np
from jax import lax
from jax.experimental import pallas as pl
from jax.experimental.pallas import tpu as pltpu

F32 = jnp.float32
BF16 = jnp.bfloat16

EPS = 1e-6
CHUNK = 64
MLA_NOPE = 128
MLA_ROPE = 64
MLA_V = 128
MLA_QK_DIM = MLA_NOPE + MLA_ROPE
MLA_SCALE = MLA_QK_DIM ** -0.5
MLA_HEAD_PAD = 256
ROPE_THETA = 10000.0
GLA_TAU = 16.0
GLA_LEAF = 32
LANE = 128
VMEM_LIMIT = 56 * 1024 * 1024

NT_DIMS = (((1,), (1,)), ((), ()))
TN_DIMS = (((0,), (0,)), ((), ()))


def _params(*sem):
    return pltpu.CompilerParams(dimension_semantics=sem, vmem_limit_bytes=VMEM_LIMIT)


def _dot(a, b):
    return jnp.dot(a, b, preferred_element_type=F32)


def _dot_nt(a, b):
    return lax.dot_general(a, b, NT_DIMS, preferred_element_type=F32)


def _rms_inv(x, n):
    return lax.rsqrt(jnp.sum(x * x, axis=-1, keepdims=True) / n + EPS)


def _prenorm_kernel(x_ref, g_ref, h_ref):
    x = x_ref[...]
    h_ref[...] = (x * _rms_inv(x, x.shape[-1]) * g_ref[...]).astype(h_ref.dtype)


def _prenorm(x, g, tm):
    t, d = x.shape
    return pl.pallas_call(
        _prenorm_kernel,
        grid=(t // tm,),
        in_specs=[pl.BlockSpec((tm, d), lambda i: (i, 0)), pl.BlockSpec((1, d), lambda i: (0, 0))],
        out_specs=pl.BlockSpec((tm, d), lambda i: (i, 0)),
        out_shape=jax.ShapeDtypeStruct((t, d), BF16),
        compiler_params=_params("parallel"),
        name="prenorm",
    )(x, g)


def _mla_lat_kernel(h_ref, w_ref, qg_ref, kvg_ref, cs_ref, wa2_ref, ba_ref,
                    qlat_ref, ckv_ref, kr_ref, la_ref, *, q_rank, kv_rank, gate_rank):
    z = _dot(h_ref[...], w_ref[...])
    q_lat = z[:, :q_rank]
    qlat_ref[...] = (q_lat * _rms_inv(q_lat, q_rank) * qg_ref[...]).astype(qlat_ref.dtype)
    kv_lat = z[:, q_rank:q_rank + kv_rank]
    ckv_ref[...] = kv_lat * _rms_inv(kv_lat, kv_rank) * kvg_ref[...]
    o = q_rank + kv_rank
    rr = z[:, o:o + LANE] * cs_ref[...]
    kr_ref[...] = rr[:, :MLA_ROPE] + rr[:, MLA_ROPE:]
    a_lr = z[:, o + LANE:o + LANE + gate_rank]
    u = jnp.dot(a_lr, wa2_ref[...], preferred_element_type=F32, precision=lax.Precision.HIGHEST) + ba_ref[...]
    log_sig = jnp.minimum(u, 0.0) - jnp.log1p(jnp.exp(-jnp.abs(u)))
    la_ref[...] = log_sig / GLA_TAU


def _mla_lat(h, w_lat, qg, kvg, cs, wa2, ba, tm, q_rank, kv_rank):
    t, d = h.shape
    n = w_lat.shape[1]
    gate_rank, gk = wa2.shape
    row = lambda i: (i, 0)
    fix = lambda i: (0, 0)
    return pl.pallas_call(
        functools.partial(_mla_lat_kernel, q_rank=q_rank, kv_rank=kv_rank, gate_rank=gate_rank),
        grid=(t // tm,),
        in_specs=[pl.BlockSpec((tm, d), row), pl.BlockSpec((d, n), fix),
                  pl.BlockSpec((1, q_rank), fix), pl.BlockSpec((1, kv_rank), fix),
                  pl.BlockSpec((tm, LANE), row), pl.BlockSpec((gate_rank, gk), fix), pl.BlockSpec((1, gk), fix)],
        out_specs=[pl.BlockSpec((tm, q_rank), row), pl.BlockSpec((tm, kv_rank), row),
                   pl.BlockSpec((tm, MLA_ROPE), row), pl.BlockSpec((tm, gk), row)],
        out_shape=[jax.ShapeDtypeStruct((t, q_rank), BF16), jax.ShapeDtypeStruct((t, kv_rank), F32),
                   jax.ShapeDtypeStruct((t, MLA_ROPE), F32), jax.ShapeDtypeStruct((t, gk), F32)],
        compiler_params=_params("parallel"),
        name="mla_lat",
    )(h, w_lat, qg, kvg, cs, wa2, ba)


def _proj_kernel(h_ref, w_ref, o_ref):
    o_ref[...] = _dot(h_ref[...], w_ref[...]).astype(o_ref.dtype)


def _proj(h, w, tm, tn):
    t, d = h.shape
    n = w.shape[1]
    return pl.pallas_call(
        _proj_kernel,
        grid=(t // tm, n // tn),
        in_specs=[pl.BlockSpec((tm, d), lambda i, j: (i, 0)), pl.BlockSpec((d, tn), lambda i, j: (0, j))],
        out_specs=pl.BlockSpec((tm, tn), lambda i, j: (i, j)),
        out_shape=jax.ShapeDtypeStruct((t, n), BF16),
        compiler_params=_params("parallel", "parallel"),
        name="proj",
    )(h, w)


def _mla_q_kernel(ql_ref, w_ref, cs_ref, gain_ref, q_ref, *, heads):
    ql = ql_ref[...]
    cs = cs_ref[...]
    gain = gain_ref[...]
    lane = lax.broadcasted_iota(jnp.int32, (1, LANE), 1)
    for hd in range(heads):
        c0 = hd * MLA_HEAD_PAD
        z = _dot(ql, w_ref[:, c0:c0 + MLA_HEAD_PAD])
        nope = z[:, :MLA_NOPE]
        rr = z[:, MLA_NOPE:] * cs
        rot = rr + pltpu.roll(rr, MLA_ROPE, 1)
        ss = (jnp.sum(nope * nope, axis=-1, keepdims=True)
              + jnp.sum(jnp.where(lane < MLA_ROPE, rot * rot, 0.0), axis=-1, keepdims=True))
        inv = lax.rsqrt(ss / MLA_QK_DIM + EPS)
        q_ref[:, c0:c0 + MLA_NOPE] = (nope * inv * gain[:, :MLA_NOPE]).astype(q_ref.dtype)
        q_ref[:, c0 + MLA_NOPE:c0 + MLA_HEAD_PAD] = (rot * inv * gain[:, MLA_NOPE:]).astype(q_ref.dtype)


def _mla_q(qlat, w_uq, cs, gain, tm, heads):
    t, r = qlat.shape
    n = heads * MLA_HEAD_PAD
    return pl.pallas_call(
        functools.partial(_mla_q_kernel, heads=heads),
        grid=(t // tm,),
        in_specs=[pl.BlockSpec((tm, r), lambda i: (i, 0)), pl.BlockSpec((r, n), lambda i: (0, 0)),
                  pl.BlockSpec((tm, LANE), lambda i: (i, 0)), pl.BlockSpec((1, MLA_HEAD_PAD), lambda i: (0, 0))],
        out_specs=pl.BlockSpec((tm, n), lambda i: (i, 0)),
        out_shape=jax.ShapeDtypeStruct((t, n), BF16),
        compiler_params=_params("parallel"),
        name="mla_q",
    )(qlat, w_uq, cs, gain)


def _mla_kv_kernel(ckv_ref, kr_ref, w_ref, k_ref, v_ref, *, heads):
    c = ckv_ref[...].astype(BF16)
    kr = kr_ref[...]
    kr2 = jnp.sum(kr * kr, axis=-1, keepdims=True)
    kr_pad = jnp.concatenate([kr, jnp.zeros_like(kr)], axis=1)
    for hd in range(heads):
        z = _dot(c, w_ref[:, hd * MLA_HEAD_PAD:(hd + 1) * MLA_HEAD_PAD])
        kn = z[:, :MLA_NOPE]
        inv = lax.rsqrt((jnp.sum(kn * kn, axis=-1, keepdims=True) + kr2) / MLA_QK_DIM + EPS)
        c0 = hd * MLA_HEAD_PAD
        k_ref[:, c0:c0 + MLA_NOPE] = (kn * inv).astype(k_ref.dtype)
        k_ref[:, c0 + MLA_NOPE:c0 + MLA_HEAD_PAD] = (kr_pad * inv).astype(k_ref.dtype)
        v_ref[:, hd * MLA_V:(hd + 1) * MLA_V] = z[:, MLA_NOPE:].astype(v_ref.dtype)


def _mla_kv(ckv, kr, w_ukv, tm, heads):
    t, r = ckv.shape
    return pl.pallas_call(
        functools.partial(_mla_kv_kernel, heads=heads),
        grid=(t // tm,),
        in_specs=[pl.BlockSpec((tm, r), lambda i: (i, 0)), pl.BlockSpec((tm, MLA_ROPE), lambda i: (i, 0)),
                  pl.BlockSpec((r, heads * MLA_HEAD_PAD), lambda i: (0, 0))],
        out_specs=[pl.BlockSpec((tm, heads * MLA_HEAD_PAD), lambda i: (i, 0)),
                   pl.BlockSpec((tm, heads * MLA_V), lambda i: (i, 0))],
        out_shape=[jax.ShapeDtypeStruct((t, heads * MLA_HEAD_PAD), BF16),
                   jax.ShapeDtypeStruct((t, heads * MLA_V), BF16)],
        compiler_params=_params("parallel"),
        name="mla_kv",
    )(ckv, kr, w_ukv)


def _flash_kernel(q_ref, k_ref, v_ref, o_ref, m_ref, l_ref, acc_ref, *, tq, tk):
    i = pl.program_id(1)
    kk = pl.program_id(2)
    last = ((i + 1) * tq - 1) // tk

    @pl.when(kk == 0)
    def _():
        m_ref[...] = jnp.full_like(m_ref, -jnp.inf)
        l_ref[...] = jnp.zeros_like(l_ref)
        acc_ref[...] = jnp.zeros_like(acc_ref)

    def update(s):
        m_prev = m_ref[...]
        m_new = jnp.maximum(m_prev, jnp.max(s, axis=-1, keepdims=True))
        alpha = jnp.exp(m_prev - m_new)
        p = jnp.exp(s - m_new)
        l_ref[...] = alpha * l_ref[...] + jnp.sum(p, axis=-1, keepdims=True)
        acc_ref[...] = alpha * acc_ref[...] + _dot(p.astype(v_ref.dtype), v_ref[...])
        m_ref[...] = m_new

    straddles = ((kk + 1) * tk - 1) // CHUNK > (i * tq) // CHUNK

    @pl.when(jnp.logical_and(kk <= last, jnp.logical_not(straddles)))
    def _():
        update(_dot_nt(q_ref[...], k_ref[...]))

    @pl.when(jnp.logical_and(kk <= last, straddles))
    def _():
        s = _dot_nt(q_ref[...], k_ref[...])
        q_chunk = (i * tq + lax.broadcasted_iota(jnp.int32, (tq, tk), 0)) // CHUNK
        k_chunk = (kk * tk + lax.broadcasted_iota(jnp.int32, (tq, tk), 1)) // CHUNK
        update(jnp.where(k_chunk <= q_chunk, s, -jnp.inf))

    @pl.when(kk == last)
    def _():
        o_ref[...] = (acc_ref[...] / l_ref[...]).astype(o_ref.dtype)


def _flash(q, k, v, heads, tq, tk):
    t = q.shape[0]

    def kv_map(hd, i, kk):
        return (jnp.minimum(kk, ((i + 1) * tq - 1) // tk), hd)

    return pl.pallas_call(
        functools.partial(_flash_kernel, tq=tq, tk=tk),
        grid=(heads, t // tq, t // tk),
        in_specs=[pl.BlockSpec((tq, MLA_HEAD_PAD), lambda hd, i, kk: (i, hd)),
                  pl.BlockSpec((tk, MLA_HEAD_PAD), kv_map),
                  pl.BlockSpec((tk, MLA_V), kv_map)],
        out_specs=pl.BlockSpec((tq, MLA_V), lambda hd, i, kk: (i, hd)),
        out_shape=jax.ShapeDtypeStruct((t, heads * MLA_V), BF16),
        scratch_shapes=[pltpu.VMEM((tq, 1), F32), pltpu.VMEM((tq, 1), F32), pltpu.VMEM((tq, MLA_V), F32)],
        compiler_params=_params("parallel", "parallel", "arbitrary"),
        name="flash",
    )(q, k, v)


def _sattn_kernel(q_ref, cc_ref, ck_ref, nc_ref, nk_ref, wuk_ref, wuv_ref, o_ref,
                  qabs_ref, qr_ref, m_ref, l_ref, acc_ref, *, heads, nq, tk, n_tiles, n_new):
    kk = pl.program_id(1)
    rank = cc_ref.shape[-1]

    @pl.when(kk == 0)
    def _():
        for hd in range(heads):
            c0 = hd * MLA_HEAD_PAD
            qn = q_ref[:, c0:c0 + MLA_NOPE]
            qabs_ref[hd * nq:(hd + 1) * nq, :] = _dot(
                qn, wuk_ref[hd * MLA_NOPE:(hd + 1) * MLA_NOPE, :]).astype(qabs_ref.dtype)
            qr_ref[hd * nq:(hd + 1) * nq, :] = q_ref[:, c0 + MLA_NOPE:c0 + MLA_NOPE + MLA_ROPE]
        m_ref[...] = jnp.full_like(m_ref, -jnp.inf)
        l_ref[...] = jnp.zeros_like(l_ref)
        acc_ref[...] = jnp.zeros_like(acc_ref)

    def process(c_f32, kr, n_valid):
        n_keys = c_f32.shape[0]
        c = c_f32.astype(BF16)
        kn_t = _dot_nt(wuk_ref[...], c)
        ss_t = jnp.sum((kn_t * kn_t).reshape(heads, MLA_NOPE, n_keys), axis=1)
        kr2_t = lax.dot_general(jnp.ones((8, MLA_ROPE), F32), kr * kr, NT_DIMS,
                                preferred_element_type=F32, precision=lax.Precision.HIGHEST)[0:1]
        inv_t = lax.rsqrt((ss_t + kr2_t) / MLA_QK_DIM + EPS)
        s = _dot_nt(qabs_ref[...], c) + _dot_nt(qr_ref[...], kr.astype(BF16))
        inv_rows = jnp.broadcast_to(inv_t[:, None, :], (heads, nq, n_keys)).reshape(heads * nq, n_keys)
        s = s * inv_rows
        if n_valid < n_keys:
            s = jnp.where(lax.broadcasted_iota(jnp.int32, s.shape, 1) < n_valid, s, -jnp.inf)
        m_prev = m_ref[...]
        m_new = jnp.maximum(m_prev, jnp.max(s, axis=-1, keepdims=True))
        alpha = jnp.exp(m_prev - m_new)
        p = jnp.exp(s - m_new)
        l_ref[...] = alpha * l_ref[...] + jnp.sum(p, axis=-1, keepdims=True)
        acc_ref[...] = alpha * acc_ref[...] + _dot(p.astype(BF16), c)
        m_ref[...] = m_new

    @pl.when(kk < n_tiles)
    def _():
        process(cc_ref[...], ck_ref[...], tk)

    @pl.when(kk == n_tiles)
    def _():
        pad = LANE - n_new
        c_new = jnp.concatenate([nc_ref[...], jnp.zeros((pad, rank), F32)], axis=0)
        k_new = jnp.concatenate([nk_ref[...], jnp.zeros((pad, MLA_ROPE), F32)], axis=0)
        process(c_new, k_new, n_new)
        o_lat = (acc_ref[...] / l_ref[...]).astype(BF16)
        for hd in range(heads):
            o_ref[:, hd * MLA_V:(hd + 1) * MLA_V] = _dot(
                o_lat[hd * nq:(hd + 1) * nq, :], wuv_ref[:, hd * MLA_V:(hd + 1) * MLA_V]).astype(o_ref.dtype)


def _sattn(q, cache_c, cache_k, new_c, new_k, wuk_t, wuv, heads, nq, tk):
    nb, past, rank = cache_c.shape
    n_tiles = past // tk
    cache_map = lambda b, kk: (b, jnp.minimum(kk, n_tiles - 1), 0)
    return pl.pallas_call(
        functools.partial(_sattn_kernel, heads=heads, nq=nq, tk=tk, n_tiles=n_tiles, n_new=nq),
        grid=(nb, n_tiles + 1),
        in_specs=[pl.BlockSpec((nq, heads * MLA_HEAD_PAD), lambda b, kk: (b, 0)),
                  pl.BlockSpec((None, tk, rank), cache_map),
                  pl.BlockSpec((None, tk, MLA_ROPE), cache_map),
                  pl.BlockSpec((nq, rank), lambda b, kk: (b, 0)),
                  pl.BlockSpec((nq, MLA_ROPE), lambda b, kk: (b, 0)),
                  pl.BlockSpec(wuk_t.shape, lambda b, kk: (0, 0)),
                  pl.BlockSpec(wuv.shape, lambda b, kk: (0, 0))],
        out_specs=pl.BlockSpec((nq, heads * MLA_V), lambda b, kk: (b, 0)),
        out_shape=jax.ShapeDtypeStruct((nb * nq, heads * MLA_V), BF16),
        scratch_shapes=[pltpu.VMEM((heads * nq, rank), BF16), pltpu.VMEM((heads * nq, MLA_ROPE), BF16),
                        pltpu.VMEM((heads * nq, 1), F32), pltpu.VMEM((heads * nq, 1), F32),
                        pltpu.VMEM((heads * nq, rank), F32)],
        compiler_params=_params("parallel", "arbitrary"),
        name="sattn",
    )(q, cache_c, cache_k, new_c, new_k, wuk_t, wuv)


def _gla_levels(c):
    leaf = min(c, GLA_LEAF)
    levels = [(leaf, leaf // 2 - 1)]
    g = 2 * leaf
    while g <= c:
        levels.append((g, g // 2 - 1))
        g *= 2
    return levels


def _gla_kernel(*refs, heads, dk, dv, c, has_s0):
    if has_s0:
        q_ref, k_ref, v_ref, la_ref, g_ref, s0_ref, o_ref, sout_ref, st_ref = refs
    else:
        q_ref, k_ref, v_ref, la_ref, g_ref, o_ref, sout_ref, st_ref = refs
    j = pl.program_id(1)
    nj = pl.num_programs(1)

    @pl.when(j == 0)
    def _():
        for hd in range(heads):
            if has_s0:
                st_ref[hd] = s0_ref[hd].T
            else:
                st_ref[hd] = jnp.zeros((dv, dk), F32)

    levels = _gla_levels(c)
    row = lax.broadcasted_iota(jnp.int32, (c, c), 0)
    col = lax.broadcasted_iota(jnp.int32, (c, c), 1)
    masks = []
    for lv, (g, _) in enumerate(levels):
        same = (row // g) == (col // g)
        if lv == 0:
            masks.append(jnp.logical_and(same, col <= row))
        else:
            half = g // 2
            masks.append(jnp.logical_and(same, jnp.logical_and((row // half) % 2 == 1, (col // half) % 2 == 0)))
    rid = lax.broadcasted_iota(jnp.int32, (c, dk), 0)
    gain = g_ref[...]

    for hd in range(heads):
        la = la_ref[:, hd * dk:(hd + 1) * dk]
        b = la
        sh = 1
        while sh < c:
            b = b + jnp.where(rid >= sh, pltpu.roll(b, sh, 0), 0.0)
            sh *= 2
        b_end = b[c - 1:c, :]
        q = q_ref[:, hd * dk:(hd + 1) * dk].astype(F32) * (dk ** -0.5)
        k = k_ref[:, hd * dk:(hd + 1) * dk].astype(F32)
        v = v_ref[:, hd * dv:(hd + 1) * dv]

        a = jnp.zeros((c, c), F32)
        for lv, (g, r) in enumerate(levels):
            ref_rows = jnp.broadcast_to(b.reshape(c // g, g, dk)[:, r:r + 1, :], (c // g, g, dk)).reshape(c, dk)
            d = b - ref_rows
            if lv == 0:
                dq, dkk = d, -d
            else:
                dq, dkk = jnp.minimum(d, 0.0), jnp.minimum(-d, 0.0)
            a_lv = _dot_nt((q * jnp.exp(dq)).astype(BF16), (k * jnp.exp(dkk)).astype(BF16))
            a = jnp.where(masks[lv], a_lv, a)

        st = st_ref[hd]
        o = _dot(a.astype(BF16), v) + _dot_nt((q * jnp.exp(b)).astype(BF16), st.astype(BF16))
        k_end = (k * jnp.exp(b_end - b)).astype(BF16)
        st_ref[hd] = st * jnp.exp(b_end) + lax.dot_general(v, k_end, TN_DIMS, preferred_element_type=F32)
        o_ref[:, hd * dv:(hd + 1) * dv] = (o * _rms_inv(o, dv) * gain).astype(o_ref.dtype)

    @pl.when(j == nj - 1)
    def _():
        for hd in range(heads):
            sout_ref[hd] = st_ref[hd].T


def _gla(zp, la, gain, s0, nb, c, heads, dk, dv):
    t = la.shape[0]
    nj = t // (nb * c)
    gk, gv = heads * dk, heads * dv
    assert gv % gk == 0
    row = lambda bb, j: (bb * nj + j, 0)
    in_specs = [pl.BlockSpec((c, gk), row),
                pl.BlockSpec((c, gk), lambda bb, j: (bb * nj + j, 1)),
                pl.BlockSpec((c, gv), lambda bb, j: (bb * nj + j, 2 * gk // gv)),
                pl.BlockSpec((c, gk), row),
                pl.BlockSpec((1, dv), lambda bb, j: (0, 0))]
    args = [zp, zp, zp, la, gain]
    if s0 is not None:
        in_specs.append(pl.BlockSpec((None, heads, dk, dv), lambda bb, j: (bb, 0, 0, 0)))
        args.append(s0)
    return pl.pallas_call(
        functools.partial(_gla_kernel, heads=heads, dk=dk, dv=dv, c=c, has_s0=s0 is not None),
        grid=(nb, nj),
        in_specs=in_specs,
        out_specs=[pl.BlockSpec((c, gv), row),
                   pl.BlockSpec((None, heads, dk, dv), lambda bb, j: (bb, 0, 0, 0))],
        out_shape=[jax.ShapeDtypeStruct((t, gv), BF16), jax.ShapeDtypeStruct((nb, heads, dk, dv), F32)],
        scratch_shapes=[pltpu.VMEM((heads, dv, dk), F32)],
        compiler_params=_params("parallel", "arbitrary"),
        name="gla",
    )(*args)


def _mix_kernel(om_ref, og_ref, gate_o_ref, gate_m_ref, gate_g_ref, x_ref, w_ref, g_ref, y_ref, h_ref):
    o_gla = og_ref[...].astype(F32) * jax.nn.silu(gate_o_ref[...].astype(F32))
    mixed = (jax.nn.sigmoid(gate_m_ref[...].astype(F32)) * om_ref[...].astype(F32)
             + jax.nn.sigmoid(gate_g_ref[...].astype(F32)) * o_gla)
    y = x_ref[...] + _dot(mixed.astype(BF16), w_ref[...])
    y_ref[...] = y
    h_ref[...] = (y * _rms_inv(y, y.shape[-1]) * g_ref[...]).astype(h_ref.dtype)


def _mix(o_mla, o_gla, zp, gate_block0, x, w_o, g, tm):
    t, d = x.shape
    row = lambda i: (i, 0)
    return pl.pallas_call(
        _mix_kernel,
        grid=(t // tm,),
        in_specs=[pl.BlockSpec((tm, d), row), pl.BlockSpec((tm, d), row),
                  pl.BlockSpec((tm, d), lambda i: (i, gate_block0)),
                  pl.BlockSpec((tm, d), lambda i: (i, gate_block0 + 1)),
                  pl.BlockSpec((tm, d), lambda i: (i, gate_block0 + 2)),
                  pl.BlockSpec((tm, d), row), pl.BlockSpec((d, d), lambda i: (0, 0)),
                  pl.BlockSpec((1, d), lambda i: (0, 0))],
        out_specs=[pl.BlockSpec((tm, d), row), pl.BlockSpec((tm, d), row)],
        out_shape=[jax.ShapeDtypeStruct((t, d), F32), jax.ShapeDtypeStruct((t, d), BF16)],
        compiler_params=_params("parallel"),
        name="mix",
    )(o_mla, o_gla, zp, zp, zp, x, w_o, g)


def _ffn_kernel(h_ref, y_ref, wu_ref, wd_ref, o_ref):
    f = pl.program_id(1)

    @pl.when(f == 0)
    def _():
        o_ref[...] = y_ref[...]

    u = jnp.maximum(_dot(h_ref[...], wu_ref[...]), 0.0)
    o_ref[...] += _dot((u * u).astype(BF16), wd_ref[...])


def _ffn(h, y, w_up, w_down, tm, tf):
    t, d = y.shape
    dff = w_up.shape[1]
    return pl.pallas_call(
        _ffn_kernel,
        grid=(t // tm, dff // tf),
        in_specs=[pl.BlockSpec((tm, d), lambda i, f: (i, 0)), pl.BlockSpec((tm, d), lambda i, f: (i, 0)),
                  pl.BlockSpec((d, tf), lambda i, f: (0, f)), pl.BlockSpec((tf, d), lambda i, f: (f, 0))],
        out_specs=pl.BlockSpec((tm, d), lambda i, f: (i, 0)),
        out_shape=jax.ShapeDtypeStruct((t, d), F32),
        compiler_params=_params("parallel", "arbitrary"),
        name="ffn",
    )(h, y, w_up, w_down)


def _rope_table(pos):
    half = MLA_ROPE // 2
    freqs = jnp.power(ROPE_THETA, -jnp.arange(half, dtype=F32) / half)
    ang = pos[:, None] * freqs[None, :]
    cos, sin = jnp.cos(ang), jnp.sin(ang)
    return jnp.concatenate([cos, cos, -sin, sin], axis=1)


def _swap_halves(w):
    half = w.shape[-1] // 2
    return jnp.concatenate([w[..., half:], w[..., :half]], axis=-1)


def _pick(t, pref):
    while t % pref:
        pref //= 2
    return pref


def kernel(x_prompt, x_sample, cache_mla_ckv, cache_mla_krope, state_gla, norm_mix_g, w_in, mla_q_norm_g,
           mla_w_uq, mla_kv_norm_g, mla_w_ukv, mla_q_gain_nope, mla_q_gain_rope, mla_k_gain_nope,
           mla_k_gain_rope, gla_w_a2, gla_b_a, gla_norm_g, w_o, norm_ffn_g, ffn_w_up, ffn_w_down):
    depth = w_in.shape[0]
    bp, seq, d = x_prompt.shape
    nb, dec_seq, _ = x_sample.shape
    past = cache_mla_ckv.shape[2]
    q_rank, heads = mla_w_uq.shape[1], mla_w_uq.shape[2]
    kv_rank = mla_w_ukv.shape[1]
    gla_heads, dk, dv = state_gla.shape[2], state_gla.shape[3], state_gla.shape[4]
    gate_rank = gla_w_a2.shape[1]
    gk, gv = gla_heads * dk, gla_heads * dv
    assert bp == 1 and heads * MLA_V == d and gv == d and seq % CHUNK == 0

    cs_p = _rope_table(jnp.arange(seq, dtype=F32))
    cs_s = jnp.tile(_rope_table(past + jnp.arange(dec_seq, dtype=F32)), (nb, 1))

    xp = x_prompt.reshape(seq, d)
    xs = x_sample.reshape(nb * dec_seq, d)
    outs = [[] for _ in range(6)]
    for l in range(depth):
        wi = w_in[l]
        pts = np.cumsum([q_rank, kv_rank, MLA_ROPE, gk, gk, gv, gate_rank, gv, d]).tolist()
        w_qkv_lat = wi[:, :pts[1]]
        w_kr = wi[:, pts[1]:pts[2]]
        w_alr = wi[:, pts[5]:pts[6]]
        w_lat = jnp.concatenate(
            [w_qkv_lat, w_kr, _swap_halves(w_kr), w_alr, jnp.zeros((d, LANE - gate_rank), F32)], axis=1).astype(BF16)
        w_proj = jnp.concatenate([wi[:, pts[2]:pts[5]], wi[:, pts[6]:]], axis=1).astype(BF16)
        gate_block0 = (2 * gk + gv) // d
        assert (2 * gk + gv) % d == 0

        wq = mla_w_uq[l]
        wq_r = wq[..., MLA_NOPE:]
        w_uq = jnp.concatenate([wq[..., :MLA_NOPE], wq_r, _swap_halves(wq_r)], axis=-1)
        w_uq = w_uq.reshape(q_rank, heads * MLA_HEAD_PAD).astype(BF16)
        w_ukv = mla_w_ukv[l].reshape(kv_rank, heads * MLA_HEAD_PAD).astype(BF16)
        wuk_t = mla_w_ukv[l][..., :MLA_NOPE].reshape(kv_rank, heads * MLA_NOPE).T.astype(BF16)
        wuv = mla_w_ukv[l][..., MLA_NOPE:].reshape(kv_rank, heads * MLA_V).astype(BF16)
        gain = jnp.concatenate([mla_q_gain_nope[l] * mla_k_gain_nope[l],
                                jnp.tile(mla_q_gain_rope[l] * mla_k_gain_rope[l], 2),
                                jnp.zeros((MLA_HEAD_PAD - MLA_QK_DIM,), F32)])[None, :] * MLA_SCALE
        wo_b = w_o[l].astype(BF16)
        wup_b = ffn_w_up[l].astype(BF16)
        wdn_b = ffn_w_down[l].astype(BF16)
        g_mix = norm_mix_g[l][None, :]
        g_q = mla_q_norm_g[l][None, :]
        g_kv = mla_kv_norm_g[l][None, :]
        g_gla = gla_norm_g[l][None, :]
        g_ffn = norm_ffn_g[l][None, :]
        ba = gla_b_a[l][None, :]

        def front(x, cs, tm):
            h = _prenorm(x, g_mix, _pick(x.shape[0], 256))
            qlat, ckv, kr, la = _mla_lat(h, w_lat, g_q, g_kv, cs, gla_w_a2[l], ba, tm, q_rank, kv_rank)
            zp = _proj(h, w_proj, _pick(x.shape[0], 1024), 1024)
            qcat = _mla_q(qlat, w_uq, cs, gain, tm, heads)
            return qcat, ckv, kr, la, zp

        def back(x, o_mla, o_gla, zp, tm):
            y1, h2 = _mix(o_mla, o_gla, zp, gate_block0, x, wo_b, g_ffn, tm)
            return _ffn(h2, y1, wup_b, wdn_b, tm, 1024)

        qcat, ckv, kr, la, zp = front(xp, cs_p, 512)
        kcat, v = _mla_kv(ckv, kr, w_ukv, 512, heads)
        o_mla = _flash(qcat, kcat, v, heads, _pick(seq, 1024), _pick(seq, 1024))
        o_gla, st = _gla(zp, la, g_gla, None, 1, _pick(seq, 128), gla_heads, dk, dv)
        xp_next = back(xp, o_mla, o_gla, zp, 512)
        outs[0].append(ckv.reshape(bp, seq, kv_rank))
        outs[1].append(kr.reshape(bp, seq, MLA_ROPE))
        outs[2].append(st)
        xp = xp_next

        qcat, ckv, kr, la, zp = front(xs, cs_s, nb * dec_seq)
        o_mla = _sattn(qcat, cache_mla_ckv[l], cache_mla_krope[l], ckv, kr, wuk_t, wuv, heads, dec_seq, 512)
        o_gla, st = _gla(zp, la, g_gla, state_gla[l], nb, dec_seq, gla_heads, dk, dv)
        xs_next = back(xs, o_mla, o_gla, zp, nb * dec_seq)
        outs[3].append(ckv.reshape(nb, dec_seq, kv_rank))
        outs[4].append(kr.reshape(nb, dec_seq, MLA_ROPE))
        outs[5].append(st)
        xs = xs_next

    return (xp.reshape(bp, seq, d), xs.reshape(nb, dec_seq, d),
            jnp.stack(outs[0]), jnp.stack(outs[1]), jnp.stack(outs[2]),
            jnp.stack(outs[3]), jnp.stack(outs[4]), jnp.stack(outs[5]))
```

```python
import functools

import jax
import jax.numpy as jnp
import numpy as np
from jax import lax
from jax.experimental import pallas as pl
from jax.experimental.pallas import tpu as pltpu

F32 = jnp.float32
BF16 = jnp.bfloat16

EPS = 1e-6
CHUNK = 64
MLA_NOPE = 128
MLA_ROPE = 64
MLA_V = 128
MLA_QK_DIM = MLA_NOPE + MLA_ROPE
MLA_SCALE = MLA_QK_DIM ** -0.5
MLA_HEAD_PAD = 256
V_ROWS = MLA_V + 16
LOG2E = 1.4426950408889634
ROPE_THETA = 10000.0
GLA_TAU = 16.0
GLA_LEAF = 32
LANE = 128
VMEM_LIMIT = 56 * 1024 * 1024

NT_DIMS = (((1,), (1,)), ((), ()))
TN_DIMS = (((0,), (0,)), ((), ()))


def _params(*sem):
    return pltpu.CompilerParams(dimension_semantics=sem, vmem_limit_bytes=VMEM_LIMIT)


def _dot(a, b):
    return jnp.dot(a, b, preferred_element_type=F32)


def _dot_nt(a, b):
    return lax.dot_general(a, b, NT_DIMS, preferred_element_type=F32)


def _rms_inv(x, n):
    return lax.rsqrt(jnp.sum(x * x, axis=-1, keepdims=True) / n + EPS)


def _prenorm_kernel(x_ref, g_ref, h_ref):
    x = x_ref[...]
    h_ref[...] = (x * _rms_inv(x, x.shape[-1]) * g_ref[...]).astype(h_ref.dtype)


def _prenorm(x, g, tm):
    t, d = x.shape
    return pl.pallas_call(
        _prenorm_kernel,
        grid=(t // tm,),
        in_specs=[pl.BlockSpec((tm, d), lambda i: (i, 0)), pl.BlockSpec((1, d), lambda i: (0, 0))],
        out_specs=pl.BlockSpec((tm, d), lambda i: (i, 0)),
        out_shape=jax.ShapeDtypeStruct((t, d), BF16),
        compiler_params=_params("parallel"),
        name="prenorm",
    )(x, g)


def _mla_lat_kernel(h_ref, w_ref, qg_ref, kvg_ref, cs_ref, wa2_ref, ba_ref,
                    qlat_ref, ckv_ref, kr_ref, la_ref, *, q_rank, kv_rank, gate_rank):
    z = _dot(h_ref[...], w_ref[...])
    q_lat = z[:, :q_rank]
    qlat_ref[...] = (q_lat * _rms_inv(q_lat, q_rank) * qg_ref[...]).astype(qlat_ref.dtype)
    kv_lat = z[:, q_rank:q_rank + kv_rank]
    ckv_ref[...] = kv_lat * _rms_inv(kv_lat, kv_rank) * kvg_ref[...]
    o = q_rank + kv_rank
    rr = z[:, o:o + LANE] * cs_ref[...]
    kr_ref[...] = rr[:, :MLA_ROPE] + rr[:, MLA_ROPE:]
    a_lr = z[:, o + LANE:o + LANE + gate_rank]
    u = jnp.dot(a_lr, wa2_ref[...], preferred_element_type=F32, precision=lax.Precision.HIGHEST) + ba_ref[...]
    log_sig = jnp.minimum(u, 0.0) - jnp.log1p(jnp.exp(-jnp.abs(u)))
    la_ref[...] = log_sig / GLA_TAU


def _mla_lat(h, w_lat, qg, kvg, cs, wa2, ba, tm, q_rank, kv_rank):
    t, d = h.shape
    n = w_lat.shape[1]
    gate_rank, gk = wa2.shape
    row = lambda i: (i, 0)
    fix = lambda i: (0, 0)
    return pl.pallas_call(
        functools.partial(_mla_lat_kernel, q_rank=q_rank, kv_rank=kv_rank, gate_rank=gate_rank),
        grid=(t // tm,),
        in_specs=[pl.BlockSpec((tm, d), row), pl.BlockSpec((d, n), fix),
                  pl.BlockSpec((1, q_rank), fix), pl.BlockSpec((1, kv_rank), fix),
                  pl.BlockSpec((tm, LANE), row), pl.BlockSpec((gate_rank, gk), fix), pl.BlockSpec((1, gk), fix)],
        out_specs=[pl.BlockSpec((tm, q_rank), row), pl.BlockSpec((tm, kv_rank), row),
                   pl.BlockSpec((tm, MLA_ROPE), row), pl.BlockSpec((tm, gk), row)],
        out_shape=[jax.ShapeDtypeStruct((t, q_rank), BF16), jax.ShapeDtypeStruct((t, kv_rank), F32),
                   jax.ShapeDtypeStruct((t, MLA_ROPE), F32), jax.ShapeDtypeStruct((t, gk), F32)],
        compiler_params=_params("parallel"),
        name="mla_lat",
    )(h, w_lat, qg, kvg, cs, wa2, ba)


def _proj_kernel(h_ref, w_ref, o_ref):
    o_ref[...] = _dot(h_ref[...], w_ref[...]).astype(o_ref.dtype)


def _proj(h, w, tm, tn):
    t, d = h.shape
    n = w.shape[1]
    return pl.pallas_call(
        _proj_kernel,
        grid=(t // tm, n // tn),
        in_specs=[pl.BlockSpec((tm, d), lambda i, j: (i, 0)), pl.BlockSpec((d, tn), lambda i, j: (0, j))],
        out_specs=pl.BlockSpec((tm, tn), lambda i, j: (i, j)),
        out_shape=jax.ShapeDtypeStruct((t, n), BF16),
        compiler_params=_params("parallel", "parallel"),
        name="proj",
    )(h, w)


def _mla_q_kernel(ql_ref, w_ref, cs_ref, gain_ref, q_ref, *, heads):
    ql = ql_ref[...]
    cs = cs_ref[...]
    gain = gain_ref[...]
    lane = lax.broadcasted_iota(jnp.int32, (1, LANE), 1)
    for hd in range(heads):
        c0 = hd * MLA_HEAD_PAD
        z = _dot(ql, w_ref[:, c0:c0 + MLA_HEAD_PAD])
        nope = z[:, :MLA_NOPE]
        rr = z[:, MLA_NOPE:] * cs
        rot = rr + pltpu.roll(rr, MLA_ROPE, 1)
        ss = (jnp.sum(nope * nope, axis=-1, keepdims=True)
              + jnp.sum(jnp.where(lane < MLA_ROPE, rot * rot, 0.0), axis=-1, keepdims=True))
        inv = lax.rsqrt(ss / MLA_QK_DIM + EPS)
        q_ref[:, c0:c0 + MLA_NOPE] = (nope * inv * gain[:, :MLA_NOPE]).astype(q_ref.dtype)
        q_ref[:, c0 + MLA_NOPE:c0 + MLA_HEAD_PAD] = (rot * inv * gain[:, MLA_NOPE:]).astype(q_ref.dtype)


def _mla_q(qlat, w_uq, cs, gain, tm, heads):
    t, r = qlat.shape
    n = heads * MLA_HEAD_PAD
    return pl.pallas_call(
        functools.partial(_mla_q_kernel, heads=heads),
        grid=(t // tm,),
        in_specs=[pl.BlockSpec((tm, r), lambda i: (i, 0)), pl.BlockSpec((r, n), lambda i: (0, 0)),
                  pl.BlockSpec((tm, LANE), lambda i: (i, 0)), pl.BlockSpec((1, MLA_HEAD_PAD), lambda i: (0, 0))],
        out_specs=pl.BlockSpec((tm, n), lambda i: (i, 0)),
        out_shape=jax.ShapeDtypeStruct((t, n), BF16),
        compiler_params=_params("parallel"),
        name="mla_q",
    )(qlat, w_uq, cs, gain)


def _mla_qt_kernel(ql_ref, w_ref, cs_ref, gain_ref, q_ref, *, heads):
    ql = ql_ref[...]
    cs = cs_ref[...]
    gain = gain_ref[...]
    for hd in range(heads):
        r0 = hd * MLA_HEAD_PAD
        z = _dot_nt(w_ref[r0:r0 + MLA_HEAD_PAD, :], ql)
        nope = z[:MLA_NOPE]
        rr = z[MLA_NOPE:] * cs
        rot = rr[:MLA_ROPE] + rr[MLA_ROPE:]
        ss = jnp.sum(nope * nope, axis=0, keepdims=True) + jnp.sum(rot * rot, axis=0, keepdims=True)
        inv = lax.rsqrt(ss / MLA_QK_DIM + EPS)
        q_ref[r0:r0 + MLA_NOPE, :] = (nope * inv * gain[:MLA_NOPE]).astype(q_ref.dtype)
        q_ref[r0 + MLA_NOPE:r0 + MLA_QK_DIM, :] = (rot * inv * gain[MLA_NOPE:MLA_QK_DIM]).astype(q_ref.dtype)
        q_ref[r0 + MLA_QK_DIM:r0 + MLA_HEAD_PAD, :] = jnp.zeros((MLA_HEAD_PAD - MLA_QK_DIM, ql.shape[0]), q_ref.dtype)


def _mla_qt(qlat, w_uq_t, cs_t, gain_col, tm, heads):
    t, r = qlat.shape
    n = heads * MLA_HEAD_PAD
    return pl.pallas_call(
        functools.partial(_mla_qt_kernel, heads=heads),
        grid=(t // tm,),
        in_specs=[pl.BlockSpec((tm, r), lambda i: (i, 0)), pl.BlockSpec((n, r), lambda i: (0, 0)),
                  pl.BlockSpec((LANE, tm), lambda i: (0, i)), pl.BlockSpec((MLA_HEAD_PAD, 1), lambda i: (0, 0))],
        out_specs=pl.BlockSpec((n, tm), lambda i: (0, i)),
        out_shape=jax.ShapeDtypeStruct((n, t), BF16),
        compiler_params=_params("parallel"),
        name="mla_qt",
    )(qlat, w_uq_t, cs_t, gain_col)


def _mla_kv_kernel(ckv_ref, kr_ref, wk_ref, wvt_ref, k_ref, vt_ref, *, heads):
    c = ckv_ref[...].astype(BF16)
    kr = kr_ref[...]
    kr2 = jnp.sum(kr * kr, axis=-1, keepdims=True)
    kr_pad = jnp.concatenate([kr, jnp.zeros_like(kr)], axis=1)
    for pair in range(heads // 2):
        z = _dot(c, wk_ref[:, pair * 2 * MLA_NOPE:(pair + 1) * 2 * MLA_NOPE])
        for sub in range(2):
            kn = z[:, sub * MLA_NOPE:(sub + 1) * MLA_NOPE]
            inv = lax.rsqrt((jnp.sum(kn * kn, axis=-1, keepdims=True) + kr2) / MLA_QK_DIM + EPS)
            c0 = (2 * pair + sub) * MLA_HEAD_PAD
            k_ref[:, c0:c0 + MLA_NOPE] = (kn * inv).astype(k_ref.dtype)
            k_ref[:, c0 + MLA_NOPE:c0 + MLA_HEAD_PAD] = (kr_pad * inv).astype(k_ref.dtype)
    vt = _dot_nt(wvt_ref[...], c)
    for hd in range(heads):
        vt_ref[hd, :MLA_V, :] = vt[hd * MLA_V:(hd + 1) * MLA_V].astype(vt_ref.dtype)
        vt_ref[hd, MLA_V:, :] = jnp.ones((V_ROWS - MLA_V, vt.shape[1]), vt_ref.dtype)


def _mla_kv(ckv, kr, w_uk, w_uv_t, tm, heads):
    t, r = ckv.shape
    return pl.pallas_call(
        functools.partial(_mla_kv_kernel, heads=heads),
        grid=(t // tm,),
        in_specs=[pl.BlockSpec((tm, r), lambda i: (i, 0)), pl.BlockSpec((tm, MLA_ROPE), lambda i: (i, 0)),
                  pl.BlockSpec((r, heads * MLA_NOPE), lambda i: (0, 0)),
                  pl.BlockSpec((heads * MLA_V, r), lambda i: (0, 0))],
        out_specs=[pl.BlockSpec((tm, heads * MLA_HEAD_PAD), lambda i: (i, 0)),
                   pl.BlockSpec((heads, None, V_ROWS, tm), lambda i: (0, i, 0, 0))],
        out_shape=[jax.ShapeDtypeStruct((t, heads * MLA_HEAD_PAD), BF16),
                   jax.ShapeDtypeStruct((heads, t // tm, V_ROWS, tm), BF16)],
        compiler_params=_params("parallel"),
        name="mla_kv",
    )(ckv, kr, w_uk, w_uv_t)


def _flash_kernel(qt_ref, k_ref, vt_ref, o_ref, m_ref, acc_ref, s_ref, mx_ref, p_ref, al_ref, *, tq, tk):
    i = pl.program_id(1)
    m_ref[...] = jnp.full_like(m_ref, -jnp.inf)
    acc_ref[...] = jnp.zeros_like(acc_ref)

    def score(t, slot, masked):
        k = k_ref[pl.ds(pl.multiple_of(t * tk, tk), tk), :]
        s = _dot(k, qt_ref[...])
        if masked:
            k_chunk = (t * tk + lax.broadcasted_iota(jnp.int32, (tk, tq), 0)) // CHUNK
            q_chunk = (i * tq + lax.broadcasted_iota(jnp.int32, (tk, tq), 1)) // CHUNK
            s = jnp.where(k_chunk <= q_chunk, s, -jnp.inf)
        s_ref[slot] = s
        mx_ref[slot] = jnp.max(s, axis=0, keepdims=True)

    def soften(slot):
        m_prev = m_ref[...]
        m_new = jnp.maximum(m_prev, mx_ref[slot])
        p_ref[slot] = jnp.exp2(s_ref[slot] - m_new).astype(BF16)
        al_ref[slot] = jnp.exp2(m_prev - m_new)
        m_ref[...] = m_new

    def gather(t, slot):
        acc_ref[...] = al_ref[slot] * acc_ref[...] + _dot(vt_ref[t], p_ref[slot])

    def pair(g, masked):
        score(g, 0, masked)
        soften(1)
        gather(g - 2, 0)
        score(g + 1, 1, masked)
        soften(0)
        gather(g - 1, 1)

    def head(masked):
        score(0, 0, masked)
        score(1, 1, masked)
        soften(0)

    @pl.when(i == 0)
    def _():
        head(True)

    @pl.when(i > 0)
    def _():
        head(False)

    def body(j, carry):
        pair(2 * j, False)
        return carry

    lax.fori_loop(1, i, body, 0)

    @pl.when(i > 0)
    def _():
        pair(2 * i, True)

    soften(1)
    gather(2 * i, 0)
    gather(2 * i + 1, 1)
    acc = acc_ref[...]
    o_ref[...] = (acc[:MLA_V] / acc[MLA_V:MLA_V + 1]).T.astype(o_ref.dtype)


def _flash(qt, k, vt, heads, tq):
    t = k.shape[0]
    n_kt, tk = vt.shape[1], vt.shape[3]
    assert tq == 2 * tk and tk % CHUNK == 0 and t % tq == 0
    return pl.pallas_call(
        functools.partial(_flash_kernel, tq=tq, tk=tk),
        grid=(heads, t // tq),
        in_specs=[pl.BlockSpec((MLA_HEAD_PAD, tq), lambda hd, i: (hd, i)),
                  pl.BlockSpec((t, MLA_HEAD_PAD), lambda hd, i: (0, hd)),
                  pl.BlockSpec((None, n_kt, V_ROWS, tk), lambda hd, i: (hd, 0, 0, 0))],
        out_specs=pl.BlockSpec((tq, MLA_V), lambda hd, i: (i, hd)),
        out_shape=jax.ShapeDtypeStruct((t, heads * MLA_V), BF16),
        scratch_shapes=[pltpu.VMEM((1, tq), F32), pltpu.VMEM((V_ROWS, tq), F32),
                        pltpu.VMEM((2, tk, tq), F32), pltpu.VMEM((2, 1, tq), F32),
                        pltpu.VMEM((2, tk, tq), BF16), pltpu.VMEM((2, 1, tq), F32)],
        compiler_params=_params("parallel", "arbitrary"),
        name="flash",
    )(qt, k, vt)


def _sattn_kernel(q_ref, cc_ref, ck_ref, nc_ref, nk_ref, wuk_ref, wuv_ref, o_ref,
                  qabs_ref, qr_ref, m_ref, l_ref, acc_ref, *, heads, nq, tk, n_tiles, n_new):
    kk = pl.program_id(1)
    rank = cc_ref.shape[-1]

    @pl.when(kk == 0)
    def _():
        for hd in range(heads):
            c0 = hd * MLA_HEAD_PAD
            qn = q_ref[:, c0:c0 + MLA_NOPE]
            qabs_ref[hd * nq:(hd + 1) * nq, :] = _dot(
                qn, wuk_ref[hd * MLA_NOPE:(hd + 1) * MLA_NOPE, :]).astype(qabs_ref.dtype)
            qr_ref[hd * nq:(hd + 1) * nq, :] = q_ref[:, c0 + MLA_NOPE:c0 + MLA_NOPE + MLA_ROPE]
        m_ref[...] = jnp.full_like(m_ref, -jnp.inf)
        l_ref[...] = jnp.zeros_like(l_ref)
        acc_ref[...] = jnp.zeros_like(acc_ref)

    def process(c_f32, kr, n_valid):
        n_keys = c_f32.shape[0]
        c = c_f32.astype(BF16)
        kn_t = _dot_nt(wuk_ref[...], c)
        ss_t = jnp.sum((kn_t * kn_t).reshape(heads, MLA_NOPE, n_keys), axis=1)
        kr2_t = lax.dot_general(jnp.ones((8, MLA_ROPE), F32), kr * kr, NT_DIMS,
                                preferred_element_type=F32, precision=lax.Precision.HIGHEST)[0:1]
        inv_t = lax.rsqrt((ss_t + kr2_t) / MLA_QK_DIM + EPS)
        s = _dot_nt(qabs_ref[...], c) + _dot_nt(qr_ref[...], kr.astype(BF16))
        inv_rows = jnp.broadcast_to(inv_t[:, None, :], (heads, nq, n_keys)).reshape(heads * nq, n_keys)
        s = s * inv_rows
        if n_valid < n_keys:
            s = jnp.where(lax.broadcasted_iota(jnp.int32, s.shape, 1) < n_valid, s, -jnp.inf)
        m_prev = m_ref[...]
        m_new = jnp.maximum(m_prev, jnp.max(s, axis=-1, keepdims=True))
        alpha = jnp.exp(m_prev - m_new)
        p = jnp.exp(s - m_new)
        l_ref[...] = alpha * l_ref[...] + jnp.sum(p, axis=-1, keepdims=True)
        acc_ref[...] = alpha * acc_ref[...] + _dot(p.astype(BF16), c)
        m_ref[...] = m_new

    @pl.when(kk < n_tiles)
    def _():
        process(cc_ref[...], ck_ref[...], tk)

    @pl.when(kk == n_tiles)
    def _():
        pad = LANE - n_new
        c_new = jnp.concatenate([nc_ref[...], jnp.zeros((pad, rank), F32)], axis=0)
        k_new = jnp.concatenate([nk_ref[...], jnp.zeros((pad, MLA_ROPE), F32)], axis=0)
        process(c_new, k_new, n_new)
        o_lat = (acc_ref[...] / l_ref[...]).astype(BF16)
        for hd in range(heads):
            o_ref[:, hd * MLA_V:(hd + 1) * MLA_V] = _dot(
                o_lat[hd * nq:(hd + 1) * nq, :], wuv_ref[:, hd * MLA_V:(hd + 1) * MLA_V]).astype(o_ref.dtype)


def _sattn(q, cache_c, cache_k, new_c, new_k, wuk_t, wuv, heads, nq, tk):
    nb, past, rank = cache_c.shape
    n_tiles = past // tk
    cache_map = lambda b, kk: (b, jnp.minimum(kk, n_tiles - 1), 0)
    return pl.pallas_call(
        functools.partial(_sattn_kernel, heads=heads, nq=nq, tk=tk, n_tiles=n_tiles, n_new=nq),
        grid=(nb, n_tiles + 1),
        in_specs=[pl.BlockSpec((nq, heads * MLA_HEAD_PAD), lambda b, kk: (b, 0)),
                  pl.BlockSpec((None, tk, rank), cache_map),
                  pl.BlockSpec((None, tk, MLA_ROPE), cache_map),
                  pl.BlockSpec((nq, rank), lambda b, kk: (b, 0)),
                  pl.BlockSpec((nq, MLA_ROPE), lambda b, kk: (b, 0)),
                  pl.BlockSpec(wuk_t.shape, lambda b, kk: (0, 0)),
                  pl.BlockSpec(wuv.shape, lambda b, kk: (0, 0))],
        out_specs=pl.BlockSpec((nq, heads * MLA_V), lambda b, kk: (b, 0)),
        out_shape=jax.ShapeDtypeStruct((nb * nq, heads * MLA_V), BF16),
        scratch_shapes=[pltpu.VMEM((heads * nq, rank), BF16), pltpu.VMEM((heads * nq, MLA_ROPE), BF16),
                        pltpu.VMEM((heads * nq, 1), F32), pltpu.VMEM((heads * nq, 1), F32),
                        pltpu.VMEM((heads * nq, rank), F32)],
        compiler_params=_params("parallel", "arbitrary"),
        name="sattn",
    )(q, cache_c, cache_k, new_c, new_k, wuk_t, wuv)


def _gla_levels(c):
    leaf = min(c, GLA_LEAF)
    levels = [(leaf, leaf // 2 - 1)]
    g = 2 * leaf
    while g <= c:
        levels.append((g, g // 2 - 1))
        g *= 2
    return levels


def _gla_kernel(*refs, heads, dk, dv, c, has_s0):
    if has_s0:
        q_ref, k_ref, v_ref, la_ref, g_ref, s0_ref, o_ref, sout_ref, st_ref = refs
    else:
        q_ref, k_ref, v_ref, la_ref, g_ref, o_ref, sout_ref, st_ref = refs
    j = pl.program_id(1)
    nj = pl.num_programs(1)

    @pl.when(j == 0)
    def _():
        for hd in range(heads):
            if has_s0:
                st_ref[hd] = s0_ref[hd].T
            else:
                st_ref[hd] = jnp.zeros((dv, dk), F32)

    levels = _gla_levels(c)
    row = lax.broadcasted_iota(jnp.int32, (c, c), 0)
    col = lax.broadcasted_iota(jnp.int32, (c, c), 1)
    masks = []
    for lv, (g, _) in enumerate(levels):
        same = (row // g) == (col // g)
        if lv == 0:
            masks.append(jnp.logical_and(same, col <= row))
        else:
            half = g // 2
            masks.append(jnp.logical_and(same, jnp.logical_and((row // half) % 2 == 1, (col // half) % 2 == 0)))
    rid = lax.broadcasted_iota(jnp.int32, (c, dk), 0)
    gain = g_ref[...]

    for hd in range(heads):
        la = la_ref[:, hd * dk:(hd + 1) * dk]
        b = la
        sh = 1
        while sh < c:
            b = b + jnp.where(rid >= sh, pltpu.roll(b, sh, 0), 0.0)
            sh *= 2
        b_end = b[c - 1:c, :]
        q = q_ref[:, hd * dk:(hd + 1) * dk].astype(F32) * (dk ** -0.5)
        k = k_ref[:, hd * dk:(hd + 1) * dk].astype(F32)
        v = v_ref[:, hd * dv:(hd + 1) * dv]

        a = jnp.zeros((c, c), F32)
        for lv, (g, r) in enumerate(levels):
            ref_rows = jnp.broadcast_to(b.reshape(c // g, g, dk)[:, r:r + 1, :], (c // g, g, dk)).reshape(c, dk)
            d = b - ref_rows
            if lv == 0:
                dq, dkk = d, -d
            else:
                dq, dkk = jnp.minimum(d, 0.0), jnp.minimum(-d, 0.0)
            a_lv = _dot_nt((q * jnp.exp(dq)).astype(BF16), (k * jnp.exp(dkk)).astype(BF16))
            a = jnp.where(masks[lv], a_lv, a)

        st = st_ref[hd]
        o = _dot(a.astype(BF16), v) + _dot_nt((q * jnp.exp(b)).astype(BF16), st.astype(BF16))
        k_end = (k * jnp.exp(b_end - b)).astype(BF16)
        st_ref[hd] = st * jnp.exp(b_end) + lax.dot_general(v, k_end, TN_DIMS, preferred_element_type=F32)
        o_ref[:, hd * dv:(hd + 1) * dv] = (o * _rms_inv(o, dv) * gain).astype(o_ref.dtype)

    @pl.when(j == nj - 1)
    def _():
        for hd in range(heads):
            sout_ref[hd] = st_ref[hd].T


def _gla(zp, la, gain, s0, nb, c, heads, dk, dv):
    t = la.shape[0]
    nj = t // (nb * c)
    gk, gv = heads * dk, heads * dv
    assert gv % gk == 0
    row = lambda bb, j: (bb * nj + j, 0)
    in_specs = [pl.BlockSpec((c, gk), row),
                pl.BlockSpec((c, gk), lambda bb, j: (bb * nj + j, 1)),
                pl.BlockSpec((c, gv), lambda bb, j: (bb * nj + j, 2 * gk // gv)),
                pl.BlockSpec((c, gk), row),
                pl.BlockSpec((1, dv), lambda bb, j: (0, 0))]
    args = [zp, zp, zp, la, gain]
    if s0 is not None:
        in_specs.append(pl.BlockSpec((None, heads, dk, dv), lambda bb, j: (bb, 0, 0, 0)))
        args.append(s0)
    return pl.pallas_call(
        functools.partial(_gla_kernel, heads=heads, dk=dk, dv=dv, c=c, has_s0=s0 is not None),
        grid=(nb, nj),
        in_specs=in_specs,
        out_specs=[pl.BlockSpec((c, gv), row),
                   pl.BlockSpec((None, heads, dk, dv), lambda bb, j: (bb, 0, 0, 0))],
        out_shape=[jax.ShapeDtypeStruct((t, gv), BF16), jax.ShapeDtypeStruct((nb, heads, dk, dv), F32)],
        scratch_shapes=[pltpu.VMEM((heads, dv, dk), F32)],
        compiler_params=_params("parallel", "arbitrary"),
        name="gla",
    )(*args)


def _mix_kernel(om_ref, og_ref, gate_o_ref, gate_m_ref, gate_g_ref, x_ref, w_ref, g_ref, y_ref, h_ref):
    o_gla = og_ref[...].astype(F32) * jax.nn.silu(gate_o_ref[...].astype(F32))
    mixed = (jax.nn.sigmoid(gate_m_ref[...].astype(F32)) * om_ref[...].astype(F32)
             + jax.nn.sigmoid(gate_g_ref[...].astype(F32)) * o_gla)
    y = x_ref[...] + _dot(mixed.astype(BF16), w_ref[...])
    y_ref[...] = y
    h_ref[...] = (y * _rms_inv(y, y.shape[-1]) * g_ref[...]).astype(h_ref.dtype)


def _mix(o_mla, o_gla, zp, gate_block0, x, w_o, g, tm):
    t, d = x.shape
    row = lambda i: (i, 0)
    return pl.pallas_call(
        _mix_kernel,
        grid=(t // tm,),
        in_specs=[pl.BlockSpec((tm, d), row), pl.BlockSpec((tm, d), row),
                  pl.BlockSpec((tm, d), lambda i: (i, gate_block0)),
                  pl.BlockSpec((tm, d), lambda i: (i, gate_block0 + 1)),
                  pl.BlockSpec((tm, d), lambda i: (i, gate_block0 + 2)),
                  pl.BlockSpec((tm, d), row), pl.BlockSpec((d, d), lambda i: (0, 0)),
                  pl.BlockSpec((1, d), lambda i: (0, 0))],
        out_specs=[pl.BlockSpec((tm, d), row), pl.BlockSpec((tm, d), row)],
        out_shape=[jax.ShapeDtypeStruct((t, d), F32), jax.ShapeDtypeStruct((t, d), BF16)],
        compiler_params=_params("parallel"),
        name="mix",
    )(o_mla, o_gla, zp, zp, zp, x, w_o, g)


def _ffn_kernel(h_ref, y_ref, wu_ref, wd_ref, o_ref):
    f = pl.program_id(1)

    @pl.when(f == 0)
    def _():
        o_ref[...] = y_ref[...]

    u = jnp.maximum(_dot(h_ref[...], wu_ref[...]), 0.0)
    o_ref[...] += _dot((u * u).astype(BF16), wd_ref[...])


def _ffn(h, y, w_up, w_down, tm, tf):
    t, d = y.shape
    dff = w_up.shape[1]
    return pl.pallas_call(
        _ffn_kernel,
        grid=(t // tm, dff // tf),
        in_specs=[pl.BlockSpec((tm, d), lambda i, f: (i, 0)), pl.BlockSpec((tm, d), lambda i, f: (i, 0)),
                  pl.BlockSpec((d, tf), lambda i, f: (0, f)), pl.BlockSpec((tf, d), lambda i, f: (f, 0))],
        out_specs=pl.BlockSpec((tm, d), lambda i, f: (i, 0)),
        out_shape=jax.ShapeDtypeStruct((t, d), F32),
        compiler_params=_params("parallel", "arbitrary"),
        name="ffn",
    )(h, y, w_up, w_down)


def _rope_table(pos):
    half = MLA_ROPE // 2
    freqs = jnp.power(ROPE_THETA, -jnp.arange(half, dtype=F32) / half)
    ang = pos[:, None] * freqs[None, :]
    cos, sin = jnp.cos(ang), jnp.sin(ang)
    return jnp.concatenate([cos, cos, -sin, sin], axis=1)


def _swap_halves(w):
    half = w.shape[-1] // 2
    return jnp.concatenate([w[..., half:], w[..., :half]], axis=-1)


def _pick(t, pref):
    while t % pref:
        pref //= 2
    return pref


def kernel(x_prompt, x_sample, cache_mla_ckv, cache_mla_krope, state_gla, norm_mix_g, w_in, mla_q_norm_g,
           mla_w_uq, mla_kv_norm_g, mla_w_ukv, mla_q_gain_nope, mla_q_gain_rope, mla_k_gain_nope,
           mla_k_gain_rope, gla_w_a2, gla_b_a, gla_norm_g, w_o, norm_ffn_g, ffn_w_up, ffn_w_down):
    depth = w_in.shape[0]
    bp, seq, d = x_prompt.shape
    nb, dec_seq, _ = x_sample.shape
    past = cache_mla_ckv.shape[2]
    q_rank, heads = mla_w_uq.shape[1], mla_w_uq.shape[2]
    kv_rank = mla_w_ukv.shape[1]
    gla_heads, dk, dv = state_gla.shape[2], state_gla.shape[3], state_gla.shape[4]
    gate_rank = gla_w_a2.shape[1]
    gk, gv = gla_heads * dk, gla_heads * dv
    assert bp == 1 and heads * MLA_V == d and gv == d and seq % CHUNK == 0

    cs_p = _rope_table(jnp.arange(seq, dtype=F32))
    cs_s = jnp.tile(_rope_table(past + jnp.arange(dec_seq, dtype=F32)), (nb, 1))

    xp = x_prompt.reshape(seq, d)
    xs = x_sample.reshape(nb * dec_seq, d)
    outs = [[] for _ in range(6)]
    for l in range(depth):
        wi = w_in[l]
        pts = np.cumsum([q_rank, kv_rank, MLA_ROPE, gk, gk, gv, gate_rank, gv, d]).tolist()
        w_qkv_lat = wi[:, :pts[1]]
        w_kr = wi[:, pts[1]:pts[2]]
        w_alr = wi[:, pts[5]:pts[6]]
        w_lat = jnp.concatenate(
            [w_qkv_lat, w_kr, _swap_halves(w_kr), w_alr, jnp.zeros((d, LANE - gate_rank), F32)], axis=1).astype(BF16)
        w_proj = jnp.concatenate([wi[:, pts[2]:pts[5]], wi[:, pts[6]:]], axis=1).astype(BF16)
        gate_block0 = (2 * gk + gv) // d
        assert (2 * gk + gv) % d == 0

        wq = mla_w_uq[l]
        wq_r = wq[..., MLA_NOPE:]
        w_uq = jnp.concatenate([wq[..., :MLA_NOPE], wq_r, _swap_halves(wq_r)], axis=-1)
        w_uq = w_uq.reshape(q_rank, heads * MLA_HEAD_PAD).astype(BF16)
        w_uq_t = w_uq.T
        wuk = mla_w_ukv[l][..., :MLA_NOPE].reshape(kv_rank, heads * MLA_NOPE).astype(BF16)
        wuk_t = wuk.T
        wuv = mla_w_ukv[l][..., MLA_NOPE:].reshape(kv_rank, heads * MLA_V).astype(BF16)
        wuv_t = wuv.T
        gain = jnp.concatenate([mla_q_gain_nope[l] * mla_k_gain_nope[l],
                                jnp.tile(mla_q_gain_rope[l] * mla_k_gain_rope[l], 2),
                                jnp.zeros((MLA_HEAD_PAD - MLA_QK_DIM,), F32)])[None, :] * MLA_SCALE
        gain_col = gain.T * LOG2E
        wo_b = w_o[l].astype(BF16)
        wup_b = ffn_w_up[l].astype(BF16)
        wdn_b = ffn_w_down[l].astype(BF16)
        g_mix = norm_mix_g[l][None, :]
        g_q = mla_q_norm_g[l][None, :]
        g_kv = mla_kv_norm_g[l][None, :]
        g_gla = gla_norm_g[l][None, :]
        g_ffn = norm_ffn_g[l][None, :]
        ba = gla_b_a[l][None, :]

        def front(x, cs, tm):
            h = _prenorm(x, g_mix, _pick(x.shape[0], 256))
            qlat, ckv, kr, la = _mla_lat(h, w_lat, g_q, g_kv, cs, gla_w_a2[l], ba, tm, q_rank, kv_rank)
            zp = _proj(h, w_proj, _pick(x.shape[0], 1024), 1024)
            return qlat, ckv, kr, la, zp

        def back(x, o_mla, o_gla, zp, tm):
            y1, h2 = _mix(o_mla, o_gla, zp, gate_block0, x, wo_b, g_ffn, tm)
            return _ffn(h2, y1, wup_b, wdn_b, tm, 1024)

        tile = _pick(seq, 512)
        qlat, ckv, kr, la, zp = front(xp, cs_p, tile)
        qt = _mla_qt(qlat, w_uq_t, cs_p.T, gain_col, tile, heads)
        kcat, vt = _mla_kv(ckv, kr, wuk, wuv_t, tile, heads)
        o_mla = _flash(qt, kcat, vt, heads, 2 * tile)
        o_gla, st = _gla(zp, la, g_gla, None, 1, _pick(seq, 128), gla_heads, dk, dv)
        xp_next = back(xp, o_mla, o_gla, zp, tile)
        outs[0].append(ckv.reshape(bp, seq, kv_rank))
        outs[1].append(kr.reshape(bp, seq, MLA_ROPE))
        outs[2].append(st)
        xp = xp_next

        qlat, ckv, kr, la, zp = front(xs, cs_s, nb * dec_seq)
        qcat = _mla_q(qlat, w_uq, cs_s, gain, nb * dec_seq, heads)
        o_mla = _sattn(qcat, cache_mla_ckv[l], cache_mla_krope[l], ckv, kr, wuk_t, wuv, heads, dec_seq, 512)
        o_gla, st = _gla(zp, la, g_gla, state_gla[l], nb, dec_seq, gla_heads, dk, dv)
        xs_next = back(xs, o_mla, o_gla, zp, nb * dec_seq)
        outs[3].append(ckv.reshape(nb, dec_seq, kv_rank))
        outs[4].append(kr.reshape(nb, dec_seq, MLA_ROPE))
        outs[5].append(st)
        xs = xs_next

    return (xp.reshape(bp, seq, d), xs.reshape(nb, dec_seq, d),
            jnp.stack(outs[0]), jnp.stack(outs[1]), jnp.stack(outs[2]),
            jnp.stack(outs[3]), jnp.stack(outs[4]), jnp.stack(outs[5]))
```

```python
import functools

import jax
import jax.numpy as jnp
import numpy as np
from jax import lax
from jax.experimental import pallas as pl
from jax.experimental.pallas import tpu as pltpu

F32 = jnp.float32
BF16 = jnp.bfloat16

EPS = 1e-6
CHUNK = 64
MLA_NOPE = 128
MLA_ROPE = 64
MLA_V = 128
MLA_QK_DIM = MLA_NOPE + MLA_ROPE
MLA_SCALE = MLA_QK_DIM ** -0.5
MLA_HEAD_PAD = 256
V_ROWS = MLA_V + 16
LOG2E = 1.4426950408889634
QBLK = 256
ROPE_THETA = 10000.0
GLA_TAU = 16.0
GLA_LEAF = 32
LANE = 128
VMEM_LIMIT = 56 * 1024 * 1024

NT_DIMS = (((1,), (1,)), ((), ()))
TN_DIMS = (((0,), (0,)), ((), ()))


def _params(*sem, flags=None):
    return pltpu.CompilerParams(dimension_semantics=sem, vmem_limit_bytes=VMEM_LIMIT, flags=flags)


def _dot(a, b):
    return jnp.dot(a, b, preferred_element_type=F32)


def _dot_nt(a, b):
    return lax.dot_general(a, b, NT_DIMS, preferred_element_type=F32)


def _rms_inv(x, n):
    return lax.rsqrt(jnp.sum(x * x, axis=-1, keepdims=True) / n + EPS)


def _prenorm_kernel(x_ref, g_ref, h_ref):
    x = x_ref[...]
    h_ref[...] = (x * _rms_inv(x, x.shape[-1]) * g_ref[...]).astype(h_ref.dtype)


def _prenorm(x, g, tm):
    t, d = x.shape
    return pl.pallas_call(
        _prenorm_kernel,
        grid=(t // tm,),
        in_specs=[pl.BlockSpec((tm, d), lambda i: (i, 0)), pl.BlockSpec((1, d), lambda i: (0, 0))],
        out_specs=pl.BlockSpec((tm, d), lambda i: (i, 0)),
        out_shape=jax.ShapeDtypeStruct((t, d), BF16),
        compiler_params=_params("parallel"),
        name="prenorm",
    )(x, g)


def _mla_lat_kernel(h_ref, w_ref, qg_ref, kvg_ref, cs_ref, wa2_ref, ba_ref,
                    qlat_ref, ckv_ref, kr_ref, la_ref, *, q_rank, kv_rank, gate_rank):
    z = _dot(h_ref[...], w_ref[...])
    q_lat = z[:, :q_rank]
    qlat_ref[...] = (q_lat * _rms_inv(q_lat, q_rank) * qg_ref[...]).astype(qlat_ref.dtype)
    kv_lat = z[:, q_rank:q_rank + kv_rank]
    ckv_ref[...] = kv_lat * _rms_inv(kv_lat, kv_rank) * kvg_ref[...]
    o = q_rank + kv_rank
    rr = z[:, o:o + LANE] * cs_ref[...]
    kr_ref[...] = rr[:, :MLA_ROPE] + rr[:, MLA_ROPE:]
    a_lr = z[:, o + LANE:o + LANE + gate_rank]
    u = jnp.dot(a_lr, wa2_ref[...], preferred_element_type=F32, precision=lax.Precision.HIGHEST) + ba_ref[...]
    log_sig = jnp.minimum(u, 0.0) - jnp.log1p(jnp.exp(-jnp.abs(u)))
    la_ref[...] = log_sig / GLA_TAU


def _mla_lat(h, w_lat, qg, kvg, cs, wa2, ba, tm, q_rank, kv_rank):
    t, d = h.shape
    n = w_lat.shape[1]
    gate_rank, gk = wa2.shape
    row = lambda i: (i, 0)
    fix = lambda i: (0, 0)
    return pl.pallas_call(
        functools.partial(_mla_lat_kernel, q_rank=q_rank, kv_rank=kv_rank, gate_rank=gate_rank),
        grid=(t // tm,),
        in_specs=[pl.BlockSpec((tm, d), row), pl.BlockSpec((d, n), fix),
                  pl.BlockSpec((1, q_rank), fix), pl.BlockSpec((1, kv_rank), fix),
                  pl.BlockSpec((tm, LANE), row), pl.BlockSpec((gate_rank, gk), fix), pl.BlockSpec((1, gk), fix)],
        out_specs=[pl.BlockSpec((tm, q_rank), row), pl.BlockSpec((tm, kv_rank), row),
                   pl.BlockSpec((tm, MLA_ROPE), row), pl.BlockSpec((tm, gk), row)],
        out_shape=[jax.ShapeDtypeStruct((t, q_rank), BF16), jax.ShapeDtypeStruct((t, kv_rank), F32),
                   jax.ShapeDtypeStruct((t, MLA_ROPE), F32), jax.ShapeDtypeStruct((t, gk), F32)],
        compiler_params=_params("parallel"),
        name="mla_lat",
    )(h, w_lat, qg, kvg, cs, wa2, ba)


def _proj_kernel(h_ref, w_ref, o_ref):
    o_ref[...] = _dot(h_ref[...], w_ref[...]).astype(o_ref.dtype)


def _proj(h, w, tm, tn):
    t, d = h.shape
    n = w.shape[1]
    return pl.pallas_call(
        _proj_kernel,
        grid=(t // tm, n // tn),
        in_specs=[pl.BlockSpec((tm, d), lambda i, j: (i, 0)), pl.BlockSpec((d, tn), lambda i, j: (0, j))],
        out_specs=pl.BlockSpec((tm, tn), lambda i, j: (i, j)),
        out_shape=jax.ShapeDtypeStruct((t, n), BF16),
        compiler_params=_params("parallel", "parallel"),
        name="proj",
    )(h, w)


def _mla_q_kernel(ql_ref, w_ref, cs_ref, gain_ref, q_ref, *, heads):
    ql = ql_ref[...]
    cs = cs_ref[...]
    gain = gain_ref[...]
    lane = lax.broadcasted_iota(jnp.int32, (1, LANE), 1)
    for hd in range(heads):
        c0 = hd * MLA_HEAD_PAD
        z = _dot(ql, w_ref[:, c0:c0 + MLA_HEAD_PAD])
        nope = z[:, :MLA_NOPE]
        rr = z[:, MLA_NOPE:] * cs
        rot = rr + pltpu.roll(rr, MLA_ROPE, 1)
        ss = (jnp.sum(nope * nope, axis=-1, keepdims=True)
              + jnp.sum(jnp.where(lane < MLA_ROPE, rot * rot, 0.0), axis=-1, keepdims=True))
        inv = lax.rsqrt(ss / MLA_QK_DIM + EPS)
        q_ref[:, c0:c0 + MLA_NOPE] = (nope * inv * gain[:, :MLA_NOPE]).astype(q_ref.dtype)
        q_ref[:, c0 + MLA_NOPE:c0 + MLA_HEAD_PAD] = (rot * inv * gain[:, MLA_NOPE:]).astype(q_ref.dtype)


def _mla_q(qlat, w_uq, cs, gain, tm, heads):
    t, r = qlat.shape
    n = heads * MLA_HEAD_PAD
    return pl.pallas_call(
        functools.partial(_mla_q_kernel, heads=heads),
        grid=(t // tm,),
        in_specs=[pl.BlockSpec((tm, r), lambda i: (i, 0)), pl.BlockSpec((r, n), lambda i: (0, 0)),
                  pl.BlockSpec((tm, LANE), lambda i: (i, 0)), pl.BlockSpec((1, MLA_HEAD_PAD), lambda i: (0, 0))],
        out_specs=pl.BlockSpec((tm, n), lambda i: (i, 0)),
        out_shape=jax.ShapeDtypeStruct((t, n), BF16),
        compiler_params=_params("parallel"),
        name="mla_q",
    )(qlat, w_uq, cs, gain)


def _mla_qt_kernel(ql_ref, w_ref, cs_ref, gain_ref, q_ref, *, heads):
    ql = ql_ref[...]
    cs = cs_ref[...]
    gain = gain_ref[...]
    for hd in range(heads):
        r0 = hd * MLA_HEAD_PAD
        z = _dot_nt(w_ref[r0:r0 + MLA_HEAD_PAD, :], ql)
        nope = z[:MLA_NOPE]
        rr = z[MLA_NOPE:] * cs
        rot = rr[:MLA_ROPE] + rr[MLA_ROPE:]
        ss = jnp.sum(nope * nope, axis=0, keepdims=True) + jnp.sum(rot * rot, axis=0, keepdims=True)
        inv = lax.rsqrt(ss / MLA_QK_DIM + EPS)
        q_ref[r0:r0 + MLA_NOPE, :] = (nope * inv * gain[:MLA_NOPE]).astype(q_ref.dtype)
        q_ref[r0 + MLA_NOPE:r0 + MLA_QK_DIM, :] = (rot * inv * gain[MLA_NOPE:MLA_QK_DIM]).astype(q_ref.dtype)
        q_ref[r0 + MLA_QK_DIM:r0 + MLA_HEAD_PAD, :] = jnp.zeros((MLA_HEAD_PAD - MLA_QK_DIM, ql.shape[0]), q_ref.dtype)


def _mla_qt(qlat, w_uq_t, cs_t, gain_col, tm, heads):
    t, r = qlat.shape
    n = heads * MLA_HEAD_PAD
    return pl.pallas_call(
        functools.partial(_mla_qt_kernel, heads=heads),
        grid=(t // tm,),
        in_specs=[pl.BlockSpec((tm, r), lambda i: (i, 0)), pl.BlockSpec((n, r), lambda i: (0, 0)),
                  pl.BlockSpec((LANE, tm), lambda i: (0, i)), pl.BlockSpec((MLA_HEAD_PAD, 1), lambda i: (0, 0))],
        out_specs=pl.BlockSpec((n, tm), lambda i: (0, i)),
        out_shape=jax.ShapeDtypeStruct((n, t), BF16),
        compiler_params=_params("parallel"),
        name="mla_qt",
    )(qlat, w_uq_t, cs_t, gain_col)


def _mla_kv_kernel(ckv_ref, kr_ref, wk_ref, wvt_ref, k_ref, vt_ref, *, heads):
    c = ckv_ref[...].astype(BF16)
    kr = kr_ref[...]
    kr2 = jnp.sum(kr * kr, axis=-1, keepdims=True)
    kr_pad = jnp.concatenate([kr, jnp.zeros_like(kr)], axis=1)
    for pair in range(heads // 2):
        z = _dot(c, wk_ref[:, pair * 2 * MLA_NOPE:(pair + 1) * 2 * MLA_NOPE])
        for sub in range(2):
            kn = z[:, sub * MLA_NOPE:(sub + 1) * MLA_NOPE]
            inv = lax.rsqrt((jnp.sum(kn * kn, axis=-1, keepdims=True) + kr2) / MLA_QK_DIM + EPS)
            c0 = (2 * pair + sub) * MLA_HEAD_PAD
            k_ref[:, c0:c0 + MLA_NOPE] = (kn * inv).astype(k_ref.dtype)
            k_ref[:, c0 + MLA_NOPE:c0 + MLA_HEAD_PAD] = (kr_pad * inv).astype(k_ref.dtype)
    vt = _dot_nt(wvt_ref[...], c)
    for hd in range(heads):
        vt_ref[hd, :MLA_V, :] = vt[hd * MLA_V:(hd + 1) * MLA_V].astype(vt_ref.dtype)
        vt_ref[hd, MLA_V:, :] = jnp.ones((V_ROWS - MLA_V, vt.shape[1]), vt_ref.dtype)


def _mla_kv(ckv, kr, w_uk, w_uv_t, tm, heads):
    t, r = ckv.shape
    return pl.pallas_call(
        functools.partial(_mla_kv_kernel, heads=heads),
        grid=(t // tm,),
        in_specs=[pl.BlockSpec((tm, r), lambda i: (i, 0)), pl.BlockSpec((tm, MLA_ROPE), lambda i: (i, 0)),
                  pl.BlockSpec((r, heads * MLA_NOPE), lambda i: (0, 0)),
                  pl.BlockSpec((heads * MLA_V, r), lambda i: (0, 0))],
        out_specs=[pl.BlockSpec((tm, heads * MLA_HEAD_PAD), lambda i: (i, 0)),
                   pl.BlockSpec((heads, None, V_ROWS, tm), lambda i: (0, i, 0, 0))],
        out_shape=[jax.ShapeDtypeStruct((t, heads * MLA_HEAD_PAD), BF16),
                   jax.ShapeDtypeStruct((heads, t // tm, V_ROWS, tm), BF16)],
        compiler_params=_params("parallel"),
        name="mla_kv",
    )(ckv, kr, w_uk, w_uv_t)


def _flash_kernel(qt_ref, k_ref, vt_ref, o_ref, m_ref, acc_ref, s_ref, mx_ref, p_ref, al_ref, *, tq, tk):
    i = pl.program_id(1)
    m_ref[...] = jnp.full_like(m_ref, -jnp.inf)
    acc_ref[...] = jnp.zeros_like(acc_ref)

    n_blk = tq // QBLK
    cols_of = [pl.ds(c * QBLK, QBLK) for c in range(n_blk)]

    def diag_mode(u, c):
        k_lo, k_hi = (u * tk) // CHUNK, ((u + 1) * tk - 1) // CHUNK
        q_lo, q_hi = (c * QBLK) // CHUNK, ((c + 1) * QBLK - 1) // CHUNK
        return "all" if k_hi <= q_lo else "none" if k_lo > q_hi else "some"

    def score(t, slot, c, mode):
        if mode == "none":
            return
        cols = cols_of[c]
        k = k_ref[pl.ds(pl.multiple_of(t * tk, tk), tk), :]
        s = _dot(k, qt_ref[:, cols])
        if mode == "some":
            k_chunk = (t * tk + lax.broadcasted_iota(jnp.int32, (tk, QBLK), 0)) // CHUNK
            q_chunk = (i * tq + c * QBLK + lax.broadcasted_iota(jnp.int32, (tk, QBLK), 1)) // CHUNK
            s = jnp.where(k_chunk <= q_chunk, s, -jnp.inf)
        s_ref[slot, :, cols] = s
        mx_ref[slot, :, cols] = jnp.max(s, axis=0, keepdims=True)

    def soften(slot, c, mode="all"):
        if mode == "none":
            return
        cols = cols_of[c]
        m_prev = m_ref[:, cols]
        m_new = jnp.maximum(m_prev, mx_ref[slot, :, cols])
        p_ref[slot, :, cols] = jnp.exp2(s_ref[slot, :, cols] - m_new).astype(BF16)
        al_ref[slot, :, cols] = jnp.exp2(m_prev - m_new)
        m_ref[:, cols] = m_new

    def gather(t, slot, c, mode="all"):
        if mode == "none":
            return
        cols = cols_of[c]
        acc_ref[:, cols] = al_ref[slot, :, cols] * acc_ref[:, cols] + _dot(vt_ref[t], p_ref[slot, :, cols])

    def pair(g, diag):
        for c in range(n_blk):
            score(g, 0, c, diag_mode(0, c) if diag else "all")
            soften(1, c)
            gather(g - 2, 0, c)
        for c in range(n_blk):
            score(g + 1, 1, c, diag_mode(1, c) if diag else "all")
            soften(0, c, diag_mode(0, c) if diag else "all")
            gather(g - 1, 1, c)

    def head(diag):
        for c in range(n_blk):
            score(0, 0, c, diag_mode(0, c) if diag else "all")
        for c in range(n_blk):
            score(1, 1, c, diag_mode(1, c) if diag else "all")
            soften(0, c, diag_mode(0, c) if diag else "all")

    @pl.when(i == 0)
    def _():
        head(True)

    @pl.when(i > 0)
    def _():
        head(False)

    def body(j, carry):
        pair(2 * j, False)
        return carry

    lax.fori_loop(1, i, body, 0)

    @pl.when(i > 0)
    def _():
        pair(2 * i, True)

    for c in range(n_blk):
        soften(1, c, diag_mode(1, c))
        gather(2 * i, 0, c, diag_mode(0, c))
    for c in range(n_blk):
        gather(2 * i + 1, 1, c, diag_mode(1, c))
    acc = acc_ref[...]
    o_ref[...] = (acc[:MLA_V] / acc[MLA_V:MLA_V + 1]).T.astype(o_ref.dtype)


def _flash(qt, k, vt, heads, tq):
    t = k.shape[0]
    n_kt, tk = vt.shape[1], vt.shape[3]
    assert tq == 2 * tk and tk % CHUNK == 0 and t % tq == 0
    return pl.pallas_call(
        functools.partial(_flash_kernel, tq=tq, tk=tk),
        grid=(heads, t // tq),
        in_specs=[pl.BlockSpec((MLA_HEAD_PAD, tq), lambda hd, i: (hd, i)),
                  pl.BlockSpec((t, MLA_HEAD_PAD), lambda hd, i: (0, hd)),
                  pl.BlockSpec((None, n_kt, V_ROWS, tk), lambda hd, i: (hd, 0, 0, 0))],
        out_specs=pl.BlockSpec((tq, MLA_V), lambda hd, i: (i, hd)),
        out_shape=jax.ShapeDtypeStruct((t, heads * MLA_V), BF16),
        scratch_shapes=[pltpu.VMEM((1, tq), F32), pltpu.VMEM((V_ROWS, tq), F32),
                        pltpu.VMEM((2, tk, tq), F32), pltpu.VMEM((2, 1, tq), F32),
                        pltpu.VMEM((2, tk, tq), BF16), pltpu.VMEM((2, 1, tq), F32)],
        compiler_params=_params("parallel", "arbitrary"),
        name="flash",
    )(qt, k, vt)


def _sattn_kernel(q_ref, cc_ref, ck_ref, nc_ref, nk_ref, wuk_ref, wuv_ref, o_ref,
                  qabs_ref, qr_ref, m_ref, l_ref, acc_ref, *, heads, nq, tk, n_tiles, n_new):
    kk = pl.program_id(1)
    rank = cc_ref.shape[-1]

    @pl.when(kk == 0)
    def _():
        for hd in range(heads):
            c0 = hd * MLA_HEAD_PAD
            qn = q_ref[:, c0:c0 + MLA_NOPE]
            qabs_ref[hd * nq:(hd + 1) * nq, :] = _dot(
                qn, wuk_ref[hd * MLA_NOPE:(hd + 1) * MLA_NOPE, :]).astype(qabs_ref.dtype)
            qr_ref[hd * nq:(hd + 1) * nq, :] = q_ref[:, c0 + MLA_NOPE:c0 + MLA_NOPE + MLA_ROPE]
        m_ref[...] = jnp.full_like(m_ref, -jnp.inf)
        l_ref[...] = jnp.zeros_like(l_ref)
        acc_ref[...] = jnp.zeros_like(acc_ref)

    def process(c_f32, kr, n_valid):
        n_keys = c_f32.shape[0]
        c = c_f32.astype(BF16)
        kn_t = _dot_nt(wuk_ref[...], c)
        ss_t = jnp.sum((kn_t * kn_t).reshape(heads, MLA_NOPE, n_keys), axis=1)
        kr2_t = lax.dot_general(jnp.ones((8, MLA_ROPE), F32), kr * kr, NT_DIMS,
                                preferred_element_type=F32, precision=lax.Precision.HIGHEST)[0:1]
        inv_t = lax.rsqrt((ss_t + kr2_t) / MLA_QK_DIM + EPS)
        s = _dot_nt(qabs_ref[...], c) + _dot_nt(qr_ref[...], kr.astype(BF16))
        inv_rows = jnp.broadcast_to(inv_t[:, None, :], (heads, nq, n_keys)).reshape(heads * nq, n_keys)
        s = s * inv_rows
        if n_valid < n_keys:
            s = jnp.where(lax.broadcasted_iota(jnp.int32, s.shape, 1) < n_valid, s, -jnp.inf)
        m_prev = m_ref[...]
        m_new = jnp.maximum(m_prev, jnp.max(s, axis=-1, keepdims=True))
        alpha = jnp.exp(m_prev - m_new)
        p = jnp.exp(s - m_new)
        l_ref[...] = alpha * l_ref[...] + jnp.sum(p, axis=-1, keepdims=True)
        acc_ref[...] = alpha * acc_ref[...] + _dot(p.astype(BF16), c)
        m_ref[...] = m_new

    @pl.when(kk < n_tiles)
    def _():
        process(cc_ref[...], ck_ref[...], tk)

    @pl.when(kk == n_tiles)
    def _():
        pad = LANE - n_new
        c_new = jnp.concatenate([nc_ref[...], jnp.zeros((pad, rank), F32)], axis=0)
        k_new = jnp.concatenate([nk_ref[...], jnp.zeros((pad, MLA_ROPE), F32)], axis=0)
        process(c_new, k_new, n_new)
        o_lat = (acc_ref[...] / l_ref[...]).astype(BF16)
        for hd in range(heads):
            o_ref[:, hd * MLA_V:(hd + 1) * MLA_V] = _dot(
                o_lat[hd * nq:(hd + 1) * nq, :], wuv_ref[:, hd * MLA_V:(hd + 1) * MLA_V]).astype(o_ref.dtype)


def _sattn(q, cache_c, cache_k, new_c, new_k, wuk_t, wuv, heads, nq, tk):
    nb, past, rank = cache_c.shape
    n_tiles = past // tk
    cache_map = lambda b, kk: (b, jnp.minimum(kk, n_tiles - 1), 0)
    return pl.pallas_call(
        functools.partial(_sattn_kernel, heads=heads, nq=nq, tk=tk, n_tiles=n_tiles, n_new=nq),
        grid=(nb, n_tiles + 1),
        in_specs=[pl.BlockSpec((nq, heads * MLA_HEAD_PAD), lambda b, kk: (b, 0)),
                  pl.BlockSpec((None, tk, rank), cache_map),
                  pl.BlockSpec((None, tk, MLA_ROPE), cache_map),
                  pl.BlockSpec((nq, rank), lambda b, kk: (b, 0)),
                  pl.BlockSpec((nq, MLA_ROPE), lambda b, kk: (b, 0)),
                  pl.BlockSpec(wuk_t.shape, lambda b, kk: (0, 0)),
                  pl.BlockSpec(wuv.shape, lambda b, kk: (0, 0))],
        out_specs=pl.BlockSpec((nq, heads * MLA_V), lambda b, kk: (b, 0)),
        out_shape=jax.ShapeDtypeStruct((nb * nq, heads * MLA_V), BF16),
        scratch_shapes=[pltpu.VMEM((heads * nq, rank), BF16), pltpu.VMEM((heads * nq, MLA_ROPE), BF16),
                        pltpu.VMEM((heads * nq, 1), F32), pltpu.VMEM((heads * nq, 1), F32),
                        pltpu.VMEM((heads * nq, rank), F32)],
        compiler_params=_params("parallel", "arbitrary"),
        name="sattn",
    )(q, cache_c, cache_k, new_c, new_k, wuk_t, wuv)


def _gla_levels(c):
    leaf = min(c, GLA_LEAF)
    levels = [(leaf, leaf // 2 - 1)]
    g = 2 * leaf
    while g <= c:
        levels.append((g, g // 2 - 1))
        g *= 2
    return levels


def _gla_kernel(*refs, heads, dk, dv, c, has_s0):
    if has_s0:
        q_ref, k_ref, v_ref, la_ref, g_ref, s0_ref, o_ref, sout_ref, st_ref = refs
    else:
        q_ref, k_ref, v_ref, la_ref, g_ref, o_ref, sout_ref, st_ref = refs
    j = pl.program_id(1)
    nj = pl.num_programs(1)

    @pl.when(j == 0)
    def _():
        for hd in range(heads):
            if has_s0:
                st_ref[hd] = s0_ref[hd].T
            else:
                st_ref[hd] = jnp.zeros((dv, dk), F32)

    levels = _gla_levels(c)
    row = lax.broadcasted_iota(jnp.int32, (c, c), 0)
    col = lax.broadcasted_iota(jnp.int32, (c, c), 1)
    masks = []
    for lv, (g, _) in enumerate(levels):
        same = (row // g) == (col // g)
        if lv == 0:
            masks.append(jnp.logical_and(same, col <= row))
        else:
            half = g // 2
            masks.append(jnp.logical_and(same, jnp.logical_and((row // half) % 2 == 1, (col // half) % 2 == 0)))
    rid = lax.broadcasted_iota(jnp.int32, (c, dk), 0)
    gain = g_ref[...]

    for hd in range(heads):
        la = la_ref[:, hd * dk:(hd + 1) * dk]
        b = la
        sh = 1
        while sh < c:
            b = b + jnp.where(rid >= sh, pltpu.roll(b, sh, 0), 0.0)
            sh *= 2
        b_end = b[c - 1:c, :]
        q = q_ref[:, hd * dk:(hd + 1) * dk].astype(F32) * (dk ** -0.5)
        k = k_ref[:, hd * dk:(hd + 1) * dk].astype(F32)
        v = v_ref[:, hd * dv:(hd + 1) * dv]

        a = jnp.zeros((c, c), F32)
        for lv, (g, r) in enumerate(levels):
            ref_rows = jnp.broadcast_to(b.reshape(c // g, g, dk)[:, r:r + 1, :], (c // g, g, dk)).reshape(c, dk)
            d = b - ref_rows
            if lv == 0:
                dq, dkk = d, -d
            else:
                dq, dkk = jnp.minimum(d, 0.0), jnp.minimum(-d, 0.0)
            a_lv = _dot_nt((q * jnp.exp(dq)).astype(BF16), (k * jnp.exp(dkk)).astype(BF16))
            a = jnp.where(masks[lv], a_lv, a)

        st = st_ref[hd]
        o = _dot(a.astype(BF16), v) + _dot_nt((q * jnp.exp(b)).astype(BF16), st.astype(BF16))
        k_end = (k * jnp.exp(b_end - b)).astype(BF16)
        st_ref[hd] = st * jnp.exp(b_end) + lax.dot_general(v, k_end, TN_DIMS, preferred_element_type=F32)
        o_ref[:, hd * dv:(hd + 1) * dv] = (o * _rms_inv(o, dv) * gain).astype(o_ref.dtype)

    @pl.when(j == nj - 1)
    def _():
        for hd in range(heads):
            sout_ref[hd] = st_ref[hd].T


def _gla(zp, la, gain, s0, nb, c, heads, dk, dv):
    t = la.shape[0]
    nj = t // (nb * c)
    gk, gv = heads * dk, heads * dv
    assert gv % gk == 0
    row = lambda bb, j: (bb * nj + j, 0)
    in_specs = [pl.BlockSpec((c, gk), row),
                pl.BlockSpec((c, gk), lambda bb, j: (bb * nj + j, 1)),
                pl.BlockSpec((c, gv), lambda bb, j: (bb * nj + j, 2 * gk // gv)),
                pl.BlockSpec((c, gk), row),
                pl.BlockSpec((1, dv), lambda bb, j: (0, 0))]
    args = [zp, zp, zp, la, gain]
    if s0 is not None:
        in_specs.append(pl.BlockSpec((None, heads, dk, dv), lambda bb, j: (bb, 0, 0, 0)))
        args.append(s0)
    return pl.pallas_call(
        functools.partial(_gla_kernel, heads=heads, dk=dk, dv=dv, c=c, has_s0=s0 is not None),
        grid=(nb, nj),
        in_specs=in_specs,
        out_specs=[pl.BlockSpec((c, gv), row),
                   pl.BlockSpec((None, heads, dk, dv), lambda bb, j: (bb, 0, 0, 0))],
        out_shape=[jax.ShapeDtypeStruct((t, gv), BF16), jax.ShapeDtypeStruct((nb, heads, dk, dv), F32)],
        scratch_shapes=[pltpu.VMEM((heads, dv, dk), F32)],
        compiler_params=_params("parallel", "arbitrary"),
        name="gla",
    )(*args)


def _mix_kernel(om_ref, og_ref, gate_o_ref, gate_m_ref, gate_g_ref, x_ref, w_ref, g_ref, y_ref, h_ref):
    o_gla = og_ref[...].astype(F32) * jax.nn.silu(gate_o_ref[...].astype(F32))
    mixed = (jax.nn.sigmoid(gate_m_ref[...].astype(F32)) * om_ref[...].astype(F32)
             + jax.nn.sigmoid(gate_g_ref[...].astype(F32)) * o_gla)
    y = x_ref[...] + _dot(mixed.astype(BF16), w_ref[...])
    y_ref[...] = y
    h_ref[...] = (y * _rms_inv(y, y.shape[-1]) * g_ref[...]).astype(h_ref.dtype)


def _mix(o_mla, o_gla, zp, gate_block0, x, w_o, g, tm):
    t, d = x.shape
    row = lambda i: (i, 0)
    return pl.pallas_call(
        _mix_kernel,
        grid=(t // tm,),
        in_specs=[pl.BlockSpec((tm, d), row), pl.BlockSpec((tm, d), row),
                  pl.BlockSpec((tm, d), lambda i: (i, gate_block0)),
                  pl.BlockSpec((tm, d), lambda i: (i, gate_block0 + 1)),
                  pl.BlockSpec((tm, d), lambda i: (i, gate_block0 + 2)),
                  pl.BlockSpec((tm, d), row), pl.BlockSpec((d, d), lambda i: (0, 0)),
                  pl.BlockSpec((1, d), lambda i: (0, 0))],
        out_specs=[pl.BlockSpec((tm, d), row), pl.BlockSpec((tm, d), row)],
        out_shape=[jax.ShapeDtypeStruct((t, d), F32), jax.ShapeDtypeStruct((t, d), BF16)],
        compiler_params=_params("parallel"),
        name="mix",
    )(o_mla, o_gla, zp, zp, zp, x, w_o, g)


def _ffn_kernel(h_ref, y_ref, wu_ref, wd_ref, o_ref):
    f = pl.program_id(1)

    @pl.when(f == 0)
    def _():
        o_ref[...] = y_ref[...]

    u = jnp.maximum(_dot(h_ref[...], wu_ref[...]), 0.0)
    o_ref[...] += _dot((u * u).astype(BF16), wd_ref[...])


def _ffn(h, y, w_up, w_down, tm, tf):
    t, d = y.shape
    dff = w_up.shape[1]
    return pl.pallas_call(
        _ffn_kernel,
        grid=(t // tm, dff // tf),
        in_specs=[pl.BlockSpec((tm, d), lambda i, f: (i, 0)), pl.BlockSpec((tm, d), lambda i, f: (i, 0)),
                  pl.BlockSpec((d, tf), lambda i, f: (0, f)), pl.BlockSpec((tf, d), lambda i, f: (f, 0))],
        out_specs=pl.BlockSpec((tm, d), lambda i, f: (i, 0)),
        out_shape=jax.ShapeDtypeStruct((t, d), F32),
        compiler_params=_params("parallel", "arbitrary"),
        name="ffn",
    )(h, y, w_up, w_down)


def _rope_table(pos):
    half = MLA_ROPE // 2
    freqs = jnp.power(ROPE_THETA, -jnp.arange(half, dtype=F32) / half)
    ang = pos[:, None] * freqs[None, :]
    cos, sin = jnp.cos(ang), jnp.sin(ang)
    return jnp.concatenate([cos, cos, -sin, sin], axis=1)


def _swap_halves(w):
    half = w.shape[-1] // 2
    return jnp.concatenate([w[..., half:], w[..., :half]], axis=-1)


def _pick(t, pref):
    while t % pref:
        pref //= 2
    return pref


def kernel(x_prompt, x_sample, cache_mla_ckv, cache_mla_krope, state_gla, norm_mix_g, w_in, mla_q_norm_g,
           mla_w_uq, mla_kv_norm_g, mla_w_ukv, mla_q_gain_nope, mla_q_gain_rope, mla_k_gain_nope,
           mla_k_gain_rope, gla_w_a2, gla_b_a, gla_norm_g, w_o, norm_ffn_g, ffn_w_up, ffn_w_down):
    depth = w_in.shape[0]
    bp, seq, d = x_prompt.shape
    nb, dec_seq, _ = x_sample.shape
    past = cache_mla_ckv.shape[2]
    q_rank, heads = mla_w_uq.shape[1], mla_w_uq.shape[2]
    kv_rank = mla_w_ukv.shape[1]
    gla_heads, dk, dv = state_gla.shape[2], state_gla.shape[3], state_gla.shape[4]
    gate_rank = gla_w_a2.shape[1]
    gk, gv = gla_heads * dk, gla_heads * dv
    assert bp == 1 and heads * MLA_V == d and gv == d and seq % CHUNK == 0

    cs_p = _rope_table(jnp.arange(seq, dtype=F32))
    cs_s = jnp.tile(_rope_table(past + jnp.arange(dec_seq, dtype=F32)), (nb, 1))

    xp = x_prompt.reshape(seq, d)
    xs = x_sample.reshape(nb * dec_seq, d)
    outs = [[] for _ in range(6)]
    for l in range(depth):
        wi = w_in[l]
        pts = np.cumsum([q_rank, kv_rank, MLA_ROPE, gk, gk, gv, gate_rank, gv, d]).tolist()
        w_qkv_lat = wi[:, :pts[1]]
        w_kr = wi[:, pts[1]:pts[2]]
        w_alr = wi[:, pts[5]:pts[6]]
        w_lat = jnp.concatenate(
            [w_qkv_lat, w_kr, _swap_halves(w_kr), w_alr, jnp.zeros((d, LANE - gate_rank), F32)], axis=1).astype(BF16)
        w_proj = jnp.concatenate([wi[:, pts[2]:pts[5]], wi[:, pts[6]:]], axis=1).astype(BF16)
        gate_block0 = (2 * gk + gv) // d
        assert (2 * gk + gv) % d == 0

        wq = mla_w_uq[l]
        wq_r = wq[..., MLA_NOPE:]
        w_uq = jnp.concatenate([wq[..., :MLA_NOPE], wq_r, _swap_halves(wq_r)], axis=-1)
        w_uq = w_uq.reshape(q_rank, heads * MLA_HEAD_PAD).astype(BF16)
        w_uq_t = w_uq.T
        wuk = mla_w_ukv[l][..., :MLA_NOPE].reshape(kv_rank, heads * MLA_NOPE).astype(BF16)
        wuk_t = wuk.T
        wuv = mla_w_ukv[l][..., MLA_NOPE:].reshape(kv_rank, heads * MLA_V).astype(BF16)
        wuv_t = wuv.T
        gain = jnp.concatenate([mla_q_gain_nope[l] * mla_k_gain_nope[l],
                                jnp.tile(mla_q_gain_rope[l] * mla_k_gain_rope[l], 2),
                                jnp.zeros((MLA_HEAD_PAD - MLA_QK_DIM,), F32)])[None, :] * MLA_SCALE
        gain_col = gain.T * LOG2E
        wo_b = w_o[l].astype(BF16)
        wup_b = ffn_w_up[l].astype(BF16)
        wdn_b = ffn_w_down[l].astype(BF16)
        g_mix = norm_mix_g[l][None, :]
        g_q = mla_q_norm_g[l][None, :]
        g_kv = mla_kv_norm_g[l][None, :]
        g_gla = gla_norm_g[l][None, :]
        g_ffn = norm_ffn_g[l][None, :]
        ba = gla_b_a[l][None, :]

        def front(x, cs, tm):
            h = _prenorm(x, g_mix, _pick(x.shape[0], 256))
            qlat, ckv, kr, la = _mla_lat(h, w_lat, g_q, g_kv, cs, gla_w_a2[l], ba, tm, q_rank, kv_rank)
            zp = _proj(h, w_proj, _pick(x.shape[0], 1024), 1024)
            return qlat, ckv, kr, la, zp

        def back(x, o_mla, o_gla, zp, tm):
            y1, h2 = _mix(o_mla, o_gla, zp, gate_block0, x, wo_b, g_ffn, tm)
            return _ffn(h2, y1, wup_b, wdn_b, tm, 1024)

        tile = _pick(seq, 512)
        qlat, ckv, kr, la, zp = front(xp, cs_p, tile)
        qt = _mla_qt(qlat, w_uq_t, cs_p.T, gain_col, tile, heads)
        kcat, vt = _mla_kv(ckv, kr, wuk, wuv_t, tile, heads)
        o_mla = _flash(qt, kcat, vt, heads, 2 * tile)
        o_gla, st = _gla(zp, la, g_gla, None, 1, _pick(seq, 128), gla_heads, dk, dv)
        xp_next = back(xp, o_mla, o_gla, zp, tile)
        outs[0].append(ckv.reshape(bp, seq, kv_rank))
        outs[1].append(kr.reshape(bp, seq, MLA_ROPE))
        outs[2].append(st)
        xp = xp_next

        qlat, ckv, kr, la, zp = front(xs, cs_s, nb * dec_seq)
        qcat = _mla_q(qlat, w_uq, cs_s, gain, nb * dec_seq, heads)
        o_mla = _sattn(qcat, cache_mla_ckv[l], cache_mla_krope[l], ckv, kr, wuk_t, wuv, heads, dec_seq, 512)
        o_gla, st = _gla(zp, la, g_gla, state_gla[l], nb, dec_seq, gla_heads, dk, dv)
        xs_next = back(xs, o_mla, o_gla, zp, nb * dec_seq)
        outs[3].append(ckv.reshape(nb, dec_seq, kv_rank))
        outs[4].append(kr.reshape(nb, dec_seq, MLA_ROPE))
        outs[5].append(st)
        xs = xs_next

    return (xp.reshape(bp, seq, d), xs.reshape(nb, dec_seq, d),
            jnp.stack(outs[0]), jnp.stack(outs[1]), jnp.stack(outs[2]),
            jnp.stack(outs[3]), jnp.stack(outs[4]), jnp.stack(outs[5]))
```

```python
import functools

import jax
import jax.numpy as jnp
import numpy as np
from jax import lax
from jax.experimental import pallas as pl
from jax.experimental.pallas import tpu as pltpu

F32 = jnp.float32
BF16 = jnp.bfloat16

EPS = 1e-6
CHUNK = 64
MLA_NOPE = 128
MLA_ROPE = 64
MLA_V = 128
MLA_QK_DIM = MLA_NOPE + MLA_ROPE
MLA_SCALE = MLA_QK_DIM ** -0.5
MLA_HEAD_PAD = 256
V_ROWS = MLA_V + 16
LOG2E = 1.4426950408889634
QBLK = 256
ROPE_THETA = 10000.0
GLA_TAU = 16.0
GLA_LEAF = 32
LANE = 128
VMEM_LIMIT = 56 * 1024 * 1024

NT_DIMS = (((1,), (1,)), ((), ()))
TN_DIMS = (((0,), (0,)), ((), ()))


def _params(*sem, flags=None):
    return pltpu.CompilerParams(dimension_semantics=sem, vmem_limit_bytes=VMEM_LIMIT, flags=flags)


def _dot(a, b):
    return jnp.dot(a, b, preferred_element_type=F32)


def _dot_nt(a, b):
    return lax.dot_general(a, b, NT_DIMS, preferred_element_type=F32)


def _rms_inv(x, n):
    return lax.rsqrt(jnp.sum(x * x, axis=-1, keepdims=True) / n + EPS)


def _prenorm_kernel(x_ref, g_ref, h_ref):
    x = x_ref[...]
    h_ref[...] = (x * _rms_inv(x, x.shape[-1]) * g_ref[...]).astype(h_ref.dtype)


def _prenorm(x, g, tm):
    t, d = x.shape
    return pl.pallas_call(
        _prenorm_kernel,
        grid=(t // tm,),
        in_specs=[pl.BlockSpec((tm, d), lambda i: (i, 0)), pl.BlockSpec((1, d), lambda i: (0, 0))],
        out_specs=pl.BlockSpec((tm, d), lambda i: (i, 0)),
        out_shape=jax.ShapeDtypeStruct((t, d), BF16),
        compiler_params=_params("parallel"),
        name="prenorm",
    )(x, g)


def _mla_lat_kernel(h_ref, w_ref, qg_ref, kvg_ref, cs_ref, wa2_ref, ba_ref,
                    qlat_ref, ckv_ref, kr_ref, la_ref, *, q_rank, kv_rank, gate_rank):
    z = _dot(h_ref[...], w_ref[...])
    q_lat = z[:, :q_rank]
    qlat_ref[...] = (q_lat * _rms_inv(q_lat, q_rank) * qg_ref[...]).astype(qlat_ref.dtype)
    kv_lat = z[:, q_rank:q_rank + kv_rank]
    ckv_ref[...] = kv_lat * _rms_inv(kv_lat, kv_rank) * kvg_ref[...]
    o = q_rank + kv_rank
    rr = z[:, o:o + LANE] * cs_ref[...]
    kr_ref[...] = rr[:, :MLA_ROPE] + rr[:, MLA_ROPE:]
    a3 = z[:, o + LANE:o + 2 * LANE]
    a_hi = a3.astype(BF16)
    a_lo = (a3 - a_hi.astype(F32)).astype(BF16)
    lane = lax.broadcasted_iota(jnp.int32, a3.shape, 1)
    u = _dot(jnp.where(lane < 2 * gate_rank, a_hi, a_lo), wa2_ref[...]) + ba_ref[...]
    log_sig = jnp.minimum(u, 0.0) - jnp.log1p(jnp.exp(-jnp.abs(u)))
    la_ref[...] = log_sig / GLA_TAU


def _mla_lat(h, w_lat, qg, kvg, cs, wa2_split, ba, tm, q_rank, kv_rank, gate_rank):
    t, d = h.shape
    n = w_lat.shape[1]
    gk = wa2_split.shape[1]
    row = lambda i: (i, 0)
    fix = lambda i: (0, 0)
    return pl.pallas_call(
        functools.partial(_mla_lat_kernel, q_rank=q_rank, kv_rank=kv_rank, gate_rank=gate_rank),
        grid=(t // tm,),
        in_specs=[pl.BlockSpec((tm, d), row), pl.BlockSpec((d, n), fix),
                  pl.BlockSpec((1, q_rank), fix), pl.BlockSpec((1, kv_rank), fix),
                  pl.BlockSpec((tm, LANE), row), pl.BlockSpec((LANE, gk), fix), pl.BlockSpec((1, gk), fix)],
        out_specs=[pl.BlockSpec((tm, q_rank), row), pl.BlockSpec((tm, kv_rank), row),
                   pl.BlockSpec((tm, MLA_ROPE), row), pl.BlockSpec((tm, gk), row)],
        out_shape=[jax.ShapeDtypeStruct((t, q_rank), BF16), jax.ShapeDtypeStruct((t, kv_rank), F32),
                   jax.ShapeDtypeStruct((t, MLA_ROPE), F32), jax.ShapeDtypeStruct((t, gk), F32)],
        compiler_params=_params("parallel"),
        name="mla_lat",
    )(h, w_lat, qg, kvg, cs, wa2_split, ba)


def _proj_kernel(h_ref, w_ref, o_ref):
    o_ref[...] = _dot(h_ref[...], w_ref[...]).astype(o_ref.dtype)


def _proj(h, w, tm, tn):
    t, d = h.shape
    n = w.shape[1]
    return pl.pallas_call(
        _proj_kernel,
        grid=(t // tm, n // tn),
        in_specs=[pl.BlockSpec((tm, d), lambda i, j: (i, 0)), pl.BlockSpec((d, tn), lambda i, j: (0, j))],
        out_specs=pl.BlockSpec((tm, tn), lambda i, j: (i, j)),
        out_shape=jax.ShapeDtypeStruct((t, n), BF16),
        compiler_params=_params("parallel", "parallel"),
        name="proj",
    )(h, w)


def _mla_q_kernel(ql_ref, w_ref, cs_ref, gain_ref, q_ref, *, heads):
    ql = ql_ref[...]
    cs = cs_ref[...]
    gain = gain_ref[...]
    lane = lax.broadcasted_iota(jnp.int32, (1, LANE), 1)
    for hd in range(heads):
        c0 = hd * MLA_HEAD_PAD
        z = _dot(ql, w_ref[:, c0:c0 + MLA_HEAD_PAD])
        nope = z[:, :MLA_NOPE]
        rr = z[:, MLA_NOPE:] * cs
        rot = rr + pltpu.roll(rr, MLA_ROPE, 1)
        ss = (jnp.sum(nope * nope, axis=-1, keepdims=True)
              + jnp.sum(jnp.where(lane < MLA_ROPE, rot * rot, 0.0), axis=-1, keepdims=True))
        inv = lax.rsqrt(ss / MLA_QK_DIM + EPS)
        q_ref[:, c0:c0 + MLA_NOPE] = (nope * inv * gain[:, :MLA_NOPE]).astype(q_ref.dtype)
        q_ref[:, c0 + MLA_NOPE:c0 + MLA_HEAD_PAD] = (rot * inv * gain[:, MLA_NOPE:]).astype(q_ref.dtype)


def _mla_q(qlat, w_uq, cs, gain, tm, heads):
    t, r = qlat.shape
    n = heads * MLA_HEAD_PAD
    return pl.pallas_call(
        functools.partial(_mla_q_kernel, heads=heads),
        grid=(t // tm,),
        in_specs=[pl.BlockSpec((tm, r), lambda i: (i, 0)), pl.BlockSpec((r, n), lambda i: (0, 0)),
                  pl.BlockSpec((tm, LANE), lambda i: (i, 0)), pl.BlockSpec((1, MLA_HEAD_PAD), lambda i: (0, 0))],
        out_specs=pl.BlockSpec((tm, n), lambda i: (i, 0)),
        out_shape=jax.ShapeDtypeStruct((t, n), BF16),
        compiler_params=_params("parallel"),
        name="mla_q",
    )(qlat, w_uq, cs, gain)


def _mla_qt_kernel(ql_ref, w_ref, cs_ref, gain_ref, q_ref, *, heads):
    ql = ql_ref[...]
    cs = cs_ref[...]
    gain = gain_ref[...]
    for hd in range(heads):
        r0 = hd * MLA_HEAD_PAD
        z = _dot_nt(w_ref[r0:r0 + MLA_HEAD_PAD, :], ql)
        nope = z[:MLA_NOPE]
        rr = z[MLA_NOPE:] * cs
        rot = rr[:MLA_ROPE] + rr[MLA_ROPE:]
        ss = jnp.sum(nope * nope, axis=0, keepdims=True) + jnp.sum(rot * rot, axis=0, keepdims=True)
        inv = lax.rsqrt(ss / MLA_QK_DIM + EPS)
        q_ref[r0:r0 + MLA_NOPE, :] = (nope * inv * gain[:MLA_NOPE]).astype(q_ref.dtype)
        q_ref[r0 + MLA_NOPE:r0 + MLA_QK_DIM, :] = (rot * inv * gain[MLA_NOPE:MLA_QK_DIM]).astype(q_ref.dtype)
        q_ref[r0 + MLA_QK_DIM:r0 + MLA_HEAD_PAD, :] = jnp.zeros((MLA_HEAD_PAD - MLA_QK_DIM, ql.shape[0]), q_ref.dtype)


def _mla_qt(qlat, w_uq_t, cs_t, gain_col, tm, heads):
    t, r = qlat.shape
    n = heads * MLA_HEAD_PAD
    return pl.pallas_call(
        functools.partial(_mla_qt_kernel, heads=heads),
        grid=(t // tm,),
        in_specs=[pl.BlockSpec((tm, r), lambda i: (i, 0)), pl.BlockSpec((n, r), lambda i: (0, 0)),
                  pl.BlockSpec((LANE, tm), lambda i: (0, i)), pl.BlockSpec((MLA_HEAD_PAD, 1), lambda i: (0, 0))],
        out_specs=pl.BlockSpec((n, tm), lambda i: (0, i)),
        out_shape=jax.ShapeDtypeStruct((n, t), BF16),
        compiler_params=_params("parallel"),
        name="mla_qt",
    )(qlat, w_uq_t, cs_t, gain_col)


def _mla_kv_kernel(ckv_ref, kr_ref, wk_ref, wvt_ref, k_ref, vt_ref, *, heads):
    c = ckv_ref[...].astype(BF16)
    kr = kr_ref[...]
    kr2 = jnp.sum(kr * kr, axis=-1, keepdims=True)
    kr_pad = jnp.concatenate([kr, jnp.zeros_like(kr)], axis=1)
    for pair in range(heads // 2):
        z = _dot(c, wk_ref[:, pair * 2 * MLA_NOPE:(pair + 1) * 2 * MLA_NOPE])
        for sub in range(2):
            kn = z[:, sub * MLA_NOPE:(sub + 1) * MLA_NOPE]
            inv = lax.rsqrt((jnp.sum(kn * kn, axis=-1, keepdims=True) + kr2) / MLA_QK_DIM + EPS)
            c0 = (2 * pair + sub) * MLA_HEAD_PAD
            k_ref[:, c0:c0 + MLA_NOPE] = (kn * inv).astype(k_ref.dtype)
            k_ref[:, c0 + MLA_NOPE:c0 + MLA_HEAD_PAD] = (kr_pad * inv).astype(k_ref.dtype)
    vt = _dot_nt(wvt_ref[...], c)
    for hd in range(heads):
        vt_ref[hd, :MLA_V, :] = vt[hd * MLA_V:(hd + 1) * MLA_V].astype(vt_ref.dtype)
        vt_ref[hd, MLA_V:, :] = jnp.ones((V_ROWS - MLA_V, vt.shape[1]), vt_ref.dtype)


def _mla_kv(ckv, kr, w_uk, w_uv_t, tm, heads):
    t, r = ckv.shape
    return pl.pallas_call(
        functools.partial(_mla_kv_kernel, heads=heads),
        grid=(t // tm,),
        in_specs=[pl.BlockSpec((tm, r), lambda i: (i, 0)), pl.BlockSpec((tm, MLA_ROPE), lambda i: (i, 0)),
                  pl.BlockSpec((r, heads * MLA_NOPE), lambda i: (0, 0)),
                  pl.BlockSpec((heads * MLA_V, r), lambda i: (0, 0))],
        out_specs=[pl.BlockSpec((tm, heads * MLA_HEAD_PAD), lambda i: (i, 0)),
                   pl.BlockSpec((heads, None, V_ROWS, tm), lambda i: (0, i, 0, 0))],
        out_shape=[jax.ShapeDtypeStruct((t, heads * MLA_HEAD_PAD), BF16),
                   jax.ShapeDtypeStruct((heads, t // tm, V_ROWS, tm), BF16)],
        compiler_params=_params("parallel"),
        name="mla_kv",
    )(ckv, kr, w_uk, w_uv_t)


def _flash_kernel(qt_ref, k_ref, vt_ref, o_ref, m_ref, acc_ref, s_ref, mx_ref, p_ref, al_ref, *, tq, tk):
    i = pl.program_id(1)
    m_ref[...] = jnp.full_like(m_ref, -jnp.inf)
    acc_ref[...] = jnp.zeros_like(acc_ref)

    n_blk = tq // QBLK
    cols_of = [pl.ds(c * QBLK, QBLK) for c in range(n_blk)]

    def diag_mode(u, c):
        k_lo, k_hi = (u * tk) // CHUNK, ((u + 1) * tk - 1) // CHUNK
        q_lo, q_hi = (c * QBLK) // CHUNK, ((c + 1) * QBLK - 1) // CHUNK
        return "all" if k_hi <= q_lo else "none" if k_lo > q_hi else "some"

    def score(t, slot, c, mode):
        if mode == "none":
            return
        cols = cols_of[c]
        k = k_ref[pl.ds(pl.multiple_of(t * tk, tk), tk), :]
        s = _dot(k, qt_ref[:, cols])
        if mode == "some":
            k_chunk = (t * tk + lax.broadcasted_iota(jnp.int32, (tk, QBLK), 0)) // CHUNK
            q_chunk = (i * tq + c * QBLK + lax.broadcasted_iota(jnp.int32, (tk, QBLK), 1)) // CHUNK
            s = jnp.where(k_chunk <= q_chunk, s, -jnp.inf)
        s_ref[slot, :, cols] = s
        mx_ref[slot, :, cols] = jnp.max(s, axis=0, keepdims=True)

    def soften(slot, c, mode="all"):
        if mode == "none":
            return
        cols = cols_of[c]
        m_prev = m_ref[:, cols]
        m_new = jnp.maximum(m_prev, mx_ref[slot, :, cols])
        p_ref[slot, :, cols] = jnp.exp2(s_ref[slot, :, cols] - m_new).astype(BF16)
        al_ref[slot, :, cols] = jnp.exp2(m_prev - m_new)
        m_ref[:, cols] = m_new

    def gather(t, slot, c, mode="all"):
        if mode == "none":
            return
        cols = cols_of[c]
        acc_ref[:, cols] = al_ref[slot, :, cols] * acc_ref[:, cols] + _dot(vt_ref[t], p_ref[slot, :, cols])

    def pair(g, diag):
        for c in range(n_blk):
            score(g, 0, c, diag_mode(0, c) if diag else "all")
            soften(1, c)
            gather(g - 2, 0, c)
        for c in range(n_blk):
            score(g + 1, 1, c, diag_mode(1, c) if diag else "all")
            soften(0, c, diag_mode(0, c) if diag else "all")
            gather(g - 1, 1, c)

    def head(diag):
        for c in range(n_blk):
            score(0, 0, c, diag_mode(0, c) if diag else "all")
        for c in range(n_blk):
            score(1, 1, c, diag_mode(1, c) if diag else "all")
            soften(0, c, diag_mode(0, c) if diag else "all")

    @pl.when(i == 0)
    def _():
        head(True)

    @pl.when(i > 0)
    def _():
        head(False)

    def body(j, carry):
        pair(2 * j, False)
        return carry

    lax.fori_loop(1, i, body, 0)

    @pl.when(i > 0)
    def _():
        pair(2 * i, True)

    for c in range(n_blk):
        soften(1, c, diag_mode(1, c))
        gather(2 * i, 0, c, diag_mode(0, c))
    for c in range(n_blk):
        gather(2 * i + 1, 1, c, diag_mode(1, c))
    acc = acc_ref[...]
    o_ref[...] = (acc[:MLA_V] / acc[MLA_V:MLA_V + 1]).T.astype(o_ref.dtype)


def _flash(qt, k, vt, heads, tq):
    t = k.shape[0]
    n_kt, tk = vt.shape[1], vt.shape[3]
    assert tq == 2 * tk and tk % CHUNK == 0 and t % tq == 0
    return pl.pallas_call(
        functools.partial(_flash_kernel, tq=tq, tk=tk),
        grid=(heads, t // tq),
        in_specs=[pl.BlockSpec((MLA_HEAD_PAD, tq), lambda hd, i: (hd, i)),
                  pl.BlockSpec((t, MLA_HEAD_PAD), lambda hd, i: (0, hd)),
                  pl.BlockSpec((None, n_kt, V_ROWS, tk), lambda hd, i: (hd, 0, 0, 0))],
        out_specs=pl.BlockSpec((tq, MLA_V), lambda hd, i: (i, hd)),
        out_shape=jax.ShapeDtypeStruct((t, heads * MLA_V), BF16),
        scratch_shapes=[pltpu.VMEM((1, tq), F32), pltpu.VMEM((V_ROWS, tq), F32),
                        pltpu.VMEM((2, tk, tq), F32), pltpu.VMEM((2, 1, tq), F32),
                        pltpu.VMEM((2, tk, tq), BF16), pltpu.VMEM((2, 1, tq), F32)],
        compiler_params=_params("parallel", "arbitrary"),
        name="flash",
    )(qt, k, vt)


def _sattn_kernel(q_ref, cc_ref, ck_ref, nc_ref, nk_ref, wuk_ref, o_ref,
                  qabs_ref, qr_ref, m_ref, l_ref, acc_ref, *, heads, nq, tk, n_new):
    past, rank = cc_ref.shape
    n_tiles = past // tk
    for hd in range(heads):
        c0 = hd * MLA_HEAD_PAD
        qn = q_ref[:, c0:c0 + MLA_NOPE]
        qabs_ref[hd * nq:(hd + 1) * nq, :] = _dot(
            qn, wuk_ref[hd * MLA_NOPE:(hd + 1) * MLA_NOPE, :]).astype(qabs_ref.dtype)
        qr_ref[hd * nq:(hd + 1) * nq, :] = q_ref[:, c0 + MLA_NOPE:c0 + MLA_NOPE + MLA_ROPE]
    m_ref[...] = jnp.full_like(m_ref, -jnp.inf)
    l_ref[...] = jnp.zeros_like(l_ref)
    acc_ref[...] = jnp.zeros_like(acc_ref)

    def tile(t):
        if t < n_tiles:
            return cc_ref[t * tk:(t + 1) * tk, :].astype(BF16), ck_ref[t * tk:(t + 1) * tk, :]
        pad = LANE - n_new
        c_new = jnp.concatenate([nc_ref[...], jnp.zeros((pad, rank), F32)], axis=0)
        k_new = jnp.concatenate([nk_ref[...], jnp.zeros((pad, MLA_ROPE), F32)], axis=0)
        return c_new.astype(BF16), k_new

    def key_norm(c, kr):
        n_keys = c.shape[0]
        kn_t = _dot_nt(wuk_ref[...], c)
        ss_t = jnp.sum((kn_t * kn_t).reshape(heads, MLA_NOPE, n_keys), axis=1)
        kr2_t = lax.dot_general(jnp.ones((8, MLA_ROPE), F32), kr * kr, NT_DIMS,
                                preferred_element_type=F32, precision=lax.Precision.HIGHEST)[0:1]
        return lax.rsqrt((ss_t + kr2_t) / MLA_QK_DIM + EPS)

    def attend(c, kr, inv_t, n_valid):
        n_keys = c.shape[0]
        s = _dot_nt(qabs_ref[...], c) + _dot_nt(qr_ref[...], kr.astype(BF16))
        s = s * jnp.broadcast_to(inv_t[:, None, :], (heads, nq, n_keys)).reshape(heads * nq, n_keys)
        if n_valid < n_keys:
            s = jnp.where(lax.broadcasted_iota(jnp.int32, s.shape, 1) < n_valid, s, -jnp.inf)
        m_prev = m_ref[...]
        m_new = jnp.maximum(m_prev, jnp.max(s, axis=-1, keepdims=True))
        alpha = jnp.exp(m_prev - m_new)
        p = jnp.exp(s - m_new)
        l_ref[...] = alpha * l_ref[...] + jnp.sum(p, axis=-1, keepdims=True)
        acc_ref[...] = alpha * acc_ref[...] + _dot(p.astype(BF16), c)
        m_ref[...] = m_new

    c, kr = tile(0)
    inv_t = key_norm(c, kr)
    for t in range(n_tiles + 1):
        if t < n_tiles:
            c_next, kr_next = tile(t + 1)
            inv_next = key_norm(c_next, kr_next)
        attend(c, kr, inv_t, tk if t < n_tiles else n_new)
        if t < n_tiles:
            c, kr, inv_t = c_next, kr_next, inv_next
    o_ref[...] = (acc_ref[...] / l_ref[...]).astype(o_ref.dtype)


def _sattn(q, cache_c, cache_k, new_c, new_k, wuk_t, heads, nq, tk):
    nb, past, rank = cache_c.shape
    return pl.pallas_call(
        functools.partial(_sattn_kernel, heads=heads, nq=nq, tk=tk, n_new=nq),
        grid=(nb,),
        in_specs=[pl.BlockSpec((nq, heads * MLA_HEAD_PAD), lambda b: (b, 0)),
                  pl.BlockSpec((None, past, rank), lambda b: (b, 0, 0)),
                  pl.BlockSpec((None, past, MLA_ROPE), lambda b: (b, 0, 0)),
                  pl.BlockSpec((nq, rank), lambda b: (b, 0)),
                  pl.BlockSpec((nq, MLA_ROPE), lambda b: (b, 0)),
                  pl.BlockSpec(wuk_t.shape, lambda b: (0, 0))],
        out_specs=pl.BlockSpec((None, heads * nq, rank), lambda b: (b, 0, 0)),
        out_shape=jax.ShapeDtypeStruct((nb, heads * nq, rank), BF16),
        scratch_shapes=[pltpu.VMEM((heads * nq, rank), BF16), pltpu.VMEM((heads * nq, MLA_ROPE), BF16),
                        pltpu.VMEM((heads * nq, 1), F32), pltpu.VMEM((heads * nq, 1), F32),
                        pltpu.VMEM((heads * nq, rank), F32)],
        compiler_params=_params("parallel"),
        name="sattn",
    )(q, cache_c, cache_k, new_c, new_k, wuk_t)


def _svup_kernel(ol_ref, w_ref, o_ref):
    nb, nq, rank = ol_ref.shape
    o_ref[...] = _dot(ol_ref[...].reshape(nb * nq, rank), w_ref[...]).astype(o_ref.dtype)


def _svup(o_lat, wuv, heads, nq):
    nb, _, rank = o_lat.shape
    return pl.pallas_call(
        _svup_kernel,
        grid=(heads,),
        in_specs=[pl.BlockSpec((nb, nq, rank), lambda hd: (0, hd, 0)),
                  pl.BlockSpec((rank, MLA_V), lambda hd: (0, hd))],
        out_specs=pl.BlockSpec((nb * nq, MLA_V), lambda hd: (0, hd)),
        out_shape=jax.ShapeDtypeStruct((nb * nq, heads * MLA_V), BF16),
        compiler_params=_params("parallel"),
        name="svup",
    )(o_lat, wuv)


def _gla_levels(c):
    leaf = min(c, GLA_LEAF)
    levels = [(leaf, leaf // 2 - 1)]
    g = 2 * leaf
    while g <= c:
        levels.append((g, g // 2 - 1))
        g *= 2
    return levels


def _gla_kernel(*refs, heads, dk, dv, c, has_s0):
    if has_s0:
        q_ref, k_ref, v_ref, la_ref, g_ref, s0_ref, o_ref, sout_ref, st_ref = refs
    else:
        q_ref, k_ref, v_ref, la_ref, g_ref, o_ref, sout_ref, st_ref = refs
    j = pl.program_id(1)
    nj = pl.num_programs(1)

    @pl.when(j == 0)
    def _():
        for hd in range(heads):
            if has_s0:
                st_ref[hd] = s0_ref[hd].T
            else:
                st_ref[hd] = jnp.zeros((dv, dk), F32)

    levels = _gla_levels(c)
    row = lax.broadcasted_iota(jnp.int32, (c, c), 0)
    col = lax.broadcasted_iota(jnp.int32, (c, c), 1)
    masks = []
    for lv, (g, _) in enumerate(levels):
        same = (row // g) == (col // g)
        if lv == 0:
            masks.append(jnp.logical_and(same, col <= row))
        else:
            half = g // 2
            masks.append(jnp.logical_and(same, jnp.logical_and((row // half) % 2 == 1, (col // half) % 2 == 0)))
    rid = lax.broadcasted_iota(jnp.int32, (c, dk), 0)
    gain = g_ref[...]

    for hd in range(heads):
        la = la_ref[:, hd * dk:(hd + 1) * dk]
        b = la
        sh = 1
        while sh < c:
            b = b + jnp.where(rid >= sh, pltpu.roll(b, sh, 0), 0.0)
            sh *= 2
        b_end = b[c - 1:c, :]
        q = q_ref[:, hd * dk:(hd + 1) * dk].astype(F32) * (dk ** -0.5)
        k = k_ref[:, hd * dk:(hd + 1) * dk].astype(F32)
        v = v_ref[:, hd * dv:(hd + 1) * dv]

        a = jnp.zeros((c, c), F32)
        for lv, (g, r) in enumerate(levels):
            ref_rows = jnp.broadcast_to(b.reshape(c // g, g, dk)[:, r:r + 1, :], (c // g, g, dk)).reshape(c, dk)
            d = b - ref_rows
            if lv == 0:
                fq, fk = jnp.exp(d), jnp.exp(-d)
            else:
                fq = fk = jnp.exp(-jnp.abs(d))
            a_lv = _dot_nt((q * fq).astype(BF16), (k * fk).astype(BF16))
            a = jnp.where(masks[lv], a_lv, a)

        st = st_ref[hd]
        o = _dot(a.astype(BF16), v) + _dot_nt((q * jnp.exp(b)).astype(BF16), st.astype(BF16))
        k_end = (k * jnp.exp(b_end - b)).astype(BF16)
        st_ref[hd] = st * jnp.exp(b_end) + lax.dot_general(v, k_end, TN_DIMS, preferred_element_type=F32)
        o_ref[:, hd * dv:(hd + 1) * dv] = (o * _rms_inv(o, dv) * gain).astype(o_ref.dtype)

    @pl.when(j == nj - 1)
    def _():
        for hd in range(heads):
            sout_ref[hd] = st_ref[hd].T


def _gla(zp, la, gain, s0, nb, c, heads, dk, dv):
    t = la.shape[0]
    nj = t // (nb * c)
    gk, gv = heads * dk, heads * dv
    assert gv % gk == 0
    row = lambda bb, j: (bb * nj + j, 0)
    in_specs = [pl.BlockSpec((c, gk), row),
                pl.BlockSpec((c, gk), lambda bb, j: (bb * nj + j, 1)),
                pl.BlockSpec((c, gv), lambda bb, j: (bb * nj + j, 2 * gk // gv)),
                pl.BlockSpec((c, gk), row),
                pl.BlockSpec((1, dv), lambda bb, j: (0, 0))]
    args = [zp, zp, zp, la, gain]
    if s0 is not None:
        in_specs.append(pl.BlockSpec((None, heads, dk, dv), lambda bb, j: (bb, 0, 0, 0)))
        args.append(s0)
    return pl.pallas_call(
        functools.partial(_gla_kernel, heads=heads, dk=dk, dv=dv, c=c, has_s0=s0 is not None),
        grid=(nb, nj),
        in_specs=in_specs,
        out_specs=[pl.BlockSpec((c, gv), row),
                   pl.BlockSpec((None, heads, dk, dv), lambda bb, j: (bb, 0, 0, 0))],
        out_shape=[jax.ShapeDtypeStruct((t, gv), BF16), jax.ShapeDtypeStruct((nb, heads, dk, dv), F32)],
        scratch_shapes=[pltpu.VMEM((heads, dv, dk), F32)],
        compiler_params=_params("parallel", "arbitrary"),
        name="gla",
    )(*args)


def _mix_kernel(om_ref, og_ref, gate_o_ref, gate_m_ref, gate_g_ref, x_ref, w_ref, g_ref, y_ref, h_ref):
    o_gla = og_ref[...].astype(F32) * jax.nn.silu(gate_o_ref[...].astype(F32))
    mixed = (jax.nn.sigmoid(gate_m_ref[...].astype(F32)) * om_ref[...].astype(F32)
             + jax.nn.sigmoid(gate_g_ref[...].astype(F32)) * o_gla)
    y = x_ref[...] + _dot(mixed.astype(BF16), w_ref[...])
    y_ref[...] = y
    h_ref[...] = (y * _rms_inv(y, y.shape[-1]) * g_ref[...]).astype(h_ref.dtype)


def _mix(o_mla, o_gla, zp, gate_block0, x, w_o, g, tm):
    t, d = x.shape
    row = lambda i: (i, 0)
    return pl.pallas_call(
        _mix_kernel,
        grid=(t // tm,),
        in_specs=[pl.BlockSpec((tm, d), row), pl.BlockSpec((tm, d), row),
                  pl.BlockSpec((tm, d), lambda i: (i, gate_block0)),
                  pl.BlockSpec((tm, d), lambda i: (i, gate_block0 + 1)),
                  pl.BlockSpec((tm, d), lambda i: (i, gate_block0 + 2)),
                  pl.BlockSpec((tm, d), row), pl.BlockSpec((d, d), lambda i: (0, 0)),
                  pl.BlockSpec((1, d), lambda i: (0, 0))],
        out_specs=[pl.BlockSpec((tm, d), row), pl.BlockSpec((tm, d), row)],
        out_shape=[jax.ShapeDtypeStruct((t, d), F32), jax.ShapeDtypeStruct((t, d), BF16)],
        compiler_params=_params("parallel"),
        name="mix",
    )(o_mla, o_gla, zp, zp, zp, x, w_o, g)


def _ffn_kernel(h_ref, y_ref, wu_ref, wd_ref, o_ref):
    f = pl.program_id(1)

    @pl.when(f == 0)
    def _():
        o_ref[...] = y_ref[...]

    u = jnp.maximum(_dot(h_ref[...], wu_ref[...]), 0.0)
    o_ref[...] += _dot((u * u).astype(BF16), wd_ref[...])


def _ffn(h, y, w_up, w_down, tm, tf):
    t, d = y.shape
    dff = w_up.shape[1]
    return pl.pallas_call(
        _ffn_kernel,
        grid=(t // tm, dff // tf),
        in_specs=[pl.BlockSpec((tm, d), lambda i, f: (i, 0)), pl.BlockSpec((tm, d), lambda i, f: (i, 0)),
                  pl.BlockSpec((d, tf), lambda i, f: (0, f)), pl.BlockSpec((tf, d), lambda i, f: (f, 0))],
        out_specs=pl.BlockSpec((tm, d), lambda i, f: (i, 0)),
        out_shape=jax.ShapeDtypeStruct((t, d), F32),
        compiler_params=_params("parallel", "arbitrary"),
        name="ffn",
    )(h, y, w_up, w_down)


def _rope_table(pos):
    half = MLA_ROPE // 2
    freqs = jnp.power(ROPE_THETA, -jnp.arange(half, dtype=F32) / half)
    ang = pos[:, None] * freqs[None, :]
    cos, sin = jnp.cos(ang), jnp.sin(ang)
    return jnp.concatenate([cos, cos, -sin, sin], axis=1)


def _swap_halves(w):
    half = w.shape[-1] // 2
    return jnp.concatenate([w[..., half:], w[..., :half]], axis=-1)


def _pick(t, pref):
    while t % pref:
        pref //= 2
    return pref


def kernel(x_prompt, x_sample, cache_mla_ckv, cache_mla_krope, state_gla, norm_mix_g, w_in, mla_q_norm_g,
           mla_w_uq, mla_kv_norm_g, mla_w_ukv, mla_q_gain_nope, mla_q_gain_rope, mla_k_gain_nope,
           mla_k_gain_rope, gla_w_a2, gla_b_a, gla_norm_g, w_o, norm_ffn_g, ffn_w_up, ffn_w_down):
    depth = w_in.shape[0]
    bp, seq, d = x_prompt.shape
    nb, dec_seq, _ = x_sample.shape
    past = cache_mla_ckv.shape[2]
    q_rank, heads = mla_w_uq.shape[1], mla_w_uq.shape[2]
    kv_rank = mla_w_ukv.shape[1]
    gla_heads, dk, dv = state_gla.shape[2], state_gla.shape[3], state_gla.shape[4]
    gate_rank = gla_w_a2.shape[1]
    gk, gv = gla_heads * dk, gla_heads * dv
    assert bp == 1 and heads * MLA_V == d and gv == d and seq % CHUNK == 0

    cs_p = _rope_table(jnp.arange(seq, dtype=F32))
    cs_s = jnp.tile(_rope_table(past + jnp.arange(dec_seq, dtype=F32)), (nb, 1))

    xp = x_prompt.reshape(seq, d)
    xs = x_sample.reshape(nb * dec_seq, d)
    outs = [[] for _ in range(6)]
    for l in range(depth):
        wi = w_in[l]
        pts = np.cumsum([q_rank, kv_rank, MLA_ROPE, gk, gk, gv, gate_rank, gv, d]).tolist()
        w_qkv_lat = wi[:, :pts[1]]
        w_kr = wi[:, pts[1]:pts[2]]
        w_alr = wi[:, pts[5]:pts[6]]
        assert 3 * gate_rank <= LANE
        w_lat = jnp.concatenate(
            [w_qkv_lat, w_kr, _swap_halves(w_kr), w_alr, w_alr, w_alr,
             jnp.zeros((d, LANE - 3 * gate_rank), F32)], axis=1).astype(BF16)
        w_gla = wi[:, pts[2]:pts[5]].astype(BF16)
        w_gate = wi[:, pts[6]:].astype(BF16)
        wa2_hi = gla_w_a2[l].astype(BF16)
        wa2_lo = (gla_w_a2[l] - wa2_hi.astype(F32)).astype(BF16)
        wa2_split = jnp.concatenate(
            [wa2_hi, wa2_lo, wa2_hi, jnp.zeros((LANE - 3 * gate_rank, gk), BF16)], axis=0)

        wq = mla_w_uq[l]
        wq_r = wq[..., MLA_NOPE:]
        w_uq = jnp.concatenate([wq[..., :MLA_NOPE], wq_r, _swap_halves(wq_r)], axis=-1)
        w_uq = w_uq.reshape(q_rank, heads * MLA_HEAD_PAD).astype(BF16)
        w_uq_t = w_uq.T
        wuk = mla_w_ukv[l][..., :MLA_NOPE].reshape(kv_rank, heads * MLA_NOPE).astype(BF16)
        wuk_t = wuk.T
        wuv = mla_w_ukv[l][..., MLA_NOPE:].reshape(kv_rank, heads * MLA_V).astype(BF16)
        wuv_t = wuv.T
        gain = jnp.concatenate([mla_q_gain_nope[l] * mla_k_gain_nope[l],
                                jnp.tile(mla_q_gain_rope[l] * mla_k_gain_rope[l], 2),
                                jnp.zeros((MLA_HEAD_PAD - MLA_QK_DIM,), F32)])[None, :] * MLA_SCALE
        gain_col = gain.T * LOG2E
        wo_b = w_o[l].astype(BF16)
        wup_b = ffn_w_up[l].astype(BF16)
        wdn_b = ffn_w_down[l].astype(BF16)
        g_mix = norm_mix_g[l][None, :]
        g_q = mla_q_norm_g[l][None, :]
        g_kv = mla_kv_norm_g[l][None, :]
        g_gla = gla_norm_g[l][None, :]
        g_ffn = norm_ffn_g[l][None, :]
        ba = gla_b_a[l][None, :]

        def front(x, cs, tm):
            h = _prenorm(x, g_mix, _pick(x.shape[0], 256))
            qlat, ckv, kr, la = _mla_lat(h, w_lat, g_q, g_kv, cs, wa2_split, ba, tm, q_rank, kv_rank, gate_rank)
            z_gla = _proj(h, w_gla, _pick(x.shape[0], 1024), 1024)
            z_gate = _proj(h, w_gate, _pick(x.shape[0], 1024), 1024)
            return qlat, ckv, kr, la, z_gla, z_gate

        def back(x, o_mla, o_gla, z_gate, tm):
            y1, h2 = _mix(o_mla, o_gla, z_gate, 0, x, wo_b, g_ffn, tm)
            return _ffn(h2, y1, wup_b, wdn_b, tm, 1024)

        tile = _pick(seq, 512)
        qlat, ckv, kr, la, z_gla, z_gate = front(xp, cs_p, tile)
        qt = _mla_qt(qlat, w_uq_t, cs_p.T, gain_col, tile, heads)
        kcat, vt = _mla_kv(ckv, kr, wuk, wuv_t, tile, heads)
        o_mla = _flash(qt, kcat, vt, heads, 2 * tile)
        o_gla, st = _gla(z_gla, la, g_gla, None, 1, _pick(seq, 128), gla_heads, dk, dv)
        xp_next = back(xp, o_mla, o_gla, z_gate, tile)
        outs[0].append(ckv.reshape(bp, seq, kv_rank))
        outs[1].append(kr.reshape(bp, seq, MLA_ROPE))
        outs[2].append(st)
        xp = xp_next

        qlat, ckv, kr, la, z_gla, z_gate = front(xs, cs_s, nb * dec_seq)
        qcat = _mla_q(qlat, w_uq, cs_s, gain, nb * dec_seq, heads)
        o_lat = _sattn(qcat, cache_mla_ckv[l], cache_mla_krope[l], ckv, kr, wuk_t, heads, dec_seq, _pick(past, 512))
        o_mla = _svup(o_lat, wuv, heads, dec_seq)
        o_gla, st = _gla(z_gla, la, g_gla, state_gla[l], nb, dec_seq, gla_heads, dk, dv)
        xs_next = back(xs, o_mla, o_gla, z_gate, nb * dec_seq)
        outs[3].append(ckv.reshape(nb, dec_seq, kv_rank))
        outs[4].append(kr.reshape(nb, dec_seq, MLA_ROPE))
        outs[5].append(st)
        xs = xs_next

    return (xp.reshape(bp, seq, d), xs.reshape(nb, dec_seq, d),
            jnp.stack(outs[0]), jnp.stack(outs[1]), jnp.stack(outs[2]),
            jnp.stack(outs[3]), jnp.stack(outs[4]), jnp.stack(outs[5]))
```

```python
import functools

import jax
import jax.numpy as jnp
import numpy as np
from jax import lax
from jax.experimental import pallas as pl
from jax.experimental.pallas import tpu as pltpu

F32 = jnp.float32
BF16 = jnp.bfloat16

EPS = 1e-6
CHUNK = 64
MLA_NOPE = 128
MLA_ROPE = 64
MLA_V = 128
MLA_QK_DIM = MLA_NOPE + MLA_ROPE
MLA_SCALE = MLA_QK_DIM ** -0.5
MLA_HEAD_PAD = 256
V_ROWS = MLA_V + 16
LOG2E = 1.4426950408889634
QBLK = 256
ROPE_THETA = 10000.0
GLA_TAU = 16.0
GLA_LEAF = 32
LANE = 128
VMEM_LIMIT = 56 * 1024 * 1024

NT_DIMS = (((1,), (1,)), ((), ()))
TN_DIMS = (((0,), (0,)), ((), ()))


def _params(*sem, flags=None):
    return pltpu.CompilerParams(dimension_semantics=sem, vmem_limit_bytes=VMEM_LIMIT, flags=flags)


def _dot(a, b):
    return jnp.dot(a, b, preferred_element_type=F32)


def _dot_nt(a, b):
    return lax.dot_general(a, b, NT_DIMS, preferred_element_type=F32)


def _rms_inv(x, n):
    return lax.rsqrt(jnp.sum(x * x, axis=-1, keepdims=True) / n + EPS)


def _prenorm_kernel(x_ref, g_ref, h_ref):
    x = x_ref[...]
    h_ref[...] = (x * _rms_inv(x, x.shape[-1]) * g_ref[...]).astype(h_ref.dtype)


def _prenorm(x, g, tm):
    t, d = x.shape
    return pl.pallas_call(
        _prenorm_kernel,
        grid=(t // tm,),
        in_specs=[pl.BlockSpec((tm, d), lambda i: (i, 0)), pl.BlockSpec((1, d), lambda i: (0, 0))],
        out_specs=pl.BlockSpec((tm, d), lambda i: (i, 0)),
        out_shape=jax.ShapeDtypeStruct((t, d), BF16),
        compiler_params=_params("parallel"),
        name="prenorm",
    )(x, g)


def _mla_lat_kernel(h_ref, w_ref, qg_ref, kvg_ref, cs_ref, wa2_ref, ba_ref,
                    qlat_ref, ckv_ref, kr_ref, la_ref, *, q_rank, kv_rank, gate_rank):
    z = _dot(h_ref[...], w_ref[...])
    q_lat = z[:, :q_rank]
    qlat_ref[...] = (q_lat * _rms_inv(q_lat, q_rank) * qg_ref[...]).astype(qlat_ref.dtype)
    kv_lat = z[:, q_rank:q_rank + kv_rank]
    ckv_ref[...] = kv_lat * _rms_inv(kv_lat, kv_rank) * kvg_ref[...]
    o = q_rank + kv_rank
    rr = z[:, o:o + LANE] * cs_ref[...]
    kr_ref[...] = rr[:, :MLA_ROPE] + rr[:, MLA_ROPE:]
    a3 = z[:, o + LANE:o + 2 * LANE]
    a_hi = a3.astype(BF16)
    a_lo = (a3 - a_hi.astype(F32)).astype(BF16)
    lane = lax.broadcasted_iota(jnp.int32, a3.shape, 1)
    u = _dot(jnp.where(lane < 2 * gate_rank, a_hi, a_lo), wa2_ref[...]) + ba_ref[...]
    log_sig = jnp.minimum(u, 0.0) - jnp.log1p(jnp.exp(-jnp.abs(u)))
    la_ref[...] = log_sig / GLA_TAU


def _mla_lat(h, w_lat, qg, kvg, cs, wa2_split, ba, tm, q_rank, kv_rank, gate_rank):
    t, d = h.shape
    n = w_lat.shape[1]
    gk = wa2_split.shape[1]
    row = lambda i: (i, 0)
    fix = lambda i: (0, 0)
    return pl.pallas_call(
        functools.partial(_mla_lat_kernel, q_rank=q_rank, kv_rank=kv_rank, gate_rank=gate_rank),
        grid=(t // tm,),
        in_specs=[pl.BlockSpec((tm, d), row), pl.BlockSpec((d, n), fix),
                  pl.BlockSpec((1, q_rank), fix), pl.BlockSpec((1, kv_rank), fix),
                  pl.BlockSpec((tm, LANE), row), pl.BlockSpec((LANE, gk), fix), pl.BlockSpec((1, gk), fix)],
        out_specs=[pl.BlockSpec((tm, q_rank), row), pl.BlockSpec((tm, kv_rank), row),
                   pl.BlockSpec((tm, MLA_ROPE), row), pl.BlockSpec((tm, gk), row)],
        out_shape=[jax.ShapeDtypeStruct((t, q_rank), BF16), jax.ShapeDtypeStruct((t, kv_rank), F32),
                   jax.ShapeDtypeStruct((t, MLA_ROPE), F32), jax.ShapeDtypeStruct((t, gk), F32)],
        compiler_params=_params("parallel"),
        name="mla_lat",
    )(h, w_lat, qg, kvg, cs, wa2_split, ba)


def _proj_kernel(h_ref, w_ref, o_ref):
    o_ref[...] = _dot(h_ref[...], w_ref[...]).astype(o_ref.dtype)


def _proj(h, w, tm, tn):
    t, d = h.shape
    n = w.shape[1]
    return pl.pallas_call(
        _proj_kernel,
        grid=(t // tm, n // tn),
        in_specs=[pl.BlockSpec((tm, d), lambda i, j: (i, 0)), pl.BlockSpec((d, tn), lambda i, j: (0, j))],
        out_specs=pl.BlockSpec((tm, tn), lambda i, j: (i, j)),
        out_shape=jax.ShapeDtypeStruct((t, n), BF16),
        compiler_params=_params("parallel", "parallel"),
        name="proj",
    )(h, w)


def _mla_q_kernel(ql_ref, w_ref, cs_ref, gain_ref, q_ref, *, heads):
    ql = ql_ref[...]
    cs = cs_ref[...]
    gain = gain_ref[...]
    lane = lax.broadcasted_iota(jnp.int32, (1, LANE), 1)
    for hd in range(heads):
        c0 = hd * MLA_HEAD_PAD
        z = _dot(ql, w_ref[:, c0:c0 + MLA_HEAD_PAD])
        nope = z[:, :MLA_NOPE]
        rr = z[:, MLA_NOPE:] * cs
        rot = rr + pltpu.roll(rr, MLA_ROPE, 1)
        ss = (jnp.sum(nope * nope, axis=-1, keepdims=True)
              + jnp.sum(jnp.where(lane < MLA_ROPE, rot * rot, 0.0), axis=-1, keepdims=True))
        inv = lax.rsqrt(ss / MLA_QK_DIM + EPS)
        q_ref[:, c0:c0 + MLA_NOPE] = (nope * inv * gain[:, :MLA_NOPE]).astype(q_ref.dtype)
        q_ref[:, c0 + MLA_NOPE:c0 + MLA_HEAD_PAD] = (rot * inv * gain[:, MLA_NOPE:]).astype(q_ref.dtype)


def _mla_q(qlat, w_uq, cs, gain, tm, heads):
    t, r = qlat.shape
    n = heads * MLA_HEAD_PAD
    return pl.pallas_call(
        functools.partial(_mla_q_kernel, heads=heads),
        grid=(t // tm,),
        in_specs=[pl.BlockSpec((tm, r), lambda i: (i, 0)), pl.BlockSpec((r, n), lambda i: (0, 0)),
                  pl.BlockSpec((tm, LANE), lambda i: (i, 0)), pl.BlockSpec((1, MLA_HEAD_PAD), lambda i: (0, 0))],
        out_specs=pl.BlockSpec((tm, n), lambda i: (i, 0)),
        out_shape=jax.ShapeDtypeStruct((t, n), BF16),
        compiler_params=_params("parallel"),
        name="mla_q",
    )(qlat, w_uq, cs, gain)


def _mla_qt_kernel(ql_ref, w_ref, cs_ref, gain_ref, q_ref, *, heads):
    ql = ql_ref[...]
    cs = cs_ref[...]
    gain = gain_ref[...]
    for hd in range(heads):
        r0 = hd * MLA_HEAD_PAD
        z = _dot_nt(w_ref[r0:r0 + MLA_HEAD_PAD, :], ql)
        nope = z[:MLA_NOPE]
        rr = z[MLA_NOPE:] * cs
        rot = rr[:MLA_ROPE] + rr[MLA_ROPE:]
        ss = jnp.sum(nope * nope, axis=0, keepdims=True) + jnp.sum(rot * rot, axis=0, keepdims=True)
        inv = lax.rsqrt(ss / MLA_QK_DIM + EPS)
        q_ref[hd, :MLA_NOPE, :] = (nope * inv * gain[:MLA_NOPE]).astype(q_ref.dtype)
        q_ref[hd, MLA_NOPE:MLA_QK_DIM, :] = (rot * inv * gain[MLA_NOPE:MLA_QK_DIM]).astype(q_ref.dtype)
        q_ref[hd, MLA_QK_DIM:, :] = jnp.zeros((MLA_HEAD_PAD - MLA_QK_DIM, ql.shape[0]), q_ref.dtype)


def _mla_qt(qlat, w_uq_t, cs_t, gain_col, tm, tq, heads):
    t, r = qlat.shape
    n = heads * MLA_HEAD_PAD
    per = tq // tm
    return pl.pallas_call(
        functools.partial(_mla_qt_kernel, heads=heads),
        grid=(t // tm,),
        in_specs=[pl.BlockSpec((tm, r), lambda i: (i, 0)), pl.BlockSpec((n, r), lambda i: (0, 0)),
                  pl.BlockSpec((LANE, tm), lambda i: (0, i)), pl.BlockSpec((MLA_HEAD_PAD, 1), lambda i: (0, 0))],
        out_specs=pl.BlockSpec((heads, None, MLA_HEAD_PAD, tm), lambda i: (0, i // per, 0, i % per)),
        out_shape=jax.ShapeDtypeStruct((heads, t // tq, MLA_HEAD_PAD, tq), BF16),
        compiler_params=_params("parallel"),
        name="mla_qt",
    )(qlat, w_uq_t, cs_t, gain_col)


def _mla_kv_kernel(ckv_ref, kr_ref, wk_ref, wvt_ref, k_ref, vt_ref, *, heads):
    c = ckv_ref[...].astype(BF16)
    kr = kr_ref[...]
    kr2 = jnp.sum(kr * kr, axis=-1, keepdims=True)
    kr_pad = jnp.concatenate([kr, jnp.zeros_like(kr)], axis=1)
    for pair in range(heads // 2):
        z = _dot(c, wk_ref[:, pair * 2 * MLA_NOPE:(pair + 1) * 2 * MLA_NOPE])
        for sub in range(2):
            kn = z[:, sub * MLA_NOPE:(sub + 1) * MLA_NOPE]
            inv = lax.rsqrt((jnp.sum(kn * kn, axis=-1, keepdims=True) + kr2) / MLA_QK_DIM + EPS)
            c0 = (2 * pair + sub) * MLA_HEAD_PAD
            k_ref[:, c0:c0 + MLA_NOPE] = (kn * inv).astype(k_ref.dtype)
            k_ref[:, c0 + MLA_NOPE:c0 + MLA_HEAD_PAD] = (kr_pad * inv).astype(k_ref.dtype)
    vt = _dot_nt(wvt_ref[...], c)
    for hd in range(heads):
        vt_ref[hd, :MLA_V, :] = vt[hd * MLA_V:(hd + 1) * MLA_V].astype(vt_ref.dtype)
        vt_ref[hd, MLA_V:, :] = jnp.ones((V_ROWS - MLA_V, vt.shape[1]), vt_ref.dtype)


def _mla_kv(ckv, kr, w_uk, w_uv_t, tm, heads):
    t, r = ckv.shape
    return pl.pallas_call(
        functools.partial(_mla_kv_kernel, heads=heads),
        grid=(t // tm,),
        in_specs=[pl.BlockSpec((tm, r), lambda i: (i, 0)), pl.BlockSpec((tm, MLA_ROPE), lambda i: (i, 0)),
                  pl.BlockSpec((r, heads * MLA_NOPE), lambda i: (0, 0)),
                  pl.BlockSpec((heads * MLA_V, r), lambda i: (0, 0))],
        out_specs=[pl.BlockSpec((tm, heads * MLA_HEAD_PAD), lambda i: (i, 0)),
                   pl.BlockSpec((heads, None, V_ROWS, tm), lambda i: (0, i, 0, 0))],
        out_shape=[jax.ShapeDtypeStruct((t, heads * MLA_HEAD_PAD), BF16),
                   jax.ShapeDtypeStruct((heads, t // tm, V_ROWS, tm), BF16)],
        compiler_params=_params("parallel"),
        name="mla_kv",
    )(ckv, kr, w_uk, w_uv_t)


def _flash_kernel(qt_ref, k_ref, vt_ref, o_ref, m_ref, acc_ref, s_ref, mx_ref, p_ref, al_ref, *, tq, tk):
    n_q = qt_ref.shape[0]
    acc_ref[...] = jnp.zeros_like(acc_ref)

    n_blk = tq // QBLK
    cols_of = [pl.ds(c * QBLK, QBLK) for c in range(n_blk)]

    def diag_mode(u, c):
        k_lo, k_hi = (u * tk) // CHUNK, ((u + 1) * tk - 1) // CHUNK
        q_lo, q_hi = (c * QBLK) // CHUNK, ((c + 1) * QBLK - 1) // CHUNK
        return "all" if k_hi <= q_lo else "none" if k_lo > q_hi else "some"

    def score(i, t, slot, c, mode):
        if mode == "none":
            return
        cols = cols_of[c]
        k = k_ref[pl.ds(pl.multiple_of(t * tk, tk), tk), :]
        s = _dot(k, qt_ref[i, :, cols])
        if mode == "some":
            k_chunk = (t * tk + lax.broadcasted_iota(jnp.int32, (tk, QBLK), 0)) // CHUNK
            q_chunk = (i * tq + c * QBLK + lax.broadcasted_iota(jnp.int32, (tk, QBLK), 1)) // CHUNK
            s = jnp.where(k_chunk <= q_chunk, s, -jnp.inf)
        s_ref[slot, :, cols] = s
        mx_ref[slot, :, cols] = jnp.max(s, axis=0, keepdims=True)

    def soften(slot, c, mode="all", first=False):
        if mode == "none":
            return
        cols = cols_of[c]
        if first:
            m_new = mx_ref[slot, :, cols]
            al_ref[slot, :, cols] = jnp.zeros_like(m_new)
        else:
            m_prev = m_ref[:, cols]
            m_new = jnp.maximum(m_prev, mx_ref[slot, :, cols])
            al_ref[slot, :, cols] = jnp.exp2(m_prev - m_new)
        p_ref[slot, :, cols] = jnp.exp2(s_ref[slot, :, cols] - m_new).astype(BF16)
        m_ref[:, cols] = m_new

    def gather(t, slot, c, mode="all"):
        if mode == "none":
            return
        cols = cols_of[c]
        acc_ref[:, cols] = al_ref[slot, :, cols] * acc_ref[:, cols] + _dot(vt_ref[t], p_ref[slot, :, cols])

    def pair(i, g, diag):
        for c in range(n_blk):
            score(i, g, 0, c, diag_mode(0, c) if diag else "all")
            soften(1, c)
            gather(g - 2, 0, c)
        for c in range(n_blk):
            score(i, g + 1, 1, c, diag_mode(1, c) if diag else "all")
            soften(0, c, diag_mode(0, c) if diag else "all")
            gather(g - 1, 1, c)

    def head(i, diag):
        for c in range(n_blk):
            score(i, 0, 0, c, diag_mode(0, c) if diag else "all")
        for c in range(n_blk):
            score(i, 1, 1, c, diag_mode(1, c) if diag else "all")
            soften(0, c, diag_mode(0, c) if diag else "all", first=True)

    def tail(i):
        for c in range(n_blk):
            soften(1, c, diag_mode(1, c))
            gather(2 * i, 0, c, diag_mode(0, c))
        for c in range(n_blk):
            gather(2 * i + 1, 1, c, diag_mode(1, c))
        acc = acc_ref[...]
        o_ref[pl.ds(pl.multiple_of(i * tq, tq), tq), :] = (acc[:MLA_V] / acc[MLA_V:MLA_V + 1]).T.astype(o_ref.dtype)

    def middle(i):
        def body(j, carry):
            pair(i, 2 * j, False)
            return carry

        lax.fori_loop(1, i, body, 0)
        pair(i, 2 * i, True)

    head(0, True)
    if n_q > 1:
        tail(0)
        head(1, False)

        def outer(i, carry):
            middle(i)
            tail(i)
            head(i + 1, False)
            return carry

        lax.fori_loop(1, n_q - 1, outer, 0)
        middle(n_q - 1)
    tail(n_q - 1)


def _flash(qt, k, vt, heads):
    t = k.shape[0]
    n_q, tq = qt.shape[1], qt.shape[3]
    n_kt, tk = vt.shape[1], vt.shape[3]
    assert tq == 2 * tk and tk % CHUNK == 0 and tq % QBLK == 0 and n_q * tq == t
    return pl.pallas_call(
        functools.partial(_flash_kernel, tq=tq, tk=tk),
        grid=(heads,),
        in_specs=[pl.BlockSpec((None, n_q, MLA_HEAD_PAD, tq), lambda hd: (hd, 0, 0, 0)),
                  pl.BlockSpec((t, MLA_HEAD_PAD), lambda hd: (0, hd)),
                  pl.BlockSpec((None, n_kt, V_ROWS, tk), lambda hd: (hd, 0, 0, 0))],
        out_specs=pl.BlockSpec((t, MLA_V), lambda hd: (0, hd)),
        out_shape=jax.ShapeDtypeStruct((t, heads * MLA_V), BF16),
        scratch_shapes=[pltpu.VMEM((1, tq), F32), pltpu.VMEM((V_ROWS, tq), F32),
                        pltpu.VMEM((2, tk, tq), F32), pltpu.VMEM((2, 1, tq), F32),
                        pltpu.VMEM((2, tk, tq), BF16), pltpu.VMEM((2, 1, tq), F32)],
        compiler_params=_params("parallel"),
        name="flash",
    )(qt, k, vt)


def _sattn_kernel(q_ref, cc_ref, ck_ref, nc_ref, nk_ref, wuk_ref, o_ref,
                  qabs_ref, qr_ref, m_ref, l_ref, acc_ref, *, heads, nq, tk, n_new):
    past, rank = cc_ref.shape
    n_tiles = past // tk
    for hd in range(heads):
        c0 = hd * MLA_HEAD_PAD
        qn = q_ref[:, c0:c0 + MLA_NOPE]
        qabs_ref[hd * nq:(hd + 1) * nq, :] = _dot(
            qn, wuk_ref[hd * MLA_NOPE:(hd + 1) * MLA_NOPE, :]).astype(qabs_ref.dtype)
        qr_ref[hd * nq:(hd + 1) * nq, :] = q_ref[:, c0 + MLA_NOPE:c0 + MLA_NOPE + MLA_ROPE]
    m_ref[...] = jnp.full_like(m_ref, -jnp.inf)
    l_ref[...] = jnp.zeros_like(l_ref)
    acc_ref[...] = jnp.zeros_like(acc_ref)

    def tile(t):
        if t < n_tiles:
            return cc_ref[t * tk:(t + 1) * tk, :].astype(BF16), ck_ref[t * tk:(t + 1) * tk, :]
        pad = LANE - n_new
        c_new = jnp.concatenate([nc_ref[...], jnp.zeros((pad, rank), F32)], axis=0)
        k_new = jnp.concatenate([nk_ref[...], jnp.zeros((pad, MLA_ROPE), F32)], axis=0)
        return c_new.astype(BF16), k_new

    def key_norm(c, kr):
        n_keys = c.shape[0]
        kn_t = _dot_nt(wuk_ref[...], c)
        ss_t = jnp.sum((kn_t * kn_t).reshape(heads, MLA_NOPE, n_keys), axis=1)
        kr2_t = lax.dot_general(jnp.ones((8, MLA_ROPE), F32), kr * kr, NT_DIMS,
                                preferred_element_type=F32, precision=lax.Precision.HIGHEST)[0:1]
        return lax.rsqrt((ss_t + kr2_t) / MLA_QK_DIM + EPS)

    def attend(c, kr, inv_t, n_valid):
        n_keys = c.shape[0]
        s = _dot_nt(qabs_ref[...], c) + _dot_nt(qr_ref[...], kr.astype(BF16))
        s = s * jnp.broadcast_to(inv_t[:, None, :], (heads, nq, n_keys)).reshape(heads * nq, n_keys)
        if n_valid < n_keys:
            s = jnp.where(lax.broadcasted_iota(jnp.int32, s.shape, 1) < n_valid, s, -jnp.inf)
        m_prev = m_ref[...]
        m_new = jnp.maximum(m_prev, jnp.max(s, axis=-1, keepdims=True))
        alpha = jnp.exp(m_prev - m_new)
        p = jnp.exp(s - m_new)
        l_ref[...] = alpha * l_ref[...] + jnp.sum(p, axis=-1, keepdims=True)
        acc_ref[...] = alpha * acc_ref[...] + _dot(p.astype(BF16), c)
        m_ref[...] = m_new

    c, kr = tile(0)
    inv_t = key_norm(c, kr)
    for t in range(n_tiles + 1):
        if t < n_tiles:
            c_next, kr_next = tile(t + 1)
            inv_next = key_norm(c_next, kr_next)
        attend(c, kr, inv_t, tk if t < n_tiles else n_new)
        if t < n_tiles:
            c, kr, inv_t = c_next, kr_next, inv_next
    o_ref[...] = (acc_ref[...] / l_ref[...]).astype(o_ref.dtype)


def _sattn(q, cache_c, cache_k, new_c, new_k, wuk_t, heads, nq, tk):
    nb, past, rank = cache_c.shape
    return pl.pallas_call(
        functools.partial(_sattn_kernel, heads=heads, nq=nq, tk=tk, n_new=nq),
        grid=(nb,),
        in_specs=[pl.BlockSpec((nq, heads * MLA_HEAD_PAD), lambda b: (b, 0)),
                  pl.BlockSpec((None, past, rank), lambda b: (b, 0, 0)),
                  pl.BlockSpec((None, past, MLA_ROPE), lambda b: (b, 0, 0)),
                  pl.BlockSpec((nq, rank), lambda b: (b, 0)),
                  pl.BlockSpec((nq, MLA_ROPE), lambda b: (b, 0)),
                  pl.BlockSpec(wuk_t.shape, lambda b: (0, 0))],
        out_specs=pl.BlockSpec((None, heads * nq, rank), lambda b: (b, 0, 0)),
        out_shape=jax.ShapeDtypeStruct((nb, heads * nq, rank), BF16),
        scratch_shapes=[pltpu.VMEM((heads * nq, rank), BF16), pltpu.VMEM((heads * nq, MLA_ROPE), BF16),
                        pltpu.VMEM((heads * nq, 1), F32), pltpu.VMEM((heads * nq, 1), F32),
                        pltpu.VMEM((heads * nq, rank), F32)],
        compiler_params=_params("parallel"),
        name="sattn",
    )(q, cache_c, cache_k, new_c, new_k, wuk_t)


def _svup_kernel(ol_ref, w_ref, o_ref):
    nb, nq, rank = ol_ref.shape
    o_ref[...] = _dot(ol_ref[...].reshape(nb * nq, rank), w_ref[...]).astype(o_ref.dtype)


def _svup(o_lat, wuv, heads, nq):
    nb, _, rank = o_lat.shape
    return pl.pallas_call(
        _svup_kernel,
        grid=(heads,),
        in_specs=[pl.BlockSpec((nb, nq, rank), lambda hd: (0, hd, 0)),
                  pl.BlockSpec((rank, MLA_V), lambda hd: (0, hd))],
        out_specs=pl.BlockSpec((nb * nq, MLA_V), lambda hd: (0, hd)),
        out_shape=jax.ShapeDtypeStruct((nb * nq, heads * MLA_V), BF16),
        compiler_params=_params("parallel"),
        name="svup",
    )(o_lat, wuv)


def _gla_levels(c):
    leaf = min(c, GLA_LEAF)
    levels = [(leaf, leaf // 2 - 1)]
    g = 2 * leaf
    while g <= c:
        levels.append((g, g // 2 - 1))
        g *= 2
    return levels


def _gla_kernel(*refs, heads, dk, dv, c, has_s0):
    if has_s0:
        q_ref, k_ref, v_ref, la_ref, g_ref, s0_ref, o_ref, sout_ref, st_ref = refs
    else:
        q_ref, k_ref, v_ref, la_ref, g_ref, o_ref, sout_ref, st_ref = refs
    j = pl.program_id(1)
    nj = pl.num_programs(1)

    @pl.when(j == 0)
    def _():
        for hd in range(heads):
            if has_s0:
                st_ref[hd] = s0_ref[hd].T
            else:
                st_ref[hd] = jnp.zeros((dv, dk), F32)

    levels = _gla_levels(c)
    row = lax.broadcasted_iota(jnp.int32, (c, c), 0)
    col = lax.broadcasted_iota(jnp.int32, (c, c), 1)
    masks = []
    for lv, (g, _) in enumerate(levels):
        same = (row // g) == (col // g)
        if lv == 0:
            masks.append(jnp.logical_and(same, col <= row))
        else:
            half = g // 2
            masks.append(jnp.logical_and(same, jnp.logical_and((row // half) % 2 == 1, (col // half) % 2 == 0)))
    rid = lax.broadcasted_iota(jnp.int32, (c, dk), 0)
    gain = g_ref[...]

    for hd in range(heads):
        la = la_ref[:, hd * dk:(hd + 1) * dk]
        b = la
        sh = 1
        while sh < c:
            b = b + jnp.where(rid >= sh, pltpu.roll(b, sh, 0), 0.0)
            sh *= 2
        b_end = b[c - 1:c, :]
        q = q_ref[:, hd * dk:(hd + 1) * dk].astype(F32) * (dk ** -0.5)
        k = k_ref[:, hd * dk:(hd + 1) * dk].astype(F32)
        v = v_ref[:, hd * dv:(hd + 1) * dv]

        a = jnp.zeros((c, c), F32)
        for lv, (g, r) in enumerate(levels):
            ref_rows = jnp.broadcast_to(b.reshape(c // g, g, dk)[:, r:r + 1, :], (c // g, g, dk)).reshape(c, dk)
            d = b - ref_rows
            if lv == 0:
                fq, fk = jnp.exp(d), jnp.exp(-d)
            else:
                fq = fk = jnp.exp(-jnp.abs(d))
            a_lv = _dot_nt((q * fq).astype(BF16), (k * fk).astype(BF16))
            a = jnp.where(masks[lv], a_lv, a)

        st = st_ref[hd]
        o = _dot(a.astype(BF16), v) + _dot_nt((q * jnp.exp(b)).astype(BF16), st.astype(BF16))
        k_end = (k * jnp.exp(b_end - b)).astype(BF16)
        st_ref[hd] = st * jnp.exp(b_end) + lax.dot_general(v, k_end, TN_DIMS, preferred_element_type=F32)
        o_ref[:, hd * dv:(hd + 1) * dv] = (o * _rms_inv(o, dv) * gain).astype(o_ref.dtype)

    @pl.when(j == nj - 1)
    def _():
        for hd in range(heads):
            sout_ref[hd] = st_ref[hd].T


def _gla(zp, la, gain, s0, nb, c, heads, dk, dv):
    t = la.shape[0]
    nj = t // (nb * c)
    gk, gv = heads * dk, heads * dv
    assert gv % gk == 0
    row = lambda bb, j: (bb * nj + j, 0)
    in_specs = [pl.BlockSpec((c, gk), row),
                pl.BlockSpec((c, gk), lambda bb, j: (bb * nj + j, 1)),
                pl.BlockSpec((c, gv), lambda bb, j: (bb * nj + j, 2 * gk // gv)),
                pl.BlockSpec((c, gk), row),
                pl.BlockSpec((1, dv), lambda bb, j: (0, 0))]
    args = [zp, zp, zp, la, gain]
    if s0 is not None:
        in_specs.append(pl.BlockSpec((None, heads, dk, dv), lambda bb, j: (bb, 0, 0, 0)))
        args.append(s0)
    return pl.pallas_call(
        functools.partial(_gla_kernel, heads=heads, dk=dk, dv=dv, c=c, has_s0=s0 is not None),
        grid=(nb, nj),
        in_specs=in_specs,
        out_specs=[pl.BlockSpec((c, gv), row),
                   pl.BlockSpec((None, heads, dk, dv), lambda bb, j: (bb, 0, 0, 0))],
        out_shape=[jax.ShapeDtypeStruct((t, gv), BF16), jax.ShapeDtypeStruct((nb, heads, dk, dv), F32)],
        scratch_shapes=[pltpu.VMEM((heads, dv, dk), F32)],
        compiler_params=_params("parallel", "arbitrary"),
        name="gla",
    )(*args)


def _mix_kernel(om_ref, og_ref, gate_o_ref, gate_m_ref, gate_g_ref, x_ref, w_ref, g_ref, y_ref, h_ref):
    o_gla = og_ref[...].astype(F32) * jax.nn.silu(gate_o_ref[...].astype(F32))
    mixed = (jax.nn.sigmoid(gate_m_ref[...].astype(F32)) * om_ref[...].astype(F32)
             + jax.nn.sigmoid(gate_g_ref[...].astype(F32)) * o_gla)
    y = x_ref[...] + _dot(mixed.astype(BF16), w_ref[...])
    y_ref[...] = y
    h_ref[...] = (y * _rms_inv(y, y.shape[-1]) * g_ref[...]).astype(h_ref.dtype)


def _mix(o_mla, o_gla, zp, gate_block0, x, w_o, g, tm):
    t, d = x.shape
    row = lambda i: (i, 0)
    return pl.pallas_call(
        _mix_kernel,
        grid=(t // tm,),
        in_specs=[pl.BlockSpec((tm, d), row), pl.BlockSpec((tm, d), row),
                  pl.BlockSpec((tm, d), lambda i: (i, gate_block0)),
                  pl.BlockSpec((tm, d), lambda i: (i, gate_block0 + 1)),
                  pl.BlockSpec((tm, d), lambda i: (i, gate_block0 + 2)),
                  pl.BlockSpec((tm, d), row), pl.BlockSpec((d, d), lambda i: (0, 0)),
                  pl.BlockSpec((1, d), lambda i: (0, 0))],
        out_specs=[pl.BlockSpec((tm, d), row), pl.BlockSpec((tm, d), row)],
        out_shape=[jax.ShapeDtypeStruct((t, d), F32), jax.ShapeDtypeStruct((t, d), BF16)],
        compiler_params=_params("parallel"),
        name="mix",
    )(o_mla, o_gla, zp, zp, zp, x, w_o, g)


def _ffn_kernel(h_ref, y_ref, wu_ref, wd_ref, o_ref):
    f = pl.program_id(1)

    @pl.when(f == 0)
    def _():
        o_ref[...] = y_ref[...]

    u = jnp.maximum(_dot(h_ref[...], wu_ref[...]), 0.0)
    o_ref[...] += _dot((u * u).astype(BF16), wd_ref[...])


def _ffn(h, y, w_up, w_down, tm, tf):
    t, d = y.shape
    dff = w_up.shape[1]
    return pl.pallas_call(
        _ffn_kernel,
        grid=(t // tm, dff // tf),
        in_specs=[pl.BlockSpec((tm, d), lambda i, f: (i, 0)), pl.BlockSpec((tm, d), lambda i, f: (i, 0)),
                  pl.BlockSpec((d, tf), lambda i, f: (0, f)), pl.BlockSpec((tf, d), lambda i, f: (f, 0))],
        out_specs=pl.BlockSpec((tm, d), lambda i, f: (i, 0)),
        out_shape=jax.ShapeDtypeStruct((t, d), F32),
        compiler_params=_params("parallel", "arbitrary"),
        name="ffn",
    )(h, y, w_up, w_down)


def _rope_table(pos):
    half = MLA_ROPE // 2
    freqs = jnp.power(ROPE_THETA, -jnp.arange(half, dtype=F32) / half)
    ang = pos[:, None] * freqs[None, :]
    cos, sin = jnp.cos(ang), jnp.sin(ang)
    return jnp.concatenate([cos, cos, -sin, sin], axis=1)


def _swap_halves(w):
    half = w.shape[-1] // 2
    return jnp.concatenate([w[..., half:], w[..., :half]], axis=-1)


def _pick(t, pref):
    while t % pref:
        pref //= 2
    return pref


def kernel(x_prompt, x_sample, cache_mla_ckv, cache_mla_krope, state_gla, norm_mix_g, w_in, mla_q_norm_g,
           mla_w_uq, mla_kv_norm_g, mla_w_ukv, mla_q_gain_nope, mla_q_gain_rope, mla_k_gain_nope,
           mla_k_gain_rope, gla_w_a2, gla_b_a, gla_norm_g, w_o, norm_ffn_g, ffn_w_up, ffn_w_down):
    depth = w_in.shape[0]
    bp, seq, d = x_prompt.shape
    nb, dec_seq, _ = x_sample.shape
    past = cache_mla_ckv.shape[2]
    q_rank, heads = mla_w_uq.shape[1], mla_w_uq.shape[2]
    kv_rank = mla_w_ukv.shape[1]
    gla_heads, dk, dv = state_gla.shape[2], state_gla.shape[3], state_gla.shape[4]
    gate_rank = gla_w_a2.shape[1]
    gk, gv = gla_heads * dk, gla_heads * dv
    assert bp == 1 and heads * MLA_V == d and gv == d and seq % CHUNK == 0

    cs_p = _rope_table(jnp.arange(seq, dtype=F32))
    cs_s = jnp.tile(_rope_table(past + jnp.arange(dec_seq, dtype=F32)), (nb, 1))

    xp = x_prompt.reshape(seq, d)
    xs = x_sample.reshape(nb * dec_seq, d)
    outs = [[] for _ in range(6)]
    for l in range(depth):
        wi = w_in[l]
        pts = np.cumsum([q_rank, kv_rank, MLA_ROPE, gk, gk, gv, gate_rank, gv, d]).tolist()
        w_qkv_lat = wi[:, :pts[1]]
        w_kr = wi[:, pts[1]:pts[2]]
        w_alr = wi[:, pts[5]:pts[6]]
        assert 3 * gate_rank <= LANE
        w_lat = jnp.concatenate(
            [w_qkv_lat, w_kr, _swap_halves(w_kr), w_alr, w_alr, w_alr,
             jnp.zeros((d, LANE - 3 * gate_rank), F32)], axis=1).astype(BF16)
        w_gla = wi[:, pts[2]:pts[5]].astype(BF16)
        w_gate = wi[:, pts[6]:].astype(BF16)
        wa2_hi = gla_w_a2[l].astype(BF16)
        wa2_lo = (gla_w_a2[l] - wa2_hi.astype(F32)).astype(BF16)
        wa2_split = jnp.concatenate(
            [wa2_hi, wa2_lo, wa2_hi, jnp.zeros((LANE - 3 * gate_rank, gk), BF16)], axis=0)

        wq = mla_w_uq[l]
        wq_r = wq[..., MLA_NOPE:]
        w_uq = jnp.concatenate([wq[..., :MLA_NOPE], wq_r, _swap_halves(wq_r)], axis=-1)
        w_uq = w_uq.reshape(q_rank, heads * MLA_HEAD_PAD).astype(BF16)
        w_uq_t = w_uq.T
        wuk = mla_w_ukv[l][..., :MLA_NOPE].reshape(kv_rank, heads * MLA_NOPE).astype(BF16)
        wuk_t = wuk.T
        wuv = mla_w_ukv[l][..., MLA_NOPE:].reshape(kv_rank, heads * MLA_V).astype(BF16)
        wuv_t = wuv.T
        gain = jnp.concatenate([mla_q_gain_nope[l] * mla_k_gain_nope[l],
                                jnp.tile(mla_q_gain_rope[l] * mla_k_gain_rope[l], 2),
                                jnp.zeros((MLA_HEAD_PAD - MLA_QK_DIM,), F32)])[None, :] * MLA_SCALE
        gain_col = gain.T * LOG2E
        wo_b = w_o[l].astype(BF16)
        wup_b = ffn_w_up[l].astype(BF16)
        wdn_b = ffn_w_down[l].astype(BF16)
        g_mix = norm_mix_g[l][None, :]
        g_q = mla_q_norm_g[l][None, :]
        g_kv = mla_kv_norm_g[l][None, :]
        g_gla = gla_norm_g[l][None, :]
        g_ffn = norm_ffn_g[l][None, :]
        ba = gla_b_a[l][None, :]

        def front(x, cs, tm):
            h = _prenorm(x, g_mix, _pick(x.shape[0], 256))
            qlat, ckv, kr, la = _mla_lat(h, w_lat, g_q, g_kv, cs, wa2_split, ba, tm, q_rank, kv_rank, gate_rank)
            z_gla = _proj(h, w_gla, _pick(x.shape[0], 1024), 1024)
            z_gate = _proj(h, w_gate, _pick(x.shape[0], 1024), 1024)
            return qlat, ckv, kr, la, z_gla, z_gate

        def back(x, o_mla, o_gla, z_gate, tm):
            y1, h2 = _mix(o_mla, o_gla, z_gate, 0, x, wo_b, g_ffn, tm)
            return _ffn(h2, y1, wup_b, wdn_b, tm, 1024)

        tile = _pick(seq, 512)
        qlat, ckv, kr, la, z_gla, z_gate = front(xp, cs_p, tile)
        qt = _mla_qt(qlat, w_uq_t, cs_p.T, gain_col, tile, 2 * tile, heads)
        kcat, vt = _mla_kv(ckv, kr, wuk, wuv_t, tile, heads)
        o_mla = _flash(qt, kcat, vt, heads)
        o_gla, st = _gla(z_gla, la, g_gla, None, 1, _pick(seq, 128), gla_heads, dk, dv)
        xp_next = back(xp, o_mla, o_gla, z_gate, tile)
        outs[0].append(ckv.reshape(bp, seq, kv_rank))
        outs[1].append(kr.reshape(bp, seq, MLA_ROPE))
        outs[2].append(st)
        xp = xp_next

        qlat, ckv, kr, la, z_gla, z_gate = front(xs, cs_s, nb * dec_seq)
        qcat = _mla_q(qlat, w_uq, cs_s, gain, nb * dec_seq, heads)
        o_lat = _sattn(qcat, cache_mla_ckv[l], cache_mla_krope[l], ckv, kr, wuk_t, heads, dec_seq, _pick(past, 512))
        o_mla = _svup(o_lat, wuv, heads, dec_seq)
        o_gla, st = _gla(z_gla, la, g_gla, state_gla[l], nb, dec_seq, gla_heads, dk, dv)
        xs_next = back(xs, o_mla, o_gla, z_gate, nb * dec_seq)
        outs[3].append(ckv.reshape(nb, dec_seq, kv_rank))
        outs[4].append(kr.reshape(nb, dec_seq, MLA_ROPE))
        outs[5].append(st)
        xs = xs_next

    return (xp.reshape(bp, seq, d), xs.reshape(nb, dec_seq, d),
            jnp.stack(outs[0]), jnp.stack(outs[1]), jnp.stack(outs[2]),
            jnp.stack(outs[3]), jnp.stack(outs[4]), jnp.stack(outs[5]))
```

```python
import functools

import jax
import jax.numpy as jnp
import numpy as np
from jax import lax
from jax.experimental import pallas as pl
from jax.experimental.pallas import tpu as pltpu

F32 = jnp.float32
BF16 = jnp.bfloat16

EPS = 1e-6
CHUNK = 64
MLA_NOPE = 128
MLA_ROPE = 64
MLA_V = 128
MLA_QK_DIM = MLA_NOPE + MLA_ROPE
MLA_SCALE = MLA_QK_DIM ** -0.5
MLA_HEAD_PAD = 256
V_ROWS = MLA_V + 16
LOG2E = 1.4426950408889634
QBLK = 256
ROPE_THETA = 10000.0
GLA_TAU = 16.0
GLA_LEAF = 32
LANE = 128
VMEM_LIMIT = 56 * 1024 * 1024

NT_DIMS = (((1,), (1,)), ((), ()))
TN_DIMS = (((0,), (0,)), ((), ()))


def _params(*sem, flags=None):
    return pltpu.CompilerParams(dimension_semantics=sem, vmem_limit_bytes=VMEM_LIMIT, flags=flags)


def _dot(a, b):
    return jnp.dot(a, b, preferred_element_type=F32)


def _dot_nt(a, b):
    return lax.dot_general(a, b, NT_DIMS, preferred_element_type=F32)


def _rms_inv(x, n):
    return lax.rsqrt(jnp.sum(x * x, axis=-1, keepdims=True) / n + EPS)


def _prenorm_kernel(x_ref, g_ref, h_ref):
    x = x_ref[...]
    h_ref[...] = (x * _rms_inv(x, x.shape[-1]) * g_ref[...]).astype(h_ref.dtype)


def _prenorm(x, g, tm):
    t, d = x.shape
    return pl.pallas_call(
        _prenorm_kernel,
        grid=(t // tm,),
        in_specs=[pl.BlockSpec((tm, d), lambda i: (i, 0)), pl.BlockSpec((1, d), lambda i: (0, 0))],
        out_specs=pl.BlockSpec((tm, d), lambda i: (i, 0)),
        out_shape=jax.ShapeDtypeStruct((t, d), BF16),
        compiler_params=_params("parallel"),
        name="prenorm",
    )(x, g)


def _mla_lat_kernel(h_ref, w_ref, qg_ref, kvg_ref, cs_ref, wa2_ref, ba_ref,
                    qlat_ref, ckv_ref, kr_ref, la_ref, *, q_rank, kv_rank, gate_rank):
    z = _dot(h_ref[...], w_ref[...])
    q_lat = z[:, :q_rank]
    qlat_ref[...] = (q_lat * _rms_inv(q_lat, q_rank) * qg_ref[...]).astype(qlat_ref.dtype)
    kv_lat = z[:, q_rank:q_rank + kv_rank]
    ckv_ref[...] = kv_lat * _rms_inv(kv_lat, kv_rank) * kvg_ref[...]
    o = q_rank + kv_rank
    rr = z[:, o:o + LANE] * cs_ref[...]
    kr_ref[...] = rr[:, :MLA_ROPE] + rr[:, MLA_ROPE:]
    a3 = z[:, o + LANE:o + 2 * LANE]
    a_hi = a3.astype(BF16)
    a_lo = (a3 - a_hi.astype(F32)).astype(BF16)
    lane = lax.broadcasted_iota(jnp.int32, a3.shape, 1)
    u = _dot(jnp.where(lane < 2 * gate_rank, a_hi, a_lo), wa2_ref[...]) + ba_ref[...]
    log_sig = jnp.minimum(u, 0.0) - jnp.log1p(jnp.exp(-jnp.abs(u)))
    la_ref[...] = log_sig / GLA_TAU


def _mla_lat(h, w_lat, qg, kvg, cs, wa2_split, ba, tm, q_rank, kv_rank, gate_rank):
    t, d = h.shape
    n = w_lat.shape[1]
    gk = wa2_split.shape[1]
    row = lambda i: (i, 0)
    fix = lambda i: (0, 0)
    return pl.pallas_call(
        functools.partial(_mla_lat_kernel, q_rank=q_rank, kv_rank=kv_rank, gate_rank=gate_rank),
        grid=(t // tm,),
        in_specs=[pl.BlockSpec((tm, d), row), pl.BlockSpec((d, n), fix),
                  pl.BlockSpec((1, q_rank), fix), pl.BlockSpec((1, kv_rank), fix),
                  pl.BlockSpec((tm, LANE), row), pl.BlockSpec((LANE, gk), fix), pl.BlockSpec((1, gk), fix)],
        out_specs=[pl.BlockSpec((tm, q_rank), row), pl.BlockSpec((tm, kv_rank), row),
                   pl.BlockSpec((tm, MLA_ROPE), row), pl.BlockSpec((tm, gk), row)],
        out_shape=[jax.ShapeDtypeStruct((t, q_rank), BF16), jax.ShapeDtypeStruct((t, kv_rank), F32),
                   jax.ShapeDtypeStruct((t, MLA_ROPE), F32), jax.ShapeDtypeStruct((t, gk), F32)],
        compiler_params=_params("parallel"),
        name="mla_lat",
    )(h, w_lat, qg, kvg, cs, wa2_split, ba)


def _wcols_kernel(a_ref, b_ref, o_ref, *, shift, n_cols, next_col0):
    j = pl.program_id(0)
    tn = a_ref.shape[1]
    b = b_ref[...]
    col = next_col0 + j * tn + lax.broadcasted_iota(jnp.int32, b.shape, 1)
    src = jnp.concatenate([a_ref[...], jnp.where(col < n_cols, b, 0.0)], axis=1)
    o_ref[...] = pltpu.roll(src, src.shape[1] - shift, 1)[:, :tn].astype(o_ref.dtype)


def _wcols(w, col0, width, tn):
    d, n_cols = w.shape
    base = col0 // LANE * LANE
    shift = col0 - base
    assert shift > 0 and base % tn == 0 and width % tn == 0 and tn % LANE == 0
    return pl.pallas_call(
        functools.partial(_wcols_kernel, shift=shift, n_cols=n_cols, next_col0=base + tn),
        grid=(width // tn,),
        in_specs=[pl.BlockSpec((d, tn), lambda j: (0, base // tn + j)),
                  pl.BlockSpec((d, LANE), lambda j: (0, (base + (j + 1) * tn) // LANE))],
        out_specs=pl.BlockSpec((d, tn), lambda j: (0, j)),
        out_shape=jax.ShapeDtypeStruct((d, width), BF16),
        compiler_params=_params("parallel"),
        name="wcols",
    )(w, w)


def _proj_kernel(h_ref, w_ref, o_ref):
    o_ref[...] = _dot(h_ref[...], w_ref[...]).astype(o_ref.dtype)


def _proj(h, w, tm, tn):
    t, d = h.shape
    n = w.shape[1]
    return pl.pallas_call(
        _proj_kernel,
        grid=(t // tm, n // tn),
        in_specs=[pl.BlockSpec((tm, d), lambda i, j: (i, 0)), pl.BlockSpec((d, tn), lambda i, j: (0, j))],
        out_specs=pl.BlockSpec((tm, tn), lambda i, j: (i, j)),
        out_shape=jax.ShapeDtypeStruct((t, n), BF16),
        compiler_params=_params("parallel", "parallel"),
        name="proj",
    )(h, w)


def _mla_q_kernel(ql_ref, w_ref, cs_ref, gain_ref, q_ref, *, heads):
    ql = ql_ref[...]
    cs = cs_ref[...]
    gain = gain_ref[...]
    lane = lax.broadcasted_iota(jnp.int32, (1, LANE), 1)
    for hd in range(heads):
        c0 = hd * MLA_HEAD_PAD
        z = _dot(ql, w_ref[:, c0:c0 + MLA_HEAD_PAD])
        nope = z[:, :MLA_NOPE]
        rr = z[:, MLA_NOPE:] * cs
        rot = rr + pltpu.roll(rr, MLA_ROPE, 1)
        ss = (jnp.sum(nope * nope, axis=-1, keepdims=True)
              + jnp.sum(jnp.where(lane < MLA_ROPE, rot * rot, 0.0), axis=-1, keepdims=True))
        inv = lax.rsqrt(ss / MLA_QK_DIM + EPS)
        q_ref[:, c0:c0 + MLA_NOPE] = (nope * inv * gain[:, :MLA_NOPE]).astype(q_ref.dtype)
        q_ref[:, c0 + MLA_NOPE:c0 + MLA_HEAD_PAD] = (rot * inv * gain[:, MLA_NOPE:]).astype(q_ref.dtype)


def _mla_q(qlat, w_uq, cs, gain, tm, heads):
    t, r = qlat.shape
    n = heads * MLA_HEAD_PAD
    return pl.pallas_call(
        functools.partial(_mla_q_kernel, heads=heads),
        grid=(t // tm,),
        in_specs=[pl.BlockSpec((tm, r), lambda i: (i, 0)), pl.BlockSpec((r, n), lambda i: (0, 0)),
                  pl.BlockSpec((tm, LANE), lambda i: (i, 0)), pl.BlockSpec((1, MLA_HEAD_PAD), lambda i: (0, 0))],
        out_specs=pl.BlockSpec((tm, n), lambda i: (i, 0)),
        out_shape=jax.ShapeDtypeStruct((t, n), BF16),
        compiler_params=_params("parallel"),
        name="mla_q",
    )(qlat, w_uq, cs, gain)


def _mla_qt_kernel(ql_ref, w_ref, cs_ref, gain_ref, q_ref, *, heads):
    ql = ql_ref[...]
    cs = cs_ref[...]
    gain = gain_ref[...]
    for hd in range(heads):
        r0 = hd * MLA_HEAD_PAD
        z = _dot_nt(w_ref[r0:r0 + MLA_HEAD_PAD, :], ql)
        nope = z[:MLA_NOPE]
        rr = z[MLA_NOPE:] * cs
        rot = rr[:MLA_ROPE] + rr[MLA_ROPE:]
        ss = jnp.sum(nope * nope, axis=0, keepdims=True) + jnp.sum(rot * rot, axis=0, keepdims=True)
        inv = lax.rsqrt(ss / MLA_QK_DIM + EPS)
        q_ref[hd, :MLA_NOPE, :] = (nope * inv * gain[:MLA_NOPE]).astype(q_ref.dtype)
        q_ref[hd, MLA_NOPE:MLA_QK_DIM, :] = (rot * inv * gain[MLA_NOPE:MLA_QK_DIM]).astype(q_ref.dtype)
        q_ref[hd, MLA_QK_DIM:, :] = jnp.zeros((MLA_HEAD_PAD - MLA_QK_DIM, ql.shape[0]), q_ref.dtype)


def _mla_qt(qlat, w_uq_t, cs_t, gain_col, tm, tq, heads):
    t, r = qlat.shape
    n = heads * MLA_HEAD_PAD
    per = tq // tm
    return pl.pallas_call(
        functools.partial(_mla_qt_kernel, heads=heads),
        grid=(t // tm,),
        in_specs=[pl.BlockSpec((tm, r), lambda i: (i, 0)), pl.BlockSpec((n, r), lambda i: (0, 0)),
                  pl.BlockSpec((LANE, tm), lambda i: (0, i)), pl.BlockSpec((MLA_HEAD_PAD, 1), lambda i: (0, 0))],
        out_specs=pl.BlockSpec((heads, None, MLA_HEAD_PAD, tm), lambda i: (0, i // per, 0, i % per)),
        out_shape=jax.ShapeDtypeStruct((heads, t // tq, MLA_HEAD_PAD, tq), BF16),
        compiler_params=_params("parallel"),
        name="mla_qt",
    )(qlat, w_uq_t, cs_t, gain_col)


def _mla_kv_kernel(ckv_ref, kr_ref, wk_ref, wvt_ref, k_ref, vt_ref, *, heads):
    c = ckv_ref[...].astype(BF16)
    kr = kr_ref[...]
    kr2 = jnp.sum(kr * kr, axis=-1, keepdims=True)
    kr_pad = jnp.concatenate([kr, jnp.zeros_like(kr)], axis=1)
    for pair in range(heads // 2):
        z = _dot(c, wk_ref[:, pair * 2 * MLA_NOPE:(pair + 1) * 2 * MLA_NOPE])
        for sub in range(2):
            kn = z[:, sub * MLA_NOPE:(sub + 1) * MLA_NOPE]
            inv = lax.rsqrt((jnp.sum(kn * kn, axis=-1, keepdims=True) + kr2) / MLA_QK_DIM + EPS)
            c0 = (2 * pair + sub) * MLA_HEAD_PAD
            k_ref[:, c0:c0 + MLA_NOPE] = (kn * inv).astype(k_ref.dtype)
            k_ref[:, c0 + MLA_NOPE:c0 + MLA_HEAD_PAD] = (kr_pad * inv).astype(k_ref.dtype)
    vt = _dot_nt(wvt_ref[...], c)
    for hd in range(heads):
        vt_ref[hd, :MLA_V, :] = vt[hd * MLA_V:(hd + 1) * MLA_V].astype(vt_ref.dtype)
        vt_ref[hd, MLA_V:, :] = jnp.ones((V_ROWS - MLA_V, vt.shape[1]), vt_ref.dtype)


def _mla_kv(ckv, kr, w_uk, w_uv_t, tm, heads):
    t, r = ckv.shape
    return pl.pallas_call(
        functools.partial(_mla_kv_kernel, heads=heads),
        grid=(t // tm,),
        in_specs=[pl.BlockSpec((tm, r), lambda i: (i, 0)), pl.BlockSpec((tm, MLA_ROPE), lambda i: (i, 0)),
                  pl.BlockSpec((r, heads * MLA_NOPE), lambda i: (0, 0)),
                  pl.BlockSpec((heads * MLA_V, r), lambda i: (0, 0))],
        out_specs=[pl.BlockSpec((tm, heads * MLA_HEAD_PAD), lambda i: (i, 0)),
                   pl.BlockSpec((heads, None, V_ROWS, tm), lambda i: (0, i, 0, 0))],
        out_shape=[jax.ShapeDtypeStruct((t, heads * MLA_HEAD_PAD), BF16),
                   jax.ShapeDtypeStruct((heads, t // tm, V_ROWS, tm), BF16)],
        compiler_params=_params("parallel"),
        name="mla_kv",
    )(ckv, kr, w_uk, w_uv_t)


def _flash_kernel(qt_ref, k_ref, vt_ref, o_ref, m_ref, acc_ref, s_ref, mx_ref, p_ref, al_ref, *, tq, tk):
    n_q = qt_ref.shape[0]
    acc_ref[...] = jnp.zeros_like(acc_ref)

    n_blk = tq // QBLK
    cols_of = [pl.ds(c * QBLK, QBLK) for c in range(n_blk)]

    def diag_mode(u, c):
        k_lo, k_hi = (u * tk) // CHUNK, ((u + 1) * tk - 1) // CHUNK
        q_lo, q_hi = (c * QBLK) // CHUNK, ((c + 1) * QBLK - 1) // CHUNK
        return "all" if k_hi <= q_lo else "none" if k_lo > q_hi else "some"

    def score(i, t, slot, c, mode):
        if mode == "none":
            return
        cols = cols_of[c]
        k = k_ref[pl.ds(pl.multiple_of(t * tk, tk), tk), :]
        s = _dot(k, qt_ref[i, :, cols])
        if mode == "some":
            k_chunk = (t * tk + lax.broadcasted_iota(jnp.int32, (tk, QBLK), 0)) // CHUNK
            q_chunk = (i * tq + c * QBLK + lax.broadcasted_iota(jnp.int32, (tk, QBLK), 1)) // CHUNK
            s = jnp.where(k_chunk <= q_chunk, s, -jnp.inf)
        s_ref[slot, :, cols] = s
        mx_ref[slot, :, cols] = jnp.max(s, axis=0, keepdims=True)

    def soften(slot, c, mode="all", first=False):
        if mode == "none":
            return
        cols = cols_of[c]
        if first:
            m_new = mx_ref[slot, :, cols]
            al_ref[slot, :, cols] = jnp.zeros_like(m_new)
        else:
            m_prev = m_ref[:, cols]
            m_new = jnp.maximum(m_prev, mx_ref[slot, :, cols])
            al_ref[slot, :, cols] = jnp.exp2(m_prev - m_new)
        p_ref[slot, :, cols] = jnp.exp2(s_ref[slot, :, cols] - m_new).astype(BF16)
        m_ref[:, cols] = m_new

    def gather(t, slot, c, mode="all"):
        if mode == "none":
            return
        cols = cols_of[c]
        acc_ref[:, cols] = al_ref[slot, :, cols] * acc_ref[:, cols] + _dot(vt_ref[t], p_ref[slot, :, cols])

    def pair(i, g, diag):
        for c in range(n_blk):
            score(i, g, 0, c, diag_mode(0, c) if diag else "all")
            soften(1, c)
            gather(g - 2, 0, c)
        for c in range(n_blk):
            score(i, g + 1, 1, c, diag_mode(1, c) if diag else "all")
            soften(0, c, diag_mode(0, c) if diag else "all")
            gather(g - 1, 1, c)

    def head(i, diag):
        for c in range(n_blk):
            score(i, 0, 0, c, diag_mode(0, c) if diag else "all")
        for c in range(n_blk):
            score(i, 1, 1, c, diag_mode(1, c) if diag else "all")
            soften(0, c, diag_mode(0, c) if diag else "all", first=True)

    def tail(i):
        for c in range(n_blk):
            soften(1, c, diag_mode(1, c))
            gather(2 * i, 0, c, diag_mode(0, c))
        for c in range(n_blk):
            gather(2 * i + 1, 1, c, diag_mode(1, c))
        acc = acc_ref[...]
        o_ref[pl.ds(pl.multiple_of(i * tq, tq), tq), :] = (acc[:MLA_V] / acc[MLA_V:MLA_V + 1]).T.astype(o_ref.dtype)

    def middle(i):
        def body(j, carry):
            pair(i, 2 * j, False)
            return carry

        lax.fori_loop(1, i, body, 0)
        pair(i, 2 * i, True)

    head(0, True)
    if n_q > 1:
        tail(0)
        head(1, False)

        def outer(i, carry):
            middle(i)
            tail(i)
            head(i + 1, False)
            return carry

        lax.fori_loop(1, n_q - 1, outer, 0)
        middle(n_q - 1)
    tail(n_q - 1)


def _flash(qt, k, vt, heads):
    t = k.shape[0]
    n_q, tq = qt.shape[1], qt.shape[3]
    n_kt, tk = vt.shape[1], vt.shape[3]
    assert tq == 2 * tk and tk % CHUNK == 0 and tq % QBLK == 0 and n_q * tq == t
    return pl.pallas_call(
        functools.partial(_flash_kernel, tq=tq, tk=tk),
        grid=(heads,),
        in_specs=[pl.BlockSpec((None, n_q, MLA_HEAD_PAD, tq), lambda hd: (hd, 0, 0, 0)),
                  pl.BlockSpec((t, MLA_HEAD_PAD), lambda hd: (0, hd)),
                  pl.BlockSpec((None, n_kt, V_ROWS, tk), lambda hd: (hd, 0, 0, 0))],
        out_specs=pl.BlockSpec((t, MLA_V), lambda hd: (0, hd)),
        out_shape=jax.ShapeDtypeStruct((t, heads * MLA_V), BF16),
        scratch_shapes=[pltpu.VMEM((1, tq), F32), pltpu.VMEM((V_ROWS, tq), F32),
                        pltpu.VMEM((2, tk, tq), F32), pltpu.VMEM((2, 1, tq), F32),
                        pltpu.VMEM((2, tk, tq), BF16), pltpu.VMEM((2, 1, tq), F32)],
        compiler_params=_params("parallel"),
        name="flash",
    )(qt, k, vt)


def _sattn_kernel(q_ref, cc_ref, ck_ref, nc_ref, nk_ref, wuk_ref, o_ref,
                  qabs_ref, qr_ref, m_ref, l_ref, acc_ref, *, heads, nq, tk, n_new):
    past, rank = cc_ref.shape
    n_tiles = past // tk
    for hd in range(heads):
        c0 = hd * MLA_HEAD_PAD
        qn = q_ref[:, c0:c0 + MLA_NOPE]
        qabs_ref[hd * nq:(hd + 1) * nq, :] = _dot(
            qn, wuk_ref[hd * MLA_NOPE:(hd + 1) * MLA_NOPE, :]).astype(qabs_ref.dtype)
        qr_ref[hd * nq:(hd + 1) * nq, :] = q_ref[:, c0 + MLA_NOPE:c0 + MLA_NOPE + MLA_ROPE]
    m_ref[...] = jnp.full_like(m_ref, -jnp.inf)
    l_ref[...] = jnp.zeros_like(l_ref)
    acc_ref[...] = jnp.zeros_like(acc_ref)

    def tile(t):
        if t < n_tiles:
            return cc_ref[t * tk:(t + 1) * tk, :].astype(BF16), ck_ref[t * tk:(t + 1) * tk, :]
        pad = LANE - n_new
        c_new = jnp.concatenate([nc_ref[...], jnp.zeros((pad, rank), F32)], axis=0)
        k_new = jnp.concatenate([nk_ref[...], jnp.zeros((pad, MLA_ROPE), F32)], axis=0)
        return c_new.astype(BF16), k_new

    def key_norm(c, kr):
        n_keys = c.shape[0]
        kn_t = _dot_nt(wuk_ref[...], c)
        ss_t = jnp.sum((kn_t * kn_t).reshape(heads, MLA_NOPE, n_keys), axis=1)
        kr2_t = lax.dot_general(jnp.ones((8, MLA_ROPE), F32), kr * kr, NT_DIMS,
                                preferred_element_type=F32, precision=lax.Precision.HIGHEST)[0:1]
        return lax.rsqrt((ss_t + kr2_t) / MLA_QK_DIM + EPS)

    def attend(c, kr, inv_t, n_valid):
        n_keys = c.shape[0]
        s = _dot_nt(qabs_ref[...], c) + _dot_nt(qr_ref[...], kr.astype(BF16))
        s = s * jnp.broadcast_to(inv_t[:, None, :], (heads, nq, n_keys)).reshape(heads * nq, n_keys)
        if n_valid < n_keys:
            s = jnp.where(lax.broadcasted_iota(jnp.int32, s.shape, 1) < n_valid, s, -jnp.inf)
        m_prev = m_ref[...]
        m_new = jnp.maximum(m_prev, jnp.max(s, axis=-1, keepdims=True))
        alpha = jnp.exp(m_prev - m_new)
        p = jnp.exp(s - m_new)
        l_ref[...] = alpha * l_ref[...] + jnp.sum(p, axis=-1, keepdims=True)
        acc_ref[...] = alpha * acc_ref[...] + _dot(p.astype(BF16), c)
        m_ref[...] = m_new

    c, kr = tile(0)
    inv_t = key_norm(c, kr)
    for t in range(n_tiles + 1):
        if t < n_tiles:
            c_next, kr_next = tile(t + 1)
            inv_next = key_norm(c_next, kr_next)
        attend(c, kr, inv_t, tk if t < n_tiles else n_new)
        if t < n_tiles:
            c, kr, inv_t = c_next, kr_next, inv_next
    o_ref[...] = (acc_ref[...] / l_ref[...]).astype(o_ref.dtype)


def _sattn(q, cache_c, cache_k, new_c, new_k, wuk_t, heads, nq, tk):
    nb, past, rank = cache_c.shape
    return pl.pallas_call(
        functools.partial(_sattn_kernel, heads=heads, nq=nq, tk=tk, n_new=nq),
        grid=(nb,),
        in_specs=[pl.BlockSpec((nq, heads * MLA_HEAD_PAD), lambda b: (b, 0)),
                  pl.BlockSpec((None, past, rank), lambda b: (b, 0, 0)),
                  pl.BlockSpec((None, past, MLA_ROPE), lambda b: (b, 0, 0)),
                  pl.BlockSpec((nq, rank), lambda b: (b, 0)),
                  pl.BlockSpec((nq, MLA_ROPE), lambda b: (b, 0)),
                  pl.BlockSpec(wuk_t.shape, lambda b: (0, 0))],
        out_specs=pl.BlockSpec((None, heads * nq, rank), lambda b: (b, 0, 0)),
        out_shape=jax.ShapeDtypeStruct((nb, heads * nq, rank), BF16),
        scratch_shapes=[pltpu.VMEM((heads * nq, rank), BF16), pltpu.VMEM((heads * nq, MLA_ROPE), BF16),
                        pltpu.VMEM((heads * nq, 1), F32), pltpu.VMEM((heads * nq, 1), F32),
                        pltpu.VMEM((heads * nq, rank), F32)],
        compiler_params=_params("parallel"),
        name="sattn",
    )(q, cache_c, cache_k, new_c, new_k, wuk_t)


def _svup_kernel(ol_ref, w_ref, o_ref):
    nb, nq, rank = ol_ref.shape
    o_ref[...] = _dot(ol_ref[...].reshape(nb * nq, rank), w_ref[...]).astype(o_ref.dtype)


def _svup(o_lat, wuv, heads, nq):
    nb, _, rank = o_lat.shape
    return pl.pallas_call(
        _svup_kernel,
        grid=(heads,),
        in_specs=[pl.BlockSpec((nb, nq, rank), lambda hd: (0, hd, 0)),
                  pl.BlockSpec((rank, MLA_V), lambda hd: (0, hd))],
        out_specs=pl.BlockSpec((nb * nq, MLA_V), lambda hd: (0, hd)),
        out_shape=jax.ShapeDtypeStruct((nb * nq, heads * MLA_V), BF16),
        compiler_params=_params("parallel"),
        name="svup",
    )(o_lat, wuv)


def _gla_levels(c):
    leaf = min(c, GLA_LEAF)
    levels = [(leaf, leaf // 2 - 1)]
    g = 2 * leaf
    while g <= c:
        levels.append((g, g // 2 - 1))
        g *= 2
    return levels


def _gla_kernel(*refs, heads, dk, dv, c, has_s0):
    if has_s0:
        q_ref, k_ref, v_ref, la_ref, g_ref, s0_ref, o_ref, sout_ref, st_ref = refs
    else:
        q_ref, k_ref, v_ref, la_ref, g_ref, o_ref, sout_ref, st_ref = refs
    j = pl.program_id(1)
    nj = pl.num_programs(1)

    @pl.when(j == 0)
    def _():
        for hd in range(heads):
            if has_s0:
                st_ref[hd] = s0_ref[hd].T
            else:
                st_ref[hd] = jnp.zeros((dv, dk), F32)

    levels = _gla_levels(c)
    row = lax.broadcasted_iota(jnp.int32, (c, c), 0)
    col = lax.broadcasted_iota(jnp.int32, (c, c), 1)
    masks = []
    for lv, (g, _) in enumerate(levels):
        same = (row // g) == (col // g)
        if lv == 0:
            masks.append(jnp.logical_and(same, col <= row))
        else:
            half = g // 2
            masks.append(jnp.logical_and(same, jnp.logical_and((row // half) % 2 == 1, (col // half) % 2 == 0)))
    rid = lax.broadcasted_iota(jnp.int32, (c, dk), 0)
    gain = g_ref[...]

    for hd in range(heads):
        la = la_ref[:, hd * dk:(hd + 1) * dk]
        b = la
        sh = 1
        while sh < c:
            b = b + jnp.where(rid >= sh, pltpu.roll(b, sh, 0), 0.0)
            sh *= 2
        b_end = b[c - 1:c, :]
        q = q_ref[:, hd * dk:(hd + 1) * dk].astype(F32) * (dk ** -0.5)
        k = k_ref[:, hd * dk:(hd + 1) * dk].astype(F32)
        v = v_ref[:, hd * dv:(hd + 1) * dv]

        a = jnp.zeros((c, c), F32)
        for lv, (g, r) in enumerate(levels):
            ref_rows = jnp.broadcast_to(b.reshape(c // g, g, dk)[:, r:r + 1, :], (c // g, g, dk)).reshape(c, dk)
            d = b - ref_rows
            if lv == 0:
                fq, fk = jnp.exp(d), jnp.exp(-d)
            else:
                fq = fk = jnp.exp(-jnp.abs(d))
            a_lv = _dot_nt((q * fq).astype(BF16), (k * fk).astype(BF16))
            a = jnp.where(masks[lv], a_lv, a)

        st = st_ref[hd]
        o = _dot(a.astype(BF16), v) + _dot_nt((q * jnp.exp(b)).astype(BF16), st.astype(BF16))
        k_end = (k * jnp.exp(b_end - b)).astype(BF16)
        st_ref[hd] = st * jnp.exp(b_end) + lax.dot_general(v, k_end, TN_DIMS, preferred_element_type=F32)
        o_ref[:, hd * dv:(hd + 1) * dv] = (o * _rms_inv(o, dv) * gain).astype(o_ref.dtype)

    @pl.when(j == nj - 1)
    def _():
        for hd in range(heads):
            sout_ref[hd] = st_ref[hd].T


def _gla(zp, la, gain, s0, nb, c, heads, dk, dv):
    t = la.shape[0]
    nj = t // (nb * c)
    gk, gv = heads * dk, heads * dv
    assert gv % gk == 0
    row = lambda bb, j: (bb * nj + j, 0)
    in_specs = [pl.BlockSpec((c, gk), row),
                pl.BlockSpec((c, gk), lambda bb, j: (bb * nj + j, 1)),
                pl.BlockSpec((c, gv), lambda bb, j: (bb * nj + j, 2 * gk // gv)),
                pl.BlockSpec((c, gk), row),
                pl.BlockSpec((1, dv), lambda bb, j: (0, 0))]
    args = [zp, zp, zp, la, gain]
    if s0 is not None:
        in_specs.append(pl.BlockSpec((None, heads, dk, dv), lambda bb, j: (bb, 0, 0, 0)))
        args.append(s0)
    return pl.pallas_call(
        functools.partial(_gla_kernel, heads=heads, dk=dk, dv=dv, c=c, has_s0=s0 is not None),
        grid=(nb, nj),
        in_specs=in_specs,
        out_specs=[pl.BlockSpec((c, gv), row),
                   pl.BlockSpec((None, heads, dk, dv), lambda bb, j: (bb, 0, 0, 0))],
        out_shape=[jax.ShapeDtypeStruct((t, gv), BF16), jax.ShapeDtypeStruct((nb, heads, dk, dv), F32)],
        scratch_shapes=[pltpu.VMEM((heads, dv, dk), F32)],
        compiler_params=_params("parallel", "arbitrary"),
        name="gla",
    )(*args)


def _mix_kernel(om_ref, og_ref, gate_o_ref, gate_m_ref, gate_g_ref, x_ref, w_ref, g_ref, y_ref, h_ref):
    o_gla = og_ref[...].astype(F32) * jax.nn.silu(gate_o_ref[...].astype(F32))
    mixed = (jax.nn.sigmoid(gate_m_ref[...].astype(F32)) * om_ref[...].astype(F32)
             + jax.nn.sigmoid(gate_g_ref[...].astype(F32)) * o_gla)
    y = x_ref[...] + _dot(mixed.astype(BF16), w_ref[...])
    y_ref[...] = y
    h_ref[...] = (y * _rms_inv(y, y.shape[-1]) * g_ref[...]).astype(h_ref.dtype)


def _mix(o_mla, o_gla, zp, gate_block0, x, w_o, g, tm):
    t, d = x.shape
    row = lambda i: (i, 0)
    return pl.pallas_call(
        _mix_kernel,
        grid=(t // tm,),
        in_specs=[pl.BlockSpec((tm, d), row), pl.BlockSpec((tm, d), row),
                  pl.BlockSpec((tm, d), lambda i: (i, gate_block0)),
                  pl.BlockSpec((tm, d), lambda i: (i, gate_block0 + 1)),
                  pl.BlockSpec((tm, d), lambda i: (i, gate_block0 + 2)),
                  pl.BlockSpec((tm, d), row), pl.BlockSpec((d, d), lambda i: (0, 0)),
                  pl.BlockSpec((1, d), lambda i: (0, 0))],
        out_specs=[pl.BlockSpec((tm, d), row), pl.BlockSpec((tm, d), row)],
        out_shape=[jax.ShapeDtypeStruct((t, d), F32), jax.ShapeDtypeStruct((t, d), BF16)],
        compiler_params=_params("parallel"),
        name="mix",
    )(o_mla, o_gla, zp, zp, zp, x, w_o, g)


def _ffn_kernel(h_ref, y_ref, wu_ref, wd_ref, o_ref):
    f = pl.program_id(1)

    @pl.when(f == 0)
    def _():
        o_ref[...] = y_ref[...]

    u = jnp.maximum(_dot(h_ref[...], wu_ref[...]), 0.0)
    o_ref[...] += _dot((u * u).astype(BF16), wd_ref[...])


def _ffn(h, y, w_up, w_down, tm, tf):
    t, d = y.shape
    dff = w_up.shape[1]
    return pl.pallas_call(
        _ffn_kernel,
        grid=(t // tm, dff // tf),
        in_specs=[pl.BlockSpec((tm, d), lambda i, f: (i, 0)), pl.BlockSpec((tm, d), lambda i, f: (i, 0)),
                  pl.BlockSpec((d, tf), lambda i, f: (0, f)), pl.BlockSpec((tf, d), lambda i, f: (f, 0))],
        out_specs=pl.BlockSpec((tm, d), lambda i, f: (i, 0)),
        out_shape=jax.ShapeDtypeStruct((t, d), F32),
        compiler_params=_params("parallel", "arbitrary"),
        name="ffn",
    )(h, y, w_up, w_down)


def _rope_table(pos):
    half = MLA_ROPE // 2
    freqs = jnp.power(ROPE_THETA, -jnp.arange(half, dtype=F32) / half)
    ang = pos[:, None] * freqs[None, :]
    cos, sin = jnp.cos(ang), jnp.sin(ang)
    return jnp.concatenate([cos, cos, -sin, sin], axis=1)


def _swap_halves(w):
    half = w.shape[-1] // 2
    return jnp.concatenate([w[..., half:], w[..., :half]], axis=-1)


def _pick(t, pref):
    while t % pref:
        pref //= 2
    return pref


def kernel(x_prompt, x_sample, cache_mla_ckv, cache_mla_krope, state_gla, norm_mix_g, w_in, mla_q_norm_g,
           mla_w_uq, mla_kv_norm_g, mla_w_ukv, mla_q_gain_nope, mla_q_gain_rope, mla_k_gain_nope,
           mla_k_gain_rope, gla_w_a2, gla_b_a, gla_norm_g, w_o, norm_ffn_g, ffn_w_up, ffn_w_down):
    depth = w_in.shape[0]
    bp, seq, d = x_prompt.shape
    nb, dec_seq, _ = x_sample.shape
    past = cache_mla_ckv.shape[2]
    q_rank, heads = mla_w_uq.shape[1], mla_w_uq.shape[2]
    kv_rank = mla_w_ukv.shape[1]
    gla_heads, dk, dv = state_gla.shape[2], state_gla.shape[3], state_gla.shape[4]
    gate_rank = gla_w_a2.shape[1]
    gk, gv = gla_heads * dk, gla_heads * dv
    assert bp == 1 and heads * MLA_V == d and gv == d and seq % CHUNK == 0

    cs_p = _rope_table(jnp.arange(seq, dtype=F32))
    cs_s = jnp.tile(_rope_table(past + jnp.arange(dec_seq, dtype=F32)), (nb, 1))

    xp = x_prompt.reshape(seq, d)
    xs = x_sample.reshape(nb * dec_seq, d)
    outs = [[] for _ in range(6)]
    for l in range(depth):
        wi = w_in[l]
        pts = np.cumsum([q_rank, kv_rank, MLA_ROPE, gk, gk, gv, gate_rank, gv, d]).tolist()
        w_qkv_lat = wi[:, :pts[1]]
        w_kr = wi[:, pts[1]:pts[2]]
        w_alr = wi[:, pts[5]:pts[6]]
        assert 3 * gate_rank <= LANE
        w_lat = jnp.concatenate(
            [w_qkv_lat, w_kr, _swap_halves(w_kr), w_alr, w_alr, w_alr,
             jnp.zeros((d, LANE - 3 * gate_rank), F32)], axis=1).astype(BF16)
        w_gla = _wcols(wi, pts[2], pts[5] - pts[2], 512)
        w_gate = _wcols(wi, pts[6], wi.shape[1] - pts[6], 512)
        wa2_hi = gla_w_a2[l].astype(BF16)
        wa2_lo = (gla_w_a2[l] - wa2_hi.astype(F32)).astype(BF16)
        wa2_split = jnp.concatenate(
            [wa2_hi, wa2_lo, wa2_hi, jnp.zeros((LANE - 3 * gate_rank, gk), BF16)], axis=0)

        wq = mla_w_uq[l]
        wq_r = wq[..., MLA_NOPE:]
        w_uq = jnp.concatenate([wq[..., :MLA_NOPE], wq_r, _swap_halves(wq_r)], axis=-1)
        w_uq = w_uq.reshape(q_rank, heads * MLA_HEAD_PAD).astype(BF16)
        w_uq_t = w_uq.T
        wuk = mla_w_ukv[l][..., :MLA_NOPE].reshape(kv_rank, heads * MLA_NOPE).astype(BF16)
        wuk_t = wuk.T
        wuv = mla_w_ukv[l][..., MLA_NOPE:].reshape(kv_rank, heads * MLA_V).astype(BF16)
        wuv_t = wuv.T
        gain = jnp.concatenate([mla_q_gain_nope[l] * mla_k_gain_nope[l],
                                jnp.tile(mla_q_gain_rope[l] * mla_k_gain_rope[l], 2),
                                jnp.zeros((MLA_HEAD_PAD - MLA_QK_DIM,), F32)])[None, :] * MLA_SCALE
        gain_col = gain.T * LOG2E
        wo_b = w_o[l].astype(BF16)
        wup_b = ffn_w_up[l].astype(BF16)
        wdn_b = ffn_w_down[l].astype(BF16)
        g_mix = norm_mix_g[l][None, :]
        g_q = mla_q_norm_g[l][None, :]
        g_kv = mla_kv_norm_g[l][None, :]
        g_gla = gla_norm_g[l][None, :]
        g_ffn = norm_ffn_g[l][None, :]
        ba = gla_b_a[l][None, :]

        def front(x, cs, tm):
            h = _prenorm(x, g_mix, _pick(x.shape[0], 256))
            qlat, ckv, kr, la = _mla_lat(h, w_lat, g_q, g_kv, cs, wa2_split, ba, tm, q_rank, kv_rank, gate_rank)
            z_gla = _proj(h, w_gla, _pick(x.shape[0], 1024), 1024)
            z_gate = _proj(h, w_gate, _pick(x.shape[0], 1024), 1024)
            return qlat, ckv, kr, la, z_gla, z_gate

        def back(x, o_mla, o_gla, z_gate, tm):
            y1, h2 = _mix(o_mla, o_gla, z_gate, 0, x, wo_b, g_ffn, tm)
            return _ffn(h2, y1, wup_b, wdn_b, tm, 1024)

        tile = _pick(seq, 512)
        qlat, ckv, kr, la, z_gla, z_gate = front(xp, cs_p, tile)
        qt = _mla_qt(qlat, w_uq_t, cs_p.T, gain_col, tile, 2 * tile, heads)
        kcat, vt = _mla_kv(ckv, kr, wuk, wuv_t, tile, heads)
        o_mla = _flash(qt, kcat, vt, heads)
        o_gla, st = _gla(z_gla, la, g_gla, None, 1, _pick(seq, 128), gla_heads, dk, dv)
        xp_next = back(xp, o_mla, o_gla, z_gate, tile)
        outs[0].append(ckv.reshape(bp, seq, kv_rank))
        outs[1].append(kr.reshape(bp, seq, MLA_ROPE))
        outs[2].append(st)
        xp = xp_next

        qlat, ckv, kr, la, z_gla, z_gate = front(xs, cs_s, nb * dec_seq)
        qcat = _mla_q(qlat, w_uq, cs_s, gain, nb * dec_seq, heads)
        o_lat = _sattn(qcat, cache_mla_ckv[l], cache_mla_krope[l], ckv, kr, wuk_t, heads, dec_seq, _pick(past, 512))
        o_mla = _svup(o_lat, wuv, heads, dec_seq)
        o_gla, st = _gla(z_gla, la, g_gla, state_gla[l], nb, dec_seq, gla_heads, dk, dv)
        xs_next = back(xs, o_mla, o_gla, z_gate, nb * dec_seq)
        outs[3].append(ckv.reshape(nb, dec_seq, kv_rank))
        outs[4].append(kr.reshape(nb, dec_seq, MLA_ROPE))
        outs[5].append(st)
        xs = xs_next

    return (xp.reshape(bp, seq, d), xs.reshape(nb, dec_seq, d),
            jnp.stack(outs[0]), jnp.stack(outs[1]), jnp.stack(outs[2]),
            jnp.stack(outs[3]), jnp.stack(outs[4]), jnp.stack(outs[5]))
```

```python
import functools

import jax
import jax.numpy as jnp
import numpy as np
from jax import lax
from jax.experimental import pallas as pl
from jax.experimental.pallas import tpu as pltpu

F32 = jnp.float32
BF16 = jnp.bfloat16

EPS = 1e-6
CHUNK = 64
MLA_NOPE = 128
MLA_ROPE = 64
MLA_V = 128
MLA_QK_DIM = MLA_NOPE + MLA_ROPE
MLA_SCALE = MLA_QK_DIM ** -0.5
MLA_HEAD_PAD = 256
V_ROWS = MLA_V + 16
LOG2E = 1.4426950408889634
QBLK = 256
ROPE_THETA = 10000.0
GLA_TAU = 16.0
GLA_LEAF = 32
LANE = 128
VMEM_LIMIT = 56 * 1024 * 1024

NT_DIMS = (((1,), (1,)), ((), ()))
TN_DIMS = (((0,), (0,)), ((), ()))


def _params(*sem, flags=None):
    return pltpu.CompilerParams(dimension_semantics=sem, vmem_limit_bytes=VMEM_LIMIT, flags=flags)


def _dot(a, b):
    return jnp.dot(a, b, preferred_element_type=F32)


def _dot_nt(a, b):
    return lax.dot_general(a, b, NT_DIMS, preferred_element_type=F32)


def _rms_inv(x, n):
    return lax.rsqrt(jnp.sum(x * x, axis=-1, keepdims=True) / n + EPS)


def _prenorm_kernel(x_ref, g_ref, h_ref):
    x = x_ref[...]
    h_ref[...] = (x * _rms_inv(x, x.shape[-1]) * g_ref[...]).astype(h_ref.dtype)


def _prenorm(x, g, tm):
    t, d = x.shape
    return pl.pallas_call(
        _prenorm_kernel,
        grid=(t // tm,),
        in_specs=[pl.BlockSpec((tm, d), lambda i: (i, 0)), pl.BlockSpec((1, d), lambda i: (0, 0))],
        out_specs=pl.BlockSpec((tm, d), lambda i: (i, 0)),
        out_shape=jax.ShapeDtypeStruct((t, d), BF16),
        compiler_params=_params("parallel"),
        name="prenorm",
    )(x, g)


def _mla_lat_kernel(h_ref, w_ref, qg_ref, kvg_ref, cs_ref, wa2_ref, ba_ref,
                    qlat_ref, ckv_ref, kr_ref, la_ref, *, q_rank, kv_rank, gate_rank):
    z = _dot(h_ref[...], w_ref[...])
    q_lat = z[:, :q_rank]
    qlat_ref[...] = (q_lat * _rms_inv(q_lat, q_rank) * qg_ref[...]).astype(qlat_ref.dtype)
    kv_lat = z[:, q_rank:q_rank + kv_rank]
    ckv_ref[...] = kv_lat * _rms_inv(kv_lat, kv_rank) * kvg_ref[...]
    o = q_rank + kv_rank
    rr = z[:, o:o + LANE] * cs_ref[...]
    kr_ref[...] = rr[:, :MLA_ROPE] + rr[:, MLA_ROPE:]
    a3 = z[:, o + LANE:o + 2 * LANE]
    a_hi = a3.astype(BF16)
    a_lo = (a3 - a_hi.astype(F32)).astype(BF16)
    lane = lax.broadcasted_iota(jnp.int32, a3.shape, 1)
    u = _dot(jnp.where(lane < 2 * gate_rank, a_hi, a_lo), wa2_ref[...]) + ba_ref[...]
    log_sig = jnp.minimum(u, 0.0) - jnp.log1p(jnp.exp(-jnp.abs(u)))
    la_ref[...] = log_sig / GLA_TAU


def _mla_lat(h, w_lat, qg, kvg, cs, wa2_split, ba, tm, q_rank, kv_rank, gate_rank):
    t, d = h.shape
    n = w_lat.shape[1]
    gk = wa2_split.shape[1]
    row = lambda i: (i, 0)
    fix = lambda i: (0, 0)
    return pl.pallas_call(
        functools.partial(_mla_lat_kernel, q_rank=q_rank, kv_rank=kv_rank, gate_rank=gate_rank),
        grid=(t // tm,),
        in_specs=[pl.BlockSpec((tm, d), row), pl.BlockSpec((d, n), fix),
                  pl.BlockSpec((1, q_rank), fix), pl.BlockSpec((1, kv_rank), fix),
                  pl.BlockSpec((tm, LANE), row), pl.BlockSpec((LANE, gk), fix), pl.BlockSpec((1, gk), fix)],
        out_specs=[pl.BlockSpec((tm, q_rank), row), pl.BlockSpec((tm, kv_rank), row),
                   pl.BlockSpec((tm, MLA_ROPE), row), pl.BlockSpec((tm, gk), row)],
        out_shape=[jax.ShapeDtypeStruct((t, q_rank), BF16), jax.ShapeDtypeStruct((t, kv_rank), F32),
                   jax.ShapeDtypeStruct((t, MLA_ROPE), F32), jax.ShapeDtypeStruct((t, gk), F32)],
        compiler_params=_params("parallel"),
        name="mla_lat",
    )(h, w_lat, qg, kvg, cs, wa2_split, ba)


def _proj_kernel(h_ref, w_ref, o_ref):
    o_ref[...] = _dot(h_ref[...], w_ref[...]).astype(o_ref.dtype)


def _proj(h, w, tm, tn):
    t, d = h.shape
    n = w.shape[1]
    return pl.pallas_call(
        _proj_kernel,
        grid=(t // tm, n // tn),
        in_specs=[pl.BlockSpec((tm, d), lambda i, j: (i, 0)), pl.BlockSpec((d, tn), lambda i, j: (0, j))],
        out_specs=pl.BlockSpec((tm, tn), lambda i, j: (i, j)),
        out_shape=jax.ShapeDtypeStruct((t, n), BF16),
        compiler_params=_params("parallel", "parallel"),
        name="proj",
    )(h, w)


def _mla_q_kernel(ql_ref, w_ref, cs_ref, gain_ref, q_ref, *, heads):
    ql = ql_ref[...]
    cs = cs_ref[...]
    gain = gain_ref[...]
    lane = lax.broadcasted_iota(jnp.int32, (1, LANE), 1)
    for hd in range(heads):
        c0 = hd * MLA_HEAD_PAD
        z = _dot(ql, w_ref[:, c0:c0 + MLA_HEAD_PAD])
        nope = z[:, :MLA_NOPE]
        rr = z[:, MLA_NOPE:] * cs
        rot = rr + pltpu.roll(rr, MLA_ROPE, 1)
        ss = (jnp.sum(nope * nope, axis=-1, keepdims=True)
              + jnp.sum(jnp.where(lane < MLA_ROPE, rot * rot, 0.0), axis=-1, keepdims=True))
        inv = lax.rsqrt(ss / MLA_QK_DIM + EPS)
        q_ref[:, c0:c0 + MLA_NOPE] = (nope * inv * gain[:, :MLA_NOPE]).astype(q_ref.dtype)
        q_ref[:, c0 + MLA_NOPE:c0 + MLA_HEAD_PAD] = (rot * inv * gain[:, MLA_NOPE:]).astype(q_ref.dtype)


def _mla_q(qlat, w_uq, cs, gain, tm, heads):
    t, r = qlat.shape
    n = heads * MLA_HEAD_PAD
    return pl.pallas_call(
        functools.partial(_mla_q_kernel, heads=heads),
        grid=(t // tm,),
        in_specs=[pl.BlockSpec((tm, r), lambda i: (i, 0)), pl.BlockSpec((r, n), lambda i: (0, 0)),
                  pl.BlockSpec((tm, LANE), lambda i: (i, 0)), pl.BlockSpec((1, MLA_HEAD_PAD), lambda i: (0, 0))],
        out_specs=pl.BlockSpec((tm, n), lambda i: (i, 0)),
        out_shape=jax.ShapeDtypeStruct((t, n), BF16),
        compiler_params=_params("parallel"),
        name="mla_q",
    )(qlat, w_uq, cs, gain)


def _mla_qt_kernel(ql_ref, w_ref, cs_ref, gain_ref, q_ref, *, heads):
    ql = ql_ref[...]
    cs = cs_ref[...]
    gain = gain_ref[...]
    for hd in range(heads):
        r0 = hd * MLA_HEAD_PAD
        z = _dot_nt(w_ref[r0:r0 + MLA_HEAD_PAD, :], ql)
        nope = z[:MLA_NOPE]
        rr = z[MLA_NOPE:] * cs
        rot = rr[:MLA_ROPE] + rr[MLA_ROPE:]
        ss = jnp.sum(nope * nope, axis=0, keepdims=True) + jnp.sum(rot * rot, axis=0, keepdims=True)
        inv = lax.rsqrt(ss / MLA_QK_DIM + EPS)
        qn = (nope * inv * gain[:MLA_NOPE]).astype(q_ref.dtype)
        qr = (rot * inv * gain[MLA_NOPE:MLA_QK_DIM]).astype(q_ref.dtype)
        for bb in range(q_ref.shape[1]):
            q_ref[hd, bb, :MLA_NOPE, :] = qn[:, bb * QBLK:(bb + 1) * QBLK]
            q_ref[hd, bb, MLA_NOPE:MLA_QK_DIM, :] = qr[:, bb * QBLK:(bb + 1) * QBLK]
            q_ref[hd, bb, MLA_QK_DIM:, :] = jnp.zeros((MLA_HEAD_PAD - MLA_QK_DIM, QBLK), q_ref.dtype)


def _mla_qt(qlat, w_uq_t, cs_t, gain_col, tm, tq, heads):
    t, r = qlat.shape
    n = heads * MLA_HEAD_PAD
    per = tq // tm
    assert tm % QBLK == 0 and tq % tm == 0
    return pl.pallas_call(
        functools.partial(_mla_qt_kernel, heads=heads),
        grid=(t // tm,),
        in_specs=[pl.BlockSpec((tm, r), lambda i: (i, 0)), pl.BlockSpec((n, r), lambda i: (0, 0)),
                  pl.BlockSpec((LANE, tm), lambda i: (0, i)), pl.BlockSpec((MLA_HEAD_PAD, 1), lambda i: (0, 0))],
        out_specs=pl.BlockSpec((heads, None, tm // QBLK, MLA_HEAD_PAD, QBLK),
                               lambda i: (0, i // per, i % per, 0, 0)),
        out_shape=jax.ShapeDtypeStruct((heads, t // tq, tq // QBLK, MLA_HEAD_PAD, QBLK), BF16),
        compiler_params=_params("parallel"),
        name="mla_qt",
    )(qlat, w_uq_t, cs_t, gain_col)


def _mla_kv_kernel(ckv_ref, kr_ref, wk_ref, wvt_ref, k_ref, vt_ref, *, heads):
    c = ckv_ref[...].astype(BF16)
    kr = kr_ref[...]
    kr2 = jnp.sum(kr * kr, axis=-1, keepdims=True)
    kr_pad = jnp.concatenate([kr, jnp.zeros_like(kr)], axis=1)
    for pair in range(heads // 2):
        z = _dot(c, wk_ref[:, pair * 2 * MLA_NOPE:(pair + 1) * 2 * MLA_NOPE])
        for sub in range(2):
            kn = z[:, sub * MLA_NOPE:(sub + 1) * MLA_NOPE]
            inv = lax.rsqrt((jnp.sum(kn * kn, axis=-1, keepdims=True) + kr2) / MLA_QK_DIM + EPS)
            c0 = (2 * pair + sub) * MLA_HEAD_PAD
            k_ref[:, c0:c0 + MLA_NOPE] = (kn * inv).astype(k_ref.dtype)
            k_ref[:, c0 + MLA_NOPE:c0 + MLA_HEAD_PAD] = (kr_pad * inv).astype(k_ref.dtype)
    vt = _dot_nt(wvt_ref[...], c)
    for hd in range(heads):
        vt_ref[hd, :MLA_V, :] = vt[hd * MLA_V:(hd + 1) * MLA_V].astype(vt_ref.dtype)
        vt_ref[hd, MLA_V:, :] = jnp.ones((V_ROWS - MLA_V, vt.shape[1]), vt_ref.dtype)


def _mla_kv(ckv, kr, w_uk, w_uv_t, tm, heads):
    t, r = ckv.shape
    return pl.pallas_call(
        functools.partial(_mla_kv_kernel, heads=heads),
        grid=(t // tm,),
        in_specs=[pl.BlockSpec((tm, r), lambda i: (i, 0)), pl.BlockSpec((tm, MLA_ROPE), lambda i: (i, 0)),
                  pl.BlockSpec((r, heads * MLA_NOPE), lambda i: (0, 0)),
                  pl.BlockSpec((heads * MLA_V, r), lambda i: (0, 0))],
        out_specs=[pl.BlockSpec((tm, heads * MLA_HEAD_PAD), lambda i: (i, 0)),
                   pl.BlockSpec((heads, None, V_ROWS, tm), lambda i: (0, i, 0, 0))],
        out_shape=[jax.ShapeDtypeStruct((t, heads * MLA_HEAD_PAD), BF16),
                   jax.ShapeDtypeStruct((heads, t // tm, V_ROWS, tm), BF16)],
        compiler_params=_params("parallel"),
        name="mla_kv",
    )(ckv, kr, w_uk, w_uv_t)


def _flash_kernel(qt_ref, k_ref, vt_ref, o_ref, m_ref, acc_ref, s_ref, mx_ref, p_ref, al_ref, *, tq, tk):
    n_q, n_blk = qt_ref.shape[0], qt_ref.shape[1]
    acc_ref[...] = jnp.zeros_like(acc_ref)


    def diag_mode(u, c):
        k_lo, k_hi = (u * tk) // CHUNK, ((u + 1) * tk - 1) // CHUNK
        q_lo, q_hi = (c * QBLK) // CHUNK, ((c + 1) * QBLK - 1) // CHUNK
        return "all" if k_hi <= q_lo else "none" if k_lo > q_hi else "some"

    def score(i, t, slot, c, mode):
        if mode == "none":
            return
        k = k_ref[pl.ds(pl.multiple_of(t * tk, tk), tk), :]
        s = _dot(k, qt_ref[i, c])
        if mode == "some":
            k_chunk = (t * tk + lax.broadcasted_iota(jnp.int32, (tk, QBLK), 0)) // CHUNK
            q_chunk = (i * tq + c * QBLK + lax.broadcasted_iota(jnp.int32, (tk, QBLK), 1)) // CHUNK
            s = jnp.where(k_chunk <= q_chunk, s, -jnp.inf)
        s_ref[slot, c] = s
        mx_ref[slot, c] = jnp.max(s, axis=0, keepdims=True)

    def soften(slot, c, mode="all", first=False):
        if mode == "none":
            return
        if first:
            m_new = mx_ref[slot, c]
            al_ref[slot, c] = jnp.zeros_like(m_new)
        else:
            m_prev = m_ref[c]
            m_new = jnp.maximum(m_prev, mx_ref[slot, c])
            al_ref[slot, c] = jnp.exp2(m_prev - m_new)
        p_ref[slot, c] = jnp.exp2(s_ref[slot, c] - m_new).astype(BF16)
        m_ref[c] = m_new

    def gather(t, slot, c, mode="all"):
        if mode == "none":
            return
        acc_ref[c] = al_ref[slot, c] * acc_ref[c] + _dot(vt_ref[t], p_ref[slot, c])

    def pair(i, g, diag):
        for c in range(n_blk):
            score(i, g, 0, c, diag_mode(0, c) if diag else "all")
            soften(1, c)
            gather(g - 2, 0, c)
        for c in range(n_blk):
            score(i, g + 1, 1, c, diag_mode(1, c) if diag else "all")
            soften(0, c, diag_mode(0, c) if diag else "all")
            gather(g - 1, 1, c)

    def head(i, diag):
        for c in range(n_blk):
            score(i, 0, 0, c, diag_mode(0, c) if diag else "all")
        for c in range(n_blk):
            score(i, 1, 1, c, diag_mode(1, c) if diag else "all")
            soften(0, c, diag_mode(0, c) if diag else "all", first=True)

    def tail(i):
        for c in range(n_blk):
            soften(1, c, diag_mode(1, c))
            gather(2 * i, 0, c, diag_mode(0, c))
        for c in range(n_blk):
            gather(2 * i + 1, 1, c, diag_mode(1, c))
        for c in range(n_blk):
            acc = acc_ref[c]
            rows = pl.ds(pl.multiple_of(i * tq + c * QBLK, QBLK), QBLK)
            o_ref[rows, :] = (acc[:MLA_V] / acc[MLA_V:MLA_V + 1]).T.astype(o_ref.dtype)

    def middle(i):
        def body(j, carry):
            pair(i, 2 * j, False)
            return carry

        lax.fori_loop(1, i, body, 0)
        pair(i, 2 * i, True)

    head(0, True)
    if n_q > 1:
        tail(0)
        head(1, False)

        def outer(i, carry):
            middle(i)
            tail(i)
            head(i + 1, False)
            return carry

        lax.fori_loop(1, n_q - 1, outer, 0)
        middle(n_q - 1)
    tail(n_q - 1)


def _flash(qt, k, vt, heads):
    t = k.shape[0]
    n_q, n_blk = qt.shape[1], qt.shape[2]
    tq = n_blk * QBLK
    n_kt, tk = vt.shape[1], vt.shape[3]
    assert tq == 2 * tk and tk % CHUNK == 0 and qt.shape[4] == QBLK and n_q * tq == t
    return pl.pallas_call(
        functools.partial(_flash_kernel, tq=tq, tk=tk),
        grid=(heads,),
        in_specs=[pl.BlockSpec((None, n_q, n_blk, MLA_HEAD_PAD, QBLK), lambda hd: (hd, 0, 0, 0, 0)),
                  pl.BlockSpec((t, MLA_HEAD_PAD), lambda hd: (0, hd)),
                  pl.BlockSpec((None, n_kt, V_ROWS, tk), lambda hd: (hd, 0, 0, 0))],
        out_specs=pl.BlockSpec((t, MLA_V), lambda hd: (0, hd)),
        out_shape=jax.ShapeDtypeStruct((t, heads * MLA_V), BF16),
        scratch_shapes=[pltpu.VMEM((n_blk, 1, QBLK), F32), pltpu.VMEM((n_blk, V_ROWS, QBLK), F32),
                        pltpu.VMEM((2, n_blk, tk, QBLK), F32), pltpu.VMEM((2, n_blk, 1, QBLK), F32),
                        pltpu.VMEM((2, n_blk, tk, QBLK), BF16), pltpu.VMEM((2, n_blk, 1, QBLK), F32)],
        compiler_params=_params("parallel"),
        name="flash",
    )(qt, k, vt)


def _sattn_kernel(q_ref, cc_ref, ck_ref, nc_ref, nk_ref, wuk_ref, o_ref,
                  qabs_ref, qr_ref, m_ref, l_ref, acc_ref, *, heads, nq, tk, n_new):
    past, rank = cc_ref.shape
    n_tiles = past // tk
    for hd in range(heads):
        c0 = hd * MLA_HEAD_PAD
        qn = q_ref[:, c0:c0 + MLA_NOPE]
        qabs_ref[hd * nq:(hd + 1) * nq, :] = _dot(
            qn, wuk_ref[hd * MLA_NOPE:(hd + 1) * MLA_NOPE, :]).astype(qabs_ref.dtype)
        qr_ref[hd * nq:(hd + 1) * nq, :] = q_ref[:, c0 + MLA_NOPE:c0 + MLA_NOPE + MLA_ROPE]
    m_ref[...] = jnp.full_like(m_ref, -jnp.inf)
    l_ref[...] = jnp.zeros_like(l_ref)
    acc_ref[...] = jnp.zeros_like(acc_ref)

    def tile(t):
        if t < n_tiles:
            return cc_ref[t * tk:(t + 1) * tk, :].astype(BF16), ck_ref[t * tk:(t + 1) * tk, :]
        pad = LANE - n_new
        c_new = jnp.concatenate([nc_ref[...], jnp.zeros((pad, rank), F32)], axis=0)
        k_new = jnp.concatenate([nk_ref[...], jnp.zeros((pad, MLA_ROPE), F32)], axis=0)
        return c_new.astype(BF16), k_new

    def key_norm(c, kr):
        n_keys = c.shape[0]
        kn_t = _dot_nt(wuk_ref[...], c)
        ss_t = jnp.sum((kn_t * kn_t).reshape(heads, MLA_NOPE, n_keys), axis=1)
        kr2_t = lax.dot_general(jnp.ones((8, MLA_ROPE), F32), kr * kr, NT_DIMS,
                                preferred_element_type=F32, precision=lax.Precision.HIGHEST)[0:1]
        return lax.rsqrt((ss_t + kr2_t) / MLA_QK_DIM + EPS)

    def attend(c, kr, inv_t, n_valid):
        n_keys = c.shape[0]
        s = _dot_nt(qabs_ref[...], c) + _dot_nt(qr_ref[...], kr.astype(BF16))
        s = s * jnp.broadcast_to(inv_t[:, None, :], (heads, nq, n_keys)).reshape(heads * nq, n_keys)
        if n_valid < n_keys:
            s = jnp.where(lax.broadcasted_iota(jnp.int32, s.shape, 1) < n_valid, s, -jnp.inf)
        m_prev = m_ref[...]
        m_new = jnp.maximum(m_prev, jnp.max(s, axis=-1, keepdims=True))
        alpha = jnp.exp(m_prev - m_new)
        p = jnp.exp(s - m_new)
        l_ref[...] = alpha * l_ref[...] + jnp.sum(p, axis=-1, keepdims=True)
        acc_ref[...] = alpha * acc_ref[...] + _dot(p.astype(BF16), c)
        m_ref[...] = m_new

    c, kr = tile(0)
    inv_t = key_norm(c, kr)
    for t in range(n_tiles + 1):
        if t < n_tiles:
            c_next, kr_next = tile(t + 1)
            inv_next = key_norm(c_next, kr_next)
        attend(c, kr, inv_t, tk if t < n_tiles else n_new)
        if t < n_tiles:
            c, kr, inv_t = c_next, kr_next, inv_next
    o_ref[...] = (acc_ref[...] / l_ref[...]).astype(o_ref.dtype)


def _sattn(q, cache_c, cache_k, new_c, new_k, wuk_t, heads, nq, tk):
    nb, past, rank = cache_c.shape
    return pl.pallas_call(
        functools.partial(_sattn_kernel, heads=heads, nq=nq, tk=tk, n_new=nq),
        grid=(nb,),
        in_specs=[pl.BlockSpec((nq, heads * MLA_HEAD_PAD), lambda b: (b, 0)),
                  pl.BlockSpec((None, past, rank), lambda b: (b, 0, 0)),
                  pl.BlockSpec((None, past, MLA_ROPE), lambda b: (b, 0, 0)),
                  pl.BlockSpec((nq, rank), lambda b: (b, 0)),
                  pl.BlockSpec((nq, MLA_ROPE), lambda b: (b, 0)),
                  pl.BlockSpec(wuk_t.shape, lambda b: (0, 0))],
        out_specs=pl.BlockSpec((None, heads * nq, rank), lambda b: (b, 0, 0)),
        out_shape=jax.ShapeDtypeStruct((nb, heads * nq, rank), BF16),
        scratch_shapes=[pltpu.VMEM((heads * nq, rank), BF16), pltpu.VMEM((heads * nq, MLA_ROPE), BF16),
                        pltpu.VMEM((heads * nq, 1), F32), pltpu.VMEM((heads * nq, 1), F32),
                        pltpu.VMEM((heads * nq, rank), F32)],
        compiler_params=_params("parallel"),
        name="sattn",
    )(q, cache_c, cache_k, new_c, new_k, wuk_t)


def _svup_kernel(ol_ref, w_ref, o_ref):
    nb, nq, rank = ol_ref.shape
    o_ref[...] = _dot(ol_ref[...].reshape(nb * nq, rank), w_ref[...]).astype(o_ref.dtype)


def _svup(o_lat, wuv, heads, nq):
    nb, _, rank = o_lat.shape
    return pl.pallas_call(
        _svup_kernel,
        grid=(heads,),
        in_specs=[pl.BlockSpec((nb, nq, rank), lambda hd: (0, hd, 0)),
                  pl.BlockSpec((rank, MLA_V), lambda hd: (0, hd))],
        out_specs=pl.BlockSpec((nb * nq, MLA_V), lambda hd: (0, hd)),
        out_shape=jax.ShapeDtypeStruct((nb * nq, heads * MLA_V), BF16),
        compiler_params=_params("parallel"),
        name="svup",
    )(o_lat, wuv)


def _gla_levels(c):
    leaf = min(c, GLA_LEAF)
    levels = [(leaf, leaf // 2 - 1)]
    g = 2 * leaf
    while g <= c:
        levels.append((g, g // 2 - 1))
        g *= 2
    return levels


def _gla_kernel(*refs, heads, dk, dv, c, has_s0):
    if has_s0:
        q_ref, k_ref, v_ref, la_ref, g_ref, s0_ref, o_ref, sout_ref, st_ref = refs
    else:
        q_ref, k_ref, v_ref, la_ref, g_ref, o_ref, sout_ref, st_ref = refs
    j = pl.program_id(1)
    nj = pl.num_programs(1)

    @pl.when(j == 0)
    def _():
        for hd in range(heads):
            if has_s0:
                st_ref[hd] = s0_ref[hd].T
            else:
                st_ref[hd] = jnp.zeros((dv, dk), F32)

    levels = _gla_levels(c)
    row = lax.broadcasted_iota(jnp.int32, (c, c), 0)
    col = lax.broadcasted_iota(jnp.int32, (c, c), 1)
    masks = []
    for lv, (g, _) in enumerate(levels):
        same = (row // g) == (col // g)
        if lv == 0:
            masks.append(jnp.logical_and(same, col <= row))
        else:
            half = g // 2
            masks.append(jnp.logical_and(same, jnp.logical_and((row // half) % 2 == 1, (col // half) % 2 == 0)))
    rid = lax.broadcasted_iota(jnp.int32, (c, dk), 0)
    gain = g_ref[...]

    for hd in range(heads):
        la = la_ref[:, hd * dk:(hd + 1) * dk]
        b = la
        sh = 1
        while sh < c:
            b = b + jnp.where(rid >= sh, pltpu.roll(b, sh, 0), 0.0)
            sh *= 2
        b_end = b[c - 1:c, :]
        q = q_ref[:, hd * dk:(hd + 1) * dk].astype(F32) * (dk ** -0.5)
        k = k_ref[:, hd * dk:(hd + 1) * dk].astype(F32)
        v = v_ref[:, hd * dv:(hd + 1) * dv]

        a = jnp.zeros((c, c), F32)
        for lv, (g, r) in enumerate(levels):
            ref_rows = jnp.broadcast_to(b.reshape(c // g, g, dk)[:, r:r + 1, :], (c // g, g, dk)).reshape(c, dk)
            d = b - ref_rows
            if lv == 0:
                fq, fk = jnp.exp(d), jnp.exp(-d)
            else:
                fq = fk = jnp.exp(-jnp.abs(d))
            a_lv = _dot_nt((q * fq).astype(BF16), (k * fk).astype(BF16))
            a = jnp.where(masks[lv], a_lv, a)

        st = st_ref[hd]
        o = _dot(a.astype(BF16), v) + _dot_nt((q * jnp.exp(b)).astype(BF16), st.astype(BF16))
        k_end = (k * jnp.exp(b_end - b)).astype(BF16)
        st_ref[hd] = st * jnp.exp(b_end) + lax.dot_general(v, k_end, TN_DIMS, preferred_element_type=F32)
        o_ref[:, hd * dv:(hd + 1) * dv] = (o * _rms_inv(o, dv) * gain).astype(o_ref.dtype)

    @pl.when(j == nj - 1)
    def _():
        for hd in range(heads):
            sout_ref[hd] = st_ref[hd].T


def _gla(zp, la, gain, s0, nb, c, heads, dk, dv):
    t = la.shape[0]
    nj = t // (nb * c)
    gk, gv = heads * dk, heads * dv
    assert gv % gk == 0
    row = lambda bb, j: (bb * nj + j, 0)
    in_specs = [pl.BlockSpec((c, gk), row),
                pl.BlockSpec((c, gk), lambda bb, j: (bb * nj + j, 1)),
                pl.BlockSpec((c, gv), lambda bb, j: (bb * nj + j, 2 * gk // gv)),
                pl.BlockSpec((c, gk), row),
                pl.BlockSpec((1, dv), lambda bb, j: (0, 0))]
    args = [zp, zp, zp, la, gain]
    if s0 is not None:
        in_specs.append(pl.BlockSpec((None, heads, dk, dv), lambda bb, j: (bb, 0, 0, 0)))
        args.append(s0)
    return pl.pallas_call(
        functools.partial(_gla_kernel, heads=heads, dk=dk, dv=dv, c=c, has_s0=s0 is not None),
        grid=(nb, nj),
        in_specs=in_specs,
        out_specs=[pl.BlockSpec((c, gv), row),
                   pl.BlockSpec((None, heads, dk, dv), lambda bb, j: (bb, 0, 0, 0))],
        out_shape=[jax.ShapeDtypeStruct((t, gv), BF16), jax.ShapeDtypeStruct((nb, heads, dk, dv), F32)],
        scratch_shapes=[pltpu.VMEM((heads, dv, dk), F32)],
        compiler_params=_params("parallel", "arbitrary"),
        name="gla",
    )(*args)


def _mix_kernel(om_ref, og_ref, gate_o_ref, gate_m_ref, gate_g_ref, x_ref, w_ref, g_ref, y_ref, h_ref):
    o_gla = og_ref[...].astype(F32) * jax.nn.silu(gate_o_ref[...].astype(F32))
    mixed = (jax.nn.sigmoid(gate_m_ref[...].astype(F32)) * om_ref[...].astype(F32)
             + jax.nn.sigmoid(gate_g_ref[...].astype(F32)) * o_gla)
    y = x_ref[...] + _dot(mixed.astype(BF16), w_ref[...])
    y_ref[...] = y
    h_ref[...] = (y * _rms_inv(y, y.shape[-1]) * g_ref[...]).astype(h_ref.dtype)


def _mix(o_mla, o_gla, zp, gate_block0, x, w_o, g, tm):
    t, d = x.shape
    row = lambda i: (i, 0)
    return pl.pallas_call(
        _mix_kernel,
        grid=(t // tm,),
        in_specs=[pl.BlockSpec((tm, d), row), pl.BlockSpec((tm, d), row),
                  pl.BlockSpec((tm, d), lambda i: (i, gate_block0)),
                  pl.BlockSpec((tm, d), lambda i: (i, gate_block0 + 1)),
                  pl.BlockSpec((tm, d), lambda i: (i, gate_block0 + 2)),
                  pl.BlockSpec((tm, d), row), pl.BlockSpec((d, d), lambda i: (0, 0)),
                  pl.BlockSpec((1, d), lambda i: (0, 0))],
        out_specs=[pl.BlockSpec((tm, d), row), pl.BlockSpec((tm, d), row)],
        out_shape=[jax.ShapeDtypeStruct((t, d), F32), jax.ShapeDtypeStruct((t, d), BF16)],
        compiler_params=_params("parallel"),
        name="mix",
    )(o_mla, o_gla, zp, zp, zp, x, w_o, g)


def _ffn_kernel(h_ref, y_ref, wu_ref, wd_ref, o_ref):
    f = pl.program_id(1)

    @pl.when(f == 0)
    def _():
        o_ref[...] = y_ref[...]

    u = jnp.maximum(_dot(h_ref[...], wu_ref[...]), 0.0)
    o_ref[...] += _dot((u * u).astype(BF16), wd_ref[...])


def _ffn(h, y, w_up, w_down, tm, tf):
    t, d = y.shape
    dff = w_up.shape[1]
    return pl.pallas_call(
        _ffn_kernel,
        grid=(t // tm, dff // tf),
        in_specs=[pl.BlockSpec((tm, d), lambda i, f: (i, 0)), pl.BlockSpec((tm, d), lambda i, f: (i, 0)),
                  pl.BlockSpec((d, tf), lambda i, f: (0, f)), pl.BlockSpec((tf, d), lambda i, f: (f, 0))],
        out_specs=pl.BlockSpec((tm, d), lambda i, f: (i, 0)),
        out_shape=jax.ShapeDtypeStruct((t, d), F32),
        compiler_params=_params("parallel", "arbitrary"),
        name="ffn",
    )(h, y, w_up, w_down)


def _rope_table(pos):
    half = MLA_ROPE // 2
    freqs = jnp.power(ROPE_THETA, -jnp.arange(half, dtype=F32) / half)
    ang = pos[:, None] * freqs[None, :]
    cos, sin = jnp.cos(ang), jnp.sin(ang)
    return jnp.concatenate([cos, cos, -sin, sin], axis=1)


def _swap_halves(w):
    half = w.shape[-1] // 2
    return jnp.concatenate([w[..., half:], w[..., :half]], axis=-1)


def _pick(t, pref):
    while t % pref:
        pref //= 2
    return pref


def kernel(x_prompt, x_sample, cache_mla_ckv, cache_mla_krope, state_gla, norm_mix_g, w_in, mla_q_norm_g,
           mla_w_uq, mla_kv_norm_g, mla_w_ukv, mla_q_gain_nope, mla_q_gain_rope, mla_k_gain_nope,
           mla_k_gain_rope, gla_w_a2, gla_b_a, gla_norm_g, w_o, norm_ffn_g, ffn_w_up, ffn_w_down):
    depth = w_in.shape[0]
    bp, seq, d = x_prompt.shape
    nb, dec_seq, _ = x_sample.shape
    past = cache_mla_ckv.shape[2]
    q_rank, heads = mla_w_uq.shape[1], mla_w_uq.shape[2]
    kv_rank = mla_w_ukv.shape[1]
    gla_heads, dk, dv = state_gla.shape[2], state_gla.shape[3], state_gla.shape[4]
    gate_rank = gla_w_a2.shape[1]
    gk, gv = gla_heads * dk, gla_heads * dv
    assert bp == 1 and heads * MLA_V == d and gv == d and seq % CHUNK == 0

    cs_p = _rope_table(jnp.arange(seq, dtype=F32))
    cs_s = jnp.tile(_rope_table(past + jnp.arange(dec_seq, dtype=F32)), (nb, 1))

    xp = x_prompt.reshape(seq, d)
    xs = x_sample.reshape(nb * dec_seq, d)
    outs = [[] for _ in range(6)]
    for l in range(depth):
        wi = w_in[l]
        pts = np.cumsum([q_rank, kv_rank, MLA_ROPE, gk, gk, gv, gate_rank, gv, d]).tolist()
        w_qkv_lat = wi[:, :pts[1]]
        w_kr = wi[:, pts[1]:pts[2]]
        w_alr = wi[:, pts[5]:pts[6]]
        assert 3 * gate_rank <= LANE
        w_lat = jnp.concatenate(
            [w_qkv_lat, w_kr, _swap_halves(w_kr), w_alr, w_alr, w_alr,
             jnp.zeros((d, LANE - 3 * gate_rank), F32)], axis=1).astype(BF16)
        w_gla = wi[:, pts[2]:pts[5]].astype(BF16)
        w_gate = wi[:, pts[6]:].astype(BF16)
        wa2_hi = gla_w_a2[l].astype(BF16)
        wa2_lo = (gla_w_a2[l] - wa2_hi.astype(F32)).astype(BF16)
        wa2_split = jnp.concatenate(
            [wa2_hi, wa2_lo, wa2_hi, jnp.zeros((LANE - 3 * gate_rank, gk), BF16)], axis=0)

        wq = mla_w_uq[l]
        wq_r = wq[..., MLA_NOPE:]
        w_uq = jnp.concatenate([wq[..., :MLA_NOPE], wq_r, _swap_halves(wq_r)], axis=-1)
        w_uq = w_uq.reshape(q_rank, heads * MLA_HEAD_PAD).astype(BF16)
        w_uq_t = w_uq.T
        wuk = mla_w_ukv[l][..., :MLA_NOPE].reshape(kv_rank, heads * MLA_NOPE).astype(BF16)
        wuk_t = wuk.T
        wuv = mla_w_ukv[l][..., MLA_NOPE:].reshape(kv_rank, heads * MLA_V).astype(BF16)
        wuv_t = wuv.T
        gain = jnp.concatenate([mla_q_gain_nope[l] * mla_k_gain_nope[l],
                                jnp.tile(mla_q_gain_rope[l] * mla_k_gain_rope[l], 2),
                                jnp.zeros((MLA_HEAD_PAD - MLA_QK_DIM,), F32)])[None, :] * MLA_SCALE
        gain_col = gain.T * LOG2E
        wo_b = w_o[l].astype(BF16)
        wup_b = ffn_w_up[l].astype(BF16)
        wdn_b = ffn_w_down[l].astype(BF16)
        g_mix = norm_mix_g[l][None, :]
        g_q = mla_q_norm_g[l][None, :]
        g_kv = mla_kv_norm_g[l][None, :]
        g_gla = gla_norm_g[l][None, :]
        g_ffn = norm_ffn_g[l][None, :]
        ba = gla_b_a[l][None, :]

        def front(x, cs, tm):
            h = _prenorm(x, g_mix, _pick(x.shape[0], 256))
            qlat, ckv, kr, la = _mla_lat(h, w_lat, g_q, g_kv, cs, wa2_split, ba, tm, q_rank, kv_rank, gate_rank)
            z_gla = _proj(h, w_gla, _pick(x.shape[0], 1024), 1024)
            z_gate = _proj(h, w_gate, _pick(x.shape[0], 1024), 1024)
            return qlat, ckv, kr, la, z_gla, z_gate

        def back(x, o_mla, o_gla, z_gate, tm):
            y1, h2 = _mix(o_mla, o_gla, z_gate, 0, x, wo_b, g_ffn, tm)
            return _ffn(h2, y1, wup_b, wdn_b, tm, 1024)

        tile = _pick(seq, 512)
        qlat, ckv, kr, la, z_gla, z_gate = front(xp, cs_p, tile)
        qt = _mla_qt(qlat, w_uq_t, cs_p.T, gain_col, tile, 2 * tile, heads)
        kcat, vt = _mla_kv(ckv, kr, wuk, wuv_t, tile, heads)
        o_mla = _flash(qt, kcat, vt, heads)
        o_gla, st = _gla(z_gla, la, g_gla, None, 1, _pick(seq, 128), gla_heads, dk, dv)
        xp_next = back(xp, o_mla, o_gla, z_gate, tile)
        outs[0].append(ckv.reshape(bp, seq, kv_rank))
        outs[1].append(kr.reshape(bp, seq, MLA_ROPE))
        outs[2].append(st)
        xp = xp_next

        qlat, ckv, kr, la, z_gla, z_gate = front(xs, cs_s, nb * dec_seq)
        qcat = _mla_q(qlat, w_uq, cs_s, gain, nb * dec_seq, heads)
        o_lat = _sattn(qcat, cache_mla_ckv[l], cache_mla_krope[l], ckv, kr, wuk_t, heads, dec_seq, _pick(past, 512))
        o_mla = _svup(o_lat, wuv, heads, dec_seq)
        o_gla, st = _gla(z_gla, la, g_gla, state_gla[l], nb, dec_seq, gla_heads, dk, dv)
        xs_next = back(xs, o_mla, o_gla, z_gate, nb * dec_seq)
        outs[3].append(ckv.reshape(nb, dec_seq, kv_rank))
        outs[4].append(kr.reshape(nb, dec_seq, MLA_ROPE))
        outs[5].append(st)
        xs = xs_next

    return (xp.reshape(bp, seq, d), xs.reshape(nb, dec_seq, d),
            jnp.stack(outs[0]), jnp.stack(outs[1]), jnp.stack(outs[2]),
            jnp.stack(outs[3]), jnp.stack(outs[4]), jnp.stack(outs[5]))
```

```python
import functools

import jax
import jax.numpy as jnp
import numpy as np
from jax import lax
from jax.experimental import pallas as pl
from jax.experimental.pallas import tpu as pltpu

F32 = jnp.float32
BF16 = jnp.bfloat16

EPS = 1e-6
CHUNK = 64
MLA_NOPE = 128
MLA_ROPE = 64
MLA_V = 128
MLA_QK_DIM = MLA_NOPE + MLA_ROPE
MLA_SCALE = MLA_QK_DIM ** -0.5
MLA_HEAD_PAD = 256
V_ROWS = MLA_V + 16
LOG2E = 1.4426950408889634
QBLK = 256
ROPE_THETA = 10000.0
GLA_TAU = 16.0
GLA_LEAF = 32
LANE = 128
VMEM_LIMIT = 56 * 1024 * 1024

NT_DIMS = (((1,), (1,)), ((), ()))
TN_DIMS = (((0,), (0,)), ((), ()))


def _params(*sem, flags=None):
    return pltpu.CompilerParams(dimension_semantics=sem, vmem_limit_bytes=VMEM_LIMIT, flags=flags)


def _dot(a, b):
    return jnp.dot(a, b, preferred_element_type=F32)


def _dot_nt(a, b):
    return lax.dot_general(a, b, NT_DIMS, preferred_element_type=F32)


def _rms_inv(x, n):
    return lax.rsqrt(jnp.sum(x * x, axis=-1, keepdims=True) / n + EPS)


def _prenorm_kernel(x_ref, g_ref, h_ref):
    x = x_ref[...]
    h_ref[...] = (x * _rms_inv(x, x.shape[-1]) * g_ref[...]).astype(h_ref.dtype)


def _prenorm(x, g, tm):
    t, d = x.shape
    return pl.pallas_call(
        _prenorm_kernel,
        grid=(t // tm,),
        in_specs=[pl.BlockSpec((tm, d), lambda i: (i, 0)), pl.BlockSpec((1, d), lambda i: (0, 0))],
        out_specs=pl.BlockSpec((tm, d), lambda i: (i, 0)),
        out_shape=jax.ShapeDtypeStruct((t, d), BF16),
        compiler_params=_params("parallel"),
        name="prenorm",
    )(x, g)


def _mla_lat_kernel(h_ref, w_ref, qg_ref, kvg_ref, cs_ref, wa2_ref, ba_ref,
                    qlat_ref, ckv_ref, kr_ref, la_ref, *, q_rank, kv_rank, gate_rank):
    z = _dot(h_ref[...], w_ref[...])
    q_lat = z[:, :q_rank]
    qlat_ref[...] = (q_lat * _rms_inv(q_lat, q_rank) * qg_ref[...]).astype(qlat_ref.dtype)
    kv_lat = z[:, q_rank:q_rank + kv_rank]
    ckv_ref[...] = kv_lat * _rms_inv(kv_lat, kv_rank) * kvg_ref[...]
    o = q_rank + kv_rank
    rr = z[:, o:o + LANE] * cs_ref[...]
    kr_ref[...] = rr[:, :MLA_ROPE] + rr[:, MLA_ROPE:]
    a3 = z[:, o + LANE:o + 2 * LANE]
    a_hi = a3.astype(BF16)
    a_lo = (a3 - a_hi.astype(F32)).astype(BF16)
    lane = lax.broadcasted_iota(jnp.int32, a3.shape, 1)
    u = _dot(jnp.where(lane < 2 * gate_rank, a_hi, a_lo), wa2_ref[...]) + ba_ref[...]
    log_sig = jnp.minimum(u, 0.0) - jnp.log1p(jnp.exp(-jnp.abs(u)))
    la_ref[...] = log_sig / GLA_TAU


def _mla_lat(h, w_lat, qg, kvg, cs, wa2_split, ba, tm, q_rank, kv_rank, gate_rank):
    t, d = h.shape
    n = w_lat.shape[1]
    gk = wa2_split.shape[1]
    row = lambda i: (i, 0)
    fix = lambda i: (0, 0)
    return pl.pallas_call(
        functools.partial(_mla_lat_kernel, q_rank=q_rank, kv_rank=kv_rank, gate_rank=gate_rank),
        grid=(t // tm,),
        in_specs=[pl.BlockSpec((tm, d), row), pl.BlockSpec((d, n), fix),
                  pl.BlockSpec((1, q_rank), fix), pl.BlockSpec((1, kv_rank), fix),
                  pl.BlockSpec((tm, LANE), row), pl.BlockSpec((LANE, gk), fix), pl.BlockSpec((1, gk), fix)],
        out_specs=[pl.BlockSpec((tm, q_rank), row), pl.BlockSpec((tm, kv_rank), row),
                   pl.BlockSpec((tm, MLA_ROPE), row), pl.BlockSpec((tm, gk), row)],
        out_shape=[jax.ShapeDtypeStruct((t, q_rank), BF16), jax.ShapeDtypeStruct((t, kv_rank), F32),
                   jax.ShapeDtypeStruct((t, MLA_ROPE), F32), jax.ShapeDtypeStruct((t, gk), F32)],
        compiler_params=_params("parallel"),
        name="mla_lat",
    )(h, w_lat, qg, kvg, cs, wa2_split, ba)


def _proj_kernel(h_ref, w_ref, o_ref, *, gates):
    z = _dot(h_ref[...], w_ref[...])
    if gates:
        z = jax.nn.sigmoid(z) * jnp.where(pl.program_id(1) == 0, z, 1.0)
    o_ref[...] = z.astype(o_ref.dtype)


def _proj(h, w, tm, tn, gates=False):
    t, d = h.shape
    n = w.shape[1]
    return pl.pallas_call(
        functools.partial(_proj_kernel, gates=gates),
        grid=(t // tm, n // tn),
        in_specs=[pl.BlockSpec((tm, d), lambda i, j: (i, 0)), pl.BlockSpec((d, tn), lambda i, j: (0, j))],
        out_specs=pl.BlockSpec((tm, tn), lambda i, j: (i, j)),
        out_shape=jax.ShapeDtypeStruct((t, n), BF16),
        compiler_params=_params("parallel", "parallel"),
        name="proj",
    )(h, w)


def _mla_q_kernel(ql_ref, w_ref, cs_ref, gain_ref, q_ref, *, heads):
    ql = ql_ref[...]
    cs = cs_ref[...]
    gain = gain_ref[...]
    lane = lax.broadcasted_iota(jnp.int32, (1, LANE), 1)
    for hd in range(heads):
        c0 = hd * MLA_HEAD_PAD
        z = _dot(ql, w_ref[:, c0:c0 + MLA_HEAD_PAD])
        nope = z[:, :MLA_NOPE]
        rr = z[:, MLA_NOPE:] * cs
        rot = rr + pltpu.roll(rr, MLA_ROPE, 1)
        ss = (jnp.sum(nope * nope, axis=-1, keepdims=True)
              + jnp.sum(jnp.where(lane < MLA_ROPE, rot * rot, 0.0), axis=-1, keepdims=True))
        inv = lax.rsqrt(ss / MLA_QK_DIM + EPS)
        q_ref[:, c0:c0 + MLA_NOPE] = (nope * inv * gain[:, :MLA_NOPE]).astype(q_ref.dtype)
        q_ref[:, c0 + MLA_NOPE:c0 + MLA_HEAD_PAD] = (rot * inv * gain[:, MLA_NOPE:]).astype(q_ref.dtype)


def _mla_q(qlat, w_uq, cs, gain, tm, heads):
    t, r = qlat.shape
    n = heads * MLA_HEAD_PAD
    return pl.pallas_call(
        functools.partial(_mla_q_kernel, heads=heads),
        grid=(t // tm,),
        in_specs=[pl.BlockSpec((tm, r), lambda i: (i, 0)), pl.BlockSpec((r, n), lambda i: (0, 0)),
                  pl.BlockSpec((tm, LANE), lambda i: (i, 0)), pl.BlockSpec((1, MLA_HEAD_PAD), lambda i: (0, 0))],
        out_specs=pl.BlockSpec((tm, n), lambda i: (i, 0)),
        out_shape=jax.ShapeDtypeStruct((t, n), BF16),
        compiler_params=_params("parallel"),
        name="mla_q",
    )(qlat, w_uq, cs, gain)


def _mla_qt_kernel(ql_ref, w_ref, cs_ref, gain_ref, q_ref, *, heads):
    ql = ql_ref[...]
    cs = cs_ref[...]
    gain = gain_ref[...]
    for hd in range(heads):
        r0 = hd * MLA_HEAD_PAD
        z = _dot_nt(w_ref[r0:r0 + MLA_HEAD_PAD, :], ql)
        nope = z[:MLA_NOPE]
        rr = z[MLA_NOPE:] * cs
        rot = rr[:MLA_ROPE] + rr[MLA_ROPE:]
        ss = jnp.sum(nope * nope, axis=0, keepdims=True) + jnp.sum(rot * rot, axis=0, keepdims=True)
        inv = lax.rsqrt(ss / MLA_QK_DIM + EPS)
        qn = (nope * inv * gain[:MLA_NOPE]).astype(q_ref.dtype)
        qr = (rot * inv * gain[MLA_NOPE:MLA_QK_DIM]).astype(q_ref.dtype)
        for bb in range(q_ref.shape[1]):
            q_ref[hd, bb, :MLA_NOPE, :] = qn[:, bb * QBLK:(bb + 1) * QBLK]
            q_ref[hd, bb, MLA_NOPE:MLA_QK_DIM, :] = qr[:, bb * QBLK:(bb + 1) * QBLK]
            q_ref[hd, bb, MLA_QK_DIM:, :] = jnp.zeros((MLA_HEAD_PAD - MLA_QK_DIM, QBLK), q_ref.dtype)


def _mla_qt(qlat, w_uq_t, cs_t, gain_col, tm, tq, heads):
    t, r = qlat.shape
    n = heads * MLA_HEAD_PAD
    per = tq // tm
    assert tm % QBLK == 0 and tq % tm == 0
    return pl.pallas_call(
        functools.partial(_mla_qt_kernel, heads=heads),
        grid=(t // tm,),
        in_specs=[pl.BlockSpec((tm, r), lambda i: (i, 0)), pl.BlockSpec((n, r), lambda i: (0, 0)),
                  pl.BlockSpec((LANE, tm), lambda i: (0, i)), pl.BlockSpec((MLA_HEAD_PAD, 1), lambda i: (0, 0))],
        out_specs=pl.BlockSpec((heads, None, tm // QBLK, MLA_HEAD_PAD, QBLK),
                               lambda i: (0, i // per, i % per, 0, 0)),
        out_shape=jax.ShapeDtypeStruct((heads, t // tq, tq // QBLK, MLA_HEAD_PAD, QBLK), BF16),
        compiler_params=_params("parallel"),
        name="mla_qt",
    )(qlat, w_uq_t, cs_t, gain_col)


def _mla_kv_kernel(ckv_ref, kr_ref, wk_ref, wvt_ref, k_ref, vt_ref, *, heads):
    c = ckv_ref[...].astype(BF16)
    kr = kr_ref[...]
    kr2 = jnp.sum(kr * kr, axis=-1, keepdims=True)
    kr_pad = jnp.concatenate([kr, jnp.zeros_like(kr)], axis=1)
    for pair in range(heads // 2):
        z = _dot(c, wk_ref[:, pair * 2 * MLA_NOPE:(pair + 1) * 2 * MLA_NOPE])
        for sub in range(2):
            kn = z[:, sub * MLA_NOPE:(sub + 1) * MLA_NOPE]
            inv = lax.rsqrt((jnp.sum(kn * kn, axis=-1, keepdims=True) + kr2) / MLA_QK_DIM + EPS)
            c0 = (2 * pair + sub) * MLA_HEAD_PAD
            k_ref[:, c0:c0 + MLA_NOPE] = (kn * inv).astype(k_ref.dtype)
            k_ref[:, c0 + MLA_NOPE:c0 + MLA_HEAD_PAD] = (kr_pad * inv).astype(k_ref.dtype)
    vt = _dot_nt(wvt_ref[...], c)
    for hd in range(heads):
        vt_ref[hd, :MLA_V, :] = vt[hd * MLA_V:(hd + 1) * MLA_V].astype(vt_ref.dtype)
        vt_ref[hd, MLA_V:, :] = jnp.ones((V_ROWS - MLA_V, vt.shape[1]), vt_ref.dtype)


def _mla_kv(ckv, kr, w_uk, w_uv_t, tm, heads):
    t, r = ckv.shape
    return pl.pallas_call(
        functools.partial(_mla_kv_kernel, heads=heads),
        grid=(t // tm,),
        in_specs=[pl.BlockSpec((tm, r), lambda i: (i, 0)), pl.BlockSpec((tm, MLA_ROPE), lambda i: (i, 0)),
                  pl.BlockSpec((r, heads * MLA_NOPE), lambda i: (0, 0)),
                  pl.BlockSpec((heads * MLA_V, r), lambda i: (0, 0))],
        out_specs=[pl.BlockSpec((tm, heads * MLA_HEAD_PAD), lambda i: (i, 0)),
                   pl.BlockSpec((heads, None, V_ROWS, tm), lambda i: (0, i, 0, 0))],
        out_shape=[jax.ShapeDtypeStruct((t, heads * MLA_HEAD_PAD), BF16),
                   jax.ShapeDtypeStruct((heads, t // tm, V_ROWS, tm), BF16)],
        compiler_params=_params("parallel"),
        name="mla_kv",
    )(ckv, kr, w_uk, w_uv_t)


def _diag_mode(u, c, tk):
    k_lo, k_hi = (u * tk) // CHUNK, ((u + 1) * tk - 1) // CHUNK
    q_lo, q_hi = (c * QBLK) // CHUNK, ((c + 1) * QBLK - 1) // CHUNK
    return "all" if k_hi <= q_lo else "none" if k_lo > q_hi else "some"


def _diag_bias(tk, n_blk):
    index, blocks = {}, []
    for u in range(2):
        for c in range(n_blk):
            if _diag_mode(u, c, tk) == "some":
                k_chunk = (u * tk + np.arange(tk)[:, None]) // CHUNK
                q_chunk = (c * QBLK + np.arange(QBLK)[None, :]) // CHUNK
                index[(u, c)] = len(blocks)
                blocks.append(np.where(k_chunk <= q_chunk, 0.0, -np.inf).astype(np.float32))
    return index, np.stack(blocks)


def _flash_kernel(qt_ref, k_ref, vt_ref, bias_ref, o_ref, m_ref, acc_ref, s_ref, mx_ref, p_ref, al_ref,
                  *, tq, tk, bias_index):
    n_q, n_blk = qt_ref.shape[0], qt_ref.shape[1]
    acc_ref[...] = jnp.zeros_like(acc_ref)


    def diag_mode(u, c):
        return _diag_mode(u, c, tk)

    def score(i, t, slot, c, mode):
        if mode == "none":
            return
        k = k_ref[pl.ds(pl.multiple_of(t * tk, tk), tk), :]
        s = _dot(k, qt_ref[i, c])
        if mode == "some":
            s = s + bias_ref[bias_index[(slot, c)]]
        s_ref[slot, c] = s
        mx_ref[slot, c] = jnp.max(s, axis=0, keepdims=True)

    def soften(slot, c, mode="all", first=False):
        if mode == "none":
            return
        if first:
            m_new = mx_ref[slot, c]
            al_ref[slot, c] = jnp.zeros_like(m_new)
        else:
            m_prev = m_ref[c]
            m_new = jnp.maximum(m_prev, mx_ref[slot, c])
            al_ref[slot, c] = jnp.exp2(m_prev - m_new)
        p_ref[slot, c] = jnp.exp2(s_ref[slot, c] - m_new).astype(BF16)
        m_ref[c] = m_new

    def gather(t, slot, c, mode="all"):
        if mode == "none":
            return
        acc_ref[c] = al_ref[slot, c] * acc_ref[c] + _dot(vt_ref[t], p_ref[slot, c])

    def pair(i, g, diag):
        for c in range(n_blk):
            score(i, g, 0, c, diag_mode(0, c) if diag else "all")
            soften(1, c)
            gather(g - 2, 0, c)
        for c in range(n_blk):
            score(i, g + 1, 1, c, diag_mode(1, c) if diag else "all")
            soften(0, c, diag_mode(0, c) if diag else "all")
            gather(g - 1, 1, c)

    def head(i, diag):
        for c in range(n_blk):
            score(i, 0, 0, c, diag_mode(0, c) if diag else "all")
        for c in range(n_blk):
            score(i, 1, 1, c, diag_mode(1, c) if diag else "all")
            soften(0, c, diag_mode(0, c) if diag else "all", first=True)

    def tail(i):
        for c in range(n_blk):
            soften(1, c, diag_mode(1, c))
            gather(2 * i, 0, c, diag_mode(0, c))
        for c in range(n_blk):
            gather(2 * i + 1, 1, c, diag_mode(1, c))
        for c in range(n_blk):
            acc = acc_ref[c]
            rows = pl.ds(pl.multiple_of(i * tq + c * QBLK, QBLK), QBLK)
            o_ref[rows, :] = (acc[:MLA_V] / acc[MLA_V:MLA_V + 1]).T.astype(o_ref.dtype)

    def middle(i):
        def body(j, carry):
            pair(i, 2 * j, False)
            return carry

        lax.fori_loop(1, i, body, 0)
        pair(i, 2 * i, True)

    head(0, True)
    if n_q > 1:
        tail(0)
        head(1, False)

        def outer(i, carry):
            middle(i)
            tail(i)
            head(i + 1, False)
            return carry

        lax.fori_loop(1, n_q - 1, outer, 0)
        middle(n_q - 1)
    tail(n_q - 1)


def _flash(qt, k, vt, heads):
    t = k.shape[0]
    n_q, n_blk = qt.shape[1], qt.shape[2]
    tq = n_blk * QBLK
    n_kt, tk = vt.shape[1], vt.shape[3]
    assert tq == 2 * tk and tk % CHUNK == 0 and qt.shape[4] == QBLK and n_q * tq == t
    bias_index, bias = _diag_bias(tk, n_blk)
    return pl.pallas_call(
        functools.partial(_flash_kernel, tq=tq, tk=tk, bias_index=bias_index),
        grid=(heads,),
        in_specs=[pl.BlockSpec((None, n_q, n_blk, MLA_HEAD_PAD, QBLK), lambda hd: (hd, 0, 0, 0, 0)),
                  pl.BlockSpec((t, MLA_HEAD_PAD), lambda hd: (0, hd)),
                  pl.BlockSpec((None, n_kt, V_ROWS, tk), lambda hd: (hd, 0, 0, 0)),
                  pl.BlockSpec(bias.shape, lambda hd: (0, 0, 0))],
        out_specs=pl.BlockSpec((t, MLA_V), lambda hd: (0, hd)),
        out_shape=jax.ShapeDtypeStruct((t, heads * MLA_V), BF16),
        scratch_shapes=[pltpu.VMEM((n_blk, 1, QBLK), F32), pltpu.VMEM((n_blk, V_ROWS, QBLK), F32),
                        pltpu.VMEM((2, n_blk, tk, QBLK), F32), pltpu.VMEM((2, n_blk, 1, QBLK), F32),
                        pltpu.VMEM((2, n_blk, tk, QBLK), BF16), pltpu.VMEM((2, n_blk, 1, QBLK), F32)],
        compiler_params=_params("parallel"),
        name="flash",
    )(qt, k, vt, jnp.asarray(bias))


def _sattn_kernel(q_ref, cc_ref, ck_ref, nc_ref, nk_ref, wuk_ref, o_ref,
                  qabs_ref, qr_ref, m_ref, l_ref, acc_ref, *, heads, nq, tk, n_new):
    past, rank = cc_ref.shape
    n_tiles = past // tk
    for hd in range(heads):
        c0 = hd * MLA_HEAD_PAD
        qn = q_ref[:, c0:c0 + MLA_NOPE]
        qabs_ref[hd * nq:(hd + 1) * nq, :] = _dot(
            qn, wuk_ref[hd * MLA_NOPE:(hd + 1) * MLA_NOPE, :]).astype(qabs_ref.dtype)
        qr_ref[hd * nq:(hd + 1) * nq, :] = q_ref[:, c0 + MLA_NOPE:c0 + MLA_NOPE + MLA_ROPE]
    m_ref[...] = jnp.full_like(m_ref, -jnp.inf)
    l_ref[...] = jnp.zeros_like(l_ref)
    acc_ref[...] = jnp.zeros_like(acc_ref)

    def tile(t):
        if t < n_tiles:
            return cc_ref[t * tk:(t + 1) * tk, :].astype(BF16), ck_ref[t * tk:(t + 1) * tk, :]
        pad = LANE - n_new
        c_new = jnp.concatenate([nc_ref[...], jnp.zeros((pad, rank), F32)], axis=0)
        k_new = jnp.concatenate([nk_ref[...], jnp.zeros((pad, MLA_ROPE), F32)], axis=0)
        return c_new.astype(BF16), k_new

    def key_norm(c, kr):
        n_keys = c.shape[0]
        kn_t = _dot_nt(wuk_ref[...], c)
        ss_t = jnp.sum((kn_t * kn_t).reshape(heads, MLA_NOPE, n_keys), axis=1)
        kr2_t = lax.dot_general(jnp.ones((8, MLA_ROPE), F32), kr * kr, NT_DIMS,
                                preferred_element_type=F32, precision=lax.Precision.HIGHEST)[0:1]
        return lax.rsqrt((ss_t + kr2_t) / MLA_QK_DIM + EPS)

    def attend(c, kr, inv_t, n_valid):
        n_keys = c.shape[0]
        s = _dot_nt(qabs_ref[...], c) + _dot_nt(qr_ref[...], kr.astype(BF16))
        s = s * jnp.broadcast_to(inv_t[:, None, :], (heads, nq, n_keys)).reshape(heads * nq, n_keys)
        if n_valid < n_keys:
            s = jnp.where(lax.broadcasted_iota(jnp.int32, s.shape, 1) < n_valid, s, -jnp.inf)
        m_prev = m_ref[...]
        m_new = jnp.maximum(m_prev, jnp.max(s, axis=-1, keepdims=True))
        alpha = jnp.exp(m_prev - m_new)
        p = jnp.exp(s - m_new)
        l_ref[...] = alpha * l_ref[...] + jnp.sum(p, axis=-1, keepdims=True)
        acc_ref[...] = alpha * acc_ref[...] + _dot(p.astype(BF16), c)
        m_ref[...] = m_new

    c, kr = tile(0)
    inv_t = key_norm(c, kr)
    for t in range(n_tiles + 1):
        if t < n_tiles:
            c_next, kr_next = tile(t + 1)
            inv_next = key_norm(c_next, kr_next)
        attend(c, kr, inv_t, tk if t < n_tiles else n_new)
        if t < n_tiles:
            c, kr, inv_t = c_next, kr_next, inv_next
    o_ref[...] = (acc_ref[...] / l_ref[...]).astype(o_ref.dtype)


def _sattn(q, cache_c, cache_k, new_c, new_k, wuk_t, heads, nq, tk):
    nb, past, rank = cache_c.shape
    return pl.pallas_call(
        functools.partial(_sattn_kernel, heads=heads, nq=nq, tk=tk, n_new=nq),
        grid=(nb,),
        in_specs=[pl.BlockSpec((nq, heads * MLA_HEAD_PAD), lambda b: (b, 0)),
                  pl.BlockSpec((None, past, rank), lambda b: (b, 0, 0)),
                  pl.BlockSpec((None, past, MLA_ROPE), lambda b: (b, 0, 0)),
                  pl.BlockSpec((nq, rank), lambda b: (b, 0)),
                  pl.BlockSpec((nq, MLA_ROPE), lambda b: (b, 0)),
                  pl.BlockSpec(wuk_t.shape, lambda b: (0, 0))],
        out_specs=pl.BlockSpec((None, heads * nq, rank), lambda b: (b, 0, 0)),
        out_shape=jax.ShapeDtypeStruct((nb, heads * nq, rank), BF16),
        scratch_shapes=[pltpu.VMEM((heads * nq, rank), BF16), pltpu.VMEM((heads * nq, MLA_ROPE), BF16),
                        pltpu.VMEM((heads * nq, 1), F32), pltpu.VMEM((heads * nq, 1), F32),
                        pltpu.VMEM((heads * nq, rank), F32)],
        compiler_params=_params("parallel"),
        name="sattn",
    )(q, cache_c, cache_k, new_c, new_k, wuk_t)


def _svup_kernel(ol_ref, w_ref, o_ref):
    nb, nq, rank = ol_ref.shape
    o_ref[...] = _dot(ol_ref[...].reshape(nb * nq, rank), w_ref[...]).astype(o_ref.dtype)


def _svup(o_lat, wuv, heads, nq):
    nb, _, rank = o_lat.shape
    return pl.pallas_call(
        _svup_kernel,
        grid=(heads,),
        in_specs=[pl.BlockSpec((nb, nq, rank), lambda hd: (0, hd, 0)),
                  pl.BlockSpec((rank, MLA_V), lambda hd: (0, hd))],
        out_specs=pl.BlockSpec((nb * nq, MLA_V), lambda hd: (0, hd)),
        out_shape=jax.ShapeDtypeStruct((nb * nq, heads * MLA_V), BF16),
        compiler_params=_params("parallel"),
        name="svup",
    )(o_lat, wuv)


def _gla_levels(c):
    leaf = min(c, GLA_LEAF)
    levels = [(leaf, leaf // 2 - 1)]
    g = 2 * leaf
    while g <= c:
        levels.append((g, g // 2 - 1))
        g *= 2
    return levels


def _gla_kernel(*refs, heads, dk, dv, c, has_s0):
    if has_s0:
        q_ref, k_ref, v_ref, la_ref, g_ref, s0_ref, o_ref, sout_ref, st_ref = refs
    else:
        q_ref, k_ref, v_ref, la_ref, g_ref, o_ref, sout_ref, st_ref = refs
    j = pl.program_id(1)
    nj = pl.num_programs(1)

    @pl.when(j == 0)
    def _():
        for hd in range(heads):
            if has_s0:
                st_ref[hd] = s0_ref[hd].T
            else:
                st_ref[hd] = jnp.zeros((dv, dk), F32)

    levels = _gla_levels(c)
    row = lax.broadcasted_iota(jnp.int32, (c, c), 0)
    col = lax.broadcasted_iota(jnp.int32, (c, c), 1)
    masks = []
    for lv, (g, _) in enumerate(levels):
        same = (row // g) == (col // g)
        if lv == 0:
            masks.append(jnp.logical_and(same, col <= row))
        else:
            half = g // 2
            masks.append(jnp.logical_and(same, jnp.logical_and((row // half) % 2 == 1, (col // half) % 2 == 0)))
    rid = lax.broadcasted_iota(jnp.int32, (c, dk), 0)
    gain = g_ref[...]

    for hd in range(heads):
        la = la_ref[:, hd * dk:(hd + 1) * dk]
        b = la
        sh = 1
        while sh < c:
            b = b + jnp.where(rid >= sh, pltpu.roll(b, sh, 0), 0.0)
            sh *= 2
        b_end = b[c - 1:c, :]
        q = q_ref[:, hd * dk:(hd + 1) * dk].astype(F32) * (dk ** -0.5)
        k = k_ref[:, hd * dk:(hd + 1) * dk].astype(F32)
        v = v_ref[:, hd * dv:(hd + 1) * dv]

        a = jnp.zeros((c, c), F32)
        for lv, (g, r) in enumerate(levels):
            ref_rows = jnp.broadcast_to(b.reshape(c // g, g, dk)[:, r:r + 1, :], (c // g, g, dk)).reshape(c, dk)
            d = b - ref_rows
            if lv == 0:
                fq, fk = jnp.exp(d), jnp.exp(-d)
            else:
                fq = fk = jnp.exp(-jnp.abs(d))
            a_lv = _dot_nt((q * fq).astype(BF16), (k * fk).astype(BF16))
            a = jnp.where(masks[lv], a_lv, a)

        st = st_ref[hd]
        o = _dot(a.astype(BF16), v) + _dot_nt((q * jnp.exp(b)).astype(BF16), st.astype(BF16))
        k_end = (k * jnp.exp(b_end - b)).astype(BF16)
        st_ref[hd] = st * jnp.exp(b_end) + lax.dot_general(v, k_end, TN_DIMS, preferred_element_type=F32)
        o_ref[:, hd * dv:(hd + 1) * dv] = (o * _rms_inv(o, dv) * gain).astype(o_ref.dtype)

    @pl.when(j == nj - 1)
    def _():
        for hd in range(heads):
            sout_ref[hd] = st_ref[hd].T


def _gla(zp, la, gain, s0, nb, c, heads, dk, dv):
    t = la.shape[0]
    nj = t // (nb * c)
    gk, gv = heads * dk, heads * dv
    assert gv % gk == 0
    row = lambda bb, j: (bb * nj + j, 0)
    in_specs = [pl.BlockSpec((c, gk), row),
                pl.BlockSpec((c, gk), lambda bb, j: (bb * nj + j, 1)),
                pl.BlockSpec((c, gv), lambda bb, j: (bb * nj + j, 2 * gk // gv)),
                pl.BlockSpec((c, gk), row),
                pl.BlockSpec((1, dv), lambda bb, j: (0, 0))]
    args = [zp, zp, zp, la, gain]
    if s0 is not None:
        in_specs.append(pl.BlockSpec((None, heads, dk, dv), lambda bb, j: (bb, 0, 0, 0)))
        args.append(s0)
    return pl.pallas_call(
        functools.partial(_gla_kernel, heads=heads, dk=dk, dv=dv, c=c, has_s0=s0 is not None),
        grid=(nb, nj),
        in_specs=in_specs,
        out_specs=[pl.BlockSpec((c, gv), row),
                   pl.BlockSpec((None, heads, dk, dv), lambda bb, j: (bb, 0, 0, 0))],
        out_shape=[jax.ShapeDtypeStruct((t, gv), BF16), jax.ShapeDtypeStruct((nb, heads, dk, dv), F32)],
        scratch_shapes=[pltpu.VMEM((heads, dv, dk), F32)],
        compiler_params=_params("parallel", "arbitrary"),
        name="gla",
    )(*args)


def _mix_kernel(om_ref, og_ref, gate_o_ref, gate_m_ref, gate_g_ref, x_ref, w_ref, g_ref, y_ref, h_ref):
    o_gla = og_ref[...].astype(F32) * gate_o_ref[...].astype(F32)
    mixed = gate_m_ref[...].astype(F32) * om_ref[...].astype(F32) + gate_g_ref[...].astype(F32) * o_gla
    y = x_ref[...] + _dot(mixed.astype(BF16), w_ref[...])
    y_ref[...] = y
    h_ref[...] = (y * _rms_inv(y, y.shape[-1]) * g_ref[...]).astype(h_ref.dtype)


def _mix(o_mla, o_gla, zp, gate_block0, x, w_o, g, tm):
    t, d = x.shape
    row = lambda i: (i, 0)
    return pl.pallas_call(
        _mix_kernel,
        grid=(t // tm,),
        in_specs=[pl.BlockSpec((tm, d), row), pl.BlockSpec((tm, d), row),
                  pl.BlockSpec((tm, d), lambda i: (i, gate_block0)),
                  pl.BlockSpec((tm, d), lambda i: (i, gate_block0 + 1)),
                  pl.BlockSpec((tm, d), lambda i: (i, gate_block0 + 2)),
                  pl.BlockSpec((tm, d), row), pl.BlockSpec((d, d), lambda i: (0, 0)),
                  pl.BlockSpec((1, d), lambda i: (0, 0))],
        out_specs=[pl.BlockSpec((tm, d), row), pl.BlockSpec((tm, d), row)],
        out_shape=[jax.ShapeDtypeStruct((t, d), F32), jax.ShapeDtypeStruct((t, d), BF16)],
        compiler_params=_params("parallel"),
        name="mix",
    )(o_mla, o_gla, zp, zp, zp, x, w_o, g)


def _ffn_kernel(h_ref, y_ref, wu_ref, wd_ref, o_ref):
    f = pl.program_id(1)

    @pl.when(f == 0)
    def _():
        o_ref[...] = y_ref[...]

    u = jnp.maximum(_dot(h_ref[...], wu_ref[...]), 0.0)
    o_ref[...] += _dot((u * u).astype(BF16), wd_ref[...])


def _ffn(h, y, w_up, w_down, tm, tf):
    t, d = y.shape
    dff = w_up.shape[1]
    return pl.pallas_call(
        _ffn_kernel,
        grid=(t // tm, dff // tf),
        in_specs=[pl.BlockSpec((tm, d), lambda i, f: (i, 0)), pl.BlockSpec((tm, d), lambda i, f: (i, 0)),
                  pl.BlockSpec((d, tf), lambda i, f: (0, f)), pl.BlockSpec((tf, d), lambda i, f: (f, 0))],
        out_specs=pl.BlockSpec((tm, d), lambda i, f: (i, 0)),
        out_shape=jax.ShapeDtypeStruct((t, d), F32),
        compiler_params=_params("parallel", "arbitrary"),
        name="ffn",
    )(h, y, w_up, w_down)


def _rope_table(pos):
    half = MLA_ROPE // 2
    freqs = jnp.power(ROPE_THETA, -jnp.arange(half, dtype=F32) / half)
    ang = pos[:, None] * freqs[None, :]
    cos, sin = jnp.cos(ang), jnp.sin(ang)
    return jnp.concatenate([cos, cos, -sin, sin], axis=1)


def _swap_halves(w):
    half = w.shape[-1] // 2
    return jnp.concatenate([w[..., half:], w[..., :half]], axis=-1)


def _pick(t, pref):
    while t % pref:
        pref //= 2
    return pref


def kernel(x_prompt, x_sample, cache_mla_ckv, cache_mla_krope, state_gla, norm_mix_g, w_in, mla_q_norm_g,
           mla_w_uq, mla_kv_norm_g, mla_w_ukv, mla_q_gain_nope, mla_q_gain_rope, mla_k_gain_nope,
           mla_k_gain_rope, gla_w_a2, gla_b_a, gla_norm_g, w_o, norm_ffn_g, ffn_w_up, ffn_w_down):
    depth = w_in.shape[0]
    bp, seq, d = x_prompt.shape
    nb, dec_seq, _ = x_sample.shape
    past = cache_mla_ckv.shape[2]
    q_rank, heads = mla_w_uq.shape[1], mla_w_uq.shape[2]
    kv_rank = mla_w_ukv.shape[1]
    gla_heads, dk, dv = state_gla.shape[2], state_gla.shape[3], state_gla.shape[4]
    gate_rank = gla_w_a2.shape[1]
    gk, gv = gla_heads * dk, gla_heads * dv
    assert bp == 1 and heads * MLA_V == d and gv == d and seq % CHUNK == 0

    cs_p = _rope_table(jnp.arange(seq, dtype=F32))
    cs_s = jnp.tile(_rope_table(past + jnp.arange(dec_seq, dtype=F32)), (nb, 1))

    xp = x_prompt.reshape(seq, d)
    xs = x_sample.reshape(nb * dec_seq, d)
    outs = [[] for _ in range(6)]
    for l in range(depth):
        wi = w_in[l]
        pts = np.cumsum([q_rank, kv_rank, MLA_ROPE, gk, gk, gv, gate_rank, gv, d]).tolist()
        w_qkv_lat = wi[:, :pts[1]]
        w_kr = wi[:, pts[1]:pts[2]]
        w_alr = wi[:, pts[5]:pts[6]]
        assert 3 * gate_rank <= LANE
        w_lat = jnp.concatenate(
            [w_qkv_lat, w_kr, _swap_halves(w_kr), w_alr, w_alr, w_alr,
             jnp.zeros((d, LANE - 3 * gate_rank), F32)], axis=1).astype(BF16)
        w_gla = wi[:, pts[2]:pts[5]].astype(BF16)
        w_gate = wi[:, pts[6]:].astype(BF16)
        wa2_hi = gla_w_a2[l].astype(BF16)
        wa2_lo = (gla_w_a2[l] - wa2_hi.astype(F32)).astype(BF16)
        wa2_split = jnp.concatenate(
            [wa2_hi, wa2_lo, wa2_hi, jnp.zeros((LANE - 3 * gate_rank, gk), BF16)], axis=0)

        wq = mla_w_uq[l]
        wq_r = wq[..., MLA_NOPE:]
        w_uq = jnp.concatenate([wq[..., :MLA_NOPE], wq_r, _swap_halves(wq_r)], axis=-1)
        w_uq = w_uq.reshape(q_rank, heads * MLA_HEAD_PAD).astype(BF16)
        w_uq_t = w_uq.T
        wuk = mla_w_ukv[l][..., :MLA_NOPE].reshape(kv_rank, heads * MLA_NOPE).astype(BF16)
        wuk_t = wuk.T
        wuv = mla_w_ukv[l][..., MLA_NOPE:].reshape(kv_rank, heads * MLA_V).astype(BF16)
        wuv_t = wuv.T
        gain = jnp.concatenate([mla_q_gain_nope[l] * mla_k_gain_nope[l],
                                jnp.tile(mla_q_gain_rope[l] * mla_k_gain_rope[l], 2),
                                jnp.zeros((MLA_HEAD_PAD - MLA_QK_DIM,), F32)])[None, :] * MLA_SCALE
        gain_col = gain.T * LOG2E
        wo_b = w_o[l].astype(BF16)
        wup_b = ffn_w_up[l].astype(BF16)
        wdn_b = ffn_w_down[l].astype(BF16)
        g_mix = norm_mix_g[l][None, :]
        g_q = mla_q_norm_g[l][None, :]
        g_kv = mla_kv_norm_g[l][None, :]
        g_gla = gla_norm_g[l][None, :]
        g_ffn = norm_ffn_g[l][None, :]
        ba = gla_b_a[l][None, :]

        def front(x, cs, tm):
            h = _prenorm(x, g_mix, _pick(x.shape[0], 256))
            qlat, ckv, kr, la = _mla_lat(h, w_lat, g_q, g_kv, cs, wa2_split, ba, tm, q_rank, kv_rank, gate_rank)
            z_gla = _proj(h, w_gla, _pick(x.shape[0], 1024), 2048)
            z_gate = _proj(h, w_gate, _pick(x.shape[0], 1024), gv, gates=True)
            return qlat, ckv, kr, la, z_gla, z_gate

        def back(x, o_mla, o_gla, z_gate, tm):
            y1, h2 = _mix(o_mla, o_gla, z_gate, 0, x, wo_b, g_ffn, tm)
            return _ffn(h2, y1, wup_b, wdn_b, tm, 1024)

        tile = _pick(seq, 512)
        qlat, ckv, kr, la, z_gla, z_gate = front(xp, cs_p, tile)
        qt = _mla_qt(qlat, w_uq_t, cs_p.T, gain_col, tile, 2 * tile, heads)
        kcat, vt = _mla_kv(ckv, kr, wuk, wuv_t, tile, heads)
        o_mla = _flash(qt, kcat, vt, heads)
        o_gla, st = _gla(z_gla, la, g_gla, None, 1, _pick(seq, 128), gla_heads, dk, dv)
        xp_next = back(xp, o_mla, o_gla, z_gate, tile)
        outs[0].append(ckv.reshape(bp, seq, kv_rank))
        outs[1].append(kr.reshape(bp, seq, MLA_ROPE))
        outs[2].append(st)
        xp = xp_next

        qlat, ckv, kr, la, z_gla, z_gate = front(xs, cs_s, nb * dec_seq)
        qcat = _mla_q(qlat, w_uq, cs_s, gain, nb * dec_seq, heads)
        o_lat = _sattn(qcat, cache_mla_ckv[l], cache_mla_krope[l], ckv, kr, wuk_t, heads, dec_seq, _pick(past, 512))
        o_mla = _svup(o_lat, wuv, heads, dec_seq)
        o_gla, st = _gla(z_gla, la, g_gla, state_gla[l], nb, dec_seq, gla_heads, dk, dv)
        xs_next = back(xs, o_mla, o_gla, z_gate, nb * dec_seq)
        outs[3].append(ckv.reshape(nb, dec_seq, kv_rank))
        outs[4].append(kr.reshape(nb, dec_seq, MLA_ROPE))
        outs[5].append(st)
        xs = xs_next

    return (xp.reshape(bp, seq, d), xs.reshape(nb, dec_seq, d),
            jnp.stack(outs[0]), jnp.stack(outs[1]), jnp.stack(outs[2]),
            jnp.stack(outs[3]), jnp.stack(outs[4]), jnp.stack(outs[5]))
```

```python
import functools

import jax
import jax.numpy as jnp
import numpy as np
from jax import lax
from jax.experimental import pallas as pl
from jax.experimental.pallas import tpu as pltpu

F32 = jnp.float32
BF16 = jnp.bfloat16

EPS = 1e-6
CHUNK = 64
MLA_NOPE = 128
MLA_ROPE = 64
MLA_V = 128
MLA_QK_DIM = MLA_NOPE + MLA_ROPE
MLA_SCALE = MLA_QK_DIM ** -0.5
MLA_HEAD_PAD = 256
V_ROWS = MLA_V + 16
LOG2E = 1.4426950408889634
QBLK = 256
ROPE_THETA = 10000.0
GLA_TAU = 16.0
GLA_LEAF = 32
LANE = 128
VMEM_LIMIT = 56 * 1024 * 1024

NT_DIMS = (((1,), (1,)), ((), ()))
TN_DIMS = (((0,), (0,)), ((), ()))


def _params(*sem, flags=None):
    return pltpu.CompilerParams(dimension_semantics=sem, vmem_limit_bytes=VMEM_LIMIT, flags=flags)


def _dot(a, b):
    return jnp.dot(a, b, preferred_element_type=F32)


def _dot_nt(a, b):
    return lax.dot_general(a, b, NT_DIMS, preferred_element_type=F32)


def _rms_inv(x, n):
    return lax.rsqrt(jnp.sum(x * x, axis=-1, keepdims=True) / n + EPS)


def _prenorm_kernel(x_ref, g_ref, h_ref):
    x = x_ref[...]
    h_ref[...] = (x * _rms_inv(x, x.shape[-1]) * g_ref[...]).astype(h_ref.dtype)


def _prenorm(x, g, tm):
    t, d = x.shape
    return pl.pallas_call(
        _prenorm_kernel,
        grid=(t // tm,),
        in_specs=[pl.BlockSpec((tm, d), lambda i: (i, 0)), pl.BlockSpec((1, d), lambda i: (0, 0))],
        out_specs=pl.BlockSpec((tm, d), lambda i: (i, 0)),
        out_shape=jax.ShapeDtypeStruct((t, d), BF16),
        compiler_params=_params("parallel"),
        name="prenorm",
    )(x, g)


def _mla_lat_kernel(h_ref, w_ref, qg_ref, kvg_ref, cs_ref, wa2_ref, ba_ref,
                    qlat_ref, ckv_ref, kr_ref, la_ref, *, q_rank, kv_rank, gate_rank):
    z = _dot(h_ref[...], w_ref[...])
    q_lat = z[:, :q_rank]
    qlat_ref[...] = (q_lat * _rms_inv(q_lat, q_rank) * qg_ref[...]).astype(qlat_ref.dtype)
    kv_lat = z[:, q_rank:q_rank + kv_rank]
    ckv_ref[...] = kv_lat * _rms_inv(kv_lat, kv_rank) * kvg_ref[...]
    o = q_rank + kv_rank
    rr = z[:, o:o + LANE] * cs_ref[...]
    kr_ref[...] = rr[:, :MLA_ROPE] + rr[:, MLA_ROPE:]
    a3 = z[:, o + LANE:o + 2 * LANE]
    a_hi = a3.astype(BF16)
    a_lo = (a3 - a_hi.astype(F32)).astype(BF16)
    lane = lax.broadcasted_iota(jnp.int32, a3.shape, 1)
    u = _dot(jnp.where(lane < 2 * gate_rank, a_hi, a_lo), wa2_ref[...]) + ba_ref[...]
    log_sig = jnp.minimum(u, 0.0) - jnp.log1p(jnp.exp(-jnp.abs(u)))
    la_ref[...] = log_sig / GLA_TAU


def _mla_lat(h, w_lat, qg, kvg, cs, wa2_split, ba, tm, q_rank, kv_rank, gate_rank):
    t, d = h.shape
    n = w_lat.shape[1]
    gk = wa2_split.shape[1]
    row = lambda i: (i, 0)
    fix = lambda i: (0, 0)
    return pl.pallas_call(
        functools.partial(_mla_lat_kernel, q_rank=q_rank, kv_rank=kv_rank, gate_rank=gate_rank),
        grid=(t // tm,),
        in_specs=[pl.BlockSpec((tm, d), row), pl.BlockSpec((d, n), fix),
                  pl.BlockSpec((1, q_rank), fix), pl.BlockSpec((1, kv_rank), fix),
                  pl.BlockSpec((tm, LANE), row), pl.BlockSpec((LANE, gk), fix), pl.BlockSpec((1, gk), fix)],
        out_specs=[pl.BlockSpec((tm, q_rank), row), pl.BlockSpec((tm, kv_rank), row),
                   pl.BlockSpec((tm, MLA_ROPE), row), pl.BlockSpec((tm, gk), row)],
        out_shape=[jax.ShapeDtypeStruct((t, q_rank), BF16), jax.ShapeDtypeStruct((t, kv_rank), F32),
                   jax.ShapeDtypeStruct((t, MLA_ROPE), F32), jax.ShapeDtypeStruct((t, gk), F32)],
        compiler_params=_params("parallel"),
        name="mla_lat",
    )(h, w_lat, qg, kvg, cs, wa2_split, ba)


def _proj_kernel(h_ref, w_ref, o_ref, *, gates):
    z = _dot(h_ref[...], w_ref[...])
    if gates:
        z = jax.nn.sigmoid(z) * jnp.where(pl.program_id(1) == 0, z, 1.0)
    o_ref[...] = z.astype(o_ref.dtype)


def _proj(h, w, tm, tn, gates=False):
    t, d = h.shape
    n = w.shape[1]
    return pl.pallas_call(
        functools.partial(_proj_kernel, gates=gates),
        grid=(t // tm, n // tn),
        in_specs=[pl.BlockSpec((tm, d), lambda i, j: (i, 0)), pl.BlockSpec((d, tn), lambda i, j: (0, j))],
        out_specs=pl.BlockSpec((tm, tn), lambda i, j: (i, j)),
        out_shape=jax.ShapeDtypeStruct((t, n), BF16),
        compiler_params=_params("parallel", "parallel"),
        name="proj",
    )(h, w)


def _mla_q_kernel(ql_ref, w_ref, cs_ref, gain_ref, q_ref, *, heads):
    ql = ql_ref[...]
    cs = cs_ref[...]
    gain = gain_ref[...]
    lane = lax.broadcasted_iota(jnp.int32, (1, LANE), 1)
    for hd in range(heads):
        c0 = hd * MLA_HEAD_PAD
        z = _dot(ql, w_ref[:, c0:c0 + MLA_HEAD_PAD])
        nope = z[:, :MLA_NOPE]
        rr = z[:, MLA_NOPE:] * cs
        rot = rr + pltpu.roll(rr, MLA_ROPE, 1)
        ss = (jnp.sum(nope * nope, axis=-1, keepdims=True)
              + jnp.sum(jnp.where(lane < MLA_ROPE, rot * rot, 0.0), axis=-1, keepdims=True))
        inv = lax.rsqrt(ss / MLA_QK_DIM + EPS)
        q_ref[:, c0:c0 + MLA_NOPE] = (nope * inv * gain[:, :MLA_NOPE]).astype(q_ref.dtype)
        q_ref[:, c0 + MLA_NOPE:c0 + MLA_HEAD_PAD] = (rot * inv * gain[:, MLA_NOPE:]).astype(q_ref.dtype)


def _mla_q(qlat, w_uq, cs, gain, tm, heads):
    t, r = qlat.shape
    n = heads * MLA_HEAD_PAD
    return pl.pallas_call(
        functools.partial(_mla_q_kernel, heads=heads),
        grid=(t // tm,),
        in_specs=[pl.BlockSpec((tm, r), lambda i: (i, 0)), pl.BlockSpec((r, n), lambda i: (0, 0)),
                  pl.BlockSpec((tm, LANE), lambda i: (i, 0)), pl.BlockSpec((1, MLA_HEAD_PAD), lambda i: (0, 0))],
        out_specs=pl.BlockSpec((tm, n), lambda i: (i, 0)),
        out_shape=jax.ShapeDtypeStruct((t, n), BF16),
        compiler_params=_params("parallel"),
        name="mla_q",
    )(qlat, w_uq, cs, gain)


def _mla_qt_kernel(ql_ref, w_ref, cs_ref, gain_ref, q_ref, *, heads):
    ql = ql_ref[...]
    cs = cs_ref[...]
    gain = gain_ref[...]
    for hd in range(heads):
        r0 = hd * MLA_HEAD_PAD
        z = _dot_nt(w_ref[r0:r0 + MLA_HEAD_PAD, :], ql)
        nope = z[:MLA_NOPE]
        rr = z[MLA_NOPE:] * cs
        rot = rr[:MLA_ROPE] + rr[MLA_ROPE:]
        ss = jnp.sum(nope * nope, axis=0, keepdims=True) + jnp.sum(rot * rot, axis=0, keepdims=True)
        inv = lax.rsqrt(ss / MLA_QK_DIM + EPS)
        qn = (nope * inv * gain[:MLA_NOPE]).astype(q_ref.dtype)
        qr = (rot * inv * gain[MLA_NOPE:MLA_QK_DIM]).astype(q_ref.dtype)
        for bb in range(q_ref.shape[1]):
            q_ref[hd, bb, :MLA_NOPE, :] = qn[:, bb * QBLK:(bb + 1) * QBLK]
            q_ref[hd, bb, MLA_NOPE:MLA_QK_DIM, :] = qr[:, bb * QBLK:(bb + 1) * QBLK]
            q_ref[hd, bb, MLA_QK_DIM:, :] = jnp.zeros((MLA_HEAD_PAD - MLA_QK_DIM, QBLK), q_ref.dtype)


def _mla_qt(qlat, w_uq_t, cs_t, gain_col, tm, tq, heads):
    t, r = qlat.shape
    n = heads * MLA_HEAD_PAD
    per = tq // tm
    assert tm % QBLK == 0 and tq % tm == 0
    return pl.pallas_call(
        functools.partial(_mla_qt_kernel, heads=heads),
        grid=(t // tm,),
        in_specs=[pl.BlockSpec((tm, r), lambda i: (i, 0)), pl.BlockSpec((n, r), lambda i: (0, 0)),
                  pl.BlockSpec((LANE, tm), lambda i: (0, i)), pl.BlockSpec((MLA_HEAD_PAD, 1), lambda i: (0, 0))],
        out_specs=pl.BlockSpec((heads, None, tm // QBLK, MLA_HEAD_PAD, QBLK),
                               lambda i: (0, i // per, i % per, 0, 0)),
        out_shape=jax.ShapeDtypeStruct((heads, t // tq, tq // QBLK, MLA_HEAD_PAD, QBLK), BF16),
        compiler_params=_params("parallel"),
        name="mla_qt",
    )(qlat, w_uq_t, cs_t, gain_col)


def _mla_kv_kernel(ckv_ref, kr_ref, wk_ref, wvt_ref, k_ref, vt_ref, *, heads):
    c = ckv_ref[...].astype(BF16)
    kr = kr_ref[...]
    kr2 = jnp.sum(kr * kr, axis=-1, keepdims=True)
    kr_pad = jnp.concatenate([kr, jnp.zeros_like(kr)], axis=1)
    for pair in range(heads // 2):
        z = _dot(c, wk_ref[:, pair * 2 * MLA_NOPE:(pair + 1) * 2 * MLA_NOPE])
        for sub in range(2):
            kn = z[:, sub * MLA_NOPE:(sub + 1) * MLA_NOPE]
            inv = lax.rsqrt((jnp.sum(kn * kn, axis=-1, keepdims=True) + kr2) / MLA_QK_DIM + EPS)
            c0 = (2 * pair + sub) * MLA_HEAD_PAD
            k_ref[:, c0:c0 + MLA_NOPE] = (kn * inv).astype(k_ref.dtype)
            k_ref[:, c0 + MLA_NOPE:c0 + MLA_HEAD_PAD] = (kr_pad * inv).astype(k_ref.dtype)
    vt = _dot_nt(wvt_ref[...], c)
    for hd in range(heads):
        vt_ref[hd, :MLA_V, :] = vt[hd * MLA_V:(hd + 1) * MLA_V].astype(vt_ref.dtype)
        vt_ref[hd, MLA_V:, :] = jnp.ones((V_ROWS - MLA_V, vt.shape[1]), vt_ref.dtype)


def _mla_kv(ckv, kr, w_uk, w_uv_t, tm, heads):
    t, r = ckv.shape
    return pl.pallas_call(
        functools.partial(_mla_kv_kernel, heads=heads),
        grid=(t // tm,),
        in_specs=[pl.BlockSpec((tm, r), lambda i: (i, 0)), pl.BlockSpec((tm, MLA_ROPE), lambda i: (i, 0)),
                  pl.BlockSpec((r, heads * MLA_NOPE), lambda i: (0, 0)),
                  pl.BlockSpec((heads * MLA_V, r), lambda i: (0, 0))],
        out_specs=[pl.BlockSpec((tm, heads * MLA_HEAD_PAD), lambda i: (i, 0)),
                   pl.BlockSpec((heads, None, V_ROWS, tm), lambda i: (0, i, 0, 0))],
        out_shape=[jax.ShapeDtypeStruct((t, heads * MLA_HEAD_PAD), BF16),
                   jax.ShapeDtypeStruct((heads, t // tm, V_ROWS, tm), BF16)],
        compiler_params=_params("parallel"),
        name="mla_kv",
    )(ckv, kr, w_uk, w_uv_t)


def _diag_mode(u, c, tk):
    k_lo, k_hi = (u * tk) // CHUNK, ((u + 1) * tk - 1) // CHUNK
    q_lo, q_hi = (c * QBLK) // CHUNK, ((c + 1) * QBLK - 1) // CHUNK
    return "all" if k_hi <= q_lo else "none" if k_lo > q_hi else "some"


def _diag_bias(tk, n_blk):
    index, blocks = {}, []
    for u in range(2):
        for c in range(n_blk):
            if _diag_mode(u, c, tk) == "some":
                k_chunk = (u * tk + np.arange(tk)[:, None]) // CHUNK
                q_chunk = (c * QBLK + np.arange(QBLK)[None, :]) // CHUNK
                index[(u, c)] = len(blocks)
                blocks.append(np.where(k_chunk <= q_chunk, 0.0, -np.inf).astype(np.float32))
    return index, np.stack(blocks)


def _flash_kernel(qt_ref, k_ref, vt_ref, bias_ref, o_ref, m_ref, acc_ref, s_ref, mx_ref, p_ref, al_ref,
                  *, tq, tk, bias_index):
    n_q, n_blk = qt_ref.shape[0], qt_ref.shape[1]
    acc_ref[...] = jnp.zeros_like(acc_ref)


    def diag_mode(u, c):
        return _diag_mode(u, c, tk)

    def score(i, t, slot, c, mode):
        if mode == "none":
            return
        k = k_ref[pl.ds(pl.multiple_of(t * tk, tk), tk), :]
        s = _dot(k, qt_ref[i, c])
        if mode == "some":
            s = s + bias_ref[bias_index[(slot, c)]]
        s_ref[slot, c] = s
        mx_ref[slot, c] = jnp.max(s, axis=0, keepdims=True)

    def soften(slot, c, mode="all", first=False):
        if mode == "none":
            return
        if first:
            m_new = mx_ref[slot, c]
            al_ref[slot, c] = jnp.zeros_like(m_new)
        else:
            m_prev = m_ref[c]
            m_new = jnp.maximum(m_prev, mx_ref[slot, c])
            al_ref[slot, c] = jnp.exp2(m_prev - m_new)
        p_ref[slot, c] = jnp.exp2(s_ref[slot, c] - m_new).astype(BF16)
        m_ref[c] = m_new

    def gather(t, slot, c, mode="all"):
        if mode == "none":
            return
        acc_ref[c] = al_ref[slot, c] * acc_ref[c] + _dot(vt_ref[t], p_ref[slot, c])

    def pair(i, g, diag):
        for c in range(n_blk):
            score(i, g, 0, c, diag_mode(0, c) if diag else "all")
            soften(1, c)
            gather(g - 2, 0, c)
        for c in range(n_blk):
            score(i, g + 1, 1, c, diag_mode(1, c) if diag else "all")
            soften(0, c, diag_mode(0, c) if diag else "all")
            gather(g - 1, 1, c)

    def head(i, diag):
        for c in range(n_blk):
            score(i, 0, 0, c, diag_mode(0, c) if diag else "all")
        for c in range(n_blk):
            score(i, 1, 1, c, diag_mode(1, c) if diag else "all")
            soften(0, c, diag_mode(0, c) if diag else "all", first=True)

    def tail(i):
        for c in range(n_blk):
            soften(1, c, diag_mode(1, c))
            gather(2 * i, 0, c, diag_mode(0, c))
        for c in range(n_blk):
            gather(2 * i + 1, 1, c, diag_mode(1, c))
        for c in range(n_blk):
            acc = acc_ref[c]
            rows = pl.ds(pl.multiple_of(i * tq + c * QBLK, QBLK), QBLK)
            o_ref[rows, :] = (acc[:MLA_V] / acc[MLA_V:MLA_V + 1]).T.astype(o_ref.dtype)

    def middle(i):
        def body(j, carry):
            pair(i, 2 * j, False)
            return carry

        lax.fori_loop(1, i, body, 0)
        pair(i, 2 * i, True)

    head(0, True)
    if n_q > 1:
        tail(0)
        head(1, False)

        def outer(i, carry):
            middle(i)
            tail(i)
            head(i + 1, False)
            return carry

        lax.fori_loop(1, n_q - 1, outer, 0)
        middle(n_q - 1)
    tail(n_q - 1)


def _flash(qt, k, vt, heads):
    t = k.shape[0]
    n_q, n_blk = qt.shape[1], qt.shape[2]
    tq = n_blk * QBLK
    n_kt, tk = vt.shape[1], vt.shape[3]
    assert tq == 2 * tk and tk % CHUNK == 0 and qt.shape[4] == QBLK and n_q * tq == t
    bias_index, bias = _diag_bias(tk, n_blk)
    return pl.pallas_call(
        functools.partial(_flash_kernel, tq=tq, tk=tk, bias_index=bias_index),
        grid=(heads,),
        in_specs=[pl.BlockSpec((None, n_q, n_blk, MLA_HEAD_PAD, QBLK), lambda hd: (hd, 0, 0, 0, 0)),
                  pl.BlockSpec((t, MLA_HEAD_PAD), lambda hd: (0, hd)),
                  pl.BlockSpec((None, n_kt, V_ROWS, tk), lambda hd: (hd, 0, 0, 0)),
                  pl.BlockSpec(bias.shape, lambda hd: (0, 0, 0))],
        out_specs=pl.BlockSpec((t, MLA_V), lambda hd: (0, hd)),
        out_shape=jax.ShapeDtypeStruct((t, heads * MLA_V), BF16),
        scratch_shapes=[pltpu.VMEM((n_blk, 1, QBLK), F32), pltpu.VMEM((n_blk, V_ROWS, QBLK), F32),
                        pltpu.VMEM((2, n_blk, tk, QBLK), F32), pltpu.VMEM((2, n_blk, 1, QBLK), F32),
                        pltpu.VMEM((2, n_blk, tk, QBLK), BF16), pltpu.VMEM((2, n_blk, 1, QBLK), F32)],
        compiler_params=_params("parallel"),
        name="flash",
    )(qt, k, vt, jnp.asarray(bias))


def _sattn_kernel(q_ref, cc_ref, ck_ref, nc_ref, nk_ref, wuk_ref, o_ref,
                  qabs_ref, qr_ref, m_ref, l_ref, acc_ref, *, heads, nq, tk, n_new):
    past, rank = cc_ref.shape
    n_tiles = past // tk
    for hd in range(heads):
        c0 = hd * MLA_HEAD_PAD
        qn = q_ref[:, c0:c0 + MLA_NOPE]
        qabs_ref[hd * nq:(hd + 1) * nq, :] = _dot(
            qn, wuk_ref[hd * MLA_NOPE:(hd + 1) * MLA_NOPE, :]).astype(qabs_ref.dtype)
        qr_ref[hd * nq:(hd + 1) * nq, :] = q_ref[:, c0 + MLA_NOPE:c0 + MLA_NOPE + MLA_ROPE]
    m_ref[...] = jnp.full_like(m_ref, -jnp.inf)
    l_ref[...] = jnp.zeros_like(l_ref)
    acc_ref[...] = jnp.zeros_like(acc_ref)

    def tile(t):
        if t < n_tiles:
            return cc_ref[t * tk:(t + 1) * tk, :].astype(BF16), ck_ref[t * tk:(t + 1) * tk, :]
        pad = LANE - n_new
        c_new = jnp.concatenate([nc_ref[...], jnp.zeros((pad, rank), F32)], axis=0)
        k_new = jnp.concatenate([nk_ref[...], jnp.zeros((pad, MLA_ROPE), F32)], axis=0)
        return c_new.astype(BF16), k_new

    def key_norm(c, kr):
        n_keys = c.shape[0]
        kn_t = _dot_nt(wuk_ref[...], c)
        ss_t = jnp.sum((kn_t * kn_t).reshape(heads, MLA_NOPE, n_keys), axis=1)
        kr2_t = lax.dot_general(jnp.ones((8, MLA_ROPE), F32), kr * kr, NT_DIMS,
                                preferred_element_type=F32, precision=lax.Precision.HIGHEST)[0:1]
        return lax.rsqrt((ss_t + kr2_t) / MLA_QK_DIM + EPS)

    def attend(c, kr, inv_t, n_valid):
        n_keys = c.shape[0]
        s = _dot_nt(qabs_ref[...], c) + _dot_nt(qr_ref[...], kr.astype(BF16))
        s = s * jnp.broadcast_to(inv_t[:, None, :], (heads, nq, n_keys)).reshape(heads * nq, n_keys)
        if n_valid < n_keys:
            s = jnp.where(lax.broadcasted_iota(jnp.int32, s.shape, 1) < n_valid, s, -jnp.inf)
        m_prev = m_ref[...]
        m_new = jnp.maximum(m_prev, jnp.max(s, axis=-1, keepdims=True))
        alpha = jnp.exp(m_prev - m_new)
        p = jnp.exp(s - m_new)
        l_ref[...] = alpha * l_ref[...] + jnp.sum(p, axis=-1, keepdims=True)
        acc_ref[...] = alpha * acc_ref[...] + _dot(p.astype(BF16), c)
        m_ref[...] = m_new

    c, kr = tile(0)
    inv_t = key_norm(c, kr)
    for t in range(n_tiles + 1):
        if t < n_tiles:
            c_next, kr_next = tile(t + 1)
            inv_next = key_norm(c_next, kr_next)
        attend(c, kr, inv_t, tk if t < n_tiles else n_new)
        if t < n_tiles:
            c, kr, inv_t = c_next, kr_next, inv_next
    o_ref[...] = (acc_ref[...] / l_ref[...]).astype(o_ref.dtype)


def _sattn(q, cache_c, cache_k, new_c, new_k, wuk_t, heads, nq, tk):
    nb, past, rank = cache_c.shape
    return pl.pallas_call(
        functools.partial(_sattn_kernel, heads=heads, nq=nq, tk=tk, n_new=nq),
        grid=(nb,),
        in_specs=[pl.BlockSpec((nq, heads * MLA_HEAD_PAD), lambda b: (b, 0)),
                  pl.BlockSpec((None, past, rank), lambda b: (b, 0, 0)),
                  pl.BlockSpec((None, past, MLA_ROPE), lambda b: (b, 0, 0)),
                  pl.BlockSpec((nq, rank), lambda b: (b, 0)),
                  pl.BlockSpec((nq, MLA_ROPE), lambda b: (b, 0)),
                  pl.BlockSpec(wuk_t.shape, lambda b: (0, 0))],
        out_specs=pl.BlockSpec((None, heads * nq, rank), lambda b: (b, 0, 0)),
        out_shape=jax.ShapeDtypeStruct((nb, heads * nq, rank), BF16),
        scratch_shapes=[pltpu.VMEM((heads * nq, rank), BF16), pltpu.VMEM((heads * nq, MLA_ROPE), BF16),
                        pltpu.VMEM((heads * nq, 1), F32), pltpu.VMEM((heads * nq, 1), F32),
                        pltpu.VMEM((heads * nq, rank), F32)],
        compiler_params=_params("parallel"),
        name="sattn",
    )(q, cache_c, cache_k, new_c, new_k, wuk_t)


def _svup_kernel(ol_ref, w_ref, o_ref):
    nb, nq, rank = ol_ref.shape
    o_ref[...] = _dot(ol_ref[...].reshape(nb * nq, rank), w_ref[...]).astype(o_ref.dtype)


def _svup(o_lat, wuv, heads, nq):
    nb, _, rank = o_lat.shape
    return pl.pallas_call(
        _svup_kernel,
        grid=(heads,),
        in_specs=[pl.BlockSpec((nb, nq, rank), lambda hd: (0, hd, 0)),
                  pl.BlockSpec((rank, MLA_V), lambda hd: (0, hd))],
        out_specs=pl.BlockSpec((nb * nq, MLA_V), lambda hd: (0, hd)),
        out_shape=jax.ShapeDtypeStruct((nb * nq, heads * MLA_V), BF16),
        compiler_params=_params("parallel"),
        name="svup",
    )(o_lat, wuv)


def _gla_levels(c):
    leaf = min(c, GLA_LEAF)
    levels = [(leaf, leaf // 2 - 1)]
    g = 2 * leaf
    while g <= c:
        levels.append((g, g // 2 - 1))
        g *= 2
    return levels


def _gla_kernel(q_ref, k_ref, v_ref, la_ref, g_ref, o_ref, sout_ref, st_ref, *, heads, dk, dv, c):
    j = pl.program_id(1)
    nj = pl.num_programs(1)

    @pl.when(j == 0)
    def _():
        st_ref[...] = jnp.zeros_like(st_ref)

    levels = _gla_levels(c)
    row = lax.broadcasted_iota(jnp.int32, (c, c), 0)
    col = lax.broadcasted_iota(jnp.int32, (c, c), 1)
    masks = []
    for lv, (g, _) in enumerate(levels):
        same = (row // g) == (col // g)
        if lv == 0:
            masks.append(jnp.logical_and(same, col <= row))
        else:
            half = g // 2
            masks.append(jnp.logical_and(same, jnp.logical_and((row // half) % 2 == 1, (col // half) % 2 == 0)))
    rid = lax.broadcasted_iota(jnp.int32, (c, dk), 0)
    gain = g_ref[...]

    for hd in range(heads):
        la = la_ref[:, hd * dk:(hd + 1) * dk]
        b = la
        sh = 1
        while sh < c:
            b = b + jnp.where(rid >= sh, pltpu.roll(b, sh, 0), 0.0)
            sh *= 2
        b_end = b[c - 1:c, :]
        q = q_ref[:, hd * dk:(hd + 1) * dk].astype(F32) * (dk ** -0.5)
        k = k_ref[:, hd * dk:(hd + 1) * dk].astype(F32)
        v = v_ref[:, hd * dv:(hd + 1) * dv]

        a = jnp.zeros((c, c), F32)
        for lv, (g, r) in enumerate(levels):
            ref_rows = jnp.broadcast_to(b.reshape(c // g, g, dk)[:, r:r + 1, :], (c // g, g, dk)).reshape(c, dk)
            d = b - ref_rows
            if lv == 0:
                fq, fk = jnp.exp(d), jnp.exp(-d)
            else:
                fq = fk = jnp.exp(-jnp.abs(d))
            a_lv = _dot_nt((q * fq).astype(BF16), (k * fk).astype(BF16))
            a = jnp.where(masks[lv], a_lv, a)

        st = st_ref[hd]
        o = _dot(a.astype(BF16), v) + _dot_nt((q * jnp.exp(b)).astype(BF16), st.astype(BF16))
        k_end = (k * jnp.exp(b_end - b)).astype(BF16)
        st_ref[hd] = st * jnp.exp(b_end) + lax.dot_general(v, k_end, TN_DIMS, preferred_element_type=F32)
        o_ref[:, hd * dv:(hd + 1) * dv] = (o * _rms_inv(o, dv) * gain).astype(o_ref.dtype)

    @pl.when(j == nj - 1)
    def _():
        for hd in range(heads):
            sout_ref[hd] = st_ref[hd].T


def _gla(zp, la, gain, nb, c, heads, dk, dv):
    t = la.shape[0]
    nj = t // (nb * c)
    gk, gv = heads * dk, heads * dv
    assert gv % gk == 0
    row = lambda bb, j: (bb * nj + j, 0)
    in_specs = [pl.BlockSpec((c, gk), row),
                pl.BlockSpec((c, gk), lambda bb, j: (bb * nj + j, 1)),
                pl.BlockSpec((c, gv), lambda bb, j: (bb * nj + j, 2 * gk // gv)),
                pl.BlockSpec((c, gk), row),
                pl.BlockSpec((1, dv), lambda bb, j: (0, 0))]
    return pl.pallas_call(
        functools.partial(_gla_kernel, heads=heads, dk=dk, dv=dv, c=c),
        grid=(nb, nj),
        in_specs=in_specs,
        out_specs=[pl.BlockSpec((c, gv), row),
                   pl.BlockSpec((None, heads, dk, dv), lambda bb, j: (bb, 0, 0, 0))],
        out_shape=[jax.ShapeDtypeStruct((t, gv), BF16), jax.ShapeDtypeStruct((nb, heads, dk, dv), F32)],
        scratch_shapes=[pltpu.VMEM((heads, dv, dk), F32)],
        compiler_params=_params("parallel", "arbitrary"),
        name="gla",
    )(zp, zp, zp, la, gain)


def _gla_step_kernel(q_ref, k_ref, v_ref, la_ref, g_ref, s0_ref, o_ref, sout_ref, *, heads, dk, dv):
    c = la_ref.shape[0]
    row = lax.broadcasted_iota(jnp.int32, (c, c), 0)
    col = lax.broadcasted_iota(jnp.int32, (c, c), 1)
    rid = lax.broadcasted_iota(jnp.int32, (c, dk), 0)
    gain = g_ref[...]
    ones = jnp.ones((c, LANE), F32)
    for hd in range(heads):
        la = la_ref[:, hd * dk:(hd + 1) * dk]
        b = la
        sh = 1
        while sh < c:
            b = b + jnp.where(rid >= sh, pltpu.roll(b, sh, 0), 0.0)
            sh *= 2
        q = q_ref[:, hd * dk:(hd + 1) * dk].astype(F32) * (dk ** -0.5)
        k = k_ref[:, hd * dk:(hd + 1) * dk].astype(F32)
        v = v_ref[:, hd * dv:(hd + 1) * dv]
        d = b - b[c // 2 - 1:c // 2, :]
        a = jnp.where(col <= row, _dot_nt((q * jnp.exp(d)).astype(BF16), (k * jnp.exp(-d)).astype(BF16)), 0.0)
        st = s0_ref[hd]
        o = _dot(a.astype(BF16), v) + _dot((q * jnp.exp(b)).astype(BF16), st.astype(BF16))
        k_end = (k * jnp.exp(b[c - 1:c, :] - b)).astype(BF16)
        decay = jnp.exp(lax.dot_general(la, ones, TN_DIMS, preferred_element_type=F32,
                                        precision=lax.Precision.HIGHEST))
        sout_ref[hd] = (st * jnp.tile(decay, (1, dv // LANE))
                        + lax.dot_general(k_end, v, TN_DIMS, preferred_element_type=F32))
        o_ref[:, hd * dv:(hd + 1) * dv] = (o * _rms_inv(o, dv) * gain).astype(o_ref.dtype)


def _gla_step(zp, la, gain, s0, c, heads, dk, dv):
    nb = s0.shape[0]
    gk, gv = heads * dk, heads * dv
    assert la.shape[0] == nb * c and c <= GLA_LEAF and gv % gk == 0
    state = pl.BlockSpec((None, heads, dk, dv), lambda bb: (bb, 0, 0, 0))
    return pl.pallas_call(
        functools.partial(_gla_step_kernel, heads=heads, dk=dk, dv=dv),
        grid=(nb,),
        in_specs=[pl.BlockSpec((c, gk), lambda bb: (bb, 0)), pl.BlockSpec((c, gk), lambda bb: (bb, 1)),
                  pl.BlockSpec((c, gv), lambda bb: (bb, 2 * gk // gv)), pl.BlockSpec((c, gk), lambda bb: (bb, 0)),
                  pl.BlockSpec((1, dv), lambda bb: (0, 0)), state],
        out_specs=[pl.BlockSpec((c, gv), lambda bb: (bb, 0)), state],
        out_shape=[jax.ShapeDtypeStruct((nb * c, gv), BF16), jax.ShapeDtypeStruct((nb, heads, dk, dv), F32)],
        compiler_params=_params("parallel"),
        name="gla_step",
    )(zp, zp, zp, la, gain, s0)


def _mix_kernel(om_ref, og_ref, gate_o_ref, gate_m_ref, gate_g_ref, x_ref, w_ref, g_ref, y_ref, h_ref):
    o_gla = og_ref[...].astype(F32) * gate_o_ref[...].astype(F32)
    mixed = gate_m_ref[...].astype(F32) * om_ref[...].astype(F32) + gate_g_ref[...].astype(F32) * o_gla
    y = x_ref[...] + _dot(mixed.astype(BF16), w_ref[...])
    y_ref[...] = y
    h_ref[...] = (y * _rms_inv(y, y.shape[-1]) * g_ref[...]).astype(h_ref.dtype)


def _mix(o_mla, o_gla, zp, gate_block0, x, w_o, g, tm):
    t, d = x.shape
    row = lambda i: (i, 0)
    return pl.pallas_call(
        _mix_kernel,
        grid=(t // tm,),
        in_specs=[pl.BlockSpec((tm, d), row), pl.BlockSpec((tm, d), row),
                  pl.BlockSpec((tm, d), lambda i: (i, gate_block0)),
                  pl.BlockSpec((tm, d), lambda i: (i, gate_block0 + 1)),
                  pl.BlockSpec((tm, d), lambda i: (i, gate_block0 + 2)),
                  pl.BlockSpec((tm, d), row), pl.BlockSpec((d, d), lambda i: (0, 0)),
                  pl.BlockSpec((1, d), lambda i: (0, 0))],
        out_specs=[pl.BlockSpec((tm, d), row), pl.BlockSpec((tm, d), row)],
        out_shape=[jax.ShapeDtypeStruct((t, d), F32), jax.ShapeDtypeStruct((t, d), BF16)],
        compiler_params=_params("parallel"),
        name="mix",
    )(o_mla, o_gla, zp, zp, zp, x, w_o, g)


def _ffn_kernel(h_ref, y_ref, wu_ref, wd_ref, o_ref):
    f = pl.program_id(1)

    @pl.when(f == 0)
    def _():
        o_ref[...] = y_ref[...]

    u = jnp.maximum(_dot(h_ref[...], wu_ref[...]), 0.0)
    o_ref[...] += _dot((u * u).astype(BF16), wd_ref[...])


def _ffn(h, y, w_up, w_down, tm, tf):
    t, d = y.shape
    dff = w_up.shape[1]
    return pl.pallas_call(
        _ffn_kernel,
        grid=(t // tm, dff // tf),
        in_specs=[pl.BlockSpec((tm, d), lambda i, f: (i, 0)), pl.BlockSpec((tm, d), lambda i, f: (i, 0)),
                  pl.BlockSpec((d, tf), lambda i, f: (0, f)), pl.BlockSpec((tf, d), lambda i, f: (f, 0))],
        out_specs=pl.BlockSpec((tm, d), lambda i, f: (i, 0)),
        out_shape=jax.ShapeDtypeStruct((t, d), F32),
        compiler_params=_params("parallel", "arbitrary"),
        name="ffn",
    )(h, y, w_up, w_down)


def _rope_table(pos):
    half = MLA_ROPE // 2
    freqs = jnp.power(ROPE_THETA, -jnp.arange(half, dtype=F32) / half)
    ang = pos[:, None] * freqs[None, :]
    cos, sin = jnp.cos(ang), jnp.sin(ang)
    return jnp.concatenate([cos, cos, -sin, sin], axis=1)


def _swap_halves(w):
    half = w.shape[-1] // 2
    return jnp.concatenate([w[..., half:], w[..., :half]], axis=-1)


def _pick(t, pref):
    while t % pref:
        pref //= 2
    return pref


def kernel(x_prompt, x_sample, cache_mla_ckv, cache_mla_krope, state_gla, norm_mix_g, w_in, mla_q_norm_g,
           mla_w_uq, mla_kv_norm_g, mla_w_ukv, mla_q_gain_nope, mla_q_gain_rope, mla_k_gain_nope,
           mla_k_gain_rope, gla_w_a2, gla_b_a, gla_norm_g, w_o, norm_ffn_g, ffn_w_up, ffn_w_down):
    depth = w_in.shape[0]
    bp, seq, d = x_prompt.shape
    nb, dec_seq, _ = x_sample.shape
    past = cache_mla_ckv.shape[2]
    q_rank, heads = mla_w_uq.shape[1], mla_w_uq.shape[2]
    kv_rank = mla_w_ukv.shape[1]
    gla_heads, dk, dv = state_gla.shape[2], state_gla.shape[3], state_gla.shape[4]
    gate_rank = gla_w_a2.shape[1]
    gk, gv = gla_heads * dk, gla_heads * dv
    assert bp == 1 and heads * MLA_V == d and gv == d and seq % CHUNK == 0

    cs_p = _rope_table(jnp.arange(seq, dtype=F32))
    cs_s = jnp.tile(_rope_table(past + jnp.arange(dec_seq, dtype=F32)), (nb, 1))

    xp = x_prompt.reshape(seq, d)
    xs = x_sample.reshape(nb * dec_seq, d)
    outs = [[] for _ in range(6)]
    for l in range(depth):
        wi = w_in[l]
        pts = np.cumsum([q_rank, kv_rank, MLA_ROPE, gk, gk, gv, gate_rank, gv, d]).tolist()
        w_qkv_lat = wi[:, :pts[1]]
        w_kr = wi[:, pts[1]:pts[2]]
        w_alr = wi[:, pts[5]:pts[6]]
        assert 3 * gate_rank <= LANE
        w_lat = jnp.concatenate(
            [w_qkv_lat, w_kr, _swap_halves(w_kr), w_alr, w_alr, w_alr,
             jnp.zeros((d, LANE - 3 * gate_rank), F32)], axis=1).astype(BF16)
        w_gla = wi[:, pts[2]:pts[5]].astype(BF16)
        w_gate = wi[:, pts[6]:].astype(BF16)
        wa2_hi = gla_w_a2[l].astype(BF16)
        wa2_lo = (gla_w_a2[l] - wa2_hi.astype(F32)).astype(BF16)
        wa2_split = jnp.concatenate(
            [wa2_hi, wa2_lo, wa2_hi, jnp.zeros((LANE - 3 * gate_rank, gk), BF16)], axis=0)

        wq = mla_w_uq[l]
        wq_r = wq[..., MLA_NOPE:]
        w_uq = jnp.concatenate([wq[..., :MLA_NOPE], wq_r, _swap_halves(wq_r)], axis=-1)
        w_uq = w_uq.reshape(q_rank, heads * MLA_HEAD_PAD).astype(BF16)
        w_uq_t = w_uq.T
        wuk = mla_w_ukv[l][..., :MLA_NOPE].reshape(kv_rank, heads * MLA_NOPE).astype(BF16)
        wuk_t = wuk.T
        wuv = mla_w_ukv[l][..., MLA_NOPE:].reshape(kv_rank, heads * MLA_V).astype(BF16)
        wuv_t = wuv.T
        gain = jnp.concatenate([mla_q_gain_nope[l] * mla_k_gain_nope[l],
                                jnp.tile(mla_q_gain_rope[l] * mla_k_gain_rope[l], 2),
                                jnp.zeros((MLA_HEAD_PAD - MLA_QK_DIM,), F32)])[None, :] * MLA_SCALE
        gain_col = gain.T * LOG2E
        wo_b = w_o[l].astype(BF16)
        wup_b = ffn_w_up[l].astype(BF16)
        wdn_b = ffn_w_down[l].astype(BF16)
        g_mix = norm_mix_g[l][None, :]
        g_q = mla_q_norm_g[l][None, :]
        g_kv = mla_kv_norm_g[l][None, :]
        g_gla = gla_norm_g[l][None, :]
        g_ffn = norm_ffn_g[l][None, :]
        ba = gla_b_a[l][None, :]

        def front(x, cs, tm):
            h = _prenorm(x, g_mix, _pick(x.shape[0], 256))
            qlat, ckv, kr, la = _mla_lat(h, w_lat, g_q, g_kv, cs, wa2_split, ba, tm, q_rank, kv_rank, gate_rank)
            z_gla = _proj(h, w_gla, _pick(x.shape[0], 1024), 2048)
            z_gate = _proj(h, w_gate, _pick(x.shape[0], 1024), gv, gates=True)
            return qlat, ckv, kr, la, z_gla, z_gate

        def back(x, o_mla, o_gla, z_gate, tm):
            y1, h2 = _mix(o_mla, o_gla, z_gate, 0, x, wo_b, g_ffn, tm)
            return _ffn(h2, y1, wup_b, wdn_b, tm, 1024)

        tile = _pick(seq, 512)
        qlat, ckv, kr, la, z_gla, z_gate = front(xp, cs_p, tile)
        qt = _mla_qt(qlat, w_uq_t, cs_p.T, gain_col, tile, 2 * tile, heads)
        kcat, vt = _mla_kv(ckv, kr, wuk, wuv_t, tile, heads)
        o_mla = _flash(qt, kcat, vt, heads)
        o_gla, st = _gla(z_gla, la, g_gla, 1, _pick(seq, 128), gla_heads, dk, dv)
        xp_next = back(xp, o_mla, o_gla, z_gate, tile)
        outs[0].append(ckv.reshape(bp, seq, kv_rank))
        outs[1].append(kr.reshape(bp, seq, MLA_ROPE))
        outs[2].append(st)
        xp = xp_next

        qlat, ckv, kr, la, z_gla, z_gate = front(xs, cs_s, nb * dec_seq)
        qcat = _mla_q(qlat, w_uq, cs_s, gain, nb * dec_seq, heads)
        o_lat = _sattn(qcat, cache_mla_ckv[l], cache_mla_krope[l], ckv, kr, wuk_t, heads, dec_seq, _pick(past, 512))
        o_mla = _svup(o_lat, wuv, heads, dec_seq)
        o_gla, st = _gla_step(z_gla, la, g_gla, state_gla[l], dec_seq, gla_heads, dk, dv)
        xs_next = back(xs, o_mla, o_gla, z_gate, nb * dec_seq)
        outs[3].append(ckv.reshape(nb, dec_seq, kv_rank))
        outs[4].append(kr.reshape(nb, dec_seq, MLA_ROPE))
        outs[5].append(st)
        xs = xs_next

    return (xp.reshape(bp, seq, d), xs.reshape(nb, dec_seq, d),
            jnp.stack(outs[0]), jnp.stack(outs[1]), jnp.stack(outs[2]),
            jnp.stack(outs[3]), jnp.stack(outs[4]), jnp.stack(outs[5]))
```

```python
import functools

import jax
import jax.numpy as jnp
import numpy as np
from jax import lax
from jax.experimental import pallas as pl
from jax.experimental.pallas import tpu as pltpu

F32 = jnp.float32
BF16 = jnp.bfloat16

EPS = 1e-6
CHUNK = 64
MLA_NOPE = 128
MLA_ROPE = 64
MLA_V = 128
MLA_QK_DIM = MLA_NOPE + MLA_ROPE
MLA_SCALE = MLA_QK_DIM ** -0.5
MLA_HEAD_PAD = 256
V_ROWS = MLA_V + 16
LOG2E = 1.4426950408889634
QBLK = 256
ROPE_THETA = 10000.0
GLA_TAU = 16.0
GLA_LEAF = 32
LANE = 128
V7X_VMEM_BYTES = 64 * 1024 * 1024
VMEM_LIMIT = V7X_VMEM_BYTES * 7 // 8

NT_DIMS = (((1,), (1,)), ((), ()))
TN_DIMS = (((0,), (0,)), ((), ()))


def _params(*sem):
    return pltpu.CompilerParams(dimension_semantics=sem, vmem_limit_bytes=VMEM_LIMIT)


def _dot(a, b):
    return jnp.dot(a, b, preferred_element_type=F32)


def _dot_nt(a, b):
    return lax.dot_general(a, b, NT_DIMS, preferred_element_type=F32)


def _rms_inv(x, n):
    return lax.rsqrt(jnp.sum(x * x, axis=-1, keepdims=True) / n + EPS)


def _prenorm_kernel(x_ref, g_ref, h_ref):
    x = x_ref[...]
    h_ref[...] = (x * _rms_inv(x, x.shape[-1]) * g_ref[...]).astype(h_ref.dtype)


def _prenorm(x, g, tm):
    t, d = x.shape
    return pl.pallas_call(
        _prenorm_kernel,
        grid=(t // tm,),
        in_specs=[pl.BlockSpec((tm, d), lambda i: (i, 0)), pl.BlockSpec((1, d), lambda i: (0, 0))],
        out_specs=pl.BlockSpec((tm, d), lambda i: (i, 0)),
        out_shape=jax.ShapeDtypeStruct((t, d), BF16),
        compiler_params=_params("parallel"),
        name="prenorm",
    )(x, g)


def _mla_lat_kernel(h_ref, w_ref, qg_ref, kvg_ref, cs_ref, wa2_ref, ba_ref,
                    qlat_ref, ckv_ref, kr_ref, la_ref, *, q_rank, kv_rank, gate_rank):
    z = _dot(h_ref[...], w_ref[...])
    q_lat = z[:, :q_rank]
    qlat_ref[...] = (q_lat * _rms_inv(q_lat, q_rank) * qg_ref[...]).astype(qlat_ref.dtype)
    kv_lat = z[:, q_rank:q_rank + kv_rank]
    ckv_ref[...] = kv_lat * _rms_inv(kv_lat, kv_rank) * kvg_ref[...]
    o = q_rank + kv_rank
    rr = z[:, o:o + LANE] * cs_ref[...]
    kr_ref[...] = rr[:, :MLA_ROPE] + rr[:, MLA_ROPE:]
    a3 = z[:, o + LANE:o + 2 * LANE]
    a_hi = a3.astype(BF16)
    a_lo = (a3 - a_hi.astype(F32)).astype(BF16)
    lane = lax.broadcasted_iota(jnp.int32, a3.shape, 1)
    u = _dot(jnp.where(lane < 2 * gate_rank, a_hi, a_lo), wa2_ref[...]) + ba_ref[...]
    log_sig = jnp.minimum(u, 0.0) - jnp.log1p(jnp.exp(-jnp.abs(u)))
    la_ref[...] = log_sig / GLA_TAU


def _mla_lat(h, w_lat, qg, kvg, cs, wa2_split, ba, tm, q_rank, kv_rank, gate_rank):
    t, d = h.shape
    n = w_lat.shape[1]
    gk = wa2_split.shape[1]
    row = lambda i: (i, 0)
    fix = lambda i: (0, 0)
    return pl.pallas_call(
        functools.partial(_mla_lat_kernel, q_rank=q_rank, kv_rank=kv_rank, gate_rank=gate_rank),
        grid=(t // tm,),
        in_specs=[pl.BlockSpec((tm, d), row), pl.BlockSpec((d, n), fix),
                  pl.BlockSpec((1, q_rank), fix), pl.BlockSpec((1, kv_rank), fix),
                  pl.BlockSpec((tm, LANE), row), pl.BlockSpec((LANE, gk), fix), pl.BlockSpec((1, gk), fix)],
        out_specs=[pl.BlockSpec((tm, q_rank), row), pl.BlockSpec((tm, kv_rank), row),
                   pl.BlockSpec((tm, MLA_ROPE), row), pl.BlockSpec((tm, gk), row)],
        out_shape=[jax.ShapeDtypeStruct((t, q_rank), BF16), jax.ShapeDtypeStruct((t, kv_rank), F32),
                   jax.ShapeDtypeStruct((t, MLA_ROPE), F32), jax.ShapeDtypeStruct((t, gk), F32)],
        compiler_params=_params("parallel"),
        name="mla_lat",
    )(h, w_lat, qg, kvg, cs, wa2_split, ba)


def _proj_kernel(h_ref, w_ref, o_ref, *, gates):
    z = _dot(h_ref[...], w_ref[...])
    if gates:
        z = jax.nn.sigmoid(z) * jnp.where(pl.program_id(1) == 0, z, 1.0)
    o_ref[...] = z.astype(o_ref.dtype)


def _proj(h, w, tm, tn, gates=False):
    t, d = h.shape
    n = w.shape[1]
    return pl.pallas_call(
        functools.partial(_proj_kernel, gates=gates),
        grid=(t // tm, n // tn),
        in_specs=[pl.BlockSpec((tm, d), lambda i, j: (i, 0)), pl.BlockSpec((d, tn), lambda i, j: (0, j))],
        out_specs=pl.BlockSpec((tm, tn), lambda i, j: (i, j)),
        out_shape=jax.ShapeDtypeStruct((t, n), BF16),
        compiler_params=_params("parallel", "parallel"),
        name="proj",
    )(h, w)


def _mla_q_kernel(ql_ref, w_ref, cs_ref, gain_ref, q_ref, *, heads):
    ql = ql_ref[...]
    cs = cs_ref[...]
    gain = gain_ref[...]
    lane = lax.broadcasted_iota(jnp.int32, (1, LANE), 1)
    for hd in range(heads):
        c0 = hd * MLA_HEAD_PAD
        z = _dot(ql, w_ref[:, c0:c0 + MLA_HEAD_PAD])
        nope = z[:, :MLA_NOPE]
        rr = z[:, MLA_NOPE:] * cs
        rot = rr + pltpu.roll(rr, MLA_ROPE, 1)
        ss = (jnp.sum(nope * nope, axis=-1, keepdims=True)
              + jnp.sum(jnp.where(lane < MLA_ROPE, rot * rot, 0.0), axis=-1, keepdims=True))
        inv = lax.rsqrt(ss / MLA_QK_DIM + EPS)
        q_ref[:, c0:c0 + MLA_NOPE] = (nope * inv * gain[:, :MLA_NOPE]).astype(q_ref.dtype)
        q_ref[:, c0 + MLA_NOPE:c0 + MLA_HEAD_PAD] = (rot * inv * gain[:, MLA_NOPE:]).astype(q_ref.dtype)


def _mla_q(qlat, w_uq, cs, gain, tm, heads):
    t, r = qlat.shape
    n = heads * MLA_HEAD_PAD
    return pl.pallas_call(
        functools.partial(_mla_q_kernel, heads=heads),
        grid=(t // tm,),
        in_specs=[pl.BlockSpec((tm, r), lambda i: (i, 0)), pl.BlockSpec((r, n), lambda i: (0, 0)),
                  pl.BlockSpec((tm, LANE), lambda i: (i, 0)), pl.BlockSpec((1, MLA_HEAD_PAD), lambda i: (0, 0))],
        out_specs=pl.BlockSpec((tm, n), lambda i: (i, 0)),
        out_shape=jax.ShapeDtypeStruct((t, n), BF16),
        compiler_params=_params("parallel"),
        name="mla_q",
    )(qlat, w_uq, cs, gain)


def _mla_qt_kernel(ql_ref, w_ref, cs_ref, gain_ref, q_ref, *, heads):
    ql = ql_ref[...]
    cs = cs_ref[...]
    gain = gain_ref[...]
    for hd in range(heads):
        r0 = hd * MLA_HEAD_PAD
        z = _dot_nt(w_ref[r0:r0 + MLA_HEAD_PAD, :], ql)
        nope = z[:MLA_NOPE]
        rr = z[MLA_NOPE:] * cs
        rot = rr[:MLA_ROPE] + rr[MLA_ROPE:]
        ss = jnp.sum(nope * nope, axis=0, keepdims=True) + jnp.sum(rot * rot, axis=0, keepdims=True)
        inv = lax.rsqrt(ss / MLA_QK_DIM + EPS)
        qn = (nope * inv * gain[:MLA_NOPE]).astype(q_ref.dtype)
        qr = (rot * inv * gain[MLA_NOPE:MLA_QK_DIM]).astype(q_ref.dtype)
        for bb in range(q_ref.shape[1]):
            q_ref[hd, bb, :MLA_NOPE, :] = qn[:, bb * QBLK:(bb + 1) * QBLK]
            q_ref[hd, bb, MLA_NOPE:MLA_QK_DIM, :] = qr[:, bb * QBLK:(bb + 1) * QBLK]
            q_ref[hd, bb, MLA_QK_DIM:, :] = jnp.zeros((MLA_HEAD_PAD - MLA_QK_DIM, QBLK), q_ref.dtype)


def _mla_qt(qlat, w_uq_t, cs_t, gain_col, tm, tq, heads):
    t, r = qlat.shape
    n = heads * MLA_HEAD_PAD
    per = tq // tm
    assert tm % QBLK == 0 and tq % tm == 0
    return pl.pallas_call(
        functools.partial(_mla_qt_kernel, heads=heads),
        grid=(t // tm,),
        in_specs=[pl.BlockSpec((tm, r), lambda i: (i, 0)), pl.BlockSpec((n, r), lambda i: (0, 0)),
                  pl.BlockSpec((LANE, tm), lambda i: (0, i)), pl.BlockSpec((MLA_HEAD_PAD, 1), lambda i: (0, 0))],
        out_specs=pl.BlockSpec((heads, None, tm // QBLK, MLA_HEAD_PAD, QBLK),
                               lambda i: (0, i // per, i % per, 0, 0)),
        out_shape=jax.ShapeDtypeStruct((heads, t // tq, tq // QBLK, MLA_HEAD_PAD, QBLK), BF16),
        compiler_params=_params("parallel"),
        name="mla_qt",
    )(qlat, w_uq_t, cs_t, gain_col)


def _mla_kv_kernel(ckv_ref, kr_ref, wk_ref, wvt_ref, k_ref, vt_ref, *, heads):
    c = ckv_ref[...].astype(BF16)
    kr = kr_ref[...]
    kr2 = jnp.sum(kr * kr, axis=-1, keepdims=True)
    kr_pad = jnp.concatenate([kr, jnp.zeros_like(kr)], axis=1)
    for pair in range(heads // 2):
        z = _dot(c, wk_ref[:, pair * 2 * MLA_NOPE:(pair + 1) * 2 * MLA_NOPE])
        for sub in range(2):
            kn = z[:, sub * MLA_NOPE:(sub + 1) * MLA_NOPE]
            inv = lax.rsqrt((jnp.sum(kn * kn, axis=-1, keepdims=True) + kr2) / MLA_QK_DIM + EPS)
            c0 = (2 * pair + sub) * MLA_HEAD_PAD
            k_ref[:, c0:c0 + MLA_NOPE] = (kn * inv).astype(k_ref.dtype)
            k_ref[:, c0 + MLA_NOPE:c0 + MLA_HEAD_PAD] = (kr_pad * inv).astype(k_ref.dtype)
    vt = _dot_nt(wvt_ref[...], c)
    for hd in range(heads):
        vt_ref[hd, :MLA_V, :] = vt[hd * MLA_V:(hd + 1) * MLA_V].astype(vt_ref.dtype)
        vt_ref[hd, MLA_V:, :] = jnp.ones((V_ROWS - MLA_V, vt.shape[1]), vt_ref.dtype)


def _mla_kv(ckv, kr, w_uk, w_uv_t, tm, heads):
    t, r = ckv.shape
    return pl.pallas_call(
        functools.partial(_mla_kv_kernel, heads=heads),
        grid=(t // tm,),
        in_specs=[pl.BlockSpec((tm, r), lambda i: (i, 0)), pl.BlockSpec((tm, MLA_ROPE), lambda i: (i, 0)),
                  pl.BlockSpec((r, heads * MLA_NOPE), lambda i: (0, 0)),
                  pl.BlockSpec((heads * MLA_V, r), lambda i: (0, 0))],
        out_specs=[pl.BlockSpec((tm, heads * MLA_HEAD_PAD), lambda i: (i, 0)),
                   pl.BlockSpec((heads, None, V_ROWS, tm), lambda i: (0, i, 0, 0))],
        out_shape=[jax.ShapeDtypeStruct((t, heads * MLA_HEAD_PAD), BF16),
                   jax.ShapeDtypeStruct((heads, t // tm, V_ROWS, tm), BF16)],
        compiler_params=_params("parallel"),
        name="mla_kv",
    )(ckv, kr, w_uk, w_uv_t)


def _diag_mode(u, c, tk):
    k_lo, k_hi = (u * tk) // CHUNK, ((u + 1) * tk - 1) // CHUNK
    q_lo, q_hi = (c * QBLK) // CHUNK, ((c + 1) * QBLK - 1) // CHUNK
    return "all" if k_hi <= q_lo else "none" if k_lo > q_hi else "some"


def _diag_bias(tk, n_blk):
    index, blocks = {}, []
    for u in range(2):
        for c in range(n_blk):
            if _diag_mode(u, c, tk) == "some":
                k_chunk = (u * tk + np.arange(tk)[:, None]) // CHUNK
                q_chunk = (c * QBLK + np.arange(QBLK)[None, :]) // CHUNK
                index[(u, c)] = len(blocks)
                blocks.append(np.where(k_chunk <= q_chunk, 0.0, -np.inf).astype(np.float32))
    return index, np.stack(blocks)


def _flash_kernel(qt_ref, k_ref, vt_ref, bias_ref, o_ref, m_ref, acc_ref, s_ref, mx_ref, p_ref, al_ref,
                  *, tq, tk, bias_index):
    n_q, n_blk = qt_ref.shape[0], qt_ref.shape[1]
    acc_ref[...] = jnp.zeros_like(acc_ref)


    def diag_mode(u, c):
        return _diag_mode(u, c, tk)

    def score(i, t, slot, c, mode):
        if mode == "none":
            return
        k = k_ref[pl.ds(pl.multiple_of(t * tk, tk), tk), :]
        s = _dot(k, qt_ref[i, c])
        if mode == "some":
            s = s + bias_ref[bias_index[(slot, c)]]
        s_ref[slot, c] = s
        mx_ref[slot, c] = jnp.max(s, axis=0, keepdims=True)

    def soften(slot, c, mode="all", first=False):
        if mode == "none":
            return
        if first:
            m_new = mx_ref[slot, c]
            al_ref[slot, c] = jnp.zeros_like(m_new)
        else:
            m_prev = m_ref[c]
            m_new = jnp.maximum(m_prev, mx_ref[slot, c])
            al_ref[slot, c] = jnp.exp2(m_prev - m_new)
        p_ref[slot, c] = jnp.exp2(s_ref[slot, c] - m_new).astype(BF16)
        m_ref[c] = m_new

    def gather(t, slot, c, mode="all"):
        if mode == "none":
            return
        acc_ref[c] = al_ref[slot, c] * acc_ref[c] + _dot(vt_ref[t], p_ref[slot, c])

    def pair(i, g, diag):
        for c in range(n_blk):
            score(i, g, 0, c, diag_mode(0, c) if diag else "all")
            soften(1, c)
            gather(g - 2, 0, c)
        for c in range(n_blk):
            score(i, g + 1, 1, c, diag_mode(1, c) if diag else "all")
            soften(0, c, diag_mode(0, c) if diag else "all")
            gather(g - 1, 1, c)

    def head(i, diag):
        for c in range(n_blk):
            score(i, 0, 0, c, diag_mode(0, c) if diag else "all")
        for c in range(n_blk):
            score(i, 1, 1, c, diag_mode(1, c) if diag else "all")
            soften(0, c, diag_mode(0, c) if diag else "all", first=True)

    def tail(i):
        for c in range(n_blk):
            soften(1, c, diag_mode(1, c))
            gather(2 * i, 0, c, diag_mode(0, c))
        for c in range(n_blk):
            gather(2 * i + 1, 1, c, diag_mode(1, c))
        for c in range(n_blk):
            acc = acc_ref[c]
            rows = pl.ds(pl.multiple_of(i * tq + c * QBLK, QBLK), QBLK)
            o_ref[rows, :] = (acc[:MLA_V] / acc[MLA_V:MLA_V + 1]).T.astype(o_ref.dtype)

    def middle(i):
        def body(j, carry):
            pair(i, 2 * j, False)
            return carry

        lax.fori_loop(1, i, body, 0)
        pair(i, 2 * i, True)

    head(0, True)
    if n_q > 1:
        tail(0)
        head(1, False)

        def outer(i, carry):
            middle(i)
            tail(i)
            head(i + 1, False)
            return carry

        lax.fori_loop(1, n_q - 1, outer, 0)
        middle(n_q - 1)
    tail(n_q - 1)


def _flash(qt, k, vt, heads):
    t = k.shape[0]
    n_q, n_blk = qt.shape[1], qt.shape[2]
    tq = n_blk * QBLK
    n_kt, tk = vt.shape[1], vt.shape[3]
    assert tq == 2 * tk and tk % CHUNK == 0 and qt.shape[4] == QBLK and n_q * tq == t
    bias_index, bias = _diag_bias(tk, n_blk)
    return pl.pallas_call(
        functools.partial(_flash_kernel, tq=tq, tk=tk, bias_index=bias_index),
        grid=(heads,),
        in_specs=[pl.BlockSpec((None, n_q, n_blk, MLA_HEAD_PAD, QBLK), lambda hd: (hd, 0, 0, 0, 0)),
                  pl.BlockSpec((t, MLA_HEAD_PAD), lambda hd: (0, hd)),
                  pl.BlockSpec((None, n_kt, V_ROWS, tk), lambda hd: (hd, 0, 0, 0)),
                  pl.BlockSpec(bias.shape, lambda hd: (0, 0, 0))],
        out_specs=pl.BlockSpec((t, MLA_V), lambda hd: (0, hd)),
        out_shape=jax.ShapeDtypeStruct((t, heads * MLA_V), BF16),
        scratch_shapes=[pltpu.VMEM((n_blk, 1, QBLK), F32), pltpu.VMEM((n_blk, V_ROWS, QBLK), F32),
                        pltpu.VMEM((2, n_blk, tk, QBLK), F32), pltpu.VMEM((2, n_blk, 1, QBLK), F32),
                        pltpu.VMEM((2, n_blk, tk, QBLK), BF16), pltpu.VMEM((2, n_blk, 1, QBLK), F32)],
        compiler_params=_params("parallel"),
        name="flash",
    )(qt, k, vt, jnp.asarray(bias))


def _sattn_kernel(q_ref, cc_ref, ck_ref, nc_ref, nk_ref, wuk_ref, o_ref,
                  qabs_ref, qr_ref, m_ref, l_ref, acc_ref, *, heads, nq, tk, n_new):
    past, rank = cc_ref.shape
    n_tiles = past // tk
    for hd in range(heads):
        c0 = hd * MLA_HEAD_PAD
        qn = q_ref[:, c0:c0 + MLA_NOPE]
        qabs_ref[hd * nq:(hd + 1) * nq, :] = _dot(
            qn, wuk_ref[hd * MLA_NOPE:(hd + 1) * MLA_NOPE, :]).astype(qabs_ref.dtype)
        qr_ref[hd * nq:(hd + 1) * nq, :] = q_ref[:, c0 + MLA_NOPE:c0 + MLA_NOPE + MLA_ROPE]
    m_ref[...] = jnp.full_like(m_ref, -jnp.inf)
    l_ref[...] = jnp.zeros_like(l_ref)
    acc_ref[...] = jnp.zeros_like(acc_ref)

    def tile(t):
        if t < n_tiles:
            return cc_ref[t * tk:(t + 1) * tk, :].astype(BF16), ck_ref[t * tk:(t + 1) * tk, :]
        pad = LANE - n_new
        c_new = jnp.concatenate([nc_ref[...], jnp.zeros((pad, rank), F32)], axis=0)
        k_new = jnp.concatenate([nk_ref[...], jnp.zeros((pad, MLA_ROPE), F32)], axis=0)
        return c_new.astype(BF16), k_new

    def key_norm(c, kr):
        n_keys = c.shape[0]
        kn_t = _dot_nt(wuk_ref[...], c)
        ss_t = jnp.sum((kn_t * kn_t).reshape(heads, MLA_NOPE, n_keys), axis=1)
        kr2_t = lax.dot_general(jnp.ones((8, MLA_ROPE), F32), kr * kr, NT_DIMS,
                                preferred_element_type=F32, precision=lax.Precision.HIGHEST)[0:1]
        return lax.rsqrt((ss_t + kr2_t) / MLA_QK_DIM + EPS)

    def attend(c, kr, inv_t, n_valid):
        n_keys = c.shape[0]
        s = _dot_nt(qabs_ref[...], c) + _dot_nt(qr_ref[...], kr.astype(BF16))
        s = s * jnp.broadcast_to(inv_t[:, None, :], (heads, nq, n_keys)).reshape(heads * nq, n_keys)
        if n_valid < n_keys:
            s = jnp.where(lax.broadcasted_iota(jnp.int32, s.shape, 1) < n_valid, s, -jnp.inf)
        m_prev = m_ref[...]
        m_new = jnp.maximum(m_prev, jnp.max(s, axis=-1, keepdims=True))
        alpha = jnp.exp(m_prev - m_new)
        p = jnp.exp(s - m_new)
        l_ref[...] = alpha * l_ref[...] + jnp.sum(p, axis=-1, keepdims=True)
        acc_ref[...] = alpha * acc_ref[...] + _dot(p.astype(BF16), c)
        m_ref[...] = m_new

    c, kr = tile(0)
    inv_t = key_norm(c, kr)
    for t in range(n_tiles + 1):
        if t < n_tiles:
            c_next, kr_next = tile(t + 1)
            inv_next = key_norm(c_next, kr_next)
        attend(c, kr, inv_t, tk if t < n_tiles else n_new)
        if t < n_tiles:
            c, kr, inv_t = c_next, kr_next, inv_next
    o_ref[...] = (acc_ref[...] / l_ref[...]).astype(o_ref.dtype)


def _sattn(q, cache_c, cache_k, new_c, new_k, wuk_t, heads, nq, tk):
    nb, past, rank = cache_c.shape
    return pl.pallas_call(
        functools.partial(_sattn_kernel, heads=heads, nq=nq, tk=tk, n_new=nq),
        grid=(nb,),
        in_specs=[pl.BlockSpec((nq, heads * MLA_HEAD_PAD), lambda b: (b, 0)),
                  pl.BlockSpec((None, past, rank), lambda b: (b, 0, 0)),
                  pl.BlockSpec((None, past, MLA_ROPE), lambda b: (b, 0, 0)),
                  pl.BlockSpec((nq, rank), lambda b: (b, 0)),
                  pl.BlockSpec((nq, MLA_ROPE), lambda b: (b, 0)),
                  pl.BlockSpec(wuk_t.shape, lambda b: (0, 0))],
        out_specs=pl.BlockSpec((None, heads * nq, rank), lambda b: (b, 0, 0)),
        out_shape=jax.ShapeDtypeStruct((nb, heads * nq, rank), BF16),
        scratch_shapes=[pltpu.VMEM((heads * nq, rank), BF16), pltpu.VMEM((heads * nq, MLA_ROPE), BF16),
                        pltpu.VMEM((heads * nq, 1), F32), pltpu.VMEM((heads * nq, 1), F32),
                        pltpu.VMEM((heads * nq, rank), F32)],
        compiler_params=_params("parallel"),
        name="sattn",
    )(q, cache_c, cache_k, new_c, new_k, wuk_t)


def _svup_kernel(ol_ref, w_ref, o_ref, *, nq):
    nb, rows, rank = ol_ref.shape
    for hh in range(rows // nq):
        x = ol_ref[:, hh * nq:(hh + 1) * nq, :].reshape(nb * nq, rank)
        o_ref[:, hh * MLA_V:(hh + 1) * MLA_V] = _dot(x, w_ref[:, hh * MLA_V:(hh + 1) * MLA_V]).astype(o_ref.dtype)


def _svup(o_lat, wuv, heads, nq):
    nb, _, rank = o_lat.shape
    hs = 4 if heads % 4 == 0 else 1
    return pl.pallas_call(
        functools.partial(_svup_kernel, nq=nq),
        grid=(heads // hs,),
        in_specs=[pl.BlockSpec((nb, hs * nq, rank), lambda g: (0, g, 0)),
                  pl.BlockSpec((rank, hs * MLA_V), lambda g: (0, g))],
        out_specs=pl.BlockSpec((nb * nq, hs * MLA_V), lambda g: (0, g)),
        out_shape=jax.ShapeDtypeStruct((nb * nq, heads * MLA_V), BF16),
        compiler_params=_params("parallel"),
        name="svup",
    )(o_lat, wuv)


def _gla_levels(c):
    leaf = min(c, GLA_LEAF)
    levels = [(leaf, leaf // 2 - 1)]
    g = 2 * leaf
    while g <= c:
        levels.append((g, g // 2 - 1))
        g *= 2
    return levels


def _gla_kernel(q_ref, k_ref, v_ref, la_ref, g_ref, o_ref, sout_ref, st_ref, *, heads, dk, dv, c):
    j = pl.program_id(1)
    nj = pl.num_programs(1)

    @pl.when(j == 0)
    def _():
        st_ref[...] = jnp.zeros_like(st_ref)

    levels = _gla_levels(c)
    row = lax.broadcasted_iota(jnp.int32, (c, c), 0)
    col = lax.broadcasted_iota(jnp.int32, (c, c), 1)
    masks = []
    for lv, (g, _) in enumerate(levels):
        same = (row // g) == (col // g)
        if lv == 0:
            masks.append(jnp.logical_and(same, col <= row))
        else:
            half = g // 2
            masks.append(jnp.logical_and(same, jnp.logical_and((row // half) % 2 == 1, (col // half) % 2 == 0)))
    rid = lax.broadcasted_iota(jnp.int32, (c, dk), 0)
    gain = g_ref[...]

    for hd in range(heads):
        la = la_ref[:, hd * dk:(hd + 1) * dk]
        b = la
        sh = 1
        while sh < c:
            b = b + jnp.where(rid >= sh, pltpu.roll(b, sh, 0), 0.0)
            sh *= 2
        b_end = b[c - 1:c, :]
        q = q_ref[:, hd * dk:(hd + 1) * dk].astype(F32) * (dk ** -0.5)
        k = k_ref[:, hd * dk:(hd + 1) * dk].astype(F32)
        v = v_ref[:, hd * dv:(hd + 1) * dv]

        a = jnp.zeros((c, c), F32)
        for lv, (g, r) in enumerate(levels):
            ref_rows = jnp.broadcast_to(b.reshape(c // g, g, dk)[:, r:r + 1, :], (c // g, g, dk)).reshape(c, dk)
            d = b - ref_rows
            if lv == 0:
                fq, fk = jnp.exp(d), jnp.exp(-d)
            else:
                fq = fk = jnp.exp(-jnp.abs(d))
            a_lv = _dot_nt((q * fq).astype(BF16), (k * fk).astype(BF16))
            a = jnp.where(masks[lv], a_lv, a)

        st = st_ref[hd]
        o = _dot(a.astype(BF16), v) + _dot_nt((q * jnp.exp(b)).astype(BF16), st.astype(BF16))
        k_end = (k * jnp.exp(b_end - b)).astype(BF16)
        st_ref[hd] = st * jnp.exp(b_end) + lax.dot_general(v, k_end, TN_DIMS, preferred_element_type=F32)
        o_ref[:, hd * dv:(hd + 1) * dv] = (o * _rms_inv(o, dv) * gain).astype(o_ref.dtype)

    @pl.when(j == nj - 1)
    def _():
        for hd in range(heads):
            sout_ref[hd] = st_ref[hd].T


def _gla(zp, la, gain, nb, c, heads, dk, dv):
    t = la.shape[0]
    nj = t // (nb * c)
    gk, gv = heads * dk, heads * dv
    assert gv % gk == 0
    row = lambda bb, j: (bb * nj + j, 0)
    in_specs = [pl.BlockSpec((c, gk), row),
                pl.BlockSpec((c, gk), lambda bb, j: (bb * nj + j, 1)),
                pl.BlockSpec((c, gv), lambda bb, j: (bb * nj + j, 2 * gk // gv)),
                pl.BlockSpec((c, gk), row),
                pl.BlockSpec((1, dv), lambda bb, j: (0, 0))]
    return pl.pallas_call(
        functools.partial(_gla_kernel, heads=heads, dk=dk, dv=dv, c=c),
        grid=(nb, nj),
        in_specs=in_specs,
        out_specs=[pl.BlockSpec((c, gv), row),
                   pl.BlockSpec((None, heads, dk, dv), lambda bb, j: (bb, 0, 0, 0))],
        out_shape=[jax.ShapeDtypeStruct((t, gv), BF16), jax.ShapeDtypeStruct((nb, heads, dk, dv), F32)],
        scratch_shapes=[pltpu.VMEM((heads, dv, dk), F32)],
        compiler_params=_params("parallel", "arbitrary"),
        name="gla",
    )(zp, zp, zp, la, gain)


def _gla_step_kernel(q_ref, k_ref, v_ref, la_ref, g_ref, s0_ref, o_ref, sout_ref, *, heads, dk, dv):
    c = la_ref.shape[0]
    row = lax.broadcasted_iota(jnp.int32, (c, c), 0)
    col = lax.broadcasted_iota(jnp.int32, (c, c), 1)
    rid = lax.broadcasted_iota(jnp.int32, (c, dk), 0)
    gain = g_ref[...]
    ones = jnp.ones((c, LANE), F32)
    for hd in range(heads):
        la = la_ref[:, hd * dk:(hd + 1) * dk]
        b = la
        sh = 1
        while sh < c:
            b = b + jnp.where(rid >= sh, pltpu.roll(b, sh, 0), 0.0)
            sh *= 2
        q = q_ref[:, hd * dk:(hd + 1) * dk].astype(F32) * (dk ** -0.5)
        k = k_ref[:, hd * dk:(hd + 1) * dk].astype(F32)
        v = v_ref[:, hd * dv:(hd + 1) * dv]
        d = b - b[c // 2 - 1:c // 2, :]
        a = jnp.where(col <= row, _dot_nt((q * jnp.exp(d)).astype(BF16), (k * jnp.exp(-d)).astype(BF16)), 0.0)
        st = s0_ref[hd]
        o = _dot(a.astype(BF16), v) + _dot((q * jnp.exp(b)).astype(BF16), st.astype(BF16))
        k_end = (k * jnp.exp(b[c - 1:c, :] - b)).astype(BF16)
        decay = jnp.exp(lax.dot_general(la, ones, TN_DIMS, preferred_element_type=F32,
                                        precision=lax.Precision.HIGHEST))
        sout_ref[hd] = (st * jnp.tile(decay, (1, dv // LANE))
                        + lax.dot_general(k_end, v, TN_DIMS, preferred_element_type=F32))
        o_ref[:, hd * dv:(hd + 1) * dv] = (o * _rms_inv(o, dv) * gain).astype(o_ref.dtype)


def _gla_step(zp, la, gain, s0, c, heads, dk, dv):
    nb = s0.shape[0]
    gk, gv = heads * dk, heads * dv
    assert la.shape[0] == nb * c and c <= GLA_LEAF and gv % gk == 0
    state = pl.BlockSpec((None, heads, dk, dv), lambda bb: (bb, 0, 0, 0))
    return pl.pallas_call(
        functools.partial(_gla_step_kernel, heads=heads, dk=dk, dv=dv),
        grid=(nb,),
        in_specs=[pl.BlockSpec((c, gk), lambda bb: (bb, 0)), pl.BlockSpec((c, gk), lambda bb: (bb, 1)),
                  pl.BlockSpec((c, gv), lambda bb: (bb, 2 * gk // gv)), pl.BlockSpec((c, gk), lambda bb: (bb, 0)),
                  pl.BlockSpec((1, dv), lambda bb: (0, 0)), state],
        out_specs=[pl.BlockSpec((c, gv), lambda bb: (bb, 0)), state],
        out_shape=[jax.ShapeDtypeStruct((nb * c, gv), BF16), jax.ShapeDtypeStruct((nb, heads, dk, dv), F32)],
        compiler_params=_params("parallel"),
        name="gla_step",
    )(zp, zp, zp, la, gain, s0)


def _mix_kernel(om_ref, og_ref, gate_o_ref, gate_m_ref, gate_g_ref, x_ref, w_ref, g_ref, y_ref, h_ref):
    o_gla = og_ref[...].astype(F32) * gate_o_ref[...].astype(F32)
    mixed = gate_m_ref[...].astype(F32) * om_ref[...].astype(F32) + gate_g_ref[...].astype(F32) * o_gla
    y = x_ref[...] + _dot(mixed.astype(BF16), w_ref[...])
    y_ref[...] = y
    h_ref[...] = (y * _rms_inv(y, y.shape[-1]) * g_ref[...]).astype(h_ref.dtype)


def _mix(o_mla, o_gla, zp, gate_block0, x, w_o, g, tm):
    t, d = x.shape
    row = lambda i: (i, 0)
    return pl.pallas_call(
        _mix_kernel,
        grid=(t // tm,),
        in_specs=[pl.BlockSpec((tm, d), row), pl.BlockSpec((tm, d), row),
                  pl.BlockSpec((tm, d), lambda i: (i, gate_block0)),
                  pl.BlockSpec((tm, d), lambda i: (i, gate_block0 + 1)),
                  pl.BlockSpec((tm, d), lambda i: (i, gate_block0 + 2)),
                  pl.BlockSpec((tm, d), row), pl.BlockSpec((d, d), lambda i: (0, 0)),
                  pl.BlockSpec((1, d), lambda i: (0, 0))],
        out_specs=[pl.BlockSpec((tm, d), row), pl.BlockSpec((tm, d), row)],
        out_shape=[jax.ShapeDtypeStruct((t, d), F32), jax.ShapeDtypeStruct((t, d), BF16)],
        compiler_params=_params("parallel"),
        name="mix",
    )(o_mla, o_gla, zp, zp, zp, x, w_o, g)


def _ffn_kernel(h_ref, y_ref, wu_ref, wd_ref, o_ref):
    f = pl.program_id(1)

    @pl.when(f == 0)
    def _():
        o_ref[...] = y_ref[...]

    u = jnp.maximum(_dot(h_ref[...], wu_ref[...]), 0.0)
    o_ref[...] += _dot((u * u).astype(BF16), wd_ref[...])


def _ffn(h, y, w_up, w_down, tm, tf):
    t, d = y.shape
    dff = w_up.shape[1]
    return pl.pallas_call(
        _ffn_kernel,
        grid=(t // tm, dff // tf),
        in_specs=[pl.BlockSpec((tm, d), lambda i, f: (i, 0)), pl.BlockSpec((tm, d), lambda i, f: (i, 0)),
                  pl.BlockSpec((d, tf), lambda i, f: (0, f)), pl.BlockSpec((tf, d), lambda i, f: (f, 0))],
        out_specs=pl.BlockSpec((tm, d), lambda i, f: (i, 0)),
        out_shape=jax.ShapeDtypeStruct((t, d), F32),
        compiler_params=_params("parallel", "arbitrary"),
        name="ffn",
    )(h, y, w_up, w_down)


def _rope_table(pos):
    half = MLA_ROPE // 2
    freqs = jnp.power(ROPE_THETA, -jnp.arange(half, dtype=F32) / half)
    ang = pos[:, None] * freqs[None, :]
    cos, sin = jnp.cos(ang), jnp.sin(ang)
    return jnp.concatenate([cos, cos, -sin, sin], axis=1)


def _swap_halves(w):
    half = w.shape[-1] // 2
    return jnp.concatenate([w[..., half:], w[..., :half]], axis=-1)


def _pick(t, pref):
    while t % pref:
        pref //= 2
    return pref


def kernel(x_prompt, x_sample, cache_mla_ckv, cache_mla_krope, state_gla, norm_mix_g, w_in, mla_q_norm_g,
           mla_w_uq, mla_kv_norm_g, mla_w_ukv, mla_q_gain_nope, mla_q_gain_rope, mla_k_gain_nope,
           mla_k_gain_rope, gla_w_a2, gla_b_a, gla_norm_g, w_o, norm_ffn_g, ffn_w_up, ffn_w_down):
    depth = w_in.shape[0]
    bp, seq, d = x_prompt.shape
    nb, dec_seq, _ = x_sample.shape
    past = cache_mla_ckv.shape[2]
    q_rank, heads = mla_w_uq.shape[1], mla_w_uq.shape[2]
    kv_rank = mla_w_ukv.shape[1]
    gla_heads, dk, dv = state_gla.shape[2], state_gla.shape[3], state_gla.shape[4]
    gate_rank = gla_w_a2.shape[1]
    gk, gv = gla_heads * dk, gla_heads * dv
    assert bp == 1 and heads * MLA_V == d and gv == d and seq % CHUNK == 0

    cs_p = _rope_table(jnp.arange(seq, dtype=F32))
    cs_s = jnp.tile(_rope_table(past + jnp.arange(dec_seq, dtype=F32)), (nb, 1))

    xp = x_prompt.reshape(seq, d)
    xs = x_sample.reshape(nb * dec_seq, d)
    outs = [[] for _ in range(6)]
    for l in range(depth):
        wi = w_in[l]
        pts = np.cumsum([q_rank, kv_rank, MLA_ROPE, gk, gk, gv, gate_rank, gv, d]).tolist()
        w_qkv_lat = wi[:, :pts[1]]
        w_kr = wi[:, pts[1]:pts[2]]
        w_alr = wi[:, pts[5]:pts[6]]
        assert 3 * gate_rank <= LANE
        w_lat = jnp.concatenate(
            [w_qkv_lat, w_kr, _swap_halves(w_kr), w_alr, w_alr, w_alr,
             jnp.zeros((d, LANE - 3 * gate_rank), F32)], axis=1).astype(BF16)
        w_gla = wi[:, pts[2]:pts[5]].astype(BF16)
        w_gate = wi[:, pts[6]:].astype(BF16)
        wa2_hi = gla_w_a2[l].astype(BF16)
        wa2_lo = (gla_w_a2[l] - wa2_hi.astype(F32)).astype(BF16)
        wa2_split = jnp.concatenate(
            [wa2_hi, wa2_lo, wa2_hi, jnp.zeros((LANE - 3 * gate_rank, gk), BF16)], axis=0)

        wq = mla_w_uq[l]
        wq_r = wq[..., MLA_NOPE:]
        w_uq = jnp.concatenate([wq[..., :MLA_NOPE], wq_r, _swap_halves(wq_r)], axis=-1)
        w_uq = w_uq.reshape(q_rank, heads * MLA_HEAD_PAD).astype(BF16)
        w_uq_t = w_uq.T
        wuk = mla_w_ukv[l][..., :MLA_NOPE].reshape(kv_rank, heads * MLA_NOPE).astype(BF16)
        wuk_t = wuk.T
        wuv = mla_w_ukv[l][..., MLA_NOPE:].reshape(kv_rank, heads * MLA_V).astype(BF16)
        wuv_t = wuv.T
        gain = jnp.concatenate([mla_q_gain_nope[l] * mla_k_gain_nope[l],
                                jnp.tile(mla_q_gain_rope[l] * mla_k_gain_rope[l], 2),
                                jnp.zeros((MLA_HEAD_PAD - MLA_QK_DIM,), F32)])[None, :] * MLA_SCALE
        gain_col = gain.T * LOG2E
        wo_b = w_o[l].astype(BF16)
        wup_b = ffn_w_up[l].astype(BF16)
        wdn_b = ffn_w_down[l].astype(BF16)
        g_mix = norm_mix_g[l][None, :]
        g_q = mla_q_norm_g[l][None, :]
        g_kv = mla_kv_norm_g[l][None, :]
        g_gla = gla_norm_g[l][None, :]
        g_ffn = norm_ffn_g[l][None, :]
        ba = gla_b_a[l][None, :]

        def front(x, cs, tm):
            h = _prenorm(x, g_mix, _pick(x.shape[0], 512))
            qlat, ckv, kr, la = _mla_lat(h, w_lat, g_q, g_kv, cs, wa2_split, ba, tm, q_rank, kv_rank, gate_rank)
            z_gla = _proj(h, w_gla, _pick(x.shape[0], 1024), 2048)
            z_gate = _proj(h, w_gate, _pick(x.shape[0], 1024), gv, gates=True)
            return qlat, ckv, kr, la, z_gla, z_gate

        def back(x, o_mla, o_gla, z_gate, tm):
            y1, h2 = _mix(o_mla, o_gla, z_gate, 0, x, wo_b, g_ffn, tm)
            return _ffn(h2, y1, wup_b, wdn_b, tm, 1024)

        tile = _pick(seq, 512)
        qlat, ckv, kr, la, z_gla, z_gate = front(xp, cs_p, tile)
        qt = _mla_qt(qlat, w_uq_t, cs_p.T, gain_col, tile, 2 * tile, heads)
        kcat, vt = _mla_kv(ckv, kr, wuk, wuv_t, tile, heads)
        o_mla = _flash(qt, kcat, vt, heads)
        o_gla, st = _gla(z_gla, la, g_gla, 1, _pick(seq, 128), gla_heads, dk, dv)
        xp_next = back(xp, o_mla, o_gla, z_gate, tile)
        outs[0].append(ckv.reshape(bp, seq, kv_rank))
        outs[1].append(kr.reshape(bp, seq, MLA_ROPE))
        outs[2].append(st)
        xp = xp_next

        qlat, ckv, kr, la, z_gla, z_gate = front(xs, cs_s, nb * dec_seq)
        qcat = _mla_q(qlat, w_uq, cs_s, gain, nb * dec_seq, heads)
        o_lat = _sattn(qcat, cache_mla_ckv[l], cache_mla_krope[l], ckv, kr, wuk_t, heads, dec_seq, _pick(past, 512))
        o_mla = _svup(o_lat, wuv, heads, dec_seq)
        o_gla, st = _gla_step(z_gla, la, g_gla, state_gla[l], dec_seq, gla_heads, dk, dv)
        xs_next = back(xs, o_mla, o_gla, z_gate, nb * dec_seq)
        outs[3].append(ckv.reshape(nb, dec_seq, kv_rank))
        outs[4].append(kr.reshape(nb, dec_seq, MLA_ROPE))
        outs[5].append(st)
        xs = xs_next

    return (xp.reshape(bp, seq, d), xs.reshape(nb, dec_seq, d),
            jnp.stack(outs[0]), jnp.stack(outs[1]), jnp.stack(outs[2]),
            jnp.stack(outs[3]), jnp.stack(outs[4]), jnp.stack(outs[5]))
```

```python
import functools

import jax
import jax.numpy as jnp
import numpy as np
from jax import lax
from jax.experimental import pallas as pl
from jax.experimental.pallas import tpu as pltpu

F32 = jnp.float32
BF16 = jnp.bfloat16

EPS = 1e-6
CHUNK = 64
MLA_NOPE = 128
MLA_ROPE = 64
MLA_V = 128
MLA_QK_DIM = MLA_NOPE + MLA_ROPE
MLA_SCALE = MLA_QK_DIM ** -0.5
MLA_HEAD_PAD = 256
V_ROWS = MLA_V + 16
LOG2E = 1.4426950408889634
QBLK = 256
ROPE_THETA = 10000.0
GLA_TAU = 16.0
GLA_LEAF = 32
LANE = 128
V7X_VMEM_BYTES = 64 * 1024 * 1024
VMEM_LIMIT = V7X_VMEM_BYTES * 7 // 8

NT_DIMS = (((1,), (1,)), ((), ()))
TN_DIMS = (((0,), (0,)), ((), ()))


def _params(*sem):
    return pltpu.CompilerParams(dimension_semantics=sem, vmem_limit_bytes=VMEM_LIMIT)


def _dot(a, b):
    return jnp.dot(a, b, preferred_element_type=F32)


def _dot_nt(a, b):
    return lax.dot_general(a, b, NT_DIMS, preferred_element_type=F32)


def _rms_inv(x, n):
    return lax.rsqrt(jnp.sum(x * x, axis=-1, keepdims=True) / n + EPS)


def _prenorm_kernel(x_ref, g_ref, h_ref):
    x = x_ref[...]
    h_ref[...] = (x * _rms_inv(x, x.shape[-1]) * g_ref[...]).astype(h_ref.dtype)


def _prenorm(x, g, tm):
    t, d = x.shape
    return pl.pallas_call(
        _prenorm_kernel,
        grid=(t // tm,),
        in_specs=[pl.BlockSpec((tm, d), lambda i: (i, 0)), pl.BlockSpec((1, d), lambda i: (0, 0))],
        out_specs=pl.BlockSpec((tm, d), lambda i: (i, 0)),
        out_shape=jax.ShapeDtypeStruct((t, d), BF16),
        compiler_params=_params("parallel"),
        name="prenorm",
    )(x, g)


def _mla_lat_kernel(h_ref, w_ref, qg_ref, kvg_ref, cs_ref, wa2_ref, ba_ref,
                    qlat_ref, ckv_ref, kr_ref, la_ref, *, q_rank, kv_rank, gate_rank):
    z = _dot(h_ref[...], w_ref[...])
    q_lat = z[:, :q_rank]
    qlat_ref[...] = (q_lat * _rms_inv(q_lat, q_rank) * qg_ref[...]).astype(qlat_ref.dtype)
    kv_lat = z[:, q_rank:q_rank + kv_rank]
    ckv_ref[...] = kv_lat * _rms_inv(kv_lat, kv_rank) * kvg_ref[...]
    o = q_rank + kv_rank
    rr = z[:, o:o + LANE] * cs_ref[...]
    kr_ref[...] = rr[:, :MLA_ROPE] + rr[:, MLA_ROPE:]
    a3 = z[:, o + LANE:o + 2 * LANE]
    a_hi = a3.astype(BF16)
    a_lo = (a3 - a_hi.astype(F32)).astype(BF16)
    lane = lax.broadcasted_iota(jnp.int32, a3.shape, 1)
    u = _dot(jnp.where(lane < 2 * gate_rank, a_hi, a_lo), wa2_ref[...]) + ba_ref[...]
    log_sig = jnp.minimum(u, 0.0) - jnp.log1p(jnp.exp(-jnp.abs(u)))
    la_ref[...] = log_sig / GLA_TAU


def _mla_lat(h, w_lat, qg, kvg, cs, wa2_split, ba, tm, q_rank, kv_rank, gate_rank):
    t, d = h.shape
    n = w_lat.shape[1]
    gk = wa2_split.shape[1]
    row = lambda i: (i, 0)
    fix = lambda i: (0, 0)
    return pl.pallas_call(
        functools.partial(_mla_lat_kernel, q_rank=q_rank, kv_rank=kv_rank, gate_rank=gate_rank),
        grid=(t // tm,),
        in_specs=[pl.BlockSpec((tm, d), row), pl.BlockSpec((d, n), fix),
                  pl.BlockSpec((1, q_rank), fix), pl.BlockSpec((1, kv_rank), fix),
                  pl.BlockSpec((tm, LANE), row), pl.BlockSpec((LANE, gk), fix), pl.BlockSpec((1, gk), fix)],
        out_specs=[pl.BlockSpec((tm, q_rank), row), pl.BlockSpec((tm, kv_rank), row),
                   pl.BlockSpec((tm, MLA_ROPE), row), pl.BlockSpec((tm, gk), row)],
        out_shape=[jax.ShapeDtypeStruct((t, q_rank), BF16), jax.ShapeDtypeStruct((t, kv_rank), F32),
                   jax.ShapeDtypeStruct((t, MLA_ROPE), F32), jax.ShapeDtypeStruct((t, gk), F32)],
        compiler_params=_params("parallel"),
        name="mla_lat",
    )(h, w_lat, qg, kvg, cs, wa2_split, ba)


def _proj_kernel(h_ref, w_ref, o_ref, *, gates):
    z = _dot(h_ref[...], w_ref[...])
    if gates:
        z = jax.nn.sigmoid(z) * jnp.where(pl.program_id(1) == 0, z, 1.0)
    o_ref[...] = z.astype(o_ref.dtype)


def _proj(h, w, tm, tn, gates=False):
    t, d = h.shape
    n = w.shape[1]
    return pl.pallas_call(
        functools.partial(_proj_kernel, gates=gates),
        grid=(t // tm, n // tn),
        in_specs=[pl.BlockSpec((tm, d), lambda i, j: (i, 0)), pl.BlockSpec((d, tn), lambda i, j: (0, j))],
        out_specs=pl.BlockSpec((tm, tn), lambda i, j: (i, j)),
        out_shape=jax.ShapeDtypeStruct((t, n), BF16),
        compiler_params=_params("parallel", "parallel"),
        name="proj",
    )(h, w)


def _mla_q_kernel(ql_ref, w_ref, cs_ref, gain_ref, q_ref, *, heads):
    ql = ql_ref[...]
    cs = cs_ref[...]
    gain = gain_ref[...]
    lane = lax.broadcasted_iota(jnp.int32, (1, LANE), 1)
    for hd in range(heads):
        c0 = hd * MLA_HEAD_PAD
        z = _dot(ql, w_ref[:, c0:c0 + MLA_HEAD_PAD])
        nope = z[:, :MLA_NOPE]
        rr = z[:, MLA_NOPE:] * cs
        rot = rr + pltpu.roll(rr, MLA_ROPE, 1)
        ss = (jnp.sum(nope * nope, axis=-1, keepdims=True)
              + jnp.sum(jnp.where(lane < MLA_ROPE, rot * rot, 0.0), axis=-1, keepdims=True))
        inv = lax.rsqrt(ss / MLA_QK_DIM + EPS)
        q_ref[:, c0:c0 + MLA_NOPE] = (nope * inv * gain[:, :MLA_NOPE]).astype(q_ref.dtype)
        q_ref[:, c0 + MLA_NOPE:c0 + MLA_HEAD_PAD] = (rot * inv * gain[:, MLA_NOPE:]).astype(q_ref.dtype)


def _mla_q(qlat, w_uq, cs, gain, tm, heads):
    t, r = qlat.shape
    n = heads * MLA_HEAD_PAD
    return pl.pallas_call(
        functools.partial(_mla_q_kernel, heads=heads),
        grid=(t // tm,),
        in_specs=[pl.BlockSpec((tm, r), lambda i: (i, 0)), pl.BlockSpec((r, n), lambda i: (0, 0)),
                  pl.BlockSpec((tm, LANE), lambda i: (i, 0)), pl.BlockSpec((1, MLA_HEAD_PAD), lambda i: (0, 0))],
        out_specs=pl.BlockSpec((tm, n), lambda i: (i, 0)),
        out_shape=jax.ShapeDtypeStruct((t, n), BF16),
        compiler_params=_params("parallel"),
        name="mla_q",
    )(qlat, w_uq, cs, gain)


def _mla_qt_kernel(ql_ref, w_ref, cs_ref, gain_ref, q_ref, *, heads):
    ql = ql_ref[...]
    cs = cs_ref[...]
    gain = gain_ref[...]
    for hd in range(heads):
        r0 = hd * MLA_HEAD_PAD
        z = _dot_nt(w_ref[r0:r0 + MLA_HEAD_PAD, :], ql)
        nope = z[:MLA_NOPE]
        rr = z[MLA_NOPE:] * cs
        rot = rr[:MLA_ROPE] + rr[MLA_ROPE:]
        ss = jnp.sum(nope * nope, axis=0, keepdims=True) + jnp.sum(rot * rot, axis=0, keepdims=True)
        inv = lax.rsqrt(ss / MLA_QK_DIM + EPS)
        qn = (nope * inv * gain[:MLA_NOPE]).astype(q_ref.dtype)
        qr = (rot * inv * gain[MLA_NOPE:MLA_QK_DIM]).astype(q_ref.dtype)
        for bb in range(q_ref.shape[1]):
            q_ref[hd, bb, :MLA_NOPE, :] = qn[:, bb * QBLK:(bb + 1) * QBLK]
            q_ref[hd, bb, MLA_NOPE:MLA_QK_DIM, :] = qr[:, bb * QBLK:(bb + 1) * QBLK]
            q_ref[hd, bb, MLA_QK_DIM:, :] = jnp.zeros((MLA_HEAD_PAD - MLA_QK_DIM, QBLK), q_ref.dtype)


def _mla_qt(qlat, w_uq_t, cs_t, gain_col, tm, tq, heads):
    t, r = qlat.shape
    n = heads * MLA_HEAD_PAD
    per = tq // tm
    assert tm % QBLK == 0 and tq % tm == 0
    return pl.pallas_call(
        functools.partial(_mla_qt_kernel, heads=heads),
        grid=(t // tm,),
        in_specs=[pl.BlockSpec((tm, r), lambda i: (i, 0)), pl.BlockSpec((n, r), lambda i: (0, 0)),
                  pl.BlockSpec((LANE, tm), lambda i: (0, i)), pl.BlockSpec((MLA_HEAD_PAD, 1), lambda i: (0, 0))],
        out_specs=pl.BlockSpec((heads, None, tm // QBLK, MLA_HEAD_PAD, QBLK),
                               lambda i: (0, i // per, i % per, 0, 0)),
        out_shape=jax.ShapeDtypeStruct((heads, t // tq, tq // QBLK, MLA_HEAD_PAD, QBLK), BF16),
        compiler_params=_params("parallel"),
        name="mla_qt",
    )(qlat, w_uq_t, cs_t, gain_col)


def _mla_kv_kernel(ckv_ref, kr_ref, wk_ref, wvt_ref, k_ref, vt_ref, *, heads):
    c = ckv_ref[...].astype(BF16)
    kr = kr_ref[...]
    kr2 = jnp.sum(kr * kr, axis=-1, keepdims=True)
    kr_pad = jnp.concatenate([kr, jnp.zeros_like(kr)], axis=1)
    for pair in range(heads // 2):
        z = _dot(c, wk_ref[:, pair * 2 * MLA_NOPE:(pair + 1) * 2 * MLA_NOPE])
        for sub in range(2):
            kn = z[:, sub * MLA_NOPE:(sub + 1) * MLA_NOPE]
            inv = lax.rsqrt((jnp.sum(kn * kn, axis=-1, keepdims=True) + kr2) / MLA_QK_DIM + EPS)
            c0 = (2 * pair + sub) * MLA_HEAD_PAD
            k_ref[:, c0:c0 + MLA_NOPE] = (kn * inv).astype(k_ref.dtype)
            k_ref[:, c0 + MLA_NOPE:c0 + MLA_HEAD_PAD] = (kr_pad * inv).astype(k_ref.dtype)
    vt = _dot_nt(wvt_ref[...], c)
    for hd in range(heads):
        vt_ref[hd, :MLA_V, :] = vt[hd * MLA_V:(hd + 1) * MLA_V].astype(vt_ref.dtype)
        vt_ref[hd, MLA_V:, :] = jnp.ones((V_ROWS - MLA_V, vt.shape[1]), vt_ref.dtype)


def _mla_kv(ckv, kr, w_uk, w_uv_t, tm, heads):
    t, r = ckv.shape
    return pl.pallas_call(
        functools.partial(_mla_kv_kernel, heads=heads),
        grid=(t // tm,),
        in_specs=[pl.BlockSpec((tm, r), lambda i: (i, 0)), pl.BlockSpec((tm, MLA_ROPE), lambda i: (i, 0)),
                  pl.BlockSpec((r, heads * MLA_NOPE), lambda i: (0, 0)),
                  pl.BlockSpec((heads * MLA_V, r), lambda i: (0, 0))],
        out_specs=[pl.BlockSpec((tm, heads * MLA_HEAD_PAD), lambda i: (i, 0)),
                   pl.BlockSpec((heads, None, V_ROWS, tm), lambda i: (0, i, 0, 0))],
        out_shape=[jax.ShapeDtypeStruct((t, heads * MLA_HEAD_PAD), BF16),
                   jax.ShapeDtypeStruct((heads, t // tm, V_ROWS, tm), BF16)],
        compiler_params=_params("parallel"),
        name="mla_kv",
    )(ckv, kr, w_uk, w_uv_t)


def _diag_mode(u, c, tk):
    k_lo, k_hi = (u * tk) // CHUNK, ((u + 1) * tk - 1) // CHUNK
    q_lo, q_hi = (c * QBLK) // CHUNK, ((c + 1) * QBLK - 1) // CHUNK
    return "all" if k_hi <= q_lo else "none" if k_lo > q_hi else "some"


def _diag_bias(tk, n_blk):
    index, blocks = {}, []
    for u in range(2):
        for c in range(n_blk):
            if _diag_mode(u, c, tk) == "some":
                k_chunk = (u * tk + np.arange(tk)[:, None]) // CHUNK
                q_chunk = (c * QBLK + np.arange(QBLK)[None, :]) // CHUNK
                index[(u, c)] = len(blocks)
                blocks.append(np.where(k_chunk <= q_chunk, 0.0, -np.inf).astype(np.float32))
    return index, np.stack(blocks)


def _flash_kernel(qt_ref, k_ref, vt_ref, bias_ref, o_ref, m_ref, acc_ref, s_ref, mx_ref, p_ref, al_ref,
                  *, tq, tk, bias_index):
    n_q, n_blk = qt_ref.shape[0], qt_ref.shape[1]
    acc_ref[...] = jnp.zeros_like(acc_ref)


    def diag_mode(u, c):
        return _diag_mode(u, c, tk)

    def score(i, t, slot, c, mode):
        if mode == "none":
            return
        k = k_ref[pl.ds(pl.multiple_of(t * tk, tk), tk), :]
        s = _dot(k, qt_ref[i, c])
        if mode == "some":
            s = s + bias_ref[bias_index[(slot, c)]]
        s_ref[slot, c] = s
        mx_ref[slot, c] = jnp.max(s, axis=0, keepdims=True)

    def soften(slot, c, mode="all", first=False):
        if mode == "none":
            return
        if first:
            m_new = mx_ref[slot, c]
            al_ref[slot, c] = jnp.zeros_like(m_new)
        else:
            m_prev = m_ref[c]
            m_new = jnp.maximum(m_prev, mx_ref[slot, c])
            al_ref[slot, c] = jnp.exp2(m_prev - m_new)
        p_ref[slot, c] = jnp.exp2(s_ref[slot, c] - m_new).astype(BF16)
        m_ref[c] = m_new

    def gather(t, slot, c, mode="all"):
        if mode == "none":
            return
        acc_ref[c] = al_ref[slot, c] * acc_ref[c] + _dot(vt_ref[t], p_ref[slot, c])

    def pair(i, g, diag):
        for c in range(n_blk):
            score(i, g, 0, c, diag_mode(0, c) if diag else "all")
            soften(1, c)
            gather(g - 2, 0, c)
        for c in range(n_blk):
            score(i, g + 1, 1, c, diag_mode(1, c) if diag else "all")
            soften(0, c, diag_mode(0, c) if diag else "all")
            gather(g - 1, 1, c)

    def head(i, diag):
        for c in range(n_blk):
            score(i, 0, 0, c, diag_mode(0, c) if diag else "all")
        for c in range(n_blk):
            score(i, 1, 1, c, diag_mode(1, c) if diag else "all")
            soften(0, c, diag_mode(0, c) if diag else "all", first=True)

    def tail(i):
        for c in range(n_blk):
            soften(1, c, diag_mode(1, c))
            gather(2 * i, 0, c, diag_mode(0, c))
        for c in range(n_blk):
            gather(2 * i + 1, 1, c, diag_mode(1, c))
        for c in range(n_blk):
            acc = acc_ref[c]
            rows = pl.ds(pl.multiple_of(i * tq + c * QBLK, QBLK), QBLK)
            o_ref[rows, :] = (acc[:MLA_V] / acc[MLA_V:MLA_V + 1]).T.astype(o_ref.dtype)

    def middle(i):
        def body(j, carry):
            pair(i, 2 * j, False)
            return carry

        lax.fori_loop(1, i, body, 0)
        pair(i, 2 * i, True)

    head(0, True)
    if n_q > 1:
        tail(0)
        head(1, False)

        def outer(i, carry):
            middle(i)
            tail(i)
            head(i + 1, False)
            return carry

        lax.fori_loop(1, n_q - 1, outer, 0)
        middle(n_q - 1)
    tail(n_q - 1)


def _flash(qt, k, vt, heads):
    t = k.shape[0]
    n_q, n_blk = qt.shape[1], qt.shape[2]
    tq = n_blk * QBLK
    n_kt, tk = vt.shape[1], vt.shape[3]
    assert tq == 2 * tk and tk % CHUNK == 0 and qt.shape[4] == QBLK and n_q * tq == t
    bias_index, bias = _diag_bias(tk, n_blk)
    return pl.pallas_call(
        functools.partial(_flash_kernel, tq=tq, tk=tk, bias_index=bias_index),
        grid=(heads,),
        in_specs=[pl.BlockSpec((None, n_q, n_blk, MLA_HEAD_PAD, QBLK), lambda hd: (hd, 0, 0, 0, 0)),
                  pl.BlockSpec((t, MLA_HEAD_PAD), lambda hd: (0, hd)),
                  pl.BlockSpec((None, n_kt, V_ROWS, tk), lambda hd: (hd, 0, 0, 0)),
                  pl.BlockSpec(bias.shape, lambda hd: (0, 0, 0))],
        out_specs=pl.BlockSpec((t, MLA_V), lambda hd: (0, hd)),
        out_shape=jax.ShapeDtypeStruct((t, heads * MLA_V), BF16),
        scratch_shapes=[pltpu.VMEM((n_blk, 1, QBLK), F32), pltpu.VMEM((n_blk, V_ROWS, QBLK), F32),
                        pltpu.VMEM((2, n_blk, tk, QBLK), F32), pltpu.VMEM((2, n_blk, 1, QBLK), F32),
                        pltpu.VMEM((2, n_blk, tk, QBLK), BF16), pltpu.VMEM((2, n_blk, 1, QBLK), F32)],
        compiler_params=_params("parallel"),
        name="flash",
    )(qt, k, vt, jnp.asarray(bias))


def _sattn_kernel(q_ref, cc_ref, ck_ref, nc_ref, nk_ref, wuk_ref, o_ref,
                  qabs_ref, qr_ref, m_ref, l_ref, acc_ref, *, heads, nq, tk, n_new):
    past, rank = cc_ref.shape
    n_tiles = past // tk
    for hd in range(heads):
        c0 = hd * MLA_HEAD_PAD
        qn = q_ref[:, c0:c0 + MLA_NOPE]
        qabs_ref[hd * nq:(hd + 1) * nq, :] = _dot(
            qn, wuk_ref[hd * MLA_NOPE:(hd + 1) * MLA_NOPE, :]).astype(qabs_ref.dtype)
        qr_ref[hd * nq:(hd + 1) * nq, :] = q_ref[:, c0 + MLA_NOPE:c0 + MLA_NOPE + MLA_ROPE]
    m_ref[...] = jnp.full_like(m_ref, -jnp.inf)
    l_ref[...] = jnp.zeros_like(l_ref)
    acc_ref[...] = jnp.zeros_like(acc_ref)

    def tile(t):
        if t < n_tiles:
            return cc_ref[t * tk:(t + 1) * tk, :].astype(BF16), ck_ref[t * tk:(t + 1) * tk, :]
        pad = LANE - n_new
        c_new = jnp.concatenate([nc_ref[...], jnp.zeros((pad, rank), F32)], axis=0)
        k_new = jnp.concatenate([nk_ref[...], jnp.zeros((pad, MLA_ROPE), F32)], axis=0)
        return c_new.astype(BF16), k_new

    def key_norm(c, kr):
        n_keys = c.shape[0]
        kn_t = _dot_nt(wuk_ref[...], c)
        ss_t = jnp.sum((kn_t * kn_t).reshape(heads, MLA_NOPE, n_keys), axis=1)
        kr2_t = lax.dot_general(jnp.ones((8, MLA_ROPE), F32), kr * kr, NT_DIMS,
                                preferred_element_type=F32, precision=lax.Precision.HIGHEST)[0:1]
        return lax.rsqrt((ss_t + kr2_t) / MLA_QK_DIM + EPS)

    def attend(c, kr, inv_t, n_valid):
        n_keys = c.shape[0]
        s = _dot_nt(qabs_ref[...], c) + _dot_nt(qr_ref[...], kr.astype(BF16))
        s = s * jnp.broadcast_to(inv_t[:, None, :], (heads, nq, n_keys)).reshape(heads * nq, n_keys)
        if n_valid < n_keys:
            s = jnp.where(lax.broadcasted_iota(jnp.int32, s.shape, 1) < n_valid, s, -jnp.inf)
        m_prev = m_ref[...]
        m_new = jnp.maximum(m_prev, jnp.max(s, axis=-1, keepdims=True))
        alpha = jnp.exp(m_prev - m_new)
        p = jnp.exp(s - m_new)
        l_ref[...] = alpha * l_ref[...] + jnp.sum(p, axis=-1, keepdims=True)
        acc_ref[...] = alpha * acc_ref[...] + _dot(p.astype(BF16), c)
        m_ref[...] = m_new

    c, kr = tile(0)
    inv_t = key_norm(c, kr)
    for t in range(n_tiles + 1):
        if t < n_tiles:
            c_next, kr_next = tile(t + 1)
            inv_next = key_norm(c_next, kr_next)
        attend(c, kr, inv_t, tk if t < n_tiles else n_new)
        if t < n_tiles:
            c, kr, inv_t = c_next, kr_next, inv_next
    o_ref[...] = (acc_ref[...] / l_ref[...]).astype(o_ref.dtype)


def _sattn(q, cache_c, cache_k, new_c, new_k, wuk_t, heads, nq, tk):
    nb, past, rank = cache_c.shape
    return pl.pallas_call(
        functools.partial(_sattn_kernel, heads=heads, nq=nq, tk=tk, n_new=nq),
        grid=(nb,),
        in_specs=[pl.BlockSpec((nq, heads * MLA_HEAD_PAD), lambda b: (b, 0)),
                  pl.BlockSpec((None, past, rank), lambda b: (b, 0, 0)),
                  pl.BlockSpec((None, past, MLA_ROPE), lambda b: (b, 0, 0)),
                  pl.BlockSpec((nq, rank), lambda b: (b, 0)),
                  pl.BlockSpec((nq, MLA_ROPE), lambda b: (b, 0)),
                  pl.BlockSpec(wuk_t.shape, lambda b: (0, 0))],
        out_specs=pl.BlockSpec((None, heads * nq, rank), lambda b: (b, 0, 0)),
        out_shape=jax.ShapeDtypeStruct((nb, heads * nq, rank), BF16),
        scratch_shapes=[pltpu.VMEM((heads * nq, rank), BF16), pltpu.VMEM((heads * nq, MLA_ROPE), BF16),
                        pltpu.VMEM((heads * nq, 1), F32), pltpu.VMEM((heads * nq, 1), F32),
                        pltpu.VMEM((heads * nq, rank), F32)],
        compiler_params=_params("parallel"),
        name="sattn",
    )(q, cache_c, cache_k, new_c, new_k, wuk_t)


def _svup_kernel(ol_ref, w_ref, o_ref, *, nq):
    nb, rows, rank = ol_ref.shape
    for hh in range(rows // nq):
        x = ol_ref[:, hh * nq:(hh + 1) * nq, :].reshape(nb * nq, rank)
        o_ref[:, hh * MLA_V:(hh + 1) * MLA_V] = _dot(x, w_ref[:, hh * MLA_V:(hh + 1) * MLA_V]).astype(o_ref.dtype)


def _svup(o_lat, wuv, heads, nq):
    nb, _, rank = o_lat.shape
    hs = 4 if heads % 4 == 0 else 1
    return pl.pallas_call(
        functools.partial(_svup_kernel, nq=nq),
        grid=(heads // hs,),
        in_specs=[pl.BlockSpec((nb, hs * nq, rank), lambda g: (0, g, 0)),
                  pl.BlockSpec((rank, hs * MLA_V), lambda g: (0, g))],
        out_specs=pl.BlockSpec((nb * nq, hs * MLA_V), lambda g: (0, g)),
        out_shape=jax.ShapeDtypeStruct((nb * nq, heads * MLA_V), BF16),
        compiler_params=_params("parallel"),
        name="svup",
    )(o_lat, wuv)


def _gla_levels(c):
    leaf = min(c, GLA_LEAF)
    levels = [(leaf, leaf // 2 - 1)]
    g = 2 * leaf
    while g <= c:
        levels.append((g, g // 2 - 1))
        g *= 2
    return levels


def _gla_kernel(q_ref, k_ref, v_ref, la_ref, g_ref, o_ref, sout_ref, st_ref, *, heads, dk, dv, c):
    j = pl.program_id(1)
    nj = pl.num_programs(1)

    @pl.when(j == 0)
    def _():
        st_ref[...] = jnp.zeros_like(st_ref)

    levels = _gla_levels(c)
    row = lax.broadcasted_iota(jnp.int32, (c, c), 0)
    col = lax.broadcasted_iota(jnp.int32, (c, c), 1)
    masks = []
    for lv, (g, _) in enumerate(levels):
        same = (row // g) == (col // g)
        if lv == 0:
            masks.append(jnp.logical_and(same, col <= row))
        else:
            half = g // 2
            masks.append(jnp.logical_and(same, jnp.logical_and((row // half) % 2 == 1, (col // half) % 2 == 0)))
    rid = lax.broadcasted_iota(jnp.int32, (c, dk), 0)
    gain = g_ref[...]

    for hd in range(heads):
        la = la_ref[:, hd * dk:(hd + 1) * dk]
        b = la
        sh = 1
        while sh < c:
            b = b + jnp.where(rid >= sh, pltpu.roll(b, sh, 0), 0.0)
            sh *= 2
        b_end = b[c - 1:c, :]
        q = q_ref[:, hd * dk:(hd + 1) * dk].astype(F32) * (dk ** -0.5)
        k = k_ref[:, hd * dk:(hd + 1) * dk].astype(F32)
        v = v_ref[:, hd * dv:(hd + 1) * dv]

        a = jnp.zeros((c, c), F32)
        for lv, (g, r) in enumerate(levels):
            ref_rows = jnp.broadcast_to(b.reshape(c // g, g, dk)[:, r:r + 1, :], (c // g, g, dk)).reshape(c, dk)
            d = b - ref_rows
            if lv == 0:
                fq, fk = jnp.exp(d), jnp.exp(-d)
            else:
                fq = fk = jnp.exp(-jnp.abs(d))
            a_lv = _dot_nt((q * fq).astype(BF16), (k * fk).astype(BF16))
            a = jnp.where(masks[lv], a_lv, a)

        st = st_ref[hd]
        o = _dot(a.astype(BF16), v) + _dot_nt((q * jnp.exp(b)).astype(BF16), st.astype(BF16))
        k_end = (k * jnp.exp(b_end - b)).astype(BF16)
        st_ref[hd] = st * jnp.exp(b_end) + lax.dot_general(v, k_end, TN_DIMS, preferred_element_type=F32)
        o_ref[:, hd * dv:(hd + 1) * dv] = (o * _rms_inv(o, dv) * gain).astype(o_ref.dtype)

    @pl.when(j == nj - 1)
    def _():
        for hd in range(heads):
            sout_ref[hd] = st_ref[hd].T


def _gla(zp, la, gain, nb, c, heads, dk, dv):
    t = la.shape[0]
    nj = t // (nb * c)
    gk, gv = heads * dk, heads * dv
    assert gv % gk == 0
    row = lambda bb, j: (bb * nj + j, 0)
    in_specs = [pl.BlockSpec((c, gk), row),
                pl.BlockSpec((c, gk), lambda bb, j: (bb * nj + j, 1)),
                pl.BlockSpec((c, gv), lambda bb, j: (bb * nj + j, 2 * gk // gv)),
                pl.BlockSpec((c, gk), row),
                pl.BlockSpec((1, dv), lambda bb, j: (0, 0))]
    return pl.pallas_call(
        functools.partial(_gla_kernel, heads=heads, dk=dk, dv=dv, c=c),
        grid=(nb, nj),
        in_specs=in_specs,
        out_specs=[pl.BlockSpec((c, gv), row),
                   pl.BlockSpec((None, heads, dk, dv), lambda bb, j: (bb, 0, 0, 0))],
        out_shape=[jax.ShapeDtypeStruct((t, gv), BF16), jax.ShapeDtypeStruct((nb, heads, dk, dv), F32)],
        scratch_shapes=[pltpu.VMEM((heads, dv, dk), F32)],
        compiler_params=_params("parallel", "arbitrary"),
        name="gla",
    )(zp, zp, zp, la, gain)


def _gla_step_kernel(q_ref, k_ref, v_ref, la_ref, g_ref, s0_ref, o_ref, sout_ref, *, heads, dk, dv):
    c = la_ref.shape[0]
    row = lax.broadcasted_iota(jnp.int32, (c, c), 0)
    col = lax.broadcasted_iota(jnp.int32, (c, c), 1)
    rid = lax.broadcasted_iota(jnp.int32, (c, dk), 0)
    gain = g_ref[...]
    ones = jnp.ones((c, LANE), F32)
    for hd in range(heads):
        la = la_ref[:, hd * dk:(hd + 1) * dk]
        b = la
        sh = 1
        while sh < c:
            b = b + jnp.where(rid >= sh, pltpu.roll(b, sh, 0), 0.0)
            sh *= 2
        q = q_ref[:, hd * dk:(hd + 1) * dk].astype(F32) * (dk ** -0.5)
        k = k_ref[:, hd * dk:(hd + 1) * dk].astype(F32)
        v = v_ref[:, hd * dv:(hd + 1) * dv]
        d = b - b[c // 2 - 1:c // 2, :]
        a = jnp.where(col <= row, _dot_nt((q * jnp.exp(d)).astype(BF16), (k * jnp.exp(-d)).astype(BF16)), 0.0)
        st = s0_ref[hd]
        o = _dot(a.astype(BF16), v) + _dot((q * jnp.exp(b)).astype(BF16), st.astype(BF16))
        k_end = (k * jnp.exp(b[c - 1:c, :] - b)).astype(BF16)
        decay = jnp.exp(lax.dot_general(la, ones, TN_DIMS, preferred_element_type=F32,
                                        precision=lax.Precision.HIGHEST))
        sout_ref[hd] = (st * jnp.tile(decay, (1, dv // LANE))
                        + lax.dot_general(k_end, v, TN_DIMS, preferred_element_type=F32))
        o_ref[:, hd * dv:(hd + 1) * dv] = (o * _rms_inv(o, dv) * gain).astype(o_ref.dtype)


def _gla_step(zp, la, gain, s0, c, heads, dk, dv):
    nb = s0.shape[0]
    gk, gv = heads * dk, heads * dv
    assert la.shape[0] == nb * c and c <= GLA_LEAF and gv % gk == 0
    state = pl.BlockSpec((None, heads, dk, dv), lambda bb: (bb, 0, 0, 0))
    return pl.pallas_call(
        functools.partial(_gla_step_kernel, heads=heads, dk=dk, dv=dv),
        grid=(nb,),
        in_specs=[pl.BlockSpec((c, gk), lambda bb: (bb, 0)), pl.BlockSpec((c, gk), lambda bb: (bb, 1)),
                  pl.BlockSpec((c, gv), lambda bb: (bb, 2 * gk // gv)), pl.BlockSpec((c, gk), lambda bb: (bb, 0)),
                  pl.BlockSpec((1, dv), lambda bb: (0, 0)), state],
        out_specs=[pl.BlockSpec((c, gv), lambda bb: (bb, 0)), state],
        out_shape=[jax.ShapeDtypeStruct((nb * c, gv), BF16), jax.ShapeDtypeStruct((nb, heads, dk, dv), F32)],
        compiler_params=_params("parallel"),
        name="gla_step",
    )(zp, zp, zp, la, gain, s0)


def _mix_kernel(om_ref, og_ref, gate_o_ref, gate_m_ref, gate_g_ref, x_ref, w_ref, g_ref, y_ref, h_ref):
    o_gla = og_ref[...].astype(F32) * gate_o_ref[...].astype(F32)
    mixed = gate_m_ref[...].astype(F32) * om_ref[...].astype(F32) + gate_g_ref[...].astype(F32) * o_gla
    y = x_ref[...] + _dot(mixed.astype(BF16), w_ref[...])
    y_ref[...] = y
    h_ref[...] = (y * _rms_inv(y, y.shape[-1]) * g_ref[...]).astype(h_ref.dtype)


def _mix(o_mla, o_gla, zp, gate_block0, x, w_o, g, tm):
    t, d = x.shape
    row = lambda i: (i, 0)
    return pl.pallas_call(
        _mix_kernel,
        grid=(t // tm,),
        in_specs=[pl.BlockSpec((tm, d), row), pl.BlockSpec((tm, d), row),
                  pl.BlockSpec((tm, d), lambda i: (i, gate_block0)),
                  pl.BlockSpec((tm, d), lambda i: (i, gate_block0 + 1)),
                  pl.BlockSpec((tm, d), lambda i: (i, gate_block0 + 2)),
                  pl.BlockSpec((tm, d), row), pl.BlockSpec((d, d), lambda i: (0, 0)),
                  pl.BlockSpec((1, d), lambda i: (0, 0))],
        out_specs=[pl.BlockSpec((tm, d), row), pl.BlockSpec((tm, d), row)],
        out_shape=[jax.ShapeDtypeStruct((t, d), F32), jax.ShapeDtypeStruct((t, d), BF16)],
        compiler_params=_params("parallel"),
        name="mix",
    )(o_mla, o_gla, zp, zp, zp, x, w_o, g)


def _ffn_kernel(h_ref, y_ref, wu_ref, wd_ref, *refs):
    o_ref = refs[-1]
    f = pl.program_id(1)

    @pl.when(f == 0)
    def _():
        o_ref[...] = y_ref[...]

    u = jnp.maximum(_dot(h_ref[...], wu_ref[...]), 0.0)
    o_ref[...] += _dot((u * u).astype(BF16), wd_ref[...])


def _ffn(h, y, w_up, w_down, tm, tf, done=None):
    t, d = y.shape
    dff = w_up.shape[1]
    skip = 0 if done is None else 1
    in_specs = [pl.BlockSpec((tm, d), lambda i, f: (i + skip, 0)), pl.BlockSpec((tm, d), lambda i, f: (i + skip, 0)),
                pl.BlockSpec((d, tf), lambda i, f: (0, f)), pl.BlockSpec((tf, d), lambda i, f: (f, 0))]
    args = [h, y, w_up, w_down]
    if done is not None:
        in_specs.append(pl.BlockSpec(memory_space=pl.ANY))
        args.append(done)
    return pl.pallas_call(
        _ffn_kernel,
        grid=(t // tm - skip, dff // tf),
        in_specs=in_specs,
        out_specs=pl.BlockSpec((tm, d), lambda i, f: (i + skip, 0)),
        out_shape=jax.ShapeDtypeStruct((t, d), F32),
        input_output_aliases={} if done is None else {4: 0},
        compiler_params=_params("parallel", "arbitrary"),
        name="ffn",
    )(*args)


def _ffn_first_kernel(h_ref, y_ref, wu_ref, wd_ref, o_ref, wub_ref, wdb_ref):
    f = pl.program_id(0)

    @pl.when(f == 0)
    def _():
        o_ref[...] = y_ref[...]

    wu = wu_ref[...].astype(BF16)
    wd = wd_ref[...].astype(BF16)
    wub_ref[...] = wu
    wdb_ref[...] = wd
    u = jnp.maximum(_dot(h_ref[...], wu), 0.0)
    o_ref[...] += _dot((u * u).astype(BF16), wd)


def _ffn_first(h, y, w_up, w_down, tm, tf):
    t, d = y.shape
    dff = w_up.shape[1]
    return pl.pallas_call(
        _ffn_first_kernel,
        grid=(dff // tf,),
        in_specs=[pl.BlockSpec((tm, d), lambda f: (0, 0)), pl.BlockSpec((tm, d), lambda f: (0, 0)),
                  pl.BlockSpec((d, tf), lambda f: (0, f)), pl.BlockSpec((tf, d), lambda f: (f, 0))],
        out_specs=[pl.BlockSpec((tm, d), lambda f: (0, 0)),
                   pl.BlockSpec((d, tf), lambda f: (0, f)), pl.BlockSpec((tf, d), lambda f: (f, 0))],
        out_shape=[jax.ShapeDtypeStruct((t, d), F32), jax.ShapeDtypeStruct((d, dff), BF16),
                   jax.ShapeDtypeStruct((dff, d), BF16)],
        compiler_params=_params("arbitrary"),
        name="ffn_first",
    )(h, y, w_up, w_down)


def _rope_table(pos):
    half = MLA_ROPE // 2
    freqs = jnp.power(ROPE_THETA, -jnp.arange(half, dtype=F32) / half)
    ang = pos[:, None] * freqs[None, :]
    cos, sin = jnp.cos(ang), jnp.sin(ang)
    return jnp.concatenate([cos, cos, -sin, sin], axis=1)


def _swap_halves(w):
    half = w.shape[-1] // 2
    return jnp.concatenate([w[..., half:], w[..., :half]], axis=-1)


def _pick(t, pref):
    while t % pref:
        pref //= 2
    return pref


def kernel(x_prompt, x_sample, cache_mla_ckv, cache_mla_krope, state_gla, norm_mix_g, w_in, mla_q_norm_g,
           mla_w_uq, mla_kv_norm_g, mla_w_ukv, mla_q_gain_nope, mla_q_gain_rope, mla_k_gain_nope,
           mla_k_gain_rope, gla_w_a2, gla_b_a, gla_norm_g, w_o, norm_ffn_g, ffn_w_up, ffn_w_down):
    depth = w_in.shape[0]
    bp, seq, d = x_prompt.shape
    nb, dec_seq, _ = x_sample.shape
    past = cache_mla_ckv.shape[2]
    q_rank, heads = mla_w_uq.shape[1], mla_w_uq.shape[2]
    kv_rank = mla_w_ukv.shape[1]
    gla_heads, dk, dv = state_gla.shape[2], state_gla.shape[3], state_gla.shape[4]
    gate_rank = gla_w_a2.shape[1]
    gk, gv = gla_heads * dk, gla_heads * dv
    assert bp == 1 and heads * MLA_V == d and gv == d and seq % CHUNK == 0

    cs_p = _rope_table(jnp.arange(seq, dtype=F32))
    cs_s = jnp.tile(_rope_table(past + jnp.arange(dec_seq, dtype=F32)), (nb, 1))

    xp = x_prompt.reshape(seq, d)
    xs = x_sample.reshape(nb * dec_seq, d)
    outs = [[] for _ in range(6)]
    for l in range(depth):
        wi = w_in[l]
        pts = np.cumsum([q_rank, kv_rank, MLA_ROPE, gk, gk, gv, gate_rank, gv, d]).tolist()
        w_qkv_lat = wi[:, :pts[1]]
        w_kr = wi[:, pts[1]:pts[2]]
        w_alr = wi[:, pts[5]:pts[6]]
        assert 3 * gate_rank <= LANE
        w_lat = jnp.concatenate(
            [w_qkv_lat, w_kr, _swap_halves(w_kr), w_alr, w_alr, w_alr,
             jnp.zeros((d, LANE - 3 * gate_rank), F32)], axis=1).astype(BF16)
        w_gla = wi[:, pts[2]:pts[5]].astype(BF16)
        w_gate = wi[:, pts[6]:].astype(BF16)
        wa2_hi = gla_w_a2[l].astype(BF16)
        wa2_lo = (gla_w_a2[l] - wa2_hi.astype(F32)).astype(BF16)
        wa2_split = jnp.concatenate(
            [wa2_hi, wa2_lo, wa2_hi, jnp.zeros((LANE - 3 * gate_rank, gk), BF16)], axis=0)

        wq = mla_w_uq[l]
        wq_r = wq[..., MLA_NOPE:]
        w_uq = jnp.concatenate([wq[..., :MLA_NOPE], wq_r, _swap_halves(wq_r)], axis=-1)
        w_uq = w_uq.reshape(q_rank, heads * MLA_HEAD_PAD).astype(BF16)
        w_uq_t = w_uq.T
        wuk = mla_w_ukv[l][..., :MLA_NOPE].reshape(kv_rank, heads * MLA_NOPE).astype(BF16)
        wuk_t = wuk.T
        wuv = mla_w_ukv[l][..., MLA_NOPE:].reshape(kv_rank, heads * MLA_V).astype(BF16)
        wuv_t = wuv.T
        gain = jnp.concatenate([mla_q_gain_nope[l] * mla_k_gain_nope[l],
                                jnp.tile(mla_q_gain_rope[l] * mla_k_gain_rope[l], 2),
                                jnp.zeros((MLA_HEAD_PAD - MLA_QK_DIM,), F32)])[None, :] * MLA_SCALE
        gain_col = gain.T * LOG2E
        wo_b = w_o[l].astype(BF16)
        g_mix = norm_mix_g[l][None, :]
        g_q = mla_q_norm_g[l][None, :]
        g_kv = mla_kv_norm_g[l][None, :]
        g_gla = gla_norm_g[l][None, :]
        g_ffn = norm_ffn_g[l][None, :]
        ba = gla_b_a[l][None, :]

        def front(x, cs, tm):
            h = _prenorm(x, g_mix, _pick(x.shape[0], 512))
            qlat, ckv, kr, la = _mla_lat(h, w_lat, g_q, g_kv, cs, wa2_split, ba, tm, q_rank, kv_rank, gate_rank)
            z_gla = _proj(h, w_gla, _pick(x.shape[0], 1024), 2048)
            z_gate = _proj(h, w_gate, _pick(x.shape[0], 1024), gv, gates=True)
            return qlat, ckv, kr, la, z_gla, z_gate

        def back(x, o_mla, o_gla, z_gate, tm, ffn_w=None):
            y1, h2 = _mix(o_mla, o_gla, z_gate, 0, x, wo_b, g_ffn, tm)
            if ffn_w is not None:
                return _ffn(h2, y1, *ffn_w, tm, 1024), ffn_w
            y, *ffn_w = _ffn_first(h2, y1, ffn_w_up[l], ffn_w_down[l], tm, 512)
            if x.shape[0] > tm:
                y = _ffn(h2, y1, *ffn_w, tm, 1024, done=y)
            return y, ffn_w

        tile = _pick(seq, 512)
        qlat, ckv, kr, la, z_gla, z_gate = front(xp, cs_p, tile)
        qt = _mla_qt(qlat, w_uq_t, cs_p.T, gain_col, tile, 2 * tile, heads)
        kcat, vt = _mla_kv(ckv, kr, wuk, wuv_t, tile, heads)
        o_mla = _flash(qt, kcat, vt, heads)
        o_gla, st = _gla(z_gla, la, g_gla, 1, _pick(seq, 128), gla_heads, dk, dv)
        xp_next, ffn_w = back(xp, o_mla, o_gla, z_gate, tile)
        outs[0].append(ckv.reshape(bp, seq, kv_rank))
        outs[1].append(kr.reshape(bp, seq, MLA_ROPE))
        outs[2].append(st)
        xp = xp_next

        qlat, ckv, kr, la, z_gla, z_gate = front(xs, cs_s, nb * dec_seq)
        qcat = _mla_q(qlat, w_uq, cs_s, gain, nb * dec_seq, heads)
        o_lat = _sattn(qcat, cache_mla_ckv[l], cache_mla_krope[l], ckv, kr, wuk_t, heads, dec_seq, _pick(past, 512))
        o_mla = _svup(o_lat, wuv, heads, dec_seq)
        o_gla, st = _gla_step(z_gla, la, g_gla, state_gla[l], dec_seq, gla_heads, dk, dv)
        xs_next, _ = back(xs, o_mla, o_gla, z_gate, nb * dec_seq, ffn_w)
        outs[3].append(ckv.reshape(nb, dec_seq, kv_rank))
        outs[4].append(kr.reshape(nb, dec_seq, MLA_ROPE))
        outs[5].append(st)
        xs = xs_next

    return (xp.reshape(bp, seq, d), xs.reshape(nb, dec_seq, d),
            jnp.stack(outs[0]), jnp.stack(outs[1]), jnp.stack(outs[2]),
            jnp.stack(outs[3]), jnp.stack(outs[4]), jnp.stack(outs[5]))
```

```python
import functools

import jax
import jax.numpy as jnp
import numpy as np
from jax import lax
from jax.experimental import pallas as pl
from jax.experimental.pallas import tpu as pltpu

F32 = jnp.float32
BF16 = jnp.bfloat16

EPS = 1e-6
CHUNK = 64
MLA_NOPE = 128
MLA_ROPE = 64
MLA_V = 128
MLA_QK_DIM = MLA_NOPE + MLA_ROPE
MLA_SCALE = MLA_QK_DIM ** -0.5
MLA_HEAD_PAD = 256
V_ROWS = MLA_V + 16
LOG2E = 1.4426950408889634
QBLK = 256
ROPE_THETA = 10000.0
GLA_TAU = 16.0
GLA_LEAF = 32
LANE = 128
V7X_VMEM_BYTES = 64 * 1024 * 1024
VMEM_LIMIT = V7X_VMEM_BYTES * 7 // 8

NT_DIMS = (((1,), (1,)), ((), ()))
TN_DIMS = (((0,), (0,)), ((), ()))


def _params(*sem):
    return pltpu.CompilerParams(dimension_semantics=sem, vmem_limit_bytes=VMEM_LIMIT)


def _dot(a, b):
    return jnp.dot(a, b, preferred_element_type=F32)


def _dot_nt(a, b):
    return lax.dot_general(a, b, NT_DIMS, preferred_element_type=F32)


def _rms_inv(x, n):
    return lax.rsqrt(jnp.sum(x * x, axis=-1, keepdims=True) / n + EPS)


def _prenorm_kernel(x_ref, g_ref, h_ref):
    x = x_ref[...]
    h_ref[...] = (x * _rms_inv(x, x.shape[-1]) * g_ref[...]).astype(h_ref.dtype)


def _prenorm(x, g, tm):
    t, d = x.shape
    return pl.pallas_call(
        _prenorm_kernel,
        grid=(t // tm,),
        in_specs=[pl.BlockSpec((tm, d), lambda i: (i, 0)), pl.BlockSpec((1, d), lambda i: (0, 0))],
        out_specs=pl.BlockSpec((tm, d), lambda i: (i, 0)),
        out_shape=jax.ShapeDtypeStruct((t, d), BF16),
        compiler_params=_params("parallel"),
        name="prenorm",
    )(x, g)


def _mla_lat_kernel(h_ref, w_ref, qg_ref, kvg_ref, cs_ref, wa2_ref, ba_ref,
                    qlat_ref, ckv_ref, kr_ref, la_ref, *, q_rank, kv_rank, gate_rank):
    z = _dot(h_ref[...], w_ref[...])
    q_lat = z[:, :q_rank]
    qlat_ref[...] = (q_lat * _rms_inv(q_lat, q_rank) * qg_ref[...]).astype(qlat_ref.dtype)
    kv_lat = z[:, q_rank:q_rank + kv_rank]
    ckv_ref[...] = kv_lat * _rms_inv(kv_lat, kv_rank) * kvg_ref[...]
    o = q_rank + kv_rank
    rr = z[:, o:o + LANE] * cs_ref[...]
    kr_ref[...] = rr[:, :MLA_ROPE] + rr[:, MLA_ROPE:]
    a3 = z[:, o + LANE:o + 2 * LANE]
    a_hi = a3.astype(BF16)
    a_lo = (a3 - a_hi.astype(F32)).astype(BF16)
    lane = lax.broadcasted_iota(jnp.int32, a3.shape, 1)
    u = _dot(jnp.where(lane < 2 * gate_rank, a_hi, a_lo), wa2_ref[...]) + ba_ref[...]
    log_sig = jnp.minimum(u, 0.0) - jnp.log1p(jnp.exp(-jnp.abs(u)))
    la_ref[...] = log_sig / GLA_TAU


def _mla_lat(h, w_lat, qg, kvg, cs, wa2_split, ba, tm, q_rank, kv_rank, gate_rank):
    t, d = h.shape
    n = w_lat.shape[1]
    gk = wa2_split.shape[1]
    row = lambda i: (i, 0)
    fix = lambda i: (0, 0)
    return pl.pallas_call(
        functools.partial(_mla_lat_kernel, q_rank=q_rank, kv_rank=kv_rank, gate_rank=gate_rank),
        grid=(t // tm,),
        in_specs=[pl.BlockSpec((tm, d), row), pl.BlockSpec((d, n), fix),
                  pl.BlockSpec((1, q_rank), fix), pl.BlockSpec((1, kv_rank), fix),
                  pl.BlockSpec((tm, LANE), row), pl.BlockSpec((LANE, gk), fix), pl.BlockSpec((1, gk), fix)],
        out_specs=[pl.BlockSpec((tm, q_rank), row), pl.BlockSpec((tm, kv_rank), row),
                   pl.BlockSpec((tm, MLA_ROPE), row), pl.BlockSpec((tm, gk), row)],
        out_shape=[jax.ShapeDtypeStruct((t, q_rank), BF16), jax.ShapeDtypeStruct((t, kv_rank), F32),
                   jax.ShapeDtypeStruct((t, MLA_ROPE), F32), jax.ShapeDtypeStruct((t, gk), F32)],
        compiler_params=_params("parallel"),
        name="mla_lat",
    )(h, w_lat, qg, kvg, cs, wa2_split, ba)


def _gate_act(z, swish):
    return jax.nn.sigmoid(z) * jnp.where(swish, z, 1.0)


def _proj_kernel(h_ref, w_ref, *refs, swish_tiles):
    o_ref = refs[-1]
    z = _dot(h_ref[...], w_ref[...])
    if swish_tiles is not None:
        z = _gate_act(z, pl.program_id(1) < swish_tiles)
    o_ref[...] = z.astype(o_ref.dtype)


def _proj(h, w, tm, tn, swish_cols=None, done=None):
    t, d = h.shape
    n = w.shape[1]
    skip = 0 if done is None else 1
    in_specs = [pl.BlockSpec((tm, d), lambda i, j: (i + skip, 0)), pl.BlockSpec((d, tn), lambda i, j: (0, j))]
    args = [h, w]
    if done is not None:
        in_specs.append(pl.BlockSpec(memory_space=pl.ANY))
        args.append(done)
    assert swish_cols is None or swish_cols % tn == 0
    return pl.pallas_call(
        functools.partial(_proj_kernel, swish_tiles=None if swish_cols is None else swish_cols // tn),
        grid=(t // tm - skip, n // tn),
        in_specs=in_specs,
        out_specs=pl.BlockSpec((tm, tn), lambda i, j: (i + skip, j)),
        out_shape=jax.ShapeDtypeStruct((t, n), BF16),
        input_output_aliases={} if done is None else {2: 0},
        compiler_params=_params("parallel", "parallel"),
        name="proj",
    )(*args)


def _proj_first_kernel(h_ref, a_ref, b_ref, o_ref, w_ref, *, shift, n_cols, next_col0, swish_tiles):
    j = pl.program_id(0)
    tn = a_ref.shape[1]
    b = b_ref[...]
    col = next_col0 + j * tn + lax.broadcasted_iota(jnp.int32, b.shape, 1)
    src = jnp.concatenate([a_ref[...], jnp.where(col < n_cols, b, jnp.zeros_like(b))], axis=1).astype(F32)
    w = pltpu.roll(src, src.shape[1] - shift, 1)[:, :tn].astype(BF16)
    w_ref[...] = w
    z = _dot(h_ref[...], w)
    if swish_tiles is not None:
        z = _gate_act(z, j < swish_tiles)
    o_ref[...] = z.astype(o_ref.dtype)


def _proj_first(h, w_all, col0, width, tm, tn, swish_cols=None):
    t, d = h.shape
    n_cols = w_all.shape[1]
    base = col0 // LANE * LANE
    shift = col0 - base
    assert shift > 0 and base % tn == 0 and width % tn == 0 and tn % LANE == 0
    assert swish_cols is None or swish_cols % tn == 0
    return pl.pallas_call(
        functools.partial(_proj_first_kernel, shift=shift, n_cols=n_cols, next_col0=base + tn,
                          swish_tiles=None if swish_cols is None else swish_cols // tn),
        grid=(width // tn,),
        in_specs=[pl.BlockSpec((tm, d), lambda j: (0, 0)),
                  pl.BlockSpec((d, tn), lambda j: (0, base // tn + j)),
                  pl.BlockSpec((d, LANE), lambda j: (0, (base + (j + 1) * tn) // LANE))],
        out_specs=[pl.BlockSpec((tm, tn), lambda j: (0, j)), pl.BlockSpec((d, tn), lambda j: (0, j))],
        out_shape=[jax.ShapeDtypeStruct((t, width), BF16), jax.ShapeDtypeStruct((d, width), BF16)],
        compiler_params=_params("parallel"),
        name="proj_first",
    )(h, w_all, w_all)


def _mla_q_kernel(ql_ref, w_ref, cs_ref, gain_ref, q_ref, *, heads):
    ql = ql_ref[...]
    cs = cs_ref[...]
    gain = gain_ref[...]
    lane = lax.broadcasted_iota(jnp.int32, (1, LANE), 1)
    for hd in range(heads):
        c0 = hd * MLA_HEAD_PAD
        z = _dot(ql, w_ref[:, c0:c0 + MLA_HEAD_PAD])
        nope = z[:, :MLA_NOPE]
        rr = z[:, MLA_NOPE:] * cs
        rot = rr + pltpu.roll(rr, MLA_ROPE, 1)
        ss = (jnp.sum(nope * nope, axis=-1, keepdims=True)
              + jnp.sum(jnp.where(lane < MLA_ROPE, rot * rot, 0.0), axis=-1, keepdims=True))
        inv = lax.rsqrt(ss / MLA_QK_DIM + EPS)
        q_ref[:, c0:c0 + MLA_NOPE] = (nope * inv * gain[:, :MLA_NOPE]).astype(q_ref.dtype)
        q_ref[:, c0 + MLA_NOPE:c0 + MLA_HEAD_PAD] = (rot * inv * gain[:, MLA_NOPE:]).astype(q_ref.dtype)


def _mla_q(qlat, w_uq, cs, gain, tm, heads):
    t, r = qlat.shape
    n = heads * MLA_HEAD_PAD
    return pl.pallas_call(
        functools.partial(_mla_q_kernel, heads=heads),
        grid=(t // tm,),
        in_specs=[pl.BlockSpec((tm, r), lambda i: (i, 0)), pl.BlockSpec((r, n), lambda i: (0, 0)),
                  pl.BlockSpec((tm, LANE), lambda i: (i, 0)), pl.BlockSpec((1, MLA_HEAD_PAD), lambda i: (0, 0))],
        out_specs=pl.BlockSpec((tm, n), lambda i: (i, 0)),
        out_shape=jax.ShapeDtypeStruct((t, n), BF16),
        compiler_params=_params("parallel"),
        name="mla_q",
    )(qlat, w_uq, cs, gain)


def _mla_qt_kernel(ql_ref, w_ref, cs_ref, gain_ref, q_ref, *, heads):
    ql = ql_ref[...]
    cs = cs_ref[...]
    gain = gain_ref[...]
    for hd in range(heads):
        r0 = hd * MLA_HEAD_PAD
        z = _dot_nt(w_ref[r0:r0 + MLA_HEAD_PAD, :], ql)
        nope = z[:MLA_NOPE]
        rr = z[MLA_NOPE:] * cs
        rot = rr[:MLA_ROPE] + rr[MLA_ROPE:]
        ss = jnp.sum(nope * nope, axis=0, keepdims=True) + jnp.sum(rot * rot, axis=0, keepdims=True)
        inv = lax.rsqrt(ss / MLA_QK_DIM + EPS)
        qn = (nope * inv * gain[:MLA_NOPE]).astype(q_ref.dtype)
        qr = (rot * inv * gain[MLA_NOPE:MLA_QK_DIM]).astype(q_ref.dtype)
        for bb in range(q_ref.shape[1]):
            q_ref[hd, bb, :MLA_NOPE, :] = qn[:, bb * QBLK:(bb + 1) * QBLK]
            q_ref[hd, bb, MLA_NOPE:MLA_QK_DIM, :] = qr[:, bb * QBLK:(bb + 1) * QBLK]
            q_ref[hd, bb, MLA_QK_DIM:, :] = jnp.zeros((MLA_HEAD_PAD - MLA_QK_DIM, QBLK), q_ref.dtype)


def _mla_qt(qlat, w_uq_t, cs_t, gain_col, tm, tq, heads):
    t, r = qlat.shape
    n = heads * MLA_HEAD_PAD
    per = tq // tm
    assert tm % QBLK == 0 and tq % tm == 0
    return pl.pallas_call(
        functools.partial(_mla_qt_kernel, heads=heads),
        grid=(t // tm,),
        in_specs=[pl.BlockSpec((tm, r), lambda i: (i, 0)), pl.BlockSpec((n, r), lambda i: (0, 0)),
                  pl.BlockSpec((LANE, tm), lambda i: (0, i)), pl.BlockSpec((MLA_HEAD_PAD, 1), lambda i: (0, 0))],
        out_specs=pl.BlockSpec((heads, None, tm // QBLK, MLA_HEAD_PAD, QBLK),
                               lambda i: (0, i // per, i % per, 0, 0)),
        out_shape=jax.ShapeDtypeStruct((heads, t // tq, tq // QBLK, MLA_HEAD_PAD, QBLK), BF16),
        compiler_params=_params("parallel"),
        name="mla_qt",
    )(qlat, w_uq_t, cs_t, gain_col)


def _mla_kv_kernel(ckv_ref, kr_ref, wk_ref, wvt_ref, k_ref, vt_ref, *, heads):
    c = ckv_ref[...].astype(BF16)
    kr = kr_ref[...]
    kr2 = jnp.sum(kr * kr, axis=-1, keepdims=True)
    kr_pad = jnp.concatenate([kr, jnp.zeros_like(kr)], axis=1)
    for pair in range(heads // 2):
        z = _dot(c, wk_ref[:, pair * 2 * MLA_NOPE:(pair + 1) * 2 * MLA_NOPE])
        for sub in range(2):
            kn = z[:, sub * MLA_NOPE:(sub + 1) * MLA_NOPE]
            inv = lax.rsqrt((jnp.sum(kn * kn, axis=-1, keepdims=True) + kr2) / MLA_QK_DIM + EPS)
            c0 = (2 * pair + sub) * MLA_HEAD_PAD
            k_ref[:, c0:c0 + MLA_NOPE] = (kn * inv).astype(k_ref.dtype)
            k_ref[:, c0 + MLA_NOPE:c0 + MLA_HEAD_PAD] = (kr_pad * inv).astype(k_ref.dtype)
    vt = _dot_nt(wvt_ref[...], c)
    for hd in range(heads):
        vt_ref[hd, :MLA_V, :] = vt[hd * MLA_V:(hd + 1) * MLA_V].astype(vt_ref.dtype)
        vt_ref[hd, MLA_V:, :] = jnp.ones((V_ROWS - MLA_V, vt.shape[1]), vt_ref.dtype)


def _mla_kv(ckv, kr, w_uk, w_uv_t, tm, heads):
    t, r = ckv.shape
    return pl.pallas_call(
        functools.partial(_mla_kv_kernel, heads=heads),
        grid=(t // tm,),
        in_specs=[pl.BlockSpec((tm, r), lambda i: (i, 0)), pl.BlockSpec((tm, MLA_ROPE), lambda i: (i, 0)),
                  pl.BlockSpec((r, heads * MLA_NOPE), lambda i: (0, 0)),
                  pl.BlockSpec((heads * MLA_V, r), lambda i: (0, 0))],
        out_specs=[pl.BlockSpec((tm, heads * MLA_HEAD_PAD), lambda i: (i, 0)),
                   pl.BlockSpec((heads, None, V_ROWS, tm), lambda i: (0, i, 0, 0))],
        out_shape=[jax.ShapeDtypeStruct((t, heads * MLA_HEAD_PAD), BF16),
                   jax.ShapeDtypeStruct((heads, t // tm, V_ROWS, tm), BF16)],
        compiler_params=_params("parallel"),
        name="mla_kv",
    )(ckv, kr, w_uk, w_uv_t)


def _diag_mode(u, c, tk):
    k_lo, k_hi = (u * tk) // CHUNK, ((u + 1) * tk - 1) // CHUNK
    q_lo, q_hi = (c * QBLK) // CHUNK, ((c + 1) * QBLK - 1) // CHUNK
    return "all" if k_hi <= q_lo else "none" if k_lo > q_hi else "some"


def _diag_bias(tk, n_blk):
    index, blocks = {}, []
    for u in range(2):
        for c in range(n_blk):
            if _diag_mode(u, c, tk) == "some":
                k_chunk = (u * tk + np.arange(tk)[:, None]) // CHUNK
                q_chunk = (c * QBLK + np.arange(QBLK)[None, :]) // CHUNK
                index[(u, c)] = len(blocks)
                blocks.append(np.where(k_chunk <= q_chunk, 0.0, -np.inf).astype(np.float32))
    return index, np.stack(blocks)


def _flash_kernel(qt_ref, k_ref, vt_ref, bias_ref, o_ref, m_ref, acc_ref, s_ref, mx_ref, p_ref, al_ref,
                  *, tq, tk, bias_index):
    n_q, n_blk = qt_ref.shape[0], qt_ref.shape[1]
    acc_ref[...] = jnp.zeros_like(acc_ref)


    def diag_mode(u, c):
        return _diag_mode(u, c, tk)

    def score(i, t, slot, c, mode):
        if mode == "none":
            return
        k = k_ref[pl.ds(pl.multiple_of(t * tk, tk), tk), :]
        s = _dot(k, qt_ref[i, c])
        if mode == "some":
            s = s + bias_ref[bias_index[(slot, c)]]
        s_ref[slot, c] = s
        mx_ref[slot, c] = jnp.max(s, axis=0, keepdims=True)

    def soften(slot, c, mode="all", first=False):
        if mode == "none":
            return
        if first:
            m_new = mx_ref[slot, c]
            al_ref[slot, c] = jnp.zeros_like(m_new)
        else:
            m_prev = m_ref[c]
            m_new = jnp.maximum(m_prev, mx_ref[slot, c])
            al_ref[slot, c] = jnp.exp2(m_prev - m_new)
        p_ref[slot, c] = jnp.exp2(s_ref[slot, c] - m_new).astype(BF16)
        m_ref[c] = m_new

    def gather(t, slot, c, mode="all"):
        if mode == "none":
            return
        acc_ref[c] = al_ref[slot, c] * acc_ref[c] + _dot(vt_ref[t], p_ref[slot, c])

    def pair(i, g, diag):
        for c in range(n_blk):
            score(i, g, 0, c, diag_mode(0, c) if diag else "all")
            soften(1, c)
            gather(g - 2, 0, c)
        for c in range(n_blk):
            score(i, g + 1, 1, c, diag_mode(1, c) if diag else "all")
            soften(0, c, diag_mode(0, c) if diag else "all")
            gather(g - 1, 1, c)

    def head(i, diag):
        for c in range(n_blk):
            score(i, 0, 0, c, diag_mode(0, c) if diag else "all")
        for c in range(n_blk):
            score(i, 1, 1, c, diag_mode(1, c) if diag else "all")
            soften(0, c, diag_mode(0, c) if diag else "all", first=True)

    def tail(i):
        for c in range(n_blk):
            soften(1, c, diag_mode(1, c))
            gather(2 * i, 0, c, diag_mode(0, c))
        for c in range(n_blk):
            gather(2 * i + 1, 1, c, diag_mode(1, c))
        for c in range(n_blk):
            acc = acc_ref[c]
            rows = pl.ds(pl.multiple_of(i * tq + c * QBLK, QBLK), QBLK)
            o_ref[rows, :] = (acc[:MLA_V] / acc[MLA_V:MLA_V + 1]).T.astype(o_ref.dtype)

    def middle(i):
        def body(j, carry):
            pair(i, 2 * j, False)
            return carry

        lax.fori_loop(1, i, body, 0)
        pair(i, 2 * i, True)

    head(0, True)
    if n_q > 1:
        tail(0)
        head(1, False)

        def outer(i, carry):
            middle(i)
            tail(i)
            head(i + 1, False)
            return carry

        lax.fori_loop(1, n_q - 1, outer, 0)
        middle(n_q - 1)
    tail(n_q - 1)


def _flash(qt, k, vt, heads):
    t = k.shape[0]
    n_q, n_blk = qt.shape[1], qt.shape[2]
    tq = n_blk * QBLK
    n_kt, tk = vt.shape[1], vt.shape[3]
    assert tq == 2 * tk and tk % CHUNK == 0 and qt.shape[4] == QBLK and n_q * tq == t
    bias_index, bias = _diag_bias(tk, n_blk)
    return pl.pallas_call(
        functools.partial(_flash_kernel, tq=tq, tk=tk, bias_index=bias_index),
        grid=(heads,),
        in_specs=[pl.BlockSpec((None, n_q, n_blk, MLA_HEAD_PAD, QBLK), lambda hd: (hd, 0, 0, 0, 0)),
                  pl.BlockSpec((t, MLA_HEAD_PAD), lambda hd: (0, hd)),
                  pl.BlockSpec((None, n_kt, V_ROWS, tk), lambda hd: (hd, 0, 0, 0)),
                  pl.BlockSpec(bias.shape, lambda hd: (0, 0, 0))],
        out_specs=pl.BlockSpec((t, MLA_V), lambda hd: (0, hd)),
        out_shape=jax.ShapeDtypeStruct((t, heads * MLA_V), BF16),
        scratch_shapes=[pltpu.VMEM((n_blk, 1, QBLK), F32), pltpu.VMEM((n_blk, V_ROWS, QBLK), F32),
                        pltpu.VMEM((2, n_blk, tk, QBLK), F32), pltpu.VMEM((2, n_blk, 1, QBLK), F32),
                        pltpu.VMEM((2, n_blk, tk, QBLK), BF16), pltpu.VMEM((2, n_blk, 1, QBLK), F32)],
        compiler_params=_params("parallel"),
        name="flash",
    )(qt, k, vt, jnp.asarray(bias))


def _sattn_kernel(q_ref, cc_ref, ck_ref, nc_ref, nk_ref, wuk_ref, o_ref,
                  qabs_ref, qr_ref, m_ref, l_ref, acc_ref, *, heads, nq, tk, n_new):
    past, rank = cc_ref.shape
    n_tiles = past // tk
    for hd in range(heads):
        c0 = hd * MLA_HEAD_PAD
        qn = q_ref[:, c0:c0 + MLA_NOPE]
        qabs_ref[hd * nq:(hd + 1) * nq, :] = _dot(
            qn, wuk_ref[hd * MLA_NOPE:(hd + 1) * MLA_NOPE, :]).astype(qabs_ref.dtype)
        qr_ref[hd * nq:(hd + 1) * nq, :] = q_ref[:, c0 + MLA_NOPE:c0 + MLA_NOPE + MLA_ROPE]
    m_ref[...] = jnp.full_like(m_ref, -jnp.inf)
    l_ref[...] = jnp.zeros_like(l_ref)
    acc_ref[...] = jnp.zeros_like(acc_ref)

    def tile(t):
        if t < n_tiles:
            return cc_ref[t * tk:(t + 1) * tk, :].astype(BF16), ck_ref[t * tk:(t + 1) * tk, :]
        pad = LANE - n_new
        c_new = jnp.concatenate([nc_ref[...], jnp.zeros((pad, rank), F32)], axis=0)
        k_new = jnp.concatenate([nk_ref[...], jnp.zeros((pad, MLA_ROPE), F32)], axis=0)
        return c_new.astype(BF16), k_new

    def key_norm(c, kr):
        n_keys = c.shape[0]
        kn_t = _dot_nt(wuk_ref[...], c)
        ss_t = jnp.sum((kn_t * kn_t).reshape(heads, MLA_NOPE, n_keys), axis=1)
        kr2_t = lax.dot_general(jnp.ones((8, MLA_ROPE), F32), kr * kr, NT_DIMS,
                                preferred_element_type=F32, precision=lax.Precision.HIGHEST)[0:1]
        return lax.rsqrt((ss_t + kr2_t) / MLA_QK_DIM + EPS)

    def attend(c, kr, inv_t, n_valid):
        n_keys = c.shape[0]
        s = _dot_nt(qabs_ref[...], c) + _dot_nt(qr_ref[...], kr.astype(BF16))
        s = s * jnp.broadcast_to(inv_t[:, None, :], (heads, nq, n_keys)).reshape(heads * nq, n_keys)
        if n_valid < n_keys:
            s = jnp.where(lax.broadcasted_iota(jnp.int32, s.shape, 1) < n_valid, s, -jnp.inf)
        m_prev = m_ref[...]
        m_new = jnp.maximum(m_prev, jnp.max(s, axis=-1, keepdims=True))
        alpha = jnp.exp(m_prev - m_new)
        p = jnp.exp(s - m_new)
        l_ref[...] = alpha * l_ref[...] + jnp.sum(p, axis=-1, keepdims=True)
        acc_ref[...] = alpha * acc_ref[...] + _dot(p.astype(BF16), c)
        m_ref[...] = m_new

    c, kr = tile(0)
    inv_t = key_norm(c, kr)
    for t in range(n_tiles + 1):
        if t < n_tiles:
            c_next, kr_next = tile(t + 1)
            inv_next = key_norm(c_next, kr_next)
        attend(c, kr, inv_t, tk if t < n_tiles else n_new)
        if t < n_tiles:
            c, kr, inv_t = c_next, kr_next, inv_next
    o_ref[...] = (acc_ref[...] / l_ref[...]).astype(o_ref.dtype)


def _sattn(q, cache_c, cache_k, new_c, new_k, wuk_t, heads, nq, tk):
    nb, past, rank = cache_c.shape
    return pl.pallas_call(
        functools.partial(_sattn_kernel, heads=heads, nq=nq, tk=tk, n_new=nq),
        grid=(nb,),
        in_specs=[pl.BlockSpec((nq, heads * MLA_HEAD_PAD), lambda b: (b, 0)),
                  pl.BlockSpec((None, past, rank), lambda b: (b, 0, 0)),
                  pl.BlockSpec((None, past, MLA_ROPE), lambda b: (b, 0, 0)),
                  pl.BlockSpec((nq, rank), lambda b: (b, 0)),
                  pl.BlockSpec((nq, MLA_ROPE), lambda b: (b, 0)),
                  pl.BlockSpec(wuk_t.shape, lambda b: (0, 0))],
        out_specs=pl.BlockSpec((None, heads * nq, rank), lambda b: (b, 0, 0)),
        out_shape=jax.ShapeDtypeStruct((nb, heads * nq, rank), BF16),
        scratch_shapes=[pltpu.VMEM((heads * nq, rank), BF16), pltpu.VMEM((heads * nq, MLA_ROPE), BF16),
                        pltpu.VMEM((heads * nq, 1), F32), pltpu.VMEM((heads * nq, 1), F32),
                        pltpu.VMEM((heads * nq, rank), F32)],
        compiler_params=_params("parallel"),
        name="sattn",
    )(q, cache_c, cache_k, new_c, new_k, wuk_t)


def _svup_kernel(ol_ref, w_ref, o_ref, *, nq):
    nb, rows, rank = ol_ref.shape
    for hh in range(rows // nq):
        x = ol_ref[:, hh * nq:(hh + 1) * nq, :].reshape(nb * nq, rank)
        o_ref[:, hh * MLA_V:(hh + 1) * MLA_V] = _dot(x, w_ref[:, hh * MLA_V:(hh + 1) * MLA_V]).astype(o_ref.dtype)


def _svup(o_lat, wuv, heads, nq):
    nb, _, rank = o_lat.shape
    hs = 4 if heads % 4 == 0 else 1
    return pl.pallas_call(
        functools.partial(_svup_kernel, nq=nq),
        grid=(heads // hs,),
        in_specs=[pl.BlockSpec((nb, hs * nq, rank), lambda g: (0, g, 0)),
                  pl.BlockSpec((rank, hs * MLA_V), lambda g: (0, g))],
        out_specs=pl.BlockSpec((nb * nq, hs * MLA_V), lambda g: (0, g)),
        out_shape=jax.ShapeDtypeStruct((nb * nq, heads * MLA_V), BF16),
        compiler_params=_params("parallel"),
        name="svup",
    )(o_lat, wuv)


def _gla_levels(c):
    leaf = min(c, GLA_LEAF)
    levels = [(leaf, leaf // 2 - 1)]
    g = 2 * leaf
    while g <= c:
        levels.append((g, g // 2 - 1))
        g *= 2
    return levels


def _gla_kernel(q_ref, k_ref, v_ref, la_ref, g_ref, o_ref, sout_ref, st_ref, *, heads, dk, dv, c):
    j = pl.program_id(1)
    nj = pl.num_programs(1)

    @pl.when(j == 0)
    def _():
        st_ref[...] = jnp.zeros_like(st_ref)

    levels = _gla_levels(c)
    row = lax.broadcasted_iota(jnp.int32, (c, c), 0)
    col = lax.broadcasted_iota(jnp.int32, (c, c), 1)
    masks = []
    for lv, (g, _) in enumerate(levels):
        same = (row // g) == (col // g)
        if lv == 0:
            masks.append(jnp.logical_and(same, col <= row))
        else:
            half = g // 2
            masks.append(jnp.logical_and(same, jnp.logical_and((row // half) % 2 == 1, (col // half) % 2 == 0)))
    rid = lax.broadcasted_iota(jnp.int32, (c, dk), 0)
    gain = g_ref[...]

    for hd in range(heads):
        la = la_ref[:, hd * dk:(hd + 1) * dk]
        b = la
        sh = 1
        while sh < c:
            b = b + jnp.where(rid >= sh, pltpu.roll(b, sh, 0), 0.0)
            sh *= 2
        b_end = b[c - 1:c, :]
        q = q_ref[:, hd * dk:(hd + 1) * dk].astype(F32) * (dk ** -0.5)
        k = k_ref[:, hd * dk:(hd + 1) * dk].astype(F32)
        v = v_ref[:, hd * dv:(hd + 1) * dv]

        a = jnp.zeros((c, c), F32)
        for lv, (g, r) in enumerate(levels):
            ref_rows = jnp.broadcast_to(b.reshape(c // g, g, dk)[:, r:r + 1, :], (c // g, g, dk)).reshape(c, dk)
            d = b - ref_rows
            if lv == 0:
                fq, fk = jnp.exp(d), jnp.exp(-d)
            else:
                fq = fk = jnp.exp(-jnp.abs(d))
            a_lv = _dot_nt((q * fq).astype(BF16), (k * fk).astype(BF16))
            a = jnp.where(masks[lv], a_lv, a)

        st = st_ref[hd]
        o = _dot(a.astype(BF16), v) + _dot_nt((q * jnp.exp(b)).astype(BF16), st.astype(BF16))
        k_end = (k * jnp.exp(b_end - b)).astype(BF16)
        st_ref[hd] = st * jnp.exp(b_end) + lax.dot_general(v, k_end, TN_DIMS, preferred_element_type=F32)
        o_ref[:, hd * dv:(hd + 1) * dv] = (o * _rms_inv(o, dv) * gain).astype(o_ref.dtype)

    @pl.when(j == nj - 1)
    def _():
        for hd in range(heads):
            sout_ref[hd] = st_ref[hd].T


def _gla(zp, la, gain, nb, c, heads, dk, dv):
    t = la.shape[0]
    nj = t // (nb * c)
    gk, gv = heads * dk, heads * dv
    assert gv % gk == 0
    row = lambda bb, j: (bb * nj + j, 0)
    in_specs = [pl.BlockSpec((c, gk), row),
                pl.BlockSpec((c, gk), lambda bb, j: (bb * nj + j, 1)),
                pl.BlockSpec((c, gv), lambda bb, j: (bb * nj + j, 2 * gk // gv)),
                pl.BlockSpec((c, gk), row),
                pl.BlockSpec((1, dv), lambda bb, j: (0, 0))]
    return pl.pallas_call(
        functools.partial(_gla_kernel, heads=heads, dk=dk, dv=dv, c=c),
        grid=(nb, nj),
        in_specs=in_specs,
        out_specs=[pl.BlockSpec((c, gv), row),
                   pl.BlockSpec((None, heads, dk, dv), lambda bb, j: (bb, 0, 0, 0))],
        out_shape=[jax.ShapeDtypeStruct((t, gv), BF16), jax.ShapeDtypeStruct((nb, heads, dk, dv), F32)],
        scratch_shapes=[pltpu.VMEM((heads, dv, dk), F32)],
        compiler_params=_params("parallel", "arbitrary"),
        name="gla",
    )(zp, zp, zp, la, gain)


def _gla_step_kernel(q_ref, k_ref, v_ref, la_ref, g_ref, s0_ref, o_ref, sout_ref, *, heads, dk, dv):
    c = la_ref.shape[0]
    row = lax.broadcasted_iota(jnp.int32, (c, c), 0)
    col = lax.broadcasted_iota(jnp.int32, (c, c), 1)
    rid = lax.broadcasted_iota(jnp.int32, (c, dk), 0)
    gain = g_ref[...]
    ones = jnp.ones((c, LANE), F32)
    for hd in range(heads):
        la = la_ref[:, hd * dk:(hd + 1) * dk]
        b = la
        sh = 1
        while sh < c:
            b = b + jnp.where(rid >= sh, pltpu.roll(b, sh, 0), 0.0)
            sh *= 2
        q = q_ref[:, hd * dk:(hd + 1) * dk].astype(F32) * (dk ** -0.5)
        k = k_ref[:, hd * dk:(hd + 1) * dk].astype(F32)
        v = v_ref[:, hd * dv:(hd + 1) * dv]
        d = b - b[c // 2 - 1:c // 2, :]
        a = jnp.where(col <= row, _dot_nt((q * jnp.exp(d)).astype(BF16), (k * jnp.exp(-d)).astype(BF16)), 0.0)
        st = s0_ref[hd]
        o = _dot(a.astype(BF16), v) + _dot((q * jnp.exp(b)).astype(BF16), st.astype(BF16))
        k_end = (k * jnp.exp(b[c - 1:c, :] - b)).astype(BF16)
        decay = jnp.exp(lax.dot_general(la, ones, TN_DIMS, preferred_element_type=F32,
                                        precision=lax.Precision.HIGHEST))
        sout_ref[hd] = (st * jnp.tile(decay, (1, dv // LANE))
                        + lax.dot_general(k_end, v, TN_DIMS, preferred_element_type=F32))
        o_ref[:, hd * dv:(hd + 1) * dv] = (o * _rms_inv(o, dv) * gain).astype(o_ref.dtype)


def _gla_step(zp, la, gain, s0, c, heads, dk, dv):
    nb = s0.shape[0]
    gk, gv = heads * dk, heads * dv
    assert la.shape[0] == nb * c and c <= GLA_LEAF and gv % gk == 0
    state = pl.BlockSpec((None, heads, dk, dv), lambda bb: (bb, 0, 0, 0))
    return pl.pallas_call(
        functools.partial(_gla_step_kernel, heads=heads, dk=dk, dv=dv),
        grid=(nb,),
        in_specs=[pl.BlockSpec((c, gk), lambda bb: (bb, 0)), pl.BlockSpec((c, gk), lambda bb: (bb, 1)),
                  pl.BlockSpec((c, gv), lambda bb: (bb, 2 * gk // gv)), pl.BlockSpec((c, gk), lambda bb: (bb, 0)),
                  pl.BlockSpec((1, dv), lambda bb: (0, 0)), state],
        out_specs=[pl.BlockSpec((c, gv), lambda bb: (bb, 0)), state],
        out_shape=[jax.ShapeDtypeStruct((nb * c, gv), BF16), jax.ShapeDtypeStruct((nb, heads, dk, dv), F32)],
        compiler_params=_params("parallel"),
        name="gla_step",
    )(zp, zp, zp, la, gain, s0)


def _mix_kernel(om_ref, og_ref, gate_o_ref, gate_m_ref, gate_g_ref, x_ref, w_ref, g_ref, y_ref, h_ref):
    o_gla = og_ref[...].astype(F32) * gate_o_ref[...].astype(F32)
    mixed = gate_m_ref[...].astype(F32) * om_ref[...].astype(F32) + gate_g_ref[...].astype(F32) * o_gla
    y = x_ref[...] + _dot(mixed.astype(BF16), w_ref[...])
    y_ref[...] = y
    h_ref[...] = (y * _rms_inv(y, y.shape[-1]) * g_ref[...]).astype(h_ref.dtype)


def _mix(o_mla, o_gla, zp, gate_block0, x, w_o, g, tm):
    t, d = x.shape
    row = lambda i: (i, 0)
    return pl.pallas_call(
        _mix_kernel,
        grid=(t // tm,),
        in_specs=[pl.BlockSpec((tm, d), row), pl.BlockSpec((tm, d), row),
                  pl.BlockSpec((tm, d), lambda i: (i, gate_block0)),
                  pl.BlockSpec((tm, d), lambda i: (i, gate_block0 + 1)),
                  pl.BlockSpec((tm, d), lambda i: (i, gate_block0 + 2)),
                  pl.BlockSpec((tm, d), row), pl.BlockSpec((d, d), lambda i: (0, 0)),
                  pl.BlockSpec((1, d), lambda i: (0, 0))],
        out_specs=[pl.BlockSpec((tm, d), row), pl.BlockSpec((tm, d), row)],
        out_shape=[jax.ShapeDtypeStruct((t, d), F32), jax.ShapeDtypeStruct((t, d), BF16)],
        compiler_params=_params("parallel"),
        name="mix",
    )(o_mla, o_gla, zp, zp, zp, x, w_o, g)


def _ffn_kernel(h_ref, y_ref, wu_ref, wd_ref, *refs):
    o_ref = refs[-1]
    f = pl.program_id(1)

    @pl.when(f == 0)
    def _():
        o_ref[...] = y_ref[...]

    u = jnp.maximum(_dot(h_ref[...], wu_ref[...]), 0.0)
    o_ref[...] += _dot((u * u).astype(BF16), wd_ref[...])


def _ffn(h, y, w_up, w_down, tm, tf, done=None):
    t, d = y.shape
    dff = w_up.shape[1]
    skip = 0 if done is None else 1
    in_specs = [pl.BlockSpec((tm, d), lambda i, f: (i + skip, 0)), pl.BlockSpec((tm, d), lambda i, f: (i + skip, 0)),
                pl.BlockSpec((d, tf), lambda i, f: (0, f)), pl.BlockSpec((tf, d), lambda i, f: (f, 0))]
    args = [h, y, w_up, w_down]
    if done is not None:
        in_specs.append(pl.BlockSpec(memory_space=pl.ANY))
        args.append(done)
    return pl.pallas_call(
        _ffn_kernel,
        grid=(t // tm - skip, dff // tf),
        in_specs=in_specs,
        out_specs=pl.BlockSpec((tm, d), lambda i, f: (i + skip, 0)),
        out_shape=jax.ShapeDtypeStruct((t, d), F32),
        input_output_aliases={} if done is None else {4: 0},
        compiler_params=_params("parallel", "arbitrary"),
        name="ffn",
    )(*args)


def _ffn_first_kernel(h_ref, y_ref, wu_ref, wd_ref, o_ref, wub_ref, wdb_ref):
    f = pl.program_id(0)

    @pl.when(f == 0)
    def _():
        o_ref[...] = y_ref[...]

    wu = wu_ref[...].astype(BF16)
    wd = wd_ref[...].astype(BF16)
    wub_ref[...] = wu
    wdb_ref[...] = wd
    u = jnp.maximum(_dot(h_ref[...], wu), 0.0)
    o_ref[...] += _dot((u * u).astype(BF16), wd)


def _ffn_first(h, y, w_up, w_down, tm, tf):
    t, d = y.shape
    dff = w_up.shape[1]
    return pl.pallas_call(
        _ffn_first_kernel,
        grid=(dff // tf,),
        in_specs=[pl.BlockSpec((tm, d), lambda f: (0, 0)), pl.BlockSpec((tm, d), lambda f: (0, 0)),
                  pl.BlockSpec((d, tf), lambda f: (0, f)), pl.BlockSpec((tf, d), lambda f: (f, 0))],
        out_specs=[pl.BlockSpec((tm, d), lambda f: (0, 0)),
                   pl.BlockSpec((d, tf), lambda f: (0, f)), pl.BlockSpec((tf, d), lambda f: (f, 0))],
        out_shape=[jax.ShapeDtypeStruct((t, d), F32), jax.ShapeDtypeStruct((d, dff), BF16),
                   jax.ShapeDtypeStruct((dff, d), BF16)],
        compiler_params=_params("arbitrary"),
        name="ffn_first",
    )(h, y, w_up, w_down)


def _rope_table(pos):
    half = MLA_ROPE // 2
    freqs = jnp.power(ROPE_THETA, -jnp.arange(half, dtype=F32) / half)
    ang = pos[:, None] * freqs[None, :]
    cos, sin = jnp.cos(ang), jnp.sin(ang)
    return jnp.concatenate([cos, cos, -sin, sin], axis=1)


def _swap_halves(w):
    half = w.shape[-1] // 2
    return jnp.concatenate([w[..., half:], w[..., :half]], axis=-1)


def _pick(t, pref):
    while t % pref:
        pref //= 2
    return pref


def kernel(x_prompt, x_sample, cache_mla_ckv, cache_mla_krope, state_gla, norm_mix_g, w_in, mla_q_norm_g,
           mla_w_uq, mla_kv_norm_g, mla_w_ukv, mla_q_gain_nope, mla_q_gain_rope, mla_k_gain_nope,
           mla_k_gain_rope, gla_w_a2, gla_b_a, gla_norm_g, w_o, norm_ffn_g, ffn_w_up, ffn_w_down):
    depth = w_in.shape[0]
    bp, seq, d = x_prompt.shape
    nb, dec_seq, _ = x_sample.shape
    past = cache_mla_ckv.shape[2]
    q_rank, heads = mla_w_uq.shape[1], mla_w_uq.shape[2]
    kv_rank = mla_w_ukv.shape[1]
    gla_heads, dk, dv = state_gla.shape[2], state_gla.shape[3], state_gla.shape[4]
    gate_rank = gla_w_a2.shape[1]
    gk, gv = gla_heads * dk, gla_heads * dv
    assert bp == 1 and heads * MLA_V == d and gv == d and seq % CHUNK == 0

    cs_p = _rope_table(jnp.arange(seq, dtype=F32))
    cs_s = jnp.tile(_rope_table(past + jnp.arange(dec_seq, dtype=F32)), (nb, 1))

    xp = x_prompt.reshape(seq, d)
    xs = x_sample.reshape(nb * dec_seq, d)
    outs = [[] for _ in range(6)]
    for l in range(depth):
        wi = w_in[l].astype(BF16)
        pts = np.cumsum([q_rank, kv_rank, MLA_ROPE, gk, gk, gv, gate_rank, gv, d]).tolist()
        w_qkv_lat = wi[:, :pts[1]]
        w_kr = wi[:, pts[1]:pts[2]]
        w_alr = wi[:, pts[5]:pts[6]]
        assert 3 * gate_rank <= LANE
        w_lat = jnp.concatenate(
            [w_qkv_lat, w_kr, _swap_halves(w_kr), w_alr, w_alr, w_alr,
             jnp.zeros((d, LANE - 3 * gate_rank), BF16)], axis=1)
        wa2_hi = gla_w_a2[l].astype(BF16)
        wa2_lo = (gla_w_a2[l] - wa2_hi.astype(F32)).astype(BF16)
        wa2_split = jnp.concatenate(
            [wa2_hi, wa2_lo, wa2_hi, jnp.zeros((LANE - 3 * gate_rank, gk), BF16)], axis=0)

        wq = mla_w_uq[l]
        wq_r = wq[..., MLA_NOPE:]
        w_uq = jnp.concatenate([wq[..., :MLA_NOPE], wq_r, _swap_halves(wq_r)], axis=-1)
        w_uq = w_uq.reshape(q_rank, heads * MLA_HEAD_PAD).astype(BF16)
        w_uq_t = w_uq.T
        wuk = mla_w_ukv[l][..., :MLA_NOPE].reshape(kv_rank, heads * MLA_NOPE).astype(BF16)
        wuk_t = wuk.T
        wuv = mla_w_ukv[l][..., MLA_NOPE:].reshape(kv_rank, heads * MLA_V).astype(BF16)
        wuv_t = wuv.T
        gain = jnp.concatenate([mla_q_gain_nope[l] * mla_k_gain_nope[l],
                                jnp.tile(mla_q_gain_rope[l] * mla_k_gain_rope[l], 2),
                                jnp.zeros((MLA_HEAD_PAD - MLA_QK_DIM,), F32)])[None, :] * MLA_SCALE
        gain_col = gain.T * LOG2E
        wo_b = w_o[l].astype(BF16)
        g_mix = norm_mix_g[l][None, :]
        g_q = mla_q_norm_g[l][None, :]
        g_kv = mla_kv_norm_g[l][None, :]
        g_gla = gla_norm_g[l][None, :]
        g_ffn = norm_ffn_g[l][None, :]
        ba = gla_b_a[l][None, :]

        def front(x, cs, tm, proj_w=None):
            h = _prenorm(x, g_mix, _pick(x.shape[0], 512))
            qlat, ckv, kr, la = _mla_lat(h, w_lat, g_q, g_kv, cs, wa2_split, ba, tm, q_rank, kv_rank, gate_rank)
            pm = _pick(x.shape[0], 1024)
            z_gla = z_gate = None
            if proj_w is None:
                z_gla, w_gla = _proj_first(h, wi, pts[2], pts[5] - pts[2], pm, 1024)
                z_gate, w_gate = _proj_first(h, wi, pts[6], wi.shape[1] - pts[6], pm, 1024, swish_cols=gv)
                proj_w = (w_gla, w_gate)
            if z_gla is None or x.shape[0] > pm:
                z_gla = _proj(h, proj_w[0], pm, 2048, done=z_gla)
                z_gate = _proj(h, proj_w[1], pm, gv, swish_cols=gv, done=z_gate)
            return qlat, ckv, kr, la, z_gla, z_gate, proj_w

        def back(x, o_mla, o_gla, z_gate, tm, ffn_w=None):
            y1, h2 = _mix(o_mla, o_gla, z_gate, 0, x, wo_b, g_ffn, tm)
            if ffn_w is not None:
                return _ffn(h2, y1, *ffn_w, tm, 1024), ffn_w
            y, *ffn_w = _ffn_first(h2, y1, ffn_w_up[l], ffn_w_down[l], tm, 512)
            if x.shape[0] > tm:
                y = _ffn(h2, y1, *ffn_w, tm, 1024, done=y)
            return y, ffn_w

        tile = _pick(seq, 512)
        qlat, ckv, kr, la, z_gla, z_gate, proj_w = front(xp, cs_p, tile)
        qt = _mla_qt(qlat, w_uq_t, cs_p.T, gain_col, tile, 2 * tile, heads)
        kcat, vt = _mla_kv(ckv, kr, wuk, wuv_t, tile, heads)
        o_mla = _flash(qt, kcat, vt, heads)
        o_gla, st = _gla(z_gla, la, g_gla, 1, _pick(seq, 128), gla_heads, dk, dv)
        xp_next, ffn_w = back(xp, o_mla, o_gla, z_gate, tile)
        outs[0].append(ckv.reshape(bp, seq, kv_rank))
        outs[1].append(kr.reshape(bp, seq, MLA_ROPE))
        outs[2].append(st)
        xp = xp_next

        qlat, ckv, kr, la, z_gla, z_gate, _ = front(xs, cs_s, nb * dec_seq, proj_w)
        qcat = _mla_q(qlat, w_uq, cs_s, gain, nb * dec_seq, heads)
        o_lat = _sattn(qcat, cache_mla_ckv[l], cache_mla_krope[l], ckv, kr, wuk_t, heads, dec_seq, _pick(past, 512))
        o_mla = _svup(o_lat, wuv, heads, dec_seq)
        o_gla, st = _gla_step(z_gla, la, g_gla, state_gla[l], dec_seq, gla_heads, dk, dv)
        xs_next, _ = back(xs, o_mla, o_gla, z_gate, nb * dec_seq, ffn_w)
        outs[3].append(ckv.reshape(nb, dec_seq, kv_rank))
        outs[4].append(kr.reshape(nb, dec_seq, MLA_ROPE))
        outs[5].append(st)
        xs = xs_next

    return (xp.reshape(bp, seq, d), xs.reshape(nb, dec_seq, d),
            jnp.stack(outs[0]), jnp.stack(outs[1]), jnp.stack(outs[2]),
            jnp.stack(outs[3]), jnp.stack(outs[4]), jnp.stack(outs[5]))
```

```python
import functools

import jax
import jax.numpy as jnp
import numpy as np
from jax import lax
from jax.experimental import pallas as pl
from jax.experimental.pallas import tpu as pltpu

F32 = jnp.float32
BF16 = jnp.bfloat16

EPS = 1e-6
CHUNK = 64
MLA_NOPE = 128
MLA_ROPE = 64
MLA_V = 128
MLA_QK_DIM = MLA_NOPE + MLA_ROPE
MLA_SCALE = MLA_QK_DIM ** -0.5
MLA_HEAD_PAD = 256
V_ROWS = MLA_V + 16
LOG2E = 1.4426950408889634
QBLK = 256
ROPE_THETA = 10000.0
GLA_TAU = 16.0
GLA_LEAF = 32
LANE = 128
V7X_VMEM_BYTES = 64 * 1024 * 1024
VMEM_LIMIT = V7X_VMEM_BYTES * 7 // 8

NT_DIMS = (((1,), (1,)), ((), ()))
TN_DIMS = (((0,), (0,)), ((), ()))


def _params(*sem):
    return pltpu.CompilerParams(dimension_semantics=sem, vmem_limit_bytes=VMEM_LIMIT)


def _dot(a, b):
    return jnp.dot(a, b, preferred_element_type=F32)


def _dot_nt(a, b):
    return lax.dot_general(a, b, NT_DIMS, preferred_element_type=F32)


def _rms_inv(x, n):
    return lax.rsqrt(jnp.sum(x * x, axis=-1, keepdims=True) / n + EPS)


def _mla_lat_kernel(x_ref, g_ref, w_ref, qg_ref, kvg_ref, cs_ref, wa2_ref, ba_ref,
                    h_ref, qlat_ref, ckv_ref, kr_ref, la_ref, *, q_rank, kv_rank, gate_rank):
    x = x_ref[...]
    h = (x * _rms_inv(x, x.shape[-1]) * g_ref[...]).astype(h_ref.dtype)
    h_ref[...] = h
    z = _dot(h, w_ref[...])
    q_lat = z[:, :q_rank]
    qlat_ref[...] = (q_lat * _rms_inv(q_lat, q_rank) * qg_ref[...]).astype(qlat_ref.dtype)
    kv_lat = z[:, q_rank:q_rank + kv_rank]
    ckv_ref[...] = kv_lat * _rms_inv(kv_lat, kv_rank) * kvg_ref[...]
    o = q_rank + kv_rank
    rr = z[:, o:o + LANE] * cs_ref[...]
    kr_ref[...] = rr[:, :MLA_ROPE] + rr[:, MLA_ROPE:]
    a3 = z[:, o + LANE:o + 2 * LANE]
    a_hi = a3.astype(BF16)
    a_lo = (a3 - a_hi.astype(F32)).astype(BF16)
    lane = lax.broadcasted_iota(jnp.int32, a3.shape, 1)
    u = _dot(jnp.where(lane < 2 * gate_rank, a_hi, a_lo), wa2_ref[...]) + ba_ref[...]
    log_sig = jnp.minimum(u, 0.0) - jnp.log1p(jnp.exp(-jnp.abs(u)))
    la_ref[...] = log_sig / GLA_TAU


def _mla_lat(x, g, w_lat, qg, kvg, cs, wa2_split, ba, tm, q_rank, kv_rank, gate_rank):
    t, d = x.shape
    n = w_lat.shape[1]
    gk = wa2_split.shape[1]
    row = lambda i: (i, 0)
    fix = lambda i: (0, 0)
    return pl.pallas_call(
        functools.partial(_mla_lat_kernel, q_rank=q_rank, kv_rank=kv_rank, gate_rank=gate_rank),
        grid=(t // tm,),
        in_specs=[pl.BlockSpec((tm, d), row), pl.BlockSpec((1, d), fix), pl.BlockSpec((d, n), fix),
                  pl.BlockSpec((1, q_rank), fix), pl.BlockSpec((1, kv_rank), fix),
                  pl.BlockSpec((tm, LANE), row), pl.BlockSpec((LANE, gk), fix), pl.BlockSpec((1, gk), fix)],
        out_specs=[pl.BlockSpec((tm, d), row), pl.BlockSpec((tm, q_rank), row), pl.BlockSpec((tm, kv_rank), row),
                   pl.BlockSpec((tm, MLA_ROPE), row), pl.BlockSpec((tm, gk), row)],
        out_shape=[jax.ShapeDtypeStruct((t, d), BF16),
                   jax.ShapeDtypeStruct((t, q_rank), BF16), jax.ShapeDtypeStruct((t, kv_rank), F32),
                   jax.ShapeDtypeStruct((t, MLA_ROPE), F32), jax.ShapeDtypeStruct((t, gk), F32)],
        compiler_params=_params("parallel"),
        name="mla_lat",
    )(x, g, w_lat, qg, kvg, cs, wa2_split, ba)


def _gate_act(z, swish):
    return jax.nn.sigmoid(z) * jnp.where(swish, z, 1.0)


def _proj_kernel(h_ref, w_ref, *refs, swish_tiles):
    o_ref = refs[-1]
    z = _dot(h_ref[...], w_ref[...])
    if swish_tiles is not None:
        z = _gate_act(z, pl.program_id(1) < swish_tiles)
    o_ref[...] = z.astype(o_ref.dtype)


def _proj(h, w, tm, tn, swish_cols=None, done=None):
    t, d = h.shape
    n = w.shape[1]
    skip = 0 if done is None else 1
    in_specs = [pl.BlockSpec((tm, d), lambda i, j: (i + skip, 0)), pl.BlockSpec((d, tn), lambda i, j: (0, j))]
    args = [h, w]
    if done is not None:
        in_specs.append(pl.BlockSpec(memory_space=pl.ANY))
        args.append(done)
    assert swish_cols is None or swish_cols % tn == 0
    return pl.pallas_call(
        functools.partial(_proj_kernel, swish_tiles=None if swish_cols is None else swish_cols // tn),
        grid=(t // tm - skip, n // tn),
        in_specs=in_specs,
        out_specs=pl.BlockSpec((tm, tn), lambda i, j: (i + skip, j)),
        out_shape=jax.ShapeDtypeStruct((t, n), BF16),
        input_output_aliases={} if done is None else {2: 0},
        compiler_params=_params("parallel", "parallel"),
        name="proj",
    )(*args)


def _proj_first_kernel(h_ref, a_ref, b_ref, o_ref, w_ref, *, shift, n_cols, next_col0, swish_tiles):
    j = pl.program_id(0)
    tn = a_ref.shape[1]
    b = b_ref[...]
    col = next_col0 + j * tn + lax.broadcasted_iota(jnp.int32, b.shape, 1)
    src = jnp.concatenate([a_ref[...], jnp.where(col < n_cols, b, jnp.zeros_like(b))], axis=1).astype(F32)
    w = pltpu.roll(src, src.shape[1] - shift, 1)[:, :tn].astype(BF16)
    w_ref[...] = w
    z = _dot(h_ref[...], w)
    if swish_tiles is not None:
        z = _gate_act(z, j < swish_tiles)
    o_ref[...] = z.astype(o_ref.dtype)


def _proj_first(h, w_all, col0, width, tm, tn, swish_cols=None):
    t, d = h.shape
    n_cols = w_all.shape[1]
    base = col0 // LANE * LANE
    shift = col0 - base
    assert shift > 0 and base % tn == 0 and width % tn == 0 and tn % LANE == 0
    assert swish_cols is None or swish_cols % tn == 0
    return pl.pallas_call(
        functools.partial(_proj_first_kernel, shift=shift, n_cols=n_cols, next_col0=base + tn,
                          swish_tiles=None if swish_cols is None else swish_cols // tn),
        grid=(width // tn,),
        in_specs=[pl.BlockSpec((tm, d), lambda j: (0, 0)),
                  pl.BlockSpec((d, tn), lambda j: (0, base // tn + j)),
                  pl.BlockSpec((d, LANE), lambda j: (0, (base + (j + 1) * tn) // LANE))],
        out_specs=[pl.BlockSpec((tm, tn), lambda j: (0, j)), pl.BlockSpec((d, tn), lambda j: (0, j))],
        out_shape=[jax.ShapeDtypeStruct((t, width), BF16), jax.ShapeDtypeStruct((d, width), BF16)],
        compiler_params=_params("parallel"),
        name="proj_first",
    )(h, w_all, w_all)


def _mla_q_kernel(ql_ref, w_ref, cs_ref, gain_ref, q_ref, *, heads):
    ql = ql_ref[...]
    cs = cs_ref[...]
    gain = gain_ref[...]
    lane = lax.broadcasted_iota(jnp.int32, (1, LANE), 1)
    for hd in range(heads):
        c0 = hd * MLA_HEAD_PAD
        z = _dot(ql, w_ref[:, c0:c0 + MLA_HEAD_PAD])
        nope = z[:, :MLA_NOPE]
        rr = z[:, MLA_NOPE:] * cs
        rot = rr + pltpu.roll(rr, MLA_ROPE, 1)
        ss = (jnp.sum(nope * nope, axis=-1, keepdims=True)
              + jnp.sum(jnp.where(lane < MLA_ROPE, rot * rot, 0.0), axis=-1, keepdims=True))
        inv = lax.rsqrt(ss / MLA_QK_DIM + EPS)
        q_ref[:, c0:c0 + MLA_NOPE] = (nope * inv * gain[:, :MLA_NOPE]).astype(q_ref.dtype)
        q_ref[:, c0 + MLA_NOPE:c0 + MLA_HEAD_PAD] = (rot * inv * gain[:, MLA_NOPE:]).astype(q_ref.dtype)


def _mla_q(qlat, w_uq, cs, gain, tm, heads):
    t, r = qlat.shape
    n = heads * MLA_HEAD_PAD
    return pl.pallas_call(
        functools.partial(_mla_q_kernel, heads=heads),
        grid=(t // tm,),
        in_specs=[pl.BlockSpec((tm, r), lambda i: (i, 0)), pl.BlockSpec((r, n), lambda i: (0, 0)),
                  pl.BlockSpec((tm, LANE), lambda i: (i, 0)), pl.BlockSpec((1, MLA_HEAD_PAD), lambda i: (0, 0))],
        out_specs=pl.BlockSpec((tm, n), lambda i: (i, 0)),
        out_shape=jax.ShapeDtypeStruct((t, n), BF16),
        compiler_params=_params("parallel"),
        name="mla_q",
    )(qlat, w_uq, cs, gain)


def _mla_qt_kernel(ql_ref, w_ref, cs_ref, gain_ref, q_ref, *, heads):
    ql = ql_ref[...]
    cs = cs_ref[...]
    gain = gain_ref[...]
    for hd in range(heads):
        r0 = hd * MLA_HEAD_PAD
        z = _dot_nt(w_ref[r0:r0 + MLA_HEAD_PAD, :], ql)
        nope = z[:MLA_NOPE]
        rr = z[MLA_NOPE:] * cs
        rot = rr[:MLA_ROPE] + rr[MLA_ROPE:]
        ss = jnp.sum(nope * nope, axis=0, keepdims=True) + jnp.sum(rot * rot, axis=0, keepdims=True)
        inv = lax.rsqrt(ss / MLA_QK_DIM + EPS)
        qn = (nope * inv * gain[:MLA_NOPE]).astype(q_ref.dtype)
        qr = (rot * inv * gain[MLA_NOPE:MLA_QK_DIM]).astype(q_ref.dtype)
        for bb in range(q_ref.shape[1]):
            q_ref[hd, bb, :MLA_NOPE, :] = qn[:, bb * QBLK:(bb + 1) * QBLK]
            q_ref[hd, bb, MLA_NOPE:MLA_QK_DIM, :] = qr[:, bb * QBLK:(bb + 1) * QBLK]
            q_ref[hd, bb, MLA_QK_DIM:, :] = jnp.zeros((MLA_HEAD_PAD - MLA_QK_DIM, QBLK), q_ref.dtype)


def _mla_qt(qlat, w_uq_t, cs_t, gain_col, tm, tq, heads):
    t, r = qlat.shape
    n = heads * MLA_HEAD_PAD
    per = tq // tm
    assert tm % QBLK == 0 and tq % tm == 0
    return pl.pallas_call(
        functools.partial(_mla_qt_kernel, heads=heads),
        grid=(t // tm,),
        in_specs=[pl.BlockSpec((tm, r), lambda i: (i, 0)), pl.BlockSpec((n, r), lambda i: (0, 0)),
                  pl.BlockSpec((LANE, tm), lambda i: (0, i)), pl.BlockSpec((MLA_HEAD_PAD, 1), lambda i: (0, 0))],
        out_specs=pl.BlockSpec((heads, None, tm // QBLK, MLA_HEAD_PAD, QBLK),
                               lambda i: (0, i // per, i % per, 0, 0)),
        out_shape=jax.ShapeDtypeStruct((heads, t // tq, tq // QBLK, MLA_HEAD_PAD, QBLK), BF16),
        compiler_params=_params("parallel"),
        name="mla_qt",
    )(qlat, w_uq_t, cs_t, gain_col)


def _mla_kv_kernel(ckv_ref, kr_ref, wk_ref, wvt_ref, k_ref, vt_ref, *, heads):
    c = ckv_ref[...].astype(BF16)
    kr = kr_ref[...]
    kr2 = jnp.sum(kr * kr, axis=-1, keepdims=True)
    kr_pad = jnp.concatenate([kr, jnp.zeros_like(kr)], axis=1)
    for pair in range(heads // 2):
        z = _dot(c, wk_ref[:, pair * 2 * MLA_NOPE:(pair + 1) * 2 * MLA_NOPE])
        for sub in range(2):
            kn = z[:, sub * MLA_NOPE:(sub + 1) * MLA_NOPE]
            inv = lax.rsqrt((jnp.sum(kn * kn, axis=-1, keepdims=True) + kr2) / MLA_QK_DIM + EPS)
            c0 = (2 * pair + sub) * MLA_HEAD_PAD
            k_ref[:, c0:c0 + MLA_NOPE] = (kn * inv).astype(k_ref.dtype)
            k_ref[:, c0 + MLA_NOPE:c0 + MLA_HEAD_PAD] = (kr_pad * inv).astype(k_ref.dtype)
    vt = _dot_nt(wvt_ref[...], c)
    for hd in range(heads):
        vt_ref[hd, :MLA_V, :] = vt[hd * MLA_V:(hd + 1) * MLA_V].astype(vt_ref.dtype)
        vt_ref[hd, MLA_V:, :] = jnp.ones((V_ROWS - MLA_V, vt.shape[1]), vt_ref.dtype)


def _mla_kv(ckv, kr, w_uk, w_uv_t, tm, heads):
    t, r = ckv.shape
    return pl.pallas_call(
        functools.partial(_mla_kv_kernel, heads=heads),
        grid=(t // tm,),
        in_specs=[pl.BlockSpec((tm, r), lambda i: (i, 0)), pl.BlockSpec((tm, MLA_ROPE), lambda i: (i, 0)),
                  pl.BlockSpec((r, heads * MLA_NOPE), lambda i: (0, 0)),
                  pl.BlockSpec((heads * MLA_V, r), lambda i: (0, 0))],
        out_specs=[pl.BlockSpec((tm, heads * MLA_HEAD_PAD), lambda i: (i, 0)),
                   pl.BlockSpec((heads, None, V_ROWS, tm), lambda i: (0, i, 0, 0))],
        out_shape=[jax.ShapeDtypeStruct((t, heads * MLA_HEAD_PAD), BF16),
                   jax.ShapeDtypeStruct((heads, t // tm, V_ROWS, tm), BF16)],
        compiler_params=_params("parallel"),
        name="mla_kv",
    )(ckv, kr, w_uk, w_uv_t)


def _diag_mode(u, c, tk):
    k_lo, k_hi = (u * tk) // CHUNK, ((u + 1) * tk - 1) // CHUNK
    q_lo, q_hi = (c * QBLK) // CHUNK, ((c + 1) * QBLK - 1) // CHUNK
    return "all" if k_hi <= q_lo else "none" if k_lo > q_hi else "some"


def _diag_bias(tk, n_blk):
    index, blocks = {}, []
    for u in range(2):
        for c in range(n_blk):
            if _diag_mode(u, c, tk) == "some":
                k_chunk = (u * tk + np.arange(tk)[:, None]) // CHUNK
                q_chunk = (c * QBLK + np.arange(QBLK)[None, :]) // CHUNK
                index[(u, c)] = len(blocks)
                blocks.append(np.where(k_chunk <= q_chunk, 0.0, -np.inf).astype(np.float32))
    return index, np.stack(blocks)


def _flash_kernel(qt_ref, k_ref, vt_ref, bias_ref, o_ref, m_ref, acc_ref, s_ref, mx_ref, p_ref, al_ref,
                  *, tq, tk, bias_index):
    n_q, n_blk = qt_ref.shape[0], qt_ref.shape[1]
    acc_ref[...] = jnp.zeros_like(acc_ref)


    def diag_mode(u, c):
        return _diag_mode(u, c, tk)

    def score(i, t, slot, c, mode):
        if mode == "none":
            return
        k = k_ref[pl.ds(pl.multiple_of(t * tk, tk), tk), :]
        s = _dot(k, qt_ref[i, c])
        if mode == "some":
            s = s + bias_ref[bias_index[(slot, c)]]
        s_ref[slot, c] = s
        mx_ref[slot, c] = jnp.max(s, axis=0, keepdims=True)

    def soften(slot, c, mode="all", first=False):
        if mode == "none":
            return
        if first:
            m_new = mx_ref[slot, c]
            al_ref[slot, c] = jnp.zeros_like(m_new)
        else:
            m_prev = m_ref[c]
            m_new = jnp.maximum(m_prev, mx_ref[slot, c])
            al_ref[slot, c] = jnp.exp2(m_prev - m_new)
        p_ref[slot, c] = jnp.exp2(s_ref[slot, c] - m_new).astype(BF16)
        m_ref[c] = m_new

    def gather(t, slot, c, mode="all"):
        if mode == "none":
            return
        acc_ref[c] = al_ref[slot, c] * acc_ref[c] + _dot(vt_ref[t], p_ref[slot, c])

    def pair(i, g, diag):
        for c in range(n_blk):
            score(i, g, 0, c, diag_mode(0, c) if diag else "all")
            soften(1, c)
            gather(g - 2, 0, c)
        for c in range(n_blk):
            score(i, g + 1, 1, c, diag_mode(1, c) if diag else "all")
            soften(0, c, diag_mode(0, c) if diag else "all")
            gather(g - 1, 1, c)

    def head(i, diag):
        for c in range(n_blk):
            score(i, 0, 0, c, diag_mode(0, c) if diag else "all")
        for c in range(n_blk):
            score(i, 1, 1, c, diag_mode(1, c) if diag else "all")
            soften(0, c, diag_mode(0, c) if diag else "all", first=True)

    def tail(i):
        for c in range(n_blk):
            soften(1, c, diag_mode(1, c))
            gather(2 * i, 0, c, diag_mode(0, c))
        for c in range(n_blk):
            gather(2 * i + 1, 1, c, diag_mode(1, c))
        for c in range(n_blk):
            acc = acc_ref[c]
            rows = pl.ds(pl.multiple_of(i * tq + c * QBLK, QBLK), QBLK)
            o_ref[rows, :] = (acc[:MLA_V] / acc[MLA_V:MLA_V + 1]).T.astype(o_ref.dtype)

    def middle(i):
        def body(j, carry):
            pair(i, 2 * j, False)
            return carry

        lax.fori_loop(1, i, body, 0)
        pair(i, 2 * i, True)

    head(0, True)
    if n_q > 1:
        tail(0)
        head(1, False)

        def outer(i, carry):
            middle(i)
            tail(i)
            head(i + 1, False)
            return carry

        lax.fori_loop(1, n_q - 1, outer, 0)
        middle(n_q - 1)
    tail(n_q - 1)


def _flash(qt, k, vt, heads):
    t = k.shape[0]
    n_q, n_blk = qt.shape[1], qt.shape[2]
    tq = n_blk * QBLK
    n_kt, tk = vt.shape[1], vt.shape[3]
    assert tq == 2 * tk and tk % CHUNK == 0 and qt.shape[4] == QBLK and n_q * tq == t
    bias_index, bias = _diag_bias(tk, n_blk)
    return pl.pallas_call(
        functools.partial(_flash_kernel, tq=tq, tk=tk, bias_index=bias_index),
        grid=(heads,),
        in_specs=[pl.BlockSpec((None, n_q, n_blk, MLA_HEAD_PAD, QBLK), lambda hd: (hd, 0, 0, 0, 0)),
                  pl.BlockSpec((t, MLA_HEAD_PAD), lambda hd: (0, hd)),
                  pl.BlockSpec((None, n_kt, V_ROWS, tk), lambda hd: (hd, 0, 0, 0)),
                  pl.BlockSpec(bias.shape, lambda hd: (0, 0, 0))],
        out_specs=pl.BlockSpec((t, MLA_V), lambda hd: (0, hd)),
        out_shape=jax.ShapeDtypeStruct((t, heads * MLA_V), BF16),
        scratch_shapes=[pltpu.VMEM((n_blk, 1, QBLK), F32), pltpu.VMEM((n_blk, V_ROWS, QBLK), F32),
                        pltpu.VMEM((2, n_blk, tk, QBLK), F32), pltpu.VMEM((2, n_blk, 1, QBLK), F32),
                        pltpu.VMEM((2, n_blk, tk, QBLK), BF16), pltpu.VMEM((2, n_blk, 1, QBLK), F32)],
        compiler_params=_params("parallel"),
        name="flash",
    )(qt, k, vt, jnp.asarray(bias))


def _sattn_kernel(q_ref, cc_ref, ck_ref, nc_ref, nk_ref, wuk_ref, o_ref,
                  qabs_ref, qr_ref, m_ref, l_ref, acc_ref, *, heads, nq, tk, n_new):
    past, rank = cc_ref.shape
    n_tiles = past // tk
    for hd in range(heads):
        c0 = hd * MLA_HEAD_PAD
        qn = q_ref[:, c0:c0 + MLA_NOPE]
        qabs_ref[hd * nq:(hd + 1) * nq, :] = _dot(
            qn, wuk_ref[hd * MLA_NOPE:(hd + 1) * MLA_NOPE, :]).astype(qabs_ref.dtype)
        qr_ref[hd * nq:(hd + 1) * nq, :] = q_ref[:, c0 + MLA_NOPE:c0 + MLA_NOPE + MLA_ROPE]
    m_ref[...] = jnp.full_like(m_ref, -jnp.inf)
    l_ref[...] = jnp.zeros_like(l_ref)
    acc_ref[...] = jnp.zeros_like(acc_ref)

    def tile(t):
        if t < n_tiles:
            return cc_ref[t * tk:(t + 1) * tk, :].astype(BF16), ck_ref[t * tk:(t + 1) * tk, :]
        pad = LANE - n_new
        c_new = jnp.concatenate([nc_ref[...], jnp.zeros((pad, rank), F32)], axis=0)
        k_new = jnp.concatenate([nk_ref[...], jnp.zeros((pad, MLA_ROPE), F32)], axis=0)
        return c_new.astype(BF16), k_new

    def key_norm(c, kr):
        n_keys = c.shape[0]
        kn_t = _dot_nt(wuk_ref[...], c)
        ss_t = jnp.sum((kn_t * kn_t).reshape(heads, MLA_NOPE, n_keys), axis=1)
        kr2_t = lax.dot_general(jnp.ones((8, MLA_ROPE), F32), kr * kr, NT_DIMS,
                                preferred_element_type=F32, precision=lax.Precision.HIGHEST)[0:1]
        return lax.rsqrt((ss_t + kr2_t) / MLA_QK_DIM + EPS)

    def attend(c, kr, inv_t, n_valid):
        n_keys = c.shape[0]
        s = _dot_nt(qabs_ref[...], c) + _dot_nt(qr_ref[...], kr.astype(BF16))
        s = s * jnp.broadcast_to(inv_t[:, None, :], (heads, nq, n_keys)).reshape(heads * nq, n_keys)
        if n_valid < n_keys:
            s = jnp.where(lax.broadcasted_iota(jnp.int32, s.shape, 1) < n_valid, s, -jnp.inf)
        m_prev = m_ref[...]
        m_new = jnp.maximum(m_prev, jnp.max(s, axis=-1, keepdims=True))
        alpha = jnp.exp(m_prev - m_new)
        p = jnp.exp(s - m_new)
        l_ref[...] = alpha * l_ref[...] + jnp.sum(p, axis=-1, keepdims=True)
        acc_ref[...] = alpha * acc_ref[...] + _dot(p.astype(BF16), c)
        m_ref[...] = m_new

    c, kr = tile(0)
    inv_t = key_norm(c, kr)
    for t in range(n_tiles + 1):
        if t < n_tiles:
            c_next, kr_next = tile(t + 1)
            inv_next = key_norm(c_next, kr_next)
        attend(c, kr, inv_t, tk if t < n_tiles else n_new)
        if t < n_tiles:
            c, kr, inv_t = c_next, kr_next, inv_next
    o_ref[...] = (acc_ref[...] / l_ref[...]).astype(o_ref.dtype)


def _sattn(q, cache_c, cache_k, new_c, new_k, wuk_t, heads, nq, tk):
    nb, past, rank = cache_c.shape
    return pl.pallas_call(
        functools.partial(_sattn_kernel, heads=heads, nq=nq, tk=tk, n_new=nq),
        grid=(nb,),
        in_specs=[pl.BlockSpec((nq, heads * MLA_HEAD_PAD), lambda b: (b, 0)),
                  pl.BlockSpec((None, past, rank), lambda b: (b, 0, 0)),
                  pl.BlockSpec((None, past, MLA_ROPE), lambda b: (b, 0, 0)),
                  pl.BlockSpec((nq, rank), lambda b: (b, 0)),
                  pl.BlockSpec((nq, MLA_ROPE), lambda b: (b, 0)),
                  pl.BlockSpec(wuk_t.shape, lambda b: (0, 0))],
        out_specs=pl.BlockSpec((None, heads * nq, rank), lambda b: (b, 0, 0)),
        out_shape=jax.ShapeDtypeStruct((nb, heads * nq, rank), BF16),
        scratch_shapes=[pltpu.VMEM((heads * nq, rank), BF16), pltpu.VMEM((heads * nq, MLA_ROPE), BF16),
                        pltpu.VMEM((heads * nq, 1), F32), pltpu.VMEM((heads * nq, 1), F32),
                        pltpu.VMEM((heads * nq, rank), F32)],
        compiler_params=_params("parallel"),
        name="sattn",
    )(q, cache_c, cache_k, new_c, new_k, wuk_t)


def _svup_kernel(ol_ref, w_ref, o_ref, *, nq):
    nb, rows, rank = ol_ref.shape
    for hh in range(rows // nq):
        x = ol_ref[:, hh * nq:(hh + 1) * nq, :].reshape(nb * nq, rank)
        o_ref[:, hh * MLA_V:(hh + 1) * MLA_V] = _dot(x, w_ref[:, hh * MLA_V:(hh + 1) * MLA_V]).astype(o_ref.dtype)


def _svup(o_lat, wuv, heads, nq):
    nb, _, rank = o_lat.shape
    hs = 4 if heads % 4 == 0 else 1
    return pl.pallas_call(
        functools.partial(_svup_kernel, nq=nq),
        grid=(heads // hs,),
        in_specs=[pl.BlockSpec((nb, hs * nq, rank), lambda g: (0, g, 0)),
                  pl.BlockSpec((rank, hs * MLA_V), lambda g: (0, g))],
        out_specs=pl.BlockSpec((nb * nq, hs * MLA_V), lambda g: (0, g)),
        out_shape=jax.ShapeDtypeStruct((nb * nq, heads * MLA_V), BF16),
        compiler_params=_params("parallel"),
        name="svup",
    )(o_lat, wuv)


def _gla_levels(c):
    leaf = min(c, GLA_LEAF)
    levels = [(leaf, leaf // 2 - 1)]
    g = 2 * leaf
    while g <= c:
        levels.append((g, g // 2 - 1))
        g *= 2
    return levels


def _gla_kernel(q_ref, k_ref, v_ref, la_ref, g_ref, o_ref, sout_ref, st_ref, *, heads, dk, dv, c):
    j = pl.program_id(1)
    nj = pl.num_programs(1)

    @pl.when(j == 0)
    def _():
        st_ref[...] = jnp.zeros_like(st_ref)

    levels = _gla_levels(c)
    row = lax.broadcasted_iota(jnp.int32, (c, c), 0)
    col = lax.broadcasted_iota(jnp.int32, (c, c), 1)
    masks = []
    for lv, (g, _) in enumerate(levels):
        same = (row // g) == (col // g)
        if lv == 0:
            masks.append(jnp.logical_and(same, col <= row))
        else:
            half = g // 2
            masks.append(jnp.logical_and(same, jnp.logical_and((row // half) % 2 == 1, (col // half) % 2 == 0)))
    rid = lax.broadcasted_iota(jnp.int32, (c, dk), 0)
    gain = g_ref[...]

    for hd in range(heads):
        la = la_ref[:, hd * dk:(hd + 1) * dk]
        b = la
        sh = 1
        while sh < c:
            b = b + jnp.where(rid >= sh, pltpu.roll(b, sh, 0), 0.0)
            sh *= 2
        b_end = b[c - 1:c, :]
        q = q_ref[:, hd * dk:(hd + 1) * dk].astype(F32) * (dk ** -0.5)
        k = k_ref[:, hd * dk:(hd + 1) * dk].astype(F32)
        v = v_ref[:, hd * dv:(hd + 1) * dv]

        a = jnp.zeros((c, c), F32)
        for lv, (g, r) in enumerate(levels):
            ref_rows = jnp.broadcast_to(b.reshape(c // g, g, dk)[:, r:r + 1, :], (c // g, g, dk)).reshape(c, dk)
            d = b - ref_rows
            if lv == 0:
                fq, fk = jnp.exp(d), jnp.exp(-d)
            else:
                fq = fk = jnp.exp(-jnp.abs(d))
            a_lv = _dot_nt((q * fq).astype(BF16), (k * fk).astype(BF16))
            a = jnp.where(masks[lv], a_lv, a)

        st = st_ref[hd]
        o = _dot(a.astype(BF16), v) + _dot_nt((q * jnp.exp(b)).astype(BF16), st.astype(BF16))
        k_end = (k * jnp.exp(b_end - b)).astype(BF16)
        st_ref[hd] = st * jnp.exp(b_end) + lax.dot_general(v, k_end, TN_DIMS, preferred_element_type=F32)
        o_ref[:, hd * dv:(hd + 1) * dv] = (o * _rms_inv(o, dv) * gain).astype(o_ref.dtype)

    @pl.when(j == nj - 1)
    def _():
        for hd in range(heads):
            sout_ref[hd] = st_ref[hd].T


def _gla(zp, la, gain, nb, c, heads, dk, dv):
    t = la.shape[0]
    nj = t // (nb * c)
    gk, gv = heads * dk, heads * dv
    assert gv % gk == 0
    row = lambda bb, j: (bb * nj + j, 0)
    in_specs = [pl.BlockSpec((c, gk), row),
                pl.BlockSpec((c, gk), lambda bb, j: (bb * nj + j, 1)),
                pl.BlockSpec((c, gv), lambda bb, j: (bb * nj + j, 2 * gk // gv)),
                pl.BlockSpec((c, gk), row),
                pl.BlockSpec((1, dv), lambda bb, j: (0, 0))]
    return pl.pallas_call(
        functools.partial(_gla_kernel, heads=heads, dk=dk, dv=dv, c=c),
        grid=(nb, nj),
        in_specs=in_specs,
        out_specs=[pl.BlockSpec((c, gv), row),
                   pl.BlockSpec((None, heads, dk, dv), lambda bb, j: (bb, 0, 0, 0))],
        out_shape=[jax.ShapeDtypeStruct((t, gv), BF16), jax.ShapeDtypeStruct((nb, heads, dk, dv), F32)],
        scratch_shapes=[pltpu.VMEM((heads, dv, dk), F32)],
        compiler_params=_params("parallel", "arbitrary"),
        name="gla",
    )(zp, zp, zp, la, gain)


def _gla_step_kernel(q_ref, k_ref, v_ref, la_ref, g_ref, s0_ref, o_ref, sout_ref, *, heads, dk, dv):
    c = la_ref.shape[0]
    row = lax.broadcasted_iota(jnp.int32, (c, c), 0)
    col = lax.broadcasted_iota(jnp.int32, (c, c), 1)
    rid = lax.broadcasted_iota(jnp.int32, (c, dk), 0)
    gain = g_ref[...]
    ones = jnp.ones((c, LANE), F32)
    for hd in range(heads):
        la = la_ref[:, hd * dk:(hd + 1) * dk]
        b = la
        sh = 1
        while sh < c:
            b = b + jnp.where(rid >= sh, pltpu.roll(b, sh, 0), 0.0)
            sh *= 2
        q = q_ref[:, hd * dk:(hd + 1) * dk].astype(F32) * (dk ** -0.5)
        k = k_ref[:, hd * dk:(hd + 1) * dk].astype(F32)
        v = v_ref[:, hd * dv:(hd + 1) * dv]
        d = b - b[c // 2 - 1:c // 2, :]
        a = jnp.where(col <= row, _dot_nt((q * jnp.exp(d)).astype(BF16), (k * jnp.exp(-d)).astype(BF16)), 0.0)
        st = s0_ref[hd]
        o = _dot(a.astype(BF16), v) + _dot((q * jnp.exp(b)).astype(BF16), st.astype(BF16))
        k_end = (k * jnp.exp(b[c - 1:c, :] - b)).astype(BF16)
        decay = jnp.exp(lax.dot_general(la, ones, TN_DIMS, preferred_element_type=F32,
                                        precision=lax.Precision.HIGHEST))
        sout_ref[hd] = (st * jnp.tile(decay, (1, dv // LANE))
                        + lax.dot_general(k_end, v, TN_DIMS, preferred_element_type=F32))
        o_ref[:, hd * dv:(hd + 1) * dv] = (o * _rms_inv(o, dv) * gain).astype(o_ref.dtype)


def _gla_step(zp, la, gain, s0, c, heads, dk, dv):
    nb = s0.shape[0]
    gk, gv = heads * dk, heads * dv
    assert la.shape[0] == nb * c and c <= GLA_LEAF and gv % gk == 0
    state = pl.BlockSpec((None, heads, dk, dv), lambda bb: (bb, 0, 0, 0))
    return pl.pallas_call(
        functools.partial(_gla_step_kernel, heads=heads, dk=dk, dv=dv),
        grid=(nb,),
        in_specs=[pl.BlockSpec((c, gk), lambda bb: (bb, 0)), pl.BlockSpec((c, gk), lambda bb: (bb, 1)),
                  pl.BlockSpec((c, gv), lambda bb: (bb, 2 * gk // gv)), pl.BlockSpec((c, gk), lambda bb: (bb, 0)),
                  pl.BlockSpec((1, dv), lambda bb: (0, 0)), state],
        out_specs=[pl.BlockSpec((c, gv), lambda bb: (bb, 0)), state],
        out_shape=[jax.ShapeDtypeStruct((nb * c, gv), BF16), jax.ShapeDtypeStruct((nb, heads, dk, dv), F32)],
        compiler_params=_params("parallel"),
        name="gla_step",
    )(zp, zp, zp, la, gain, s0)


def _mix_kernel(om_ref, og_ref, gate_o_ref, gate_m_ref, gate_g_ref, x_ref, w_ref, g_ref, y_ref, h_ref):
    o_gla = og_ref[...].astype(F32) * gate_o_ref[...].astype(F32)
    mixed = gate_m_ref[...].astype(F32) * om_ref[...].astype(F32) + gate_g_ref[...].astype(F32) * o_gla
    y = x_ref[...] + _dot(mixed.astype(BF16), w_ref[...])
    y_ref[...] = y
    h_ref[...] = (y * _rms_inv(y, y.shape[-1]) * g_ref[...]).astype(h_ref.dtype)


def _mix(o_mla, o_gla, zp, gate_block0, x, w_o, g, tm):
    t, d = x.shape
    row = lambda i: (i, 0)
    return pl.pallas_call(
        _mix_kernel,
        grid=(t // tm,),
        in_specs=[pl.BlockSpec((tm, d), row), pl.BlockSpec((tm, d), row),
                  pl.BlockSpec((tm, d), lambda i: (i, gate_block0)),
                  pl.BlockSpec((tm, d), lambda i: (i, gate_block0 + 1)),
                  pl.BlockSpec((tm, d), lambda i: (i, gate_block0 + 2)),
                  pl.BlockSpec((tm, d), row), pl.BlockSpec((d, d), lambda i: (0, 0)),
                  pl.BlockSpec((1, d), lambda i: (0, 0))],
        out_specs=[pl.BlockSpec((tm, d), row), pl.BlockSpec((tm, d), row)],
        out_shape=[jax.ShapeDtypeStruct((t, d), F32), jax.ShapeDtypeStruct((t, d), BF16)],
        compiler_params=_params("parallel"),
        name="mix",
    )(o_mla, o_gla, zp, zp, zp, x, w_o, g)


def _ffn_kernel(h_ref, y_ref, wu_ref, wd_ref, *refs):
    o_ref = refs[-1]
    f = pl.program_id(1)

    @pl.when(f == 0)
    def _():
        o_ref[...] = y_ref[...]

    u = jnp.maximum(_dot(h_ref[...], wu_ref[...]), 0.0)
    o_ref[...] += _dot((u * u).astype(BF16), wd_ref[...])


def _ffn(h, y, w_up, w_down, tm, tf, done=None):
    t, d = y.shape
    dff = w_up.shape[1]
    skip = 0 if done is None else 1
    in_specs = [pl.BlockSpec((tm, d), lambda i, f: (i + skip, 0)), pl.BlockSpec((tm, d), lambda i, f: (i + skip, 0)),
                pl.BlockSpec((d, tf), lambda i, f: (0, f)), pl.BlockSpec((tf, d), lambda i, f: (f, 0))]
    args = [h, y, w_up, w_down]
    if done is not None:
        in_specs.append(pl.BlockSpec(memory_space=pl.ANY))
        args.append(done)
    return pl.pallas_call(
        _ffn_kernel,
        grid=(t // tm - skip, dff // tf),
        in_specs=in_specs,
        out_specs=pl.BlockSpec((tm, d), lambda i, f: (i + skip, 0)),
        out_shape=jax.ShapeDtypeStruct((t, d), F32),
        input_output_aliases={} if done is None else {4: 0},
        compiler_params=_params("parallel", "arbitrary"),
        name="ffn",
    )(*args)


def _ffn_first_kernel(h_ref, y_ref, wu_ref, wd_ref, o_ref, wub_ref, wdb_ref):
    f = pl.program_id(0)

    @pl.when(f == 0)
    def _():
        o_ref[...] = y_ref[...]

    wu = wu_ref[...].astype(BF16)
    wd = wd_ref[...].astype(BF16)
    wub_ref[...] = wu
    wdb_ref[...] = wd
    u = jnp.maximum(_dot(h_ref[...], wu), 0.0)
    o_ref[...] += _dot((u * u).astype(BF16), wd)


def _ffn_first(h, y, w_up, w_down, tm, tf):
    t, d = y.shape
    dff = w_up.shape[1]
    return pl.pallas_call(
        _ffn_first_kernel,
        grid=(dff // tf,),
        in_specs=[pl.BlockSpec((tm, d), lambda f: (0, 0)), pl.BlockSpec((tm, d), lambda f: (0, 0)),
                  pl.BlockSpec((d, tf), lambda f: (0, f)), pl.BlockSpec((tf, d), lambda f: (f, 0))],
        out_specs=[pl.BlockSpec((tm, d), lambda f: (0, 0)),
                   pl.BlockSpec((d, tf), lambda f: (0, f)), pl.BlockSpec((tf, d), lambda f: (f, 0))],
        out_shape=[jax.ShapeDtypeStruct((t, d), F32), jax.ShapeDtypeStruct((d, dff), BF16),
                   jax.ShapeDtypeStruct((dff, d), BF16)],
        compiler_params=_params("arbitrary"),
        name="ffn_first",
    )(h, y, w_up, w_down)


def _rope_table(pos):
    half = MLA_ROPE // 2
    freqs = jnp.power(ROPE_THETA, -jnp.arange(half, dtype=F32) / half)
    ang = pos[:, None] * freqs[None, :]
    cos, sin = jnp.cos(ang), jnp.sin(ang)
    return jnp.concatenate([cos, cos, -sin, sin], axis=1)


def _swap_halves(w):
    half = w.shape[-1] // 2
    return jnp.concatenate([w[..., half:], w[..., :half]], axis=-1)


def _pick(t, pref):
    while t % pref:
        pref //= 2
    return pref


def kernel(x_prompt, x_sample, cache_mla_ckv, cache_mla_krope, state_gla, norm_mix_g, w_in, mla_q_norm_g,
           mla_w_uq, mla_kv_norm_g, mla_w_ukv, mla_q_gain_nope, mla_q_gain_rope, mla_k_gain_nope,
           mla_k_gain_rope, gla_w_a2, gla_b_a, gla_norm_g, w_o, norm_ffn_g, ffn_w_up, ffn_w_down):
    depth = w_in.shape[0]
    bp, seq, d = x_prompt.shape
    nb, dec_seq, _ = x_sample.shape
    past = cache_mla_ckv.shape[2]
    q_rank, heads = mla_w_uq.shape[1], mla_w_uq.shape[2]
    kv_rank = mla_w_ukv.shape[1]
    gla_heads, dk, dv = state_gla.shape[2], state_gla.shape[3], state_gla.shape[4]
    gate_rank = gla_w_a2.shape[1]
    gk, gv = gla_heads * dk, gla_heads * dv
    assert bp == 1 and heads * MLA_V == d and gv == d and seq % CHUNK == 0

    cs_p = _rope_table(jnp.arange(seq, dtype=F32))
    cs_s = jnp.tile(_rope_table(past + jnp.arange(dec_seq, dtype=F32)), (nb, 1))

    xp = x_prompt.reshape(seq, d)
    xs = x_sample.reshape(nb * dec_seq, d)
    outs = [[] for _ in range(6)]
    for l in range(depth):
        wi = w_in[l].astype(BF16)
        pts = np.cumsum([q_rank, kv_rank, MLA_ROPE, gk, gk, gv, gate_rank, gv, d]).tolist()
        w_qkv_lat = wi[:, :pts[1]]
        w_kr = wi[:, pts[1]:pts[2]]
        w_alr = wi[:, pts[5]:pts[6]]
        assert 3 * gate_rank <= LANE
        w_lat = jnp.concatenate(
            [w_qkv_lat, w_kr, _swap_halves(w_kr), w_alr, w_alr, w_alr,
             jnp.zeros((d, LANE - 3 * gate_rank), BF16)], axis=1)
        wa2_hi = gla_w_a2[l].astype(BF16)
        wa2_lo = (gla_w_a2[l] - wa2_hi.astype(F32)).astype(BF16)
        wa2_split = jnp.concatenate(
            [wa2_hi, wa2_lo, wa2_hi, jnp.zeros((LANE - 3 * gate_rank, gk), BF16)], axis=0)

        wq = mla_w_uq[l]
        wq_r = wq[..., MLA_NOPE:]
        w_uq = jnp.concatenate([wq[..., :MLA_NOPE], wq_r, _swap_halves(wq_r)], axis=-1)
        w_uq = w_uq.reshape(q_rank, heads * MLA_HEAD_PAD).astype(BF16)
        w_uq_t = w_uq.T
        wuk = mla_w_ukv[l][..., :MLA_NOPE].reshape(kv_rank, heads * MLA_NOPE).astype(BF16)
        wuk_t = wuk.T
        wuv = mla_w_ukv[l][..., MLA_NOPE:].reshape(kv_rank, heads * MLA_V).astype(BF16)
        wuv_t = wuv.T
        gain = jnp.concatenate([mla_q_gain_nope[l] * mla_k_gain_nope[l],
                                jnp.tile(mla_q_gain_rope[l] * mla_k_gain_rope[l], 2),
                                jnp.zeros((MLA_HEAD_PAD - MLA_QK_DIM,), F32)])[None, :] * MLA_SCALE
        gain_col = gain.T * LOG2E
        wo_b = w_o[l].astype(BF16)
        g_mix = norm_mix_g[l][None, :]
        g_q = mla_q_norm_g[l][None, :]
        g_kv = mla_kv_norm_g[l][None, :]
        g_gla = gla_norm_g[l][None, :]
        g_ffn = norm_ffn_g[l][None, :]
        ba = gla_b_a[l][None, :]

        def front(x, cs, tm, proj_w=None):
            h, qlat, ckv, kr, la = _mla_lat(x, g_mix, w_lat, g_q, g_kv, cs, wa2_split, ba, tm, q_rank, kv_rank,
                                            gate_rank)
            pm = _pick(x.shape[0], 1024)
            z_gla = z_gate = None
            if proj_w is None:
                z_gla, w_gla = _proj_first(h, wi, pts[2], pts[5] - pts[2], pm, 1024)
                z_gate, w_gate = _proj_first(h, wi, pts[6], wi.shape[1] - pts[6], pm, 1024, swish_cols=gv)
                proj_w = (w_gla, w_gate)
            if z_gla is None or x.shape[0] > pm:
                z_gla = _proj(h, proj_w[0], pm, 2048, done=z_gla)
                z_gate = _proj(h, proj_w[1], pm, gv, swish_cols=gv, done=z_gate)
            return qlat, ckv, kr, la, z_gla, z_gate, proj_w

        def back(x, o_mla, o_gla, z_gate, tm, ffn_w=None):
            y1, h2 = _mix(o_mla, o_gla, z_gate, 0, x, wo_b, g_ffn, tm)
            if ffn_w is not None:
                return _ffn(h2, y1, *ffn_w, tm, 1024), ffn_w
            y, *ffn_w = _ffn_first(h2, y1, ffn_w_up[l], ffn_w_down[l], tm, 512)
            if x.shape[0] > tm:
                y = _ffn(h2, y1, *ffn_w, tm, 1024, done=y)
            return y, ffn_w

        tile = _pick(seq, 512)
        qlat, ckv, kr, la, z_gla, z_gate, proj_w = front(xp, cs_p, tile)
        qt = _mla_qt(qlat, w_uq_t, cs_p.T, gain_col, tile, 2 * tile, heads)
        kcat, vt = _mla_kv(ckv, kr, wuk, wuv_t, tile, heads)
        o_mla = _flash(qt, kcat, vt, heads)
        o_gla, st = _gla(z_gla, la, g_gla, 1, _pick(seq, 128), gla_heads, dk, dv)
        xp_next, ffn_w = back(xp, o_mla, o_gla, z_gate, tile)
        outs[0].append(ckv.reshape(bp, seq, kv_rank))
        outs[1].append(kr.reshape(bp, seq, MLA_ROPE))
        outs[2].append(st)
        xp = xp_next

        qlat, ckv, kr, la, z_gla, z_gate, _ = front(xs, cs_s, nb * dec_seq, proj_w)
        qcat = _mla_q(qlat, w_uq, cs_s, gain, nb * dec_seq, heads)
        o_lat = _sattn(qcat, cache_mla_ckv[l], cache_mla_krope[l], ckv, kr, wuk_t, heads, dec_seq, _pick(past, 512))
        o_mla = _svup(o_lat, wuv, heads, dec_seq)
        o_gla, st = _gla_step(z_gla, la, g_gla, state_gla[l], dec_seq, gla_heads, dk, dv)
        xs_next, _ = back(xs, o_mla, o_gla, z_gate, nb * dec_seq, ffn_w)
        outs[3].append(ckv.reshape(nb, dec_seq, kv_rank))
        outs[4].append(kr.reshape(nb, dec_seq, MLA_ROPE))
        outs[5].append(st)
        xs = xs_next

    return (xp.reshape(bp, seq, d), xs.reshape(nb, dec_seq, d),
            jnp.stack(outs[0]), jnp.stack(outs[1]), jnp.stack(outs[2]),
            jnp.stack(outs[3]), jnp.stack(outs[4]), jnp.stack(outs[5]))
```

```python
import functools

import jax
import jax.numpy as jnp
import numpy as np
from jax import lax
from jax.experimental import pallas as pl
from jax.experimental.pallas import tpu as pltpu

F32 = jnp.float32
BF16 = jnp.bfloat16

EPS = 1e-6
CHUNK = 64
MLA_NOPE = 128
MLA_ROPE = 64
MLA_V = 128
MLA_QK_DIM = MLA_NOPE + MLA_ROPE
MLA_SCALE = MLA_QK_DIM ** -0.5
MLA_HEAD_PAD = 256
V_ROWS = MLA_V + 16
LOG2E = 1.4426950408889634
QBLK = 256
ROPE_THETA = 10000.0
GLA_TAU = 16.0
GLA_LEAF = 32
LANE = 128
V7X_VMEM_BYTES = 64 * 1024 * 1024
VMEM_LIMIT = V7X_VMEM_BYTES * 7 // 8

NT_DIMS = (((1,), (1,)), ((), ()))
TN_DIMS = (((0,), (0,)), ((), ()))


def _params(*sem):
    return pltpu.CompilerParams(dimension_semantics=sem, vmem_limit_bytes=VMEM_LIMIT)


def _dot(a, b):
    return jnp.dot(a, b, preferred_element_type=F32)


def _dot_nt(a, b):
    return lax.dot_general(a, b, NT_DIMS, preferred_element_type=F32)


def _rms_inv(x, n):
    return lax.rsqrt(jnp.sum(x * x, axis=-1, keepdims=True) / n + EPS)


def _mla_lat_kernel(x_ref, g_ref, w_ref, qg_ref, kvg_ref, cs_ref, wa2_ref, ba_ref,
                    h_ref, qlat_ref, ckv_ref, kr_ref, la_ref, *, q_rank, kv_rank, gate_rank):
    x = x_ref[...]
    h = (x * _rms_inv(x, x.shape[-1]) * g_ref[...]).astype(h_ref.dtype)
    h_ref[...] = h
    z = _dot(h, w_ref[...])
    q_lat = z[:, :q_rank]
    qlat_ref[...] = (q_lat * _rms_inv(q_lat, q_rank) * qg_ref[...]).astype(qlat_ref.dtype)
    kv_lat = z[:, q_rank:q_rank + kv_rank]
    ckv_ref[...] = kv_lat * _rms_inv(kv_lat, kv_rank) * kvg_ref[...]
    o = q_rank + kv_rank
    rr = z[:, o:o + LANE] * cs_ref[...]
    kr_ref[...] = rr[:, :MLA_ROPE] + rr[:, MLA_ROPE:]
    a3 = z[:, o + LANE:o + 2 * LANE]
    a_hi = a3.astype(BF16)
    a_lo = (a3 - a_hi.astype(F32)).astype(BF16)
    lane = lax.broadcasted_iota(jnp.int32, a3.shape, 1)
    u = _dot(jnp.where(lane < 2 * gate_rank, a_hi, a_lo), wa2_ref[...]) + ba_ref[...]
    log_sig = jnp.minimum(u, 0.0) - jnp.log1p(jnp.exp(-jnp.abs(u)))
    la_ref[...] = log_sig / GLA_TAU


def _mla_lat(x, g, w_lat, qg, kvg, cs, wa2_split, ba, tm, q_rank, kv_rank, gate_rank):
    t, d = x.shape
    n = w_lat.shape[1]
    gk = wa2_split.shape[1]
    row = lambda i: (i, 0)
    fix = lambda i: (0, 0)
    return pl.pallas_call(
        functools.partial(_mla_lat_kernel, q_rank=q_rank, kv_rank=kv_rank, gate_rank=gate_rank),
        grid=(t // tm,),
        in_specs=[pl.BlockSpec((tm, d), row), pl.BlockSpec((1, d), fix), pl.BlockSpec((d, n), fix),
                  pl.BlockSpec((1, q_rank), fix), pl.BlockSpec((1, kv_rank), fix),
                  pl.BlockSpec((tm, LANE), row), pl.BlockSpec((LANE, gk), fix), pl.BlockSpec((1, gk), fix)],
        out_specs=[pl.BlockSpec((tm, d), row), pl.BlockSpec((tm, q_rank), row), pl.BlockSpec((tm, kv_rank), row),
                   pl.BlockSpec((tm, MLA_ROPE), row), pl.BlockSpec((tm, gk), row)],
        out_shape=[jax.ShapeDtypeStruct((t, d), BF16),
                   jax.ShapeDtypeStruct((t, q_rank), BF16), jax.ShapeDtypeStruct((t, kv_rank), F32),
                   jax.ShapeDtypeStruct((t, MLA_ROPE), F32), jax.ShapeDtypeStruct((t, gk), F32)],
        compiler_params=_params("parallel"),
        name="mla_lat",
    )(x, g, w_lat, qg, kvg, cs, wa2_split, ba)


def _gate_act(z, swish):
    return jax.nn.sigmoid(z) * jnp.where(swish, z, 1.0)


def _proj_kernel(h_ref, w_ref, *refs, swish_tiles):
    o_ref = refs[-1]
    z = _dot(h_ref[...], w_ref[...])
    if swish_tiles is not None:
        z = _gate_act(z, pl.program_id(1) < swish_tiles)
    o_ref[...] = z.astype(o_ref.dtype)


def _proj(h, w, tm, tn, swish_cols=None, done=None):
    t, d = h.shape
    n = w.shape[1]
    skip = 0 if done is None else 1
    in_specs = [pl.BlockSpec((tm, d), lambda i, j: (i + skip, 0)), pl.BlockSpec((d, tn), lambda i, j: (0, j))]
    args = [h, w]
    if done is not None:
        in_specs.append(pl.BlockSpec(memory_space=pl.ANY))
        args.append(done)
    assert swish_cols is None or swish_cols % tn == 0
    return pl.pallas_call(
        functools.partial(_proj_kernel, swish_tiles=None if swish_cols is None else swish_cols // tn),
        grid=(t // tm - skip, n // tn),
        in_specs=in_specs,
        out_specs=pl.BlockSpec((tm, tn), lambda i, j: (i + skip, j)),
        out_shape=jax.ShapeDtypeStruct((t, n), BF16),
        input_output_aliases={} if done is None else {2: 0},
        compiler_params=_params("parallel", "parallel"),
        name="proj",
    )(*args)


def _proj_first_kernel(h_ref, a_ref, b_ref, o_ref, w_ref, *, shift, n_cols, next_col0, swish_tiles):
    j = pl.program_id(0)
    tn = a_ref.shape[1]
    b = b_ref[...]
    col = next_col0 + j * tn + lax.broadcasted_iota(jnp.int32, b.shape, 1)
    src = jnp.concatenate([a_ref[...], jnp.where(col < n_cols, b, jnp.zeros_like(b))], axis=1).astype(F32)
    w = pltpu.roll(src, src.shape[1] - shift, 1)[:, :tn].astype(BF16)
    w_ref[...] = w
    z = _dot(h_ref[...], w)
    if swish_tiles is not None:
        z = _gate_act(z, j < swish_tiles)
    o_ref[...] = z.astype(o_ref.dtype)


def _proj_first(h, w_all, col0, width, tm, tn, swish_cols=None):
    t, d = h.shape
    n_cols = w_all.shape[1]
    base = col0 // LANE * LANE
    shift = col0 - base
    assert shift > 0 and base % tn == 0 and width % tn == 0 and tn % LANE == 0
    assert swish_cols is None or swish_cols % tn == 0
    return pl.pallas_call(
        functools.partial(_proj_first_kernel, shift=shift, n_cols=n_cols, next_col0=base + tn,
                          swish_tiles=None if swish_cols is None else swish_cols // tn),
        grid=(width // tn,),
        in_specs=[pl.BlockSpec((tm, d), lambda j: (0, 0)),
                  pl.BlockSpec((d, tn), lambda j: (0, base // tn + j)),
                  pl.BlockSpec((d, LANE), lambda j: (0, (base + (j + 1) * tn) // LANE))],
        out_specs=[pl.BlockSpec((tm, tn), lambda j: (0, j)), pl.BlockSpec((d, tn), lambda j: (0, j))],
        out_shape=[jax.ShapeDtypeStruct((t, width), BF16), jax.ShapeDtypeStruct((d, width), BF16)],
        compiler_params=_params("parallel"),
        name="proj_first",
    )(h, w_all, w_all)


def _mla_q_kernel(ql_ref, w_ref, cs_ref, gain_ref, q_ref, *, heads):
    ql = ql_ref[...]
    cs = cs_ref[...]
    gain = gain_ref[...]
    lane = lax.broadcasted_iota(jnp.int32, (1, LANE), 1)
    for hd in range(heads):
        c0 = hd * MLA_HEAD_PAD
        z = _dot(ql, w_ref[:, c0:c0 + MLA_HEAD_PAD])
        nope = z[:, :MLA_NOPE]
        rr = z[:, MLA_NOPE:] * cs
        rot = rr + pltpu.roll(rr, MLA_ROPE, 1)
        ss = (jnp.sum(nope * nope, axis=-1, keepdims=True)
              + jnp.sum(jnp.where(lane < MLA_ROPE, rot * rot, 0.0), axis=-1, keepdims=True))
        inv = lax.rsqrt(ss / MLA_QK_DIM + EPS)
        q_ref[:, c0:c0 + MLA_NOPE] = (nope * inv * gain[:, :MLA_NOPE]).astype(q_ref.dtype)
        q_ref[:, c0 + MLA_NOPE:c0 + MLA_HEAD_PAD] = (rot * inv * gain[:, MLA_NOPE:]).astype(q_ref.dtype)


def _mla_q(qlat, w_uq, cs, gain, tm, heads):
    t, r = qlat.shape
    n = heads * MLA_HEAD_PAD
    return pl.pallas_call(
        functools.partial(_mla_q_kernel, heads=heads),
        grid=(t // tm,),
        in_specs=[pl.BlockSpec((tm, r), lambda i: (i, 0)), pl.BlockSpec((r, n), lambda i: (0, 0)),
                  pl.BlockSpec((tm, LANE), lambda i: (i, 0)), pl.BlockSpec((1, MLA_HEAD_PAD), lambda i: (0, 0))],
        out_specs=pl.BlockSpec((tm, n), lambda i: (i, 0)),
        out_shape=jax.ShapeDtypeStruct((t, n), BF16),
        compiler_params=_params("parallel"),
        name="mla_q",
    )(qlat, w_uq, cs, gain)


def _mla_qt_kernel(ql_ref, w_ref, cs_ref, gain_ref, q_ref, *, heads):
    ql = ql_ref[...]
    cs = cs_ref[...]
    gain = gain_ref[...]
    for hd in range(heads):
        r0 = hd * MLA_HEAD_PAD
        z = _dot_nt(w_ref[r0:r0 + MLA_HEAD_PAD, :], ql)
        nope = z[:MLA_NOPE]
        rr = z[MLA_NOPE:] * cs
        rot = rr[:MLA_ROPE] + rr[MLA_ROPE:]
        ss = jnp.sum(nope * nope, axis=0, keepdims=True) + jnp.sum(rot * rot, axis=0, keepdims=True)
        inv = lax.rsqrt(ss / MLA_QK_DIM + EPS)
        qn = (nope * inv * gain[:MLA_NOPE]).astype(q_ref.dtype)
        qr = (rot * inv * gain[MLA_NOPE:MLA_QK_DIM]).astype(q_ref.dtype)
        for bb in range(q_ref.shape[1]):
            q_ref[hd, bb, :MLA_NOPE, :] = qn[:, bb * QBLK:(bb + 1) * QBLK]
            q_ref[hd, bb, MLA_NOPE:MLA_QK_DIM, :] = qr[:, bb * QBLK:(bb + 1) * QBLK]
            q_ref[hd, bb, MLA_QK_DIM:, :] = jnp.zeros((MLA_HEAD_PAD - MLA_QK_DIM, QBLK), q_ref.dtype)


def _mla_qt(qlat, w_uq_t, cs_t, gain_col, tm, tq, heads):
    t, r = qlat.shape
    n = heads * MLA_HEAD_PAD
    per = tq // tm
    assert tm % QBLK == 0 and tq % tm == 0
    return pl.pallas_call(
        functools.partial(_mla_qt_kernel, heads=heads),
        grid=(t // tm,),
        in_specs=[pl.BlockSpec((tm, r), lambda i: (i, 0)), pl.BlockSpec((n, r), lambda i: (0, 0)),
                  pl.BlockSpec((LANE, tm), lambda i: (0, i)), pl.BlockSpec((MLA_HEAD_PAD, 1), lambda i: (0, 0))],
        out_specs=pl.BlockSpec((heads, None, tm // QBLK, MLA_HEAD_PAD, QBLK),
                               lambda i: (0, i // per, i % per, 0, 0)),
        out_shape=jax.ShapeDtypeStruct((heads, t // tq, tq // QBLK, MLA_HEAD_PAD, QBLK), BF16),
        compiler_params=_params("parallel"),
        name="mla_qt",
    )(qlat, w_uq_t, cs_t, gain_col)


def _mla_kv_kernel(ckv_ref, kr_ref, wk_ref, wvt_ref, k_ref, vt_ref, *, heads):
    c = ckv_ref[...].astype(BF16)
    kr = kr_ref[...]
    kr2 = jnp.sum(kr * kr, axis=-1, keepdims=True)
    kr_pad = jnp.concatenate([kr, jnp.zeros_like(kr)], axis=1)
    for pair in range(heads // 2):
        z = _dot(c, wk_ref[:, pair * 2 * MLA_NOPE:(pair + 1) * 2 * MLA_NOPE])
        for sub in range(2):
            kn = z[:, sub * MLA_NOPE:(sub + 1) * MLA_NOPE]
            inv = lax.rsqrt((jnp.sum(kn * kn, axis=-1, keepdims=True) + kr2) / MLA_QK_DIM + EPS)
            c0 = (2 * pair + sub) * MLA_HEAD_PAD
            k_ref[:, c0:c0 + MLA_NOPE] = (kn * inv).astype(k_ref.dtype)
            k_ref[:, c0 + MLA_NOPE:c0 + MLA_HEAD_PAD] = (kr_pad * inv).astype(k_ref.dtype)
    vt = _dot_nt(wvt_ref[...], c)
    for hd in range(heads):
        vt_ref[hd, :MLA_V, :] = vt[hd * MLA_V:(hd + 1) * MLA_V].astype(vt_ref.dtype)
        vt_ref[hd, MLA_V:, :] = jnp.ones((V_ROWS - MLA_V, vt.shape[1]), vt_ref.dtype)


def _mla_kv(ckv, kr, w_uk, w_uv_t, tm, heads):
    t, r = ckv.shape
    return pl.pallas_call(
        functools.partial(_mla_kv_kernel, heads=heads),
        grid=(t // tm,),
        in_specs=[pl.BlockSpec((tm, r), lambda i: (i, 0)), pl.BlockSpec((tm, MLA_ROPE), lambda i: (i, 0)),
                  pl.BlockSpec((r, heads * MLA_NOPE), lambda i: (0, 0)),
                  pl.BlockSpec((heads * MLA_V, r), lambda i: (0, 0))],
        out_specs=[pl.BlockSpec((tm, heads * MLA_HEAD_PAD), lambda i: (i, 0)),
                   pl.BlockSpec((heads, None, V_ROWS, tm), lambda i: (0, i, 0, 0))],
        out_shape=[jax.ShapeDtypeStruct((t, heads * MLA_HEAD_PAD), BF16),
                   jax.ShapeDtypeStruct((heads, t // tm, V_ROWS, tm), BF16)],
        compiler_params=_params("parallel"),
        name="mla_kv",
    )(ckv, kr, w_uk, w_uv_t)


def _diag_mode(u, c, tk):
    k_lo, k_hi = (u * tk) // CHUNK, ((u + 1) * tk - 1) // CHUNK
    q_lo, q_hi = (c * QBLK) // CHUNK, ((c + 1) * QBLK - 1) // CHUNK
    return "all" if k_hi <= q_lo else "none" if k_lo > q_hi else "some"


def _diag_bias(tk, n_blk):
    index, blocks = {}, []
    for u in range(2):
        for c in range(n_blk):
            if _diag_mode(u, c, tk) == "some":
                k_chunk = (u * tk + np.arange(tk)[:, None]) // CHUNK
                q_chunk = (c * QBLK + np.arange(QBLK)[None, :]) // CHUNK
                index[(u, c)] = len(blocks)
                blocks.append(np.where(k_chunk <= q_chunk, 0.0, -np.inf).astype(np.float32))
    return index, np.stack(blocks)


def _flash_kernel(qt_ref, k_ref, vt_ref, bias_ref, o_ref, m_ref, acc_ref, s_ref, mx_ref, p_ref, al_ref,
                  *, tq, tk, bias_index):
    n_q, n_blk = qt_ref.shape[0], qt_ref.shape[1]
    acc_ref[...] = jnp.zeros_like(acc_ref)


    def diag_mode(u, c):
        return _diag_mode(u, c, tk)

    def score(i, t, slot, c, mode):
        if mode == "none":
            return
        k = k_ref[pl.ds(pl.multiple_of(t * tk, tk), tk), :]
        s = _dot(k, qt_ref[i, c])
        if mode == "some":
            s = s + bias_ref[bias_index[(slot, c)]]
        s_ref[slot, c] = s
        mx_ref[slot, c] = jnp.max(s, axis=0, keepdims=True)

    def soften(slot, c, mode="all", first=False):
        if mode == "none":
            return
        if first:
            m_new = mx_ref[slot, c]
            al_ref[slot, c] = jnp.zeros_like(m_new)
        else:
            m_prev = m_ref[c]
            m_new = jnp.maximum(m_prev, mx_ref[slot, c])
            al_ref[slot, c] = jnp.exp2(m_prev - m_new)
        p_ref[slot, c] = jnp.exp2(s_ref[slot, c] - m_new).astype(BF16)
        m_ref[c] = m_new

    def gather(t, slot, c, mode="all"):
        if mode == "none":
            return
        acc_ref[c] = al_ref[slot, c] * acc_ref[c] + _dot(vt_ref[t], p_ref[slot, c])

    def pair(i, g, diag):
        for c in range(n_blk):
            score(i, g, 0, c, diag_mode(0, c) if diag else "all")
            soften(1, c)
            gather(g - 2, 0, c)
        for c in range(n_blk):
            score(i, g + 1, 1, c, diag_mode(1, c) if diag else "all")
            soften(0, c, diag_mode(0, c) if diag else "all")
            gather(g - 1, 1, c)

    def head(i, diag):
        for c in range(n_blk):
            score(i, 0, 0, c, diag_mode(0, c) if diag else "all")
        for c in range(n_blk):
            score(i, 1, 1, c, diag_mode(1, c) if diag else "all")
            soften(0, c, diag_mode(0, c) if diag else "all", first=True)

    def tail(i):
        for c in range(n_blk):
            soften(1, c, diag_mode(1, c))
            gather(2 * i, 0, c, diag_mode(0, c))
        for c in range(n_blk):
            gather(2 * i + 1, 1, c, diag_mode(1, c))
        for c in range(n_blk):
            acc = acc_ref[c]
            rows = pl.ds(pl.multiple_of(i * tq + c * QBLK, QBLK), QBLK)
            o_ref[rows, :] = (acc[:MLA_V] / acc[MLA_V:MLA_V + 1]).T.astype(o_ref.dtype)

    def middle(i):
        def body(j, carry):
            pair(i, 2 * j, False)
            return carry

        lax.fori_loop(1, i, body, 0)
        pair(i, 2 * i, True)

    head(0, True)
    if n_q > 1:
        tail(0)
        head(1, False)

        def outer(i, carry):
            middle(i)
            tail(i)
            head(i + 1, False)
            return carry

        lax.fori_loop(1, n_q - 1, outer, 0)
        middle(n_q - 1)
    tail(n_q - 1)


def _flash(qt, k, vt, heads):
    t = k.shape[0]
    n_q, n_blk = qt.shape[1], qt.shape[2]
    tq = n_blk * QBLK
    n_kt, tk = vt.shape[1], vt.shape[3]
    assert tq == 2 * tk and tk % CHUNK == 0 and qt.shape[4] == QBLK and n_q * tq == t
    bias_index, bias = _diag_bias(tk, n_blk)
    return pl.pallas_call(
        functools.partial(_flash_kernel, tq=tq, tk=tk, bias_index=bias_index),
        grid=(heads,),
        in_specs=[pl.BlockSpec((None, n_q, n_blk, MLA_HEAD_PAD, QBLK), lambda hd: (hd, 0, 0, 0, 0)),
                  pl.BlockSpec((t, MLA_HEAD_PAD), lambda hd: (0, hd)),
                  pl.BlockSpec((None, n_kt, V_ROWS, tk), lambda hd: (hd, 0, 0, 0)),
                  pl.BlockSpec(bias.shape, lambda hd: (0, 0, 0))],
        out_specs=pl.BlockSpec((t, MLA_V), lambda hd: (0, hd)),
        out_shape=jax.ShapeDtypeStruct((t, heads * MLA_V), BF16),
        scratch_shapes=[pltpu.VMEM((n_blk, 1, QBLK), F32), pltpu.VMEM((n_blk, V_ROWS, QBLK), F32),
                        pltpu.VMEM((2, n_blk, tk, QBLK), F32), pltpu.VMEM((2, n_blk, 1, QBLK), F32),
                        pltpu.VMEM((2, n_blk, tk, QBLK), BF16), pltpu.VMEM((2, n_blk, 1, QBLK), F32)],
        compiler_params=_params("parallel"),
        name="flash",
    )(qt, k, vt, jnp.asarray(bias))


def _sattn_kernel(q_ref, cc_ref, ck_ref, nc_ref, nk_ref, wuk_ref, o_ref,
                  qabs_ref, qr_ref, m_ref, l_ref, acc_ref, *, heads, nq, tk, n_new):
    past, rank = cc_ref.shape
    n_tiles = past // tk
    for hd in range(heads):
        c0 = hd * MLA_HEAD_PAD
        qn = q_ref[:, c0:c0 + MLA_NOPE]
        qabs_ref[hd * nq:(hd + 1) * nq, :] = _dot(
            qn, wuk_ref[hd * MLA_NOPE:(hd + 1) * MLA_NOPE, :]).astype(qabs_ref.dtype)
        qr_ref[hd * nq:(hd + 1) * nq, :] = q_ref[:, c0 + MLA_NOPE:c0 + MLA_NOPE + MLA_ROPE]
    m_ref[...] = jnp.full_like(m_ref, -jnp.inf)
    l_ref[...] = jnp.zeros_like(l_ref)
    acc_ref[...] = jnp.zeros_like(acc_ref)

    def tile(t):
        if t < n_tiles:
            return cc_ref[t * tk:(t + 1) * tk, :].astype(BF16), ck_ref[t * tk:(t + 1) * tk, :]
        pad = LANE - n_new
        c_new = jnp.concatenate([nc_ref[...], jnp.zeros((pad, rank), F32)], axis=0)
        k_new = jnp.concatenate([nk_ref[...], jnp.zeros((pad, MLA_ROPE), F32)], axis=0)
        return c_new.astype(BF16), k_new

    def key_norm(c, kr):
        n_keys = c.shape[0]
        kn_t = _dot_nt(wuk_ref[...], c)
        ss_t = jnp.sum((kn_t * kn_t).reshape(heads, MLA_NOPE, n_keys), axis=1)
        kr2_t = lax.dot_general(jnp.ones((8, MLA_ROPE), F32), kr * kr, NT_DIMS,
                                preferred_element_type=F32, precision=lax.Precision.HIGHEST)[0:1]
        return lax.rsqrt((ss_t + kr2_t) / MLA_QK_DIM + EPS)

    def attend(c, kr, inv_t, n_valid):
        n_keys = c.shape[0]
        s = _dot_nt(qabs_ref[...], c) + _dot_nt(qr_ref[...], kr.astype(BF16))
        s = s * jnp.broadcast_to(inv_t[:, None, :], (heads, nq, n_keys)).reshape(heads * nq, n_keys)
        if n_valid < n_keys:
            s = jnp.where(lax.broadcasted_iota(jnp.int32, s.shape, 1) < n_valid, s, -jnp.inf)
        m_prev = m_ref[...]
        m_new = jnp.maximum(m_prev, jnp.max(s, axis=-1, keepdims=True))
        alpha = jnp.exp(m_prev - m_new)
        p = jnp.exp(s - m_new)
        l_ref[...] = alpha * l_ref[...] + jnp.sum(p, axis=-1, keepdims=True)
        acc_ref[...] = alpha * acc_ref[...] + _dot(p.astype(BF16), c)
        m_ref[...] = m_new

    c, kr = tile(0)
    inv_t = key_norm(c, kr)
    for t in range(n_tiles + 1):
        if t < n_tiles:
            c_next, kr_next = tile(t + 1)
            inv_next = key_norm(c_next, kr_next)
        attend(c, kr, inv_t, tk if t < n_tiles else n_new)
        if t < n_tiles:
            c, kr, inv_t = c_next, kr_next, inv_next
    o_ref[...] = (acc_ref[...] / l_ref[...]).astype(o_ref.dtype)


def _sattn(q, cache_c, cache_k, new_c, new_k, wuk_t, heads, nq, tk):
    nb, past, rank = cache_c.shape
    return pl.pallas_call(
        functools.partial(_sattn_kernel, heads=heads, nq=nq, tk=tk, n_new=nq),
        grid=(nb,),
        in_specs=[pl.BlockSpec((nq, heads * MLA_HEAD_PAD), lambda b: (b, 0)),
                  pl.BlockSpec((None, past, rank), lambda b: (b, 0, 0)),
                  pl.BlockSpec((None, past, MLA_ROPE), lambda b: (b, 0, 0)),
                  pl.BlockSpec((nq, rank), lambda b: (b, 0)),
                  pl.BlockSpec((nq, MLA_ROPE), lambda b: (b, 0)),
                  pl.BlockSpec(wuk_t.shape, lambda b: (0, 0))],
        out_specs=pl.BlockSpec((None, heads * nq, rank), lambda b: (b, 0, 0)),
        out_shape=jax.ShapeDtypeStruct((nb, heads * nq, rank), BF16),
        scratch_shapes=[pltpu.VMEM((heads * nq, rank), BF16), pltpu.VMEM((heads * nq, MLA_ROPE), BF16),
                        pltpu.VMEM((heads * nq, 1), F32), pltpu.VMEM((heads * nq, 1), F32),
                        pltpu.VMEM((heads * nq, rank), F32)],
        compiler_params=_params("parallel"),
        name="sattn",
    )(q, cache_c, cache_k, new_c, new_k, wuk_t)


def _svup_kernel(ol_ref, w_ref, o_ref, *, nq):
    nb, rows, rank = ol_ref.shape
    for hh in range(rows // nq):
        x = ol_ref[:, hh * nq:(hh + 1) * nq, :].reshape(nb * nq, rank)
        o_ref[:, hh * MLA_V:(hh + 1) * MLA_V] = _dot(x, w_ref[:, hh * MLA_V:(hh + 1) * MLA_V]).astype(o_ref.dtype)


def _svup(o_lat, wuv, heads, nq):
    nb, _, rank = o_lat.shape
    hs = 4 if heads % 4 == 0 else 1
    return pl.pallas_call(
        functools.partial(_svup_kernel, nq=nq),
        grid=(heads // hs,),
        in_specs=[pl.BlockSpec((nb, hs * nq, rank), lambda g: (0, g, 0)),
                  pl.BlockSpec((rank, hs * MLA_V), lambda g: (0, g))],
        out_specs=pl.BlockSpec((nb * nq, hs * MLA_V), lambda g: (0, g)),
        out_shape=jax.ShapeDtypeStruct((nb * nq, heads * MLA_V), BF16),
        compiler_params=_params("parallel"),
        name="svup",
    )(o_lat, wuv)


def _gla_levels(c):
    leaf = min(c, GLA_LEAF)
    levels = [(leaf, leaf // 2 - 1)]
    g = 2 * leaf
    while g <= c:
        levels.append((g, g // 2 - 1))
        g *= 2
    return levels


def _gla_kernel(q_ref, k_ref, v_ref, la_ref, g_ref, o_ref, sout_ref, st_ref, *, heads, dk, dv, c):
    j = pl.program_id(1)
    nj = pl.num_programs(1)

    @pl.when(j == 0)
    def _():
        st_ref[...] = jnp.zeros_like(st_ref)

    levels = _gla_levels(c)
    row = lax.broadcasted_iota(jnp.int32, (c, c), 0)
    col = lax.broadcasted_iota(jnp.int32, (c, c), 1)
    masks = []
    for lv, (g, _) in enumerate(levels):
        same = (row // g) == (col // g)
        if lv == 0:
            masks.append(jnp.logical_and(same, col <= row))
        else:
            half = g // 2
            masks.append(jnp.logical_and(same, jnp.logical_and((row // half) % 2 == 1, (col // half) % 2 == 0)))
    rid = lax.broadcasted_iota(jnp.int32, (c, dk), 0)
    gain = g_ref[...]

    for hd in range(heads):
        la = la_ref[:, hd * dk:(hd + 1) * dk]
        b = la
        sh = 1
        while sh < c:
            b = b + jnp.where(rid >= sh, pltpu.roll(b, sh, 0), 0.0)
            sh *= 2
        b_end = b[c - 1:c, :]
        q = q_ref[:, hd * dk:(hd + 1) * dk].astype(F32) * (dk ** -0.5)
        k = k_ref[:, hd * dk:(hd + 1) * dk].astype(F32)
        v = v_ref[:, hd * dv:(hd + 1) * dv]

        a = jnp.zeros((c, c), F32)
        for lv, (g, r) in enumerate(levels):
            ref_rows = jnp.broadcast_to(b.reshape(c // g, g, dk)[:, r:r + 1, :], (c // g, g, dk)).reshape(c, dk)
            d = b - ref_rows
            if lv == 0:
                fq, fk = jnp.exp(d), jnp.exp(-d)
            else:
                fq = fk = jnp.exp(-jnp.abs(d))
            a_lv = _dot_nt((q * fq).astype(BF16), (k * fk).astype(BF16))
            a = jnp.where(masks[lv], a_lv, a)

        st = st_ref[hd]
        o = _dot(a.astype(BF16), v) + _dot_nt((q * jnp.exp(b)).astype(BF16), st.astype(BF16))
        k_end = (k * jnp.exp(b_end - b)).astype(BF16)
        st_ref[hd] = st * jnp.exp(b_end) + lax.dot_general(v, k_end, TN_DIMS, preferred_element_type=F32)
        o_ref[:, hd * dv:(hd + 1) * dv] = (o * _rms_inv(o, dv) * gain).astype(o_ref.dtype)

    @pl.when(j == nj - 1)
    def _():
        for hd in range(heads):
            sout_ref[hd] = st_ref[hd].T


def _gla(zp, la, gain, nb, c, heads, dk, dv):
    t = la.shape[0]
    nj = t // (nb * c)
    gk, gv = heads * dk, heads * dv
    assert gv % gk == 0
    row = lambda bb, j: (bb * nj + j, 0)
    in_specs = [pl.BlockSpec((c, gk), row),
                pl.BlockSpec((c, gk), lambda bb, j: (bb * nj + j, 1)),
                pl.BlockSpec((c, gv), lambda bb, j: (bb * nj + j, 2 * gk // gv)),
                pl.BlockSpec((c, gk), row),
                pl.BlockSpec((1, dv), lambda bb, j: (0, 0))]
    return pl.pallas_call(
        functools.partial(_gla_kernel, heads=heads, dk=dk, dv=dv, c=c),
        grid=(nb, nj),
        in_specs=in_specs,
        out_specs=[pl.BlockSpec((c, gv), row),
                   pl.BlockSpec((None, heads, dk, dv), lambda bb, j: (bb, 0, 0, 0))],
        out_shape=[jax.ShapeDtypeStruct((t, gv), BF16), jax.ShapeDtypeStruct((nb, heads, dk, dv), F32)],
        scratch_shapes=[pltpu.VMEM((heads, dv, dk), F32)],
        compiler_params=_params("parallel", "arbitrary"),
        name="gla",
    )(zp, zp, zp, la, gain)


def _gla_step_kernel(q_ref, k_ref, v_ref, la_ref, g_ref, s0_ref, o_ref, sout_ref, *, heads, dk, dv):
    c = la_ref.shape[0]
    row = lax.broadcasted_iota(jnp.int32, (c, c), 0)
    col = lax.broadcasted_iota(jnp.int32, (c, c), 1)
    rid = lax.broadcasted_iota(jnp.int32, (c, dk), 0)
    gain = g_ref[...]
    ones = jnp.ones((c, LANE), F32)
    for hd in range(heads):
        la = la_ref[:, hd * dk:(hd + 1) * dk]
        b = la
        sh = 1
        while sh < c:
            b = b + jnp.where(rid >= sh, pltpu.roll(b, sh, 0), 0.0)
            sh *= 2
        q = q_ref[:, hd * dk:(hd + 1) * dk].astype(F32) * (dk ** -0.5)
        k = k_ref[:, hd * dk:(hd + 1) * dk].astype(F32)
        v = v_ref[:, hd * dv:(hd + 1) * dv]
        d = b - b[c // 2 - 1:c // 2, :]
        a = jnp.where(col <= row, _dot_nt((q * jnp.exp(d)).astype(BF16), (k * jnp.exp(-d)).astype(BF16)), 0.0)
        st = s0_ref[hd]
        o = _dot(a.astype(BF16), v) + _dot((q * jnp.exp(b)).astype(BF16), st.astype(BF16))
        k_end = (k * jnp.exp(b[c - 1:c, :] - b)).astype(BF16)
        decay = jnp.exp(lax.dot_general(la, ones, TN_DIMS, preferred_element_type=F32,
                                        precision=lax.Precision.HIGHEST))
        sout_ref[hd] = (st * jnp.tile(decay, (1, dv // LANE))
                        + lax.dot_general(k_end, v, TN_DIMS, preferred_element_type=F32))
        o_ref[:, hd * dv:(hd + 1) * dv] = (o * _rms_inv(o, dv) * gain).astype(o_ref.dtype)


def _gla_step(zp, la, gain, s0, c, heads, dk, dv):
    nb = s0.shape[0]
    gk, gv = heads * dk, heads * dv
    assert la.shape[0] == nb * c and c <= GLA_LEAF and gv % gk == 0
    state = pl.BlockSpec((None, heads, dk, dv), lambda bb: (bb, 0, 0, 0))
    return pl.pallas_call(
        functools.partial(_gla_step_kernel, heads=heads, dk=dk, dv=dv),
        grid=(nb,),
        in_specs=[pl.BlockSpec((c, gk), lambda bb: (bb, 0)), pl.BlockSpec((c, gk), lambda bb: (bb, 1)),
                  pl.BlockSpec((c, gv), lambda bb: (bb, 2 * gk // gv)), pl.BlockSpec((c, gk), lambda bb: (bb, 0)),
                  pl.BlockSpec((1, dv), lambda bb: (0, 0)), state],
        out_specs=[pl.BlockSpec((c, gv), lambda bb: (bb, 0)), state],
        out_shape=[jax.ShapeDtypeStruct((nb * c, gv), BF16), jax.ShapeDtypeStruct((nb, heads, dk, dv), F32)],
        compiler_params=_params("parallel"),
        name="gla_step",
    )(zp, zp, zp, la, gain, s0)


def _mix_kernel(om_ref, og_ref, gate_o_ref, gate_m_ref, gate_g_ref, x_ref, w_ref, g_ref, y_ref, h_ref):
    o_gla = og_ref[...].astype(F32) * gate_o_ref[...].astype(F32)
    mixed = gate_m_ref[...].astype(F32) * om_ref[...].astype(F32) + gate_g_ref[...].astype(F32) * o_gla
    y = x_ref[...] + _dot(mixed.astype(BF16), w_ref[...])
    y_ref[...] = y
    h_ref[...] = (y * _rms_inv(y, y.shape[-1]) * g_ref[...]).astype(h_ref.dtype)


def _mix(o_mla, o_gla, zp, gate_block0, x, w_o, g, tm):
    t, d = x.shape
    row = lambda i: (i, 0)
    return pl.pallas_call(
        _mix_kernel,
        grid=(t // tm,),
        in_specs=[pl.BlockSpec((tm, d), row), pl.BlockSpec((tm, d), row),
                  pl.BlockSpec((tm, d), lambda i: (i, gate_block0)),
                  pl.BlockSpec((tm, d), lambda i: (i, gate_block0 + 1)),
                  pl.BlockSpec((tm, d), lambda i: (i, gate_block0 + 2)),
                  pl.BlockSpec((tm, d), row), pl.BlockSpec((d, d), lambda i: (0, 0)),
                  pl.BlockSpec((1, d), lambda i: (0, 0))],
        out_specs=[pl.BlockSpec((tm, d), row), pl.BlockSpec((tm, d), row)],
        out_shape=[jax.ShapeDtypeStruct((t, d), F32), jax.ShapeDtypeStruct((t, d), BF16)],
        compiler_params=_params("parallel"),
        name="mix",
    )(o_mla, o_gla, zp, zp, zp, x, w_o, g)


def _ffn_kernel(h_ref, y_ref, wu_ref, wd_ref, *refs):
    o_ref = refs[-1]
    f = pl.program_id(1)

    @pl.when(f == 0)
    def _():
        o_ref[...] = y_ref[...]

    u = jnp.maximum(_dot(h_ref[...], wu_ref[...]), 0.0)
    o_ref[...] += _dot((u * u).astype(BF16), wd_ref[...])


def _ffn(h, y, w_up, w_down, tm, tf, done=None):
    t, d = y.shape
    dff = w_up.shape[1]
    skip = 0 if done is None else 1
    in_specs = [pl.BlockSpec((tm, d), lambda i, f: (i + skip, 0)), pl.BlockSpec((tm, d), lambda i, f: (i + skip, 0)),
                pl.BlockSpec((d, tf), lambda i, f: (0, f)), pl.BlockSpec((tf, d), lambda i, f: (f, 0))]
    args = [h, y, w_up, w_down]
    if done is not None:
        in_specs.append(pl.BlockSpec(memory_space=pl.ANY))
        args.append(done)
    return pl.pallas_call(
        _ffn_kernel,
        grid=(t // tm - skip, dff // tf),
        in_specs=in_specs,
        out_specs=pl.BlockSpec((tm, d), lambda i, f: (i + skip, 0)),
        out_shape=jax.ShapeDtypeStruct((t, d), F32),
        input_output_aliases={} if done is None else {4: 0},
        compiler_params=_params("parallel", "arbitrary"),
        name="ffn",
    )(*args)


def _ffn_first_kernel(h_ref, y_ref, hs_ref, ys_ref, wu_ref, wd_ref, o_ref, os_ref, wub_ref, wdb_ref):
    f = pl.program_id(0)

    @pl.when(f == 0)
    def _():
        o_ref[...] = y_ref[...]
        os_ref[...] = ys_ref[...]

    wu = wu_ref[...].astype(BF16)
    wd = wd_ref[...].astype(BF16)
    wub_ref[...] = wu
    wdb_ref[...] = wd
    for hh_ref, acc_ref in ((h_ref, o_ref), (hs_ref, os_ref)):
        u = jnp.maximum(_dot(hh_ref[...], wu), 0.0)
        acc_ref[...] += _dot((u * u).astype(BF16), wd)


def _ffn_first(h, y, hs, ys, w_up, w_down, tm, tf):
    t, d = y.shape
    ts = ys.shape[0]
    dff = w_up.shape[1]
    fix = lambda f: (0, 0)
    return pl.pallas_call(
        _ffn_first_kernel,
        grid=(dff // tf,),
        in_specs=[pl.BlockSpec((tm, d), fix), pl.BlockSpec((tm, d), fix),
                  pl.BlockSpec((ts, d), fix), pl.BlockSpec((ts, d), fix),
                  pl.BlockSpec((d, tf), lambda f: (0, f)), pl.BlockSpec((tf, d), lambda f: (f, 0))],
        out_specs=[pl.BlockSpec((tm, d), fix), pl.BlockSpec((ts, d), fix),
                   pl.BlockSpec((d, tf), lambda f: (0, f)), pl.BlockSpec((tf, d), lambda f: (f, 0))],
        out_shape=[jax.ShapeDtypeStruct((t, d), F32), jax.ShapeDtypeStruct((ts, d), F32),
                   jax.ShapeDtypeStruct((d, dff), BF16), jax.ShapeDtypeStruct((dff, d), BF16)],
        compiler_params=_params("arbitrary"),
        name="ffn_first",
    )(h, y, hs, ys, w_up, w_down)


def _rope_table(pos):
    half = MLA_ROPE // 2
    freqs = jnp.power(ROPE_THETA, -jnp.arange(half, dtype=F32) / half)
    ang = pos[:, None] * freqs[None, :]
    cos, sin = jnp.cos(ang), jnp.sin(ang)
    return jnp.concatenate([cos, cos, -sin, sin], axis=1)


def _swap_halves(w):
    half = w.shape[-1] // 2
    return jnp.concatenate([w[..., half:], w[..., :half]], axis=-1)


def _pick(t, pref):
    while t % pref:
        pref //= 2
    return pref


def kernel(x_prompt, x_sample, cache_mla_ckv, cache_mla_krope, state_gla, norm_mix_g, w_in, mla_q_norm_g,
           mla_w_uq, mla_kv_norm_g, mla_w_ukv, mla_q_gain_nope, mla_q_gain_rope, mla_k_gain_nope,
           mla_k_gain_rope, gla_w_a2, gla_b_a, gla_norm_g, w_o, norm_ffn_g, ffn_w_up, ffn_w_down):
    depth = w_in.shape[0]
    bp, seq, d = x_prompt.shape
    nb, dec_seq, _ = x_sample.shape
    past = cache_mla_ckv.shape[2]
    q_rank, heads = mla_w_uq.shape[1], mla_w_uq.shape[2]
    kv_rank = mla_w_ukv.shape[1]
    gla_heads, dk, dv = state_gla.shape[2], state_gla.shape[3], state_gla.shape[4]
    gate_rank = gla_w_a2.shape[1]
    gk, gv = gla_heads * dk, gla_heads * dv
    assert bp == 1 and heads * MLA_V == d and gv == d and seq % CHUNK == 0

    cs_p = _rope_table(jnp.arange(seq, dtype=F32))
    cs_s = jnp.tile(_rope_table(past + jnp.arange(dec_seq, dtype=F32)), (nb, 1))

    xp = x_prompt.reshape(seq, d)
    xs = x_sample.reshape(nb * dec_seq, d)
    outs = [[] for _ in range(6)]
    for l in range(depth):
        wi = w_in[l].astype(BF16)
        pts = np.cumsum([q_rank, kv_rank, MLA_ROPE, gk, gk, gv, gate_rank, gv, d]).tolist()
        w_qkv_lat = wi[:, :pts[1]]
        w_kr = wi[:, pts[1]:pts[2]]
        w_alr = wi[:, pts[5]:pts[6]]
        assert 3 * gate_rank <= LANE
        w_lat = jnp.concatenate(
            [w_qkv_lat, w_kr, _swap_halves(w_kr), w_alr, w_alr, w_alr,
             jnp.zeros((d, LANE - 3 * gate_rank), BF16)], axis=1)
        wa2_hi = gla_w_a2[l].astype(BF16)
        wa2_lo = (gla_w_a2[l] - wa2_hi.astype(F32)).astype(BF16)
        wa2_split = jnp.concatenate(
            [wa2_hi, wa2_lo, wa2_hi, jnp.zeros((LANE - 3 * gate_rank, gk), BF16)], axis=0)

        wq = mla_w_uq[l]
        wq_r = wq[..., MLA_NOPE:]
        w_uq = jnp.concatenate([wq[..., :MLA_NOPE], wq_r, _swap_halves(wq_r)], axis=-1)
        w_uq = w_uq.reshape(q_rank, heads * MLA_HEAD_PAD).astype(BF16)
        w_uq_t = w_uq.T
        wuk = mla_w_ukv[l][..., :MLA_NOPE].reshape(kv_rank, heads * MLA_NOPE).astype(BF16)
        wuk_t = wuk.T
        wuv = mla_w_ukv[l][..., MLA_NOPE:].reshape(kv_rank, heads * MLA_V).astype(BF16)
        wuv_t = wuv.T
        gain = jnp.concatenate([mla_q_gain_nope[l] * mla_k_gain_nope[l],
                                jnp.tile(mla_q_gain_rope[l] * mla_k_gain_rope[l], 2),
                                jnp.zeros((MLA_HEAD_PAD - MLA_QK_DIM,), F32)])[None, :] * MLA_SCALE
        gain_col = gain.T * LOG2E
        wo_b = w_o[l].astype(BF16)
        g_mix = norm_mix_g[l][None, :]
        g_q = mla_q_norm_g[l][None, :]
        g_kv = mla_kv_norm_g[l][None, :]
        g_gla = gla_norm_g[l][None, :]
        g_ffn = norm_ffn_g[l][None, :]
        ba = gla_b_a[l][None, :]

        def front(x, cs, tm, proj_w=None):
            h, qlat, ckv, kr, la = _mla_lat(x, g_mix, w_lat, g_q, g_kv, cs, wa2_split, ba, tm, q_rank, kv_rank,
                                            gate_rank)
            pm = _pick(x.shape[0], 1024)
            z_gla = z_gate = None
            if proj_w is None:
                z_gla, w_gla = _proj_first(h, wi, pts[2], pts[5] - pts[2], pm, 1024)
                z_gate, w_gate = _proj_first(h, wi, pts[6], wi.shape[1] - pts[6], pm, 1024, swish_cols=gv)
                proj_w = (w_gla, w_gate)
            if z_gla is None or x.shape[0] > pm:
                z_gla = _proj(h, proj_w[0], pm, 2048, done=z_gla)
                z_gate = _proj(h, proj_w[1], pm, gv, swish_cols=gv, done=z_gate)
            return qlat, ckv, kr, la, z_gla, z_gate, proj_w

        tile = _pick(seq, 512)
        qlat, ckv, kr, la, z_gla, z_gate, proj_w = front(xp, cs_p, tile)
        qt = _mla_qt(qlat, w_uq_t, cs_p.T, gain_col, tile, 2 * tile, heads)
        kcat, vt = _mla_kv(ckv, kr, wuk, wuv_t, tile, heads)
        o_mla = _flash(qt, kcat, vt, heads)
        o_gla, st = _gla(z_gla, la, g_gla, 1, _pick(seq, 128), gla_heads, dk, dv)
        y1_p, h2_p = _mix(o_mla, o_gla, z_gate, 0, xp, wo_b, g_ffn, tile)
        outs[0].append(ckv.reshape(bp, seq, kv_rank))
        outs[1].append(kr.reshape(bp, seq, MLA_ROPE))
        outs[2].append(st)

        qlat, ckv, kr, la, z_gla, z_gate, _ = front(xs, cs_s, nb * dec_seq, proj_w)
        qcat = _mla_q(qlat, w_uq, cs_s, gain, nb * dec_seq, heads)
        o_lat = _sattn(qcat, cache_mla_ckv[l], cache_mla_krope[l], ckv, kr, wuk_t, heads, dec_seq, _pick(past, 512))
        o_mla = _svup(o_lat, wuv, heads, dec_seq)
        o_gla, st = _gla_step(z_gla, la, g_gla, state_gla[l], dec_seq, gla_heads, dk, dv)
        y1_s, h2_s = _mix(o_mla, o_gla, z_gate, 0, xs, wo_b, g_ffn, nb * dec_seq)
        outs[3].append(ckv.reshape(nb, dec_seq, kv_rank))
        outs[4].append(kr.reshape(nb, dec_seq, MLA_ROPE))
        outs[5].append(st)

        xp, xs, wup_b, wdn_b = _ffn_first(h2_p, y1_p, h2_s, y1_s, ffn_w_up[l], ffn_w_down[l], tile, 512)
        if seq > tile:
            xp = _ffn(h2_p, y1_p, wup_b, wdn_b, tile, 1024, done=xp)

    return (xp.reshape(bp, seq, d), xs.reshape(nb, dec_seq, d),
            jnp.stack(outs[0]), jnp.stack(outs[1]), jnp.stack(outs[2]),
            jnp.stack(outs[3]), jnp.stack(outs[4]), jnp.stack(outs[5]))
```

```python
import functools

import jax
import jax.numpy as jnp
import numpy as np
from jax import lax
from jax.experimental import pallas as pl
from jax.experimental.pallas import tpu as pltpu

F32 = jnp.float32
BF16 = jnp.bfloat16

EPS = 1e-6
CHUNK = 64
MLA_NOPE = 128
MLA_ROPE = 64
MLA_V = 128
MLA_QK_DIM = MLA_NOPE + MLA_ROPE
MLA_SCALE = MLA_QK_DIM ** -0.5
MLA_HEAD_PAD = 256
V_ROWS = MLA_V + 16
LOG2E = 1.4426950408889634
QBLK = 256
ROPE_THETA = 10000.0
GLA_TAU = 16.0
GLA_LEAF = 32
LANE = 128
V7X_VMEM_BYTES = 64 * 1024 * 1024
VMEM_LIMIT = V7X_VMEM_BYTES * 7 // 8

NT_DIMS = (((1,), (1,)), ((), ()))
TN_DIMS = (((0,), (0,)), ((), ()))


def _params(*sem):
    return pltpu.CompilerParams(dimension_semantics=sem, vmem_limit_bytes=VMEM_LIMIT)


def _dot(a, b):
    return jnp.dot(a, b, preferred_element_type=F32)


def _dot_nt(a, b):
    return lax.dot_general(a, b, NT_DIMS, preferred_element_type=F32)


def _rms_inv(x, n):
    return lax.rsqrt(jnp.sum(x * x, axis=-1, keepdims=True) / n + EPS)


def _mla_lat_kernel(x_ref, g_ref, w_ref, qg_ref, kvg_ref, cs_ref, wa2_ref, ba_ref,
                    h_ref, qlat_ref, ckv_ref, kr_ref, la_ref, *, q_rank, kv_rank, gate_rank):
    x = x_ref[...]
    h = (x * _rms_inv(x, x.shape[-1]) * g_ref[...]).astype(h_ref.dtype)
    h_ref[...] = h
    z = _dot(h, w_ref[...])
    q_lat = z[:, :q_rank]
    qlat_ref[...] = (q_lat * _rms_inv(q_lat, q_rank) * qg_ref[...]).astype(qlat_ref.dtype)
    kv_lat = z[:, q_rank:q_rank + kv_rank]
    ckv_ref[...] = kv_lat * _rms_inv(kv_lat, kv_rank) * kvg_ref[...]
    o = q_rank + kv_rank
    rr = z[:, o:o + LANE] * cs_ref[...]
    kr_ref[...] = rr[:, :MLA_ROPE] + rr[:, MLA_ROPE:]
    a3 = z[:, o + LANE:o + 2 * LANE]
    a_hi = a3.astype(BF16)
    a_lo = (a3 - a_hi.astype(F32)).astype(BF16)
    lane = lax.broadcasted_iota(jnp.int32, a3.shape, 1)
    u = _dot(jnp.where(lane < 2 * gate_rank, a_hi, a_lo), wa2_ref[...]) + ba_ref[...]
    log_sig = jnp.minimum(u, 0.0) - jnp.log1p(jnp.exp(-jnp.abs(u)))
    la_ref[...] = log_sig / GLA_TAU


def _mla_lat(x, g, w_lat, qg, kvg, cs, wa2_split, ba, tm, q_rank, kv_rank, gate_rank):
    t, d = x.shape
    n = w_lat.shape[1]
    gk = wa2_split.shape[1]
    row = lambda i: (i, 0)
    fix = lambda i: (0, 0)
    return pl.pallas_call(
        functools.partial(_mla_lat_kernel, q_rank=q_rank, kv_rank=kv_rank, gate_rank=gate_rank),
        grid=(t // tm,),
        in_specs=[pl.BlockSpec((tm, d), row), pl.BlockSpec((1, d), fix), pl.BlockSpec((d, n), fix),
                  pl.BlockSpec((1, q_rank), fix), pl.BlockSpec((1, kv_rank), fix),
                  pl.BlockSpec((tm, LANE), row), pl.BlockSpec((LANE, gk), fix), pl.BlockSpec((1, gk), fix)],
        out_specs=[pl.BlockSpec((tm, d), row), pl.BlockSpec((tm, q_rank), row), pl.BlockSpec((tm, kv_rank), row),
                   pl.BlockSpec((tm, MLA_ROPE), row), pl.BlockSpec((tm, gk), row)],
        out_shape=[jax.ShapeDtypeStruct((t, d), BF16),
                   jax.ShapeDtypeStruct((t, q_rank), BF16), jax.ShapeDtypeStruct((t, kv_rank), F32),
                   jax.ShapeDtypeStruct((t, MLA_ROPE), F32), jax.ShapeDtypeStruct((t, gk), F32)],
        compiler_params=_params("parallel"),
        name="mla_lat",
    )(x, g, w_lat, qg, kvg, cs, wa2_split, ba)


def _gate_act(z, swish):
    return jax.nn.sigmoid(z) * jnp.where(swish, z, 1.0)


def _proj_kernel(h_ref, w_ref, *refs, swish_tiles):
    o_ref = refs[-1]

    def body():
        z = _dot(h_ref[...], w_ref[...])
        if swish_tiles is not None:
            z = _gate_act(z, pl.program_id(1) < swish_tiles)
        o_ref[...] = z.astype(o_ref.dtype)

    if len(refs) == 1:
        body()
        return
    first_ref = refs[0]

    @pl.when(pl.program_id(0) == 0)
    def _():
        o_ref[...] = first_ref[...]

    pl.when(pl.program_id(0) > 0)(body)


def _proj(h, w, tm, tn, swish_cols=None, first=None):
    t, d = h.shape
    n = w.shape[1]
    in_specs = [pl.BlockSpec((tm, d), lambda i, j: (i, 0))]
    args = [h, w]
    if first is None:
        in_specs.append(pl.BlockSpec((d, tn), lambda i, j: (0, j)))
    else:
        in_specs.append(pl.BlockSpec((d, tn), lambda i, j: (0, jnp.where(i == 0, 0, j))))
        in_specs.append(pl.BlockSpec((tm, tn), lambda i, j: (0, jnp.where(i == 0, j, 0))))
        args.append(first)
    assert swish_cols is None or swish_cols % tn == 0
    return pl.pallas_call(
        functools.partial(_proj_kernel, swish_tiles=None if swish_cols is None else swish_cols // tn),
        grid=(t // tm, n // tn),
        in_specs=in_specs,
        out_specs=pl.BlockSpec((tm, tn), lambda i, j: (i, j)),
        out_shape=jax.ShapeDtypeStruct((t, n), BF16),
        compiler_params=_params("arbitrary", "arbitrary"),
        name="proj",
    )(*args)


def _proj_first_kernel(h_ref, a_ref, b_ref, o_ref, w_ref, *, shift, n_cols, next_col0, swish_tiles):
    j = pl.program_id(0)
    tn = a_ref.shape[1]
    b = b_ref[...]
    col = next_col0 + j * tn + lax.broadcasted_iota(jnp.int32, b.shape, 1)
    src = jnp.concatenate([a_ref[...], jnp.where(col < n_cols, b, jnp.zeros_like(b))], axis=1).astype(F32)
    w = pltpu.roll(src, src.shape[1] - shift, 1)[:, :tn].astype(BF16)
    w_ref[...] = w
    z = _dot(h_ref[...], w)
    if swish_tiles is not None:
        z = _gate_act(z, j < swish_tiles)
    o_ref[...] = z.astype(o_ref.dtype)


def _proj_first(h, w_all, col0, width, tm, tn, swish_cols=None):
    t, d = h.shape
    n_cols = w_all.shape[1]
    base = col0 // LANE * LANE
    shift = col0 - base
    assert shift > 0 and base % tn == 0 and width % tn == 0 and tn % LANE == 0
    assert swish_cols is None or swish_cols % tn == 0
    return pl.pallas_call(
        functools.partial(_proj_first_kernel, shift=shift, n_cols=n_cols, next_col0=base + tn,
                          swish_tiles=None if swish_cols is None else swish_cols // tn),
        grid=(width // tn,),
        in_specs=[pl.BlockSpec((tm, d), lambda j: (0, 0)),
                  pl.BlockSpec((d, tn), lambda j: (0, base // tn + j)),
                  pl.BlockSpec((d, LANE), lambda j: (0, (base + (j + 1) * tn) // LANE))],
        out_specs=[pl.BlockSpec((tm, tn), lambda j: (0, j)), pl.BlockSpec((d, tn), lambda j: (0, j))],
        out_shape=[jax.ShapeDtypeStruct((tm, width), BF16), jax.ShapeDtypeStruct((d, width), BF16)],
        compiler_params=_params("parallel"),
        name="proj_first",
    )(h, w_all, w_all)


def _mla_q_kernel(ql_ref, w_ref, cs_ref, gain_ref, q_ref, *, heads):
    ql = ql_ref[...]
    cs = cs_ref[...]
    gain = gain_ref[...]
    lane = lax.broadcasted_iota(jnp.int32, (1, LANE), 1)
    for hd in range(heads):
        c0 = hd * MLA_HEAD_PAD
        z = _dot(ql, w_ref[:, c0:c0 + MLA_HEAD_PAD])
        nope = z[:, :MLA_NOPE]
        rr = z[:, MLA_NOPE:] * cs
        rot = rr + pltpu.roll(rr, MLA_ROPE, 1)
        ss = (jnp.sum(nope * nope, axis=-1, keepdims=True)
              + jnp.sum(jnp.where(lane < MLA_ROPE, rot * rot, 0.0), axis=-1, keepdims=True))
        inv = lax.rsqrt(ss / MLA_QK_DIM + EPS)
        q_ref[:, c0:c0 + MLA_NOPE] = (nope * inv * gain[:, :MLA_NOPE]).astype(q_ref.dtype)
        q_ref[:, c0 + MLA_NOPE:c0 + MLA_HEAD_PAD] = (rot * inv * gain[:, MLA_NOPE:]).astype(q_ref.dtype)


def _mla_q(qlat, w_uq, cs, gain, tm, heads):
    t, r = qlat.shape
    n = heads * MLA_HEAD_PAD
    return pl.pallas_call(
        functools.partial(_mla_q_kernel, heads=heads),
        grid=(t // tm,),
        in_specs=[pl.BlockSpec((tm, r), lambda i: (i, 0)), pl.BlockSpec((r, n), lambda i: (0, 0)),
                  pl.BlockSpec((tm, LANE), lambda i: (i, 0)), pl.BlockSpec((1, MLA_HEAD_PAD), lambda i: (0, 0))],
        out_specs=pl.BlockSpec((tm, n), lambda i: (i, 0)),
        out_shape=jax.ShapeDtypeStruct((t, n), BF16),
        compiler_params=_params("parallel"),
        name="mla_q",
    )(qlat, w_uq, cs, gain)


def _mla_qt_kernel(ql_ref, w_ref, cs_ref, gain_ref, q_ref, *, heads):
    ql = ql_ref[...]
    cs = cs_ref[...]
    gain = gain_ref[...]
    for hd in range(heads):
        r0 = hd * MLA_HEAD_PAD
        z = _dot_nt(w_ref[r0:r0 + MLA_HEAD_PAD, :], ql)
        nope = z[:MLA_NOPE]
        rr = z[MLA_NOPE:] * cs
        rot = rr[:MLA_ROPE] + rr[MLA_ROPE:]
        ss = jnp.sum(nope * nope, axis=0, keepdims=True) + jnp.sum(rot * rot, axis=0, keepdims=True)
        inv = lax.rsqrt(ss / MLA_QK_DIM + EPS)
        qn = (nope * inv * gain[:MLA_NOPE]).astype(q_ref.dtype)
        qr = (rot * inv * gain[MLA_NOPE:MLA_QK_DIM]).astype(q_ref.dtype)
        for bb in range(q_ref.shape[1]):
            q_ref[hd, bb, :MLA_NOPE, :] = qn[:, bb * QBLK:(bb + 1) * QBLK]
            q_ref[hd, bb, MLA_NOPE:MLA_QK_DIM, :] = qr[:, bb * QBLK:(bb + 1) * QBLK]
            q_ref[hd, bb, MLA_QK_DIM:, :] = jnp.zeros((MLA_HEAD_PAD - MLA_QK_DIM, QBLK), q_ref.dtype)


def _mla_qt(qlat, w_uq_t, cs_t, gain_col, tm, tq, heads):
    t, r = qlat.shape
    n = heads * MLA_HEAD_PAD
    per = tq // tm
    assert tm % QBLK == 0 and tq % tm == 0
    return pl.pallas_call(
        functools.partial(_mla_qt_kernel, heads=heads),
        grid=(t // tm,),
        in_specs=[pl.BlockSpec((tm, r), lambda i: (i, 0)), pl.BlockSpec((n, r), lambda i: (0, 0)),
                  pl.BlockSpec((LANE, tm), lambda i: (0, i)), pl.BlockSpec((MLA_HEAD_PAD, 1), lambda i: (0, 0))],
        out_specs=pl.BlockSpec((heads, None, tm // QBLK, MLA_HEAD_PAD, QBLK),
                               lambda i: (0, i // per, i % per, 0, 0)),
        out_shape=jax.ShapeDtypeStruct((heads, t // tq, tq // QBLK, MLA_HEAD_PAD, QBLK), BF16),
        compiler_params=_params("parallel"),
        name="mla_qt",
    )(qlat, w_uq_t, cs_t, gain_col)


def _mla_kv_kernel(ckv_ref, kr_ref, wk_ref, wvt_ref, k_ref, vt_ref, *, heads):
    c = ckv_ref[...].astype(BF16)
    kr = kr_ref[...]
    kr2 = jnp.sum(kr * kr, axis=-1, keepdims=True)
    kr_pad = jnp.concatenate([kr, jnp.zeros_like(kr)], axis=1)
    for pair in range(heads // 2):
        z = _dot(c, wk_ref[:, pair * 2 * MLA_NOPE:(pair + 1) * 2 * MLA_NOPE])
        for sub in range(2):
            kn = z[:, sub * MLA_NOPE:(sub + 1) * MLA_NOPE]
            inv = lax.rsqrt((jnp.sum(kn * kn, axis=-1, keepdims=True) + kr2) / MLA_QK_DIM + EPS)
            c0 = (2 * pair + sub) * MLA_HEAD_PAD
            k_ref[:, c0:c0 + MLA_NOPE] = (kn * inv).astype(k_ref.dtype)
            k_ref[:, c0 + MLA_NOPE:c0 + MLA_HEAD_PAD] = (kr_pad * inv).astype(k_ref.dtype)
    vt = _dot_nt(wvt_ref[...], c)
    for hd in range(heads):
        vt_ref[hd, :MLA_V, :] = vt[hd * MLA_V:(hd + 1) * MLA_V].astype(vt_ref.dtype)
        vt_ref[hd, MLA_V:, :] = jnp.ones((V_ROWS - MLA_V, vt.shape[1]), vt_ref.dtype)


def _mla_kv(ckv, kr, w_uk, w_uv_t, tm, heads):
    t, r = ckv.shape
    return pl.pallas_call(
        functools.partial(_mla_kv_kernel, heads=heads),
        grid=(t // tm,),
        in_specs=[pl.BlockSpec((tm, r), lambda i: (i, 0)), pl.BlockSpec((tm, MLA_ROPE), lambda i: (i, 0)),
                  pl.BlockSpec((r, heads * MLA_NOPE), lambda i: (0, 0)),
                  pl.BlockSpec((heads * MLA_V, r), lambda i: (0, 0))],
        out_specs=[pl.BlockSpec((tm, heads * MLA_HEAD_PAD), lambda i: (i, 0)),
                   pl.BlockSpec((heads, None, V_ROWS, tm), lambda i: (0, i, 0, 0))],
        out_shape=[jax.ShapeDtypeStruct((t, heads * MLA_HEAD_PAD), BF16),
                   jax.ShapeDtypeStruct((heads, t // tm, V_ROWS, tm), BF16)],
        compiler_params=_params("parallel"),
        name="mla_kv",
    )(ckv, kr, w_uk, w_uv_t)


def _diag_mode(u, c, tk):
    k_lo, k_hi = (u * tk) // CHUNK, ((u + 1) * tk - 1) // CHUNK
    q_lo, q_hi = (c * QBLK) // CHUNK, ((c + 1) * QBLK - 1) // CHUNK
    return "all" if k_hi <= q_lo else "none" if k_lo > q_hi else "some"


def _diag_bias(tk, n_blk):
    index, blocks = {}, []
    for u in range(2):
        for c in range(n_blk):
            if _diag_mode(u, c, tk) == "some":
                k_chunk = (u * tk + np.arange(tk)[:, None]) // CHUNK
                q_chunk = (c * QBLK + np.arange(QBLK)[None, :]) // CHUNK
                index[(u, c)] = len(blocks)
                blocks.append(np.where(k_chunk <= q_chunk, 0.0, -np.inf).astype(np.float32))
    return index, np.stack(blocks)


def _flash_kernel(qt_ref, k_ref, vt_ref, bias_ref, o_ref, m_ref, acc_ref, s_ref, mx_ref, p_ref, al_ref,
                  *, tq, tk, bias_index):
    n_q, n_blk = qt_ref.shape[0], qt_ref.shape[1]
    acc_ref[...] = jnp.zeros_like(acc_ref)


    def diag_mode(u, c):
        return _diag_mode(u, c, tk)

    def score(i, t, slot, c, mode):
        if mode == "none":
            return
        k = k_ref[pl.ds(pl.multiple_of(t * tk, tk), tk), :]
        s = _dot(k, qt_ref[i, c])
        if mode == "some":
            s = s + bias_ref[bias_index[(slot, c)]]
        s_ref[slot, c] = s
        mx_ref[slot, c] = jnp.max(s, axis=0, keepdims=True)

    def soften(slot, c, mode="all", first=False):
        if mode == "none":
            return
        if first:
            m_new = mx_ref[slot, c]
            al_ref[slot, c] = jnp.zeros_like(m_new)
        else:
            m_prev = m_ref[c]
            m_new = jnp.maximum(m_prev, mx_ref[slot, c])
            al_ref[slot, c] = jnp.exp2(m_prev - m_new)
        p_ref[slot, c] = jnp.exp2(s_ref[slot, c] - m_new).astype(BF16)
        m_ref[c] = m_new

    def gather(t, slot, c, mode="all"):
        if mode == "none":
            return
        acc_ref[c] = al_ref[slot, c] * acc_ref[c] + _dot(vt_ref[t], p_ref[slot, c])

    def pair(i, g, diag):
        for c in range(n_blk):
            score(i, g, 0, c, diag_mode(0, c) if diag else "all")
            soften(1, c)
            gather(g - 2, 0, c)
        for c in range(n_blk):
            score(i, g + 1, 1, c, diag_mode(1, c) if diag else "all")
            soften(0, c, diag_mode(0, c) if diag else "all")
            gather(g - 1, 1, c)

    def head(i, diag):
        for c in range(n_blk):
            score(i, 0, 0, c, diag_mode(0, c) if diag else "all")
        for c in range(n_blk):
            score(i, 1, 1, c, diag_mode(1, c) if diag else "all")
            soften(0, c, diag_mode(0, c) if diag else "all", first=True)

    def tail(i):
        for c in range(n_blk):
            soften(1, c, diag_mode(1, c))
            gather(2 * i, 0, c, diag_mode(0, c))
        for c in range(n_blk):
            gather(2 * i + 1, 1, c, diag_mode(1, c))
        for c in range(n_blk):
            acc = acc_ref[c]
            rows = pl.ds(pl.multiple_of(i * tq + c * QBLK, QBLK), QBLK)
            o_ref[rows, :] = (acc[:MLA_V] / acc[MLA_V:MLA_V + 1]).T.astype(o_ref.dtype)

    def middle(i):
        def body(j, carry):
            pair(i, 2 * j, False)
            return carry

        lax.fori_loop(1, i, body, 0)
        pair(i, 2 * i, True)

    head(0, True)
    if n_q > 1:
        tail(0)
        head(1, False)

        def outer(i, carry):
            middle(i)
            tail(i)
            head(i + 1, False)
            return carry

        lax.fori_loop(1, n_q - 1, outer, 0)
        middle(n_q - 1)
    tail(n_q - 1)


def _flash(qt, k, vt, heads):
    t = k.shape[0]
    n_q, n_blk = qt.shape[1], qt.shape[2]
    tq = n_blk * QBLK
    n_kt, tk = vt.shape[1], vt.shape[3]
    assert tq == 2 * tk and tk % CHUNK == 0 and qt.shape[4] == QBLK and n_q * tq == t
    bias_index, bias = _diag_bias(tk, n_blk)
    return pl.pallas_call(
        functools.partial(_flash_kernel, tq=tq, tk=tk, bias_index=bias_index),
        grid=(heads,),
        in_specs=[pl.BlockSpec((None, n_q, n_blk, MLA_HEAD_PAD, QBLK), lambda hd: (hd, 0, 0, 0, 0)),
                  pl.BlockSpec((t, MLA_HEAD_PAD), lambda hd: (0, hd)),
                  pl.BlockSpec((None, n_kt, V_ROWS, tk), lambda hd: (hd, 0, 0, 0)),
                  pl.BlockSpec(bias.shape, lambda hd: (0, 0, 0))],
        out_specs=pl.BlockSpec((t, MLA_V), lambda hd: (0, hd)),
        out_shape=jax.ShapeDtypeStruct((t, heads * MLA_V), BF16),
        scratch_shapes=[pltpu.VMEM((n_blk, 1, QBLK), F32), pltpu.VMEM((n_blk, V_ROWS, QBLK), F32),
                        pltpu.VMEM((2, n_blk, tk, QBLK), F32), pltpu.VMEM((2, n_blk, 1, QBLK), F32),
                        pltpu.VMEM((2, n_blk, tk, QBLK), BF16), pltpu.VMEM((2, n_blk, 1, QBLK), F32)],
        compiler_params=_params("parallel"),
        name="flash",
    )(qt, k, vt, jnp.asarray(bias))


def _sattn_kernel(q_ref, cc_ref, ck_ref, nc_ref, nk_ref, wuk_ref, o_ref,
                  qabs_ref, qr_ref, m_ref, l_ref, acc_ref, *, heads, nq, tk, n_new):
    past, rank = cc_ref.shape
    n_tiles = past // tk
    for hd in range(heads):
        c0 = hd * MLA_HEAD_PAD
        qn = q_ref[:, c0:c0 + MLA_NOPE]
        qabs_ref[hd * nq:(hd + 1) * nq, :] = _dot(
            qn, wuk_ref[hd * MLA_NOPE:(hd + 1) * MLA_NOPE, :]).astype(qabs_ref.dtype)
        qr_ref[hd * nq:(hd + 1) * nq, :] = q_ref[:, c0 + MLA_NOPE:c0 + MLA_NOPE + MLA_ROPE]
    m_ref[...] = jnp.full_like(m_ref, -jnp.inf)
    l_ref[...] = jnp.zeros_like(l_ref)
    acc_ref[...] = jnp.zeros_like(acc_ref)

    def tile(t):
        if t < n_tiles:
            return cc_ref[t * tk:(t + 1) * tk, :].astype(BF16), ck_ref[t * tk:(t + 1) * tk, :]
        pad = LANE - n_new
        c_new = jnp.concatenate([nc_ref[...], jnp.zeros((pad, rank), F32)], axis=0)
        k_new = jnp.concatenate([nk_ref[...], jnp.zeros((pad, MLA_ROPE), F32)], axis=0)
        return c_new.astype(BF16), k_new

    def key_norm(c, kr):
        n_keys = c.shape[0]
        kn_t = _dot_nt(wuk_ref[...], c)
        ss_t = jnp.sum((kn_t * kn_t).reshape(heads, MLA_NOPE, n_keys), axis=1)
        kr2_t = lax.dot_general(jnp.ones((8, MLA_ROPE), F32), kr * kr, NT_DIMS,
                                preferred_element_type=F32, precision=lax.Precision.HIGHEST)[0:1]
        return lax.rsqrt((ss_t + kr2_t) / MLA_QK_DIM + EPS)

    def attend(c, kr, inv_t, n_valid):
        n_keys = c.shape[0]
        s = _dot_nt(qabs_ref[...], c) + _dot_nt(qr_ref[...], kr.astype(BF16))
        s = s * jnp.broadcast_to(inv_t[:, None, :], (heads, nq, n_keys)).reshape(heads * nq, n_keys)
        if n_valid < n_keys:
            s = jnp.where(lax.broadcasted_iota(jnp.int32, s.shape, 1) < n_valid, s, -jnp.inf)
        m_prev = m_ref[...]
        m_new = jnp.maximum(m_prev, jnp.max(s, axis=-1, keepdims=True))
        alpha = jnp.exp(m_prev - m_new)
        p = jnp.exp(s - m_new)
        l_ref[...] = alpha * l_ref[...] + jnp.sum(p, axis=-1, keepdims=True)
        acc_ref[...] = alpha * acc_ref[...] + _dot(p.astype(BF16), c)
        m_ref[...] = m_new

    c, kr = tile(0)
    inv_t = key_norm(c, kr)
    for t in range(n_tiles + 1):
        if t < n_tiles:
            c_next, kr_next = tile(t + 1)
            inv_next = key_norm(c_next, kr_next)
        attend(c, kr, inv_t, tk if t < n_tiles else n_new)
        if t < n_tiles:
            c, kr, inv_t = c_next, kr_next, inv_next
    o_ref[...] = (acc_ref[...] / l_ref[...]).astype(o_ref.dtype)


def _sattn(q, cache_c, cache_k, new_c, new_k, wuk_t, heads, nq, tk):
    nb, past, rank = cache_c.shape
    return pl.pallas_call(
        functools.partial(_sattn_kernel, heads=heads, nq=nq, tk=tk, n_new=nq),
        grid=(nb,),
        in_specs=[pl.BlockSpec((nq, heads * MLA_HEAD_PAD), lambda b: (b, 0)),
                  pl.BlockSpec((None, past, rank), lambda b: (b, 0, 0)),
                  pl.BlockSpec((None, past, MLA_ROPE), lambda b: (b, 0, 0)),
                  pl.BlockSpec((nq, rank), lambda b: (b, 0)),
                  pl.BlockSpec((nq, MLA_ROPE), lambda b: (b, 0)),
                  pl.BlockSpec(wuk_t.shape, lambda b: (0, 0))],
        out_specs=pl.BlockSpec((None, heads * nq, rank), lambda b: (b, 0, 0)),
        out_shape=jax.ShapeDtypeStruct((nb, heads * nq, rank), BF16),
        scratch_shapes=[pltpu.VMEM((heads * nq, rank), BF16), pltpu.VMEM((heads * nq, MLA_ROPE), BF16),
                        pltpu.VMEM((heads * nq, 1), F32), pltpu.VMEM((heads * nq, 1), F32),
                        pltpu.VMEM((heads * nq, rank), F32)],
        compiler_params=_params("parallel"),
        name="sattn",
    )(q, cache_c, cache_k, new_c, new_k, wuk_t)


def _svup_kernel(ol_ref, w_ref, o_ref, *, nq):
    nb, rows, rank = ol_ref.shape
    for hh in range(rows // nq):
        x = ol_ref[:, hh * nq:(hh + 1) * nq, :].reshape(nb * nq, rank)
        o_ref[:, hh * MLA_V:(hh + 1) * MLA_V] = _dot(x, w_ref[:, hh * MLA_V:(hh + 1) * MLA_V]).astype(o_ref.dtype)


def _svup(o_lat, wuv, heads, nq):
    nb, _, rank = o_lat.shape
    hs = 4 if heads % 4 == 0 else 1
    return pl.pallas_call(
        functools.partial(_svup_kernel, nq=nq),
        grid=(heads // hs,),
        in_specs=[pl.BlockSpec((nb, hs * nq, rank), lambda g: (0, g, 0)),
                  pl.BlockSpec((rank, hs * MLA_V), lambda g: (0, g))],
        out_specs=pl.BlockSpec((nb * nq, hs * MLA_V), lambda g: (0, g)),
        out_shape=jax.ShapeDtypeStruct((nb * nq, heads * MLA_V), BF16),
        compiler_params=_params("parallel"),
        name="svup",
    )(o_lat, wuv)


def _gla_levels(c):
    leaf = min(c, GLA_LEAF)
    levels = [(leaf, leaf // 2 - 1)]
    g = 2 * leaf
    while g <= c:
        levels.append((g, g // 2 - 1))
        g *= 2
    return levels


def _gla_kernel(q_ref, k_ref, v_ref, la_ref, g_ref, o_ref, sout_ref, st_ref, *, heads, dk, dv, c):
    j = pl.program_id(1)
    nj = pl.num_programs(1)

    @pl.when(j == 0)
    def _():
        st_ref[...] = jnp.zeros_like(st_ref)

    levels = _gla_levels(c)
    row = lax.broadcasted_iota(jnp.int32, (c, c), 0)
    col = lax.broadcasted_iota(jnp.int32, (c, c), 1)
    masks = []
    for lv, (g, _) in enumerate(levels):
        same = (row // g) == (col // g)
        if lv == 0:
            masks.append(jnp.logical_and(same, col <= row))
        else:
            half = g // 2
            masks.append(jnp.logical_and(same, jnp.logical_and((row // half) % 2 == 1, (col // half) % 2 == 0)))
    rid = lax.broadcasted_iota(jnp.int32, (c, dk), 0)
    gain = g_ref[...]

    for hd in range(heads):
        la = la_ref[:, hd * dk:(hd + 1) * dk]
        b = la
        sh = 1
        while sh < c:
            b = b + jnp.where(rid >= sh, pltpu.roll(b, sh, 0), 0.0)
            sh *= 2
        b_end = b[c - 1:c, :]
        q = q_ref[:, hd * dk:(hd + 1) * dk].astype(F32) * (dk ** -0.5)
        k = k_ref[:, hd * dk:(hd + 1) * dk].astype(F32)
        v = v_ref[:, hd * dv:(hd + 1) * dv]

        a = jnp.zeros((c, c), F32)
        for lv, (g, r) in enumerate(levels):
            ref_rows = jnp.broadcast_to(b.reshape(c // g, g, dk)[:, r:r + 1, :], (c // g, g, dk)).reshape(c, dk)
            d = b - ref_rows
            if lv == 0:
                fq, fk = jnp.exp(d), jnp.exp(-d)
            else:
                fq = fk = jnp.exp(-jnp.abs(d))
            a_lv = _dot_nt((q * fq).astype(BF16), (k * fk).astype(BF16))
            a = jnp.where(masks[lv], a_lv, a)

        st = st_ref[hd]
        o = _dot(a.astype(BF16), v) + _dot_nt((q * jnp.exp(b)).astype(BF16), st.astype(BF16))
        k_end = (k * jnp.exp(b_end - b)).astype(BF16)
        st_ref[hd] = st * jnp.exp(b_end) + lax.dot_general(v, k_end, TN_DIMS, preferred_element_type=F32)
        o_ref[:, hd * dv:(hd + 1) * dv] = (o * _rms_inv(o, dv) * gain).astype(o_ref.dtype)

    @pl.when(j == nj - 1)
    def _():
        for hd in range(heads):
            sout_ref[hd] = st_ref[hd].T


def _gla(zp, la, gain, nb, c, heads, dk, dv):
    t = la.shape[0]
    nj = t // (nb * c)
    gk, gv = heads * dk, heads * dv
    assert gv % gk == 0
    row = lambda bb, j: (bb * nj + j, 0)
    in_specs = [pl.BlockSpec((c, gk), row),
                pl.BlockSpec((c, gk), lambda bb, j: (bb * nj + j, 1)),
                pl.BlockSpec((c, gv), lambda bb, j: (bb * nj + j, 2 * gk // gv)),
                pl.BlockSpec((c, gk), row),
                pl.BlockSpec((1, dv), lambda bb, j: (0, 0))]
    return pl.pallas_call(
        functools.partial(_gla_kernel, heads=heads, dk=dk, dv=dv, c=c),
        grid=(nb, nj),
        in_specs=in_specs,
        out_specs=[pl.BlockSpec((c, gv), row),
                   pl.BlockSpec((None, heads, dk, dv), lambda bb, j: (bb, 0, 0, 0))],
        out_shape=[jax.ShapeDtypeStruct((t, gv), BF16), jax.ShapeDtypeStruct((nb, heads, dk, dv), F32)],
        scratch_shapes=[pltpu.VMEM((heads, dv, dk), F32)],
        compiler_params=_params("parallel", "arbitrary"),
        name="gla",
    )(zp, zp, zp, la, gain)


def _gla_step_kernel(q_ref, k_ref, v_ref, la_ref, g_ref, s0_ref, o_ref, sout_ref, *, heads, dk, dv):
    c = la_ref.shape[0]
    row = lax.broadcasted_iota(jnp.int32, (c, c), 0)
    col = lax.broadcasted_iota(jnp.int32, (c, c), 1)
    rid = lax.broadcasted_iota(jnp.int32, (c, dk), 0)
    gain = g_ref[...]
    ones = jnp.ones((c, LANE), F32)
    for hd in range(heads):
        la = la_ref[:, hd * dk:(hd + 1) * dk]
        b = la
        sh = 1
        while sh < c:
            b = b + jnp.where(rid >= sh, pltpu.roll(b, sh, 0), 0.0)
            sh *= 2
        q = q_ref[:, hd * dk:(hd + 1) * dk].astype(F32) * (dk ** -0.5)
        k = k_ref[:, hd * dk:(hd + 1) * dk].astype(F32)
        v = v_ref[:, hd * dv:(hd + 1) * dv]
        d = b - b[c // 2 - 1:c // 2, :]
        a = jnp.where(col <= row, _dot_nt((q * jnp.exp(d)).astype(BF16), (k * jnp.exp(-d)).astype(BF16)), 0.0)
        st = s0_ref[hd]
        o = _dot(a.astype(BF16), v) + _dot((q * jnp.exp(b)).astype(BF16), st.astype(BF16))
        k_end = (k * jnp.exp(b[c - 1:c, :] - b)).astype(BF16)
        decay = jnp.exp(lax.dot_general(la, ones, TN_DIMS, preferred_element_type=F32,
                                        precision=lax.Precision.HIGHEST))
        sout_ref[hd] = (st * jnp.tile(decay, (1, dv // LANE))
                        + lax.dot_general(k_end, v, TN_DIMS, preferred_element_type=F32))
        o_ref[:, hd * dv:(hd + 1) * dv] = (o * _rms_inv(o, dv) * gain).astype(o_ref.dtype)


def _gla_step(zp, la, gain, s0, c, heads, dk, dv):
    nb = s0.shape[0]
    gk, gv = heads * dk, heads * dv
    assert la.shape[0] == nb * c and c <= GLA_LEAF and gv % gk == 0
    state = pl.BlockSpec((None, heads, dk, dv), lambda bb: (bb, 0, 0, 0))
    return pl.pallas_call(
        functools.partial(_gla_step_kernel, heads=heads, dk=dk, dv=dv),
        grid=(nb,),
        in_specs=[pl.BlockSpec((c, gk), lambda bb: (bb, 0)), pl.BlockSpec((c, gk), lambda bb: (bb, 1)),
                  pl.BlockSpec((c, gv), lambda bb: (bb, 2 * gk // gv)), pl.BlockSpec((c, gk), lambda bb: (bb, 0)),
                  pl.BlockSpec((1, dv), lambda bb: (0, 0)), state],
        out_specs=[pl.BlockSpec((c, gv), lambda bb: (bb, 0)), state],
        out_shape=[jax.ShapeDtypeStruct((nb * c, gv), BF16), jax.ShapeDtypeStruct((nb, heads, dk, dv), F32)],
        compiler_params=_params("parallel"),
        name="gla_step",
    )(zp, zp, zp, la, gain, s0)


def _mix_kernel(om_ref, og_ref, gate_o_ref, gate_m_ref, gate_g_ref, x_ref, w_ref, g_ref, y_ref, h_ref):
    o_gla = og_ref[...].astype(F32) * gate_o_ref[...].astype(F32)
    mixed = gate_m_ref[...].astype(F32) * om_ref[...].astype(F32) + gate_g_ref[...].astype(F32) * o_gla
    y = x_ref[...] + _dot(mixed.astype(BF16), w_ref[...])
    y_ref[...] = y
    h_ref[...] = (y * _rms_inv(y, y.shape[-1]) * g_ref[...]).astype(h_ref.dtype)


def _mix(o_mla, o_gla, zp, gate_block0, x, w_o, g, tm):
    t, d = x.shape
    row = lambda i: (i, 0)
    return pl.pallas_call(
        _mix_kernel,
        grid=(t // tm,),
        in_specs=[pl.BlockSpec((tm, d), row), pl.BlockSpec((tm, d), row),
                  pl.BlockSpec((tm, d), lambda i: (i, gate_block0)),
                  pl.BlockSpec((tm, d), lambda i: (i, gate_block0 + 1)),
                  pl.BlockSpec((tm, d), lambda i: (i, gate_block0 + 2)),
                  pl.BlockSpec((tm, d), row), pl.BlockSpec((d, d), lambda i: (0, 0)),
                  pl.BlockSpec((1, d), lambda i: (0, 0))],
        out_specs=[pl.BlockSpec((tm, d), row), pl.BlockSpec((tm, d), row)],
        out_shape=[jax.ShapeDtypeStruct((t, d), F32), jax.ShapeDtypeStruct((t, d), BF16)],
        compiler_params=_params("parallel"),
        name="mix",
    )(o_mla, o_gla, zp, zp, zp, x, w_o, g)


def _ffn_kernel(h_ref, y_ref, wu_ref, wd_ref, *refs):
    o_ref = refs[-1]
    i, f = pl.program_id(0), pl.program_id(1)

    def body():
        @pl.when(f == 0)
        def _():
            o_ref[...] = y_ref[...]

        u = jnp.maximum(_dot(h_ref[...], wu_ref[...]), 0.0)
        o_ref[...] += _dot((u * u).astype(BF16), wd_ref[...])

    if len(refs) == 1:
        body()
        return
    first_ref = refs[0]

    @pl.when(jnp.logical_and(i == 0, f == 0))
    def _():
        o_ref[...] = first_ref[...]

    pl.when(i > 0)(body)


def _ffn(h, y, w_up, w_down, tm, tf, first=None):
    t, d = y.shape
    dff = w_up.shape[1]
    row = lambda i, f: (i, 0)
    args = [h, y, w_up, w_down]
    if first is None:
        w_specs = [pl.BlockSpec((d, tf), lambda i, f: (0, f)), pl.BlockSpec((tf, d), lambda i, f: (f, 0))]
    else:
        w_specs = [pl.BlockSpec((d, tf), lambda i, f: (0, jnp.where(i == 0, 0, f))),
                   pl.BlockSpec((tf, d), lambda i, f: (jnp.where(i == 0, 0, f), 0)),
                   pl.BlockSpec((tm, d), lambda i, f: (0, 0))]
        args.append(first)
    return pl.pallas_call(
        _ffn_kernel,
        grid=(t // tm, dff // tf),
        in_specs=[pl.BlockSpec((tm, d), row), pl.BlockSpec((tm, d), row)] + w_specs,
        out_specs=pl.BlockSpec((tm, d), row),
        out_shape=jax.ShapeDtypeStruct((t, d), F32),
        compiler_params=_params("arbitrary", "arbitrary"),
        name="ffn",
    )(*args)


def _ffn_first_kernel(h_ref, y_ref, hs_ref, ys_ref, wu_ref, wd_ref, o_ref, os_ref, wub_ref, wdb_ref):
    f = pl.program_id(0)

    @pl.when(f == 0)
    def _():
        o_ref[...] = y_ref[...]
        os_ref[...] = ys_ref[...]

    wu = wu_ref[...].astype(BF16)
    wd = wd_ref[...].astype(BF16)
    wub_ref[...] = wu
    wdb_ref[...] = wd
    for hh_ref, acc_ref in ((h_ref, o_ref), (hs_ref, os_ref)):
        u = jnp.maximum(_dot(hh_ref[...], wu), 0.0)
        acc_ref[...] += _dot((u * u).astype(BF16), wd)


def _ffn_first(h, y, hs, ys, w_up, w_down, tm, tf):
    d = y.shape[1]
    ts = ys.shape[0]
    dff = w_up.shape[1]
    fix = lambda f: (0, 0)
    return pl.pallas_call(
        _ffn_first_kernel,
        grid=(dff // tf,),
        in_specs=[pl.BlockSpec((tm, d), fix), pl.BlockSpec((tm, d), fix),
                  pl.BlockSpec((ts, d), fix), pl.BlockSpec((ts, d), fix),
                  pl.BlockSpec((d, tf), lambda f: (0, f)), pl.BlockSpec((tf, d), lambda f: (f, 0))],
        out_specs=[pl.BlockSpec((tm, d), fix), pl.BlockSpec((ts, d), fix),
                   pl.BlockSpec((d, tf), lambda f: (0, f)), pl.BlockSpec((tf, d), lambda f: (f, 0))],
        out_shape=[jax.ShapeDtypeStruct((tm, d), F32), jax.ShapeDtypeStruct((ts, d), F32),
                   jax.ShapeDtypeStruct((d, dff), BF16), jax.ShapeDtypeStruct((dff, d), BF16)],
        compiler_params=_params("arbitrary"),
        name="ffn_first",
    )(h, y, hs, ys, w_up, w_down)


def _rope_table(pos):
    half = MLA_ROPE // 2
    freqs = jnp.power(ROPE_THETA, -jnp.arange(half, dtype=F32) / half)
    ang = pos[:, None] * freqs[None, :]
    cos, sin = jnp.cos(ang), jnp.sin(ang)
    return jnp.concatenate([cos, cos, -sin, sin], axis=1)


def _swap_halves(w):
    half = w.shape[-1] // 2
    return jnp.concatenate([w[..., half:], w[..., :half]], axis=-1)


def _pick(t, pref):
    while t % pref:
        pref //= 2
    return pref


def kernel(x_prompt, x_sample, cache_mla_ckv, cache_mla_krope, state_gla, norm_mix_g, w_in, mla_q_norm_g,
           mla_w_uq, mla_kv_norm_g, mla_w_ukv, mla_q_gain_nope, mla_q_gain_rope, mla_k_gain_nope,
           mla_k_gain_rope, gla_w_a2, gla_b_a, gla_norm_g, w_o, norm_ffn_g, ffn_w_up, ffn_w_down):
    depth = w_in.shape[0]
    bp, seq, d = x_prompt.shape
    nb, dec_seq, _ = x_sample.shape
    past = cache_mla_ckv.shape[2]
    q_rank, heads = mla_w_uq.shape[1], mla_w_uq.shape[2]
    kv_rank = mla_w_ukv.shape[1]
    gla_heads, dk, dv = state_gla.shape[2], state_gla.shape[3], state_gla.shape[4]
    gate_rank = gla_w_a2.shape[1]
    gk, gv = gla_heads * dk, gla_heads * dv
    assert bp == 1 and heads * MLA_V == d and gv == d and seq % CHUNK == 0

    cs_p = _rope_table(jnp.arange(seq, dtype=F32))
    cs_s = jnp.tile(_rope_table(past + jnp.arange(dec_seq, dtype=F32)), (nb, 1))

    xp = x_prompt.reshape(seq, d)
    xs = x_sample.reshape(nb * dec_seq, d)
    outs = [[] for _ in range(6)]
    for l in range(depth):
        wi = w_in[l].astype(BF16)
        pts = np.cumsum([q_rank, kv_rank, MLA_ROPE, gk, gk, gv, gate_rank, gv, d]).tolist()
        w_qkv_lat = wi[:, :pts[1]]
        w_kr = wi[:, pts[1]:pts[2]]
        w_alr = wi[:, pts[5]:pts[6]]
        assert 3 * gate_rank <= LANE
        w_lat = jnp.concatenate(
            [w_qkv_lat, w_kr, _swap_halves(w_kr), w_alr, w_alr, w_alr,
             jnp.zeros((d, LANE - 3 * gate_rank), BF16)], axis=1)
        wa2_hi = gla_w_a2[l].astype(BF16)
        wa2_lo = (gla_w_a2[l] - wa2_hi.astype(F32)).astype(BF16)
        wa2_split = jnp.concatenate(
            [wa2_hi, wa2_lo, wa2_hi, jnp.zeros((LANE - 3 * gate_rank, gk), BF16)], axis=0)

        wq = mla_w_uq[l]
        wq_r = wq[..., MLA_NOPE:]
        w_uq = jnp.concatenate([wq[..., :MLA_NOPE], wq_r, _swap_halves(wq_r)], axis=-1)
        w_uq = w_uq.reshape(q_rank, heads * MLA_HEAD_PAD).astype(BF16)
        w_uq_t = w_uq.T
        wuk = mla_w_ukv[l][..., :MLA_NOPE].reshape(kv_rank, heads * MLA_NOPE).astype(BF16)
        wuk_t = wuk.T
        wuv = mla_w_ukv[l][..., MLA_NOPE:].reshape(kv_rank, heads * MLA_V).astype(BF16)
        wuv_t = wuv.T
        gain = jnp.concatenate([mla_q_gain_nope[l] * mla_k_gain_nope[l],
                                jnp.tile(mla_q_gain_rope[l] * mla_k_gain_rope[l], 2),
                                jnp.zeros((MLA_HEAD_PAD - MLA_QK_DIM,), F32)])[None, :] * MLA_SCALE
        gain_col = gain.T * LOG2E
        wo_b = w_o[l].astype(BF16)
        g_mix = norm_mix_g[l][None, :]
        g_q = mla_q_norm_g[l][None, :]
        g_kv = mla_kv_norm_g[l][None, :]
        g_gla = gla_norm_g[l][None, :]
        g_ffn = norm_ffn_g[l][None, :]
        ba = gla_b_a[l][None, :]

        def front(x, cs, tm, proj_w=None):
            h, qlat, ckv, kr, la = _mla_lat(x, g_mix, w_lat, g_q, g_kv, cs, wa2_split, ba, tm, q_rank, kv_rank,
                                            gate_rank)
            pm = _pick(x.shape[0], 1024)
            z_gla = z_gate = None
            if proj_w is None:
                z_gla, w_gla = _proj_first(h, wi, pts[2], pts[5] - pts[2], pm, 1024)
                z_gate, w_gate = _proj_first(h, wi, pts[6], wi.shape[1] - pts[6], pm, 1024, swish_cols=gv)
                proj_w = (w_gla, w_gate)
            if z_gla is None or x.shape[0] > pm:
                z_gla = _proj(h, proj_w[0], pm, 2048, first=z_gla)
                z_gate = _proj(h, proj_w[1], pm, gv, swish_cols=gv, first=z_gate)
            return qlat, ckv, kr, la, z_gla, z_gate, proj_w

        tile = _pick(seq, 512)
        qlat, ckv, kr, la, z_gla, z_gate, proj_w = front(xp, cs_p, tile)
        qt = _mla_qt(qlat, w_uq_t, cs_p.T, gain_col, tile, 2 * tile, heads)
        kcat, vt = _mla_kv(ckv, kr, wuk, wuv_t, tile, heads)
        o_mla = _flash(qt, kcat, vt, heads)
        o_gla, st = _gla(z_gla, la, g_gla, 1, _pick(seq, 128), gla_heads, dk, dv)
        y1_p, h2_p = _mix(o_mla, o_gla, z_gate, 0, xp, wo_b, g_ffn, tile)
        outs[0].append(ckv.reshape(bp, seq, kv_rank))
        outs[1].append(kr.reshape(bp, seq, MLA_ROPE))
        outs[2].append(st)

        qlat, ckv, kr, la, z_gla, z_gate, _ = front(xs, cs_s, nb * dec_seq, proj_w)
        qcat = _mla_q(qlat, w_uq, cs_s, gain, nb * dec_seq, heads)
        o_lat = _sattn(qcat, cache_mla_ckv[l], cache_mla_krope[l], ckv, kr, wuk_t, heads, dec_seq, _pick(past, 512))
        o_mla = _svup(o_lat, wuv, heads, dec_seq)
        o_gla, st = _gla_step(z_gla, la, g_gla, state_gla[l], dec_seq, gla_heads, dk, dv)
        y1_s, h2_s = _mix(o_mla, o_gla, z_gate, 0, xs, wo_b, g_ffn, nb * dec_seq)
        outs[3].append(ckv.reshape(nb, dec_seq, kv_rank))
        outs[4].append(kr.reshape(nb, dec_seq, MLA_ROPE))
        outs[5].append(st)

        xp, xs, wup_b, wdn_b = _ffn_first(h2_p, y1_p, h2_s, y1_s, ffn_w_up[l], ffn_w_down[l], tile, 512)
        if seq > tile:
            xp = _ffn(h2_p, y1_p, wup_b, wdn_b, tile, 1024, first=xp)

    return (xp.reshape(bp, seq, d), xs.reshape(nb, dec_seq, d),
            jnp.stack(outs[0]), jnp.stack(outs[1]), jnp.stack(outs[2]),
            jnp.stack(outs[3]), jnp.stack(outs[4]), jnp.stack(outs[5]))
```

```python
import functools

import jax
import jax.numpy as jnp
import numpy as np
from jax import lax
from jax.experimental import pallas as pl
from jax.experimental.pallas import tpu as pltpu

F32 = jnp.float32
BF16 = jnp.bfloat16

EPS = 1e-6
CHUNK = 64
MLA_NOPE = 128
MLA_ROPE = 64
MLA_V = 128
MLA_QK_DIM = MLA_NOPE + MLA_ROPE
MLA_SCALE = MLA_QK_DIM ** -0.5
MLA_HEAD_PAD = 256
V_ROWS = MLA_V + 16
LOG2E = 1.4426950408889634
QBLK = 256
ROPE_THETA = 10000.0
GLA_TAU = 16.0
GLA_LEAF = 32
LANE = 128
V7X_VMEM_BYTES = 64 * 1024 * 1024
VMEM_LIMIT = V7X_VMEM_BYTES * 7 // 8

NT_DIMS = (((1,), (1,)), ((), ()))
TN_DIMS = (((0,), (0,)), ((), ()))


def _params(*sem):
    return pltpu.CompilerParams(dimension_semantics=sem, vmem_limit_bytes=VMEM_LIMIT)


def _dot(a, b):
    return jnp.dot(a, b, preferred_element_type=F32)


def _dot_nt(a, b):
    return lax.dot_general(a, b, NT_DIMS, preferred_element_type=F32)


def _rms_inv(x, n):
    return lax.rsqrt(jnp.sum(x * x, axis=-1, keepdims=True) / n + EPS)


def _mla_lat_kernel(x_ref, g_ref, w_ref, qg_ref, kvg_ref, cs_ref, wa2_ref, ba_ref,
                    h_ref, qlat_ref, ckv_ref, kr_ref, la_ref, *, q_rank, kv_rank, gate_rank):
    x = x_ref[...]
    h = (x * _rms_inv(x, x.shape[-1]) * g_ref[...]).astype(h_ref.dtype)
    h_ref[...] = h
    z = _dot(h, w_ref[...])
    q_lat = z[:, :q_rank]
    qlat_ref[...] = (q_lat * _rms_inv(q_lat, q_rank) * qg_ref[...]).astype(qlat_ref.dtype)
    kv_lat = z[:, q_rank:q_rank + kv_rank]
    ckv_ref[...] = kv_lat * _rms_inv(kv_lat, kv_rank) * kvg_ref[...]
    o = q_rank + kv_rank
    rr = z[:, o:o + LANE] * cs_ref[...]
    kr_ref[...] = rr[:, :MLA_ROPE] + rr[:, MLA_ROPE:]
    a3 = z[:, o + LANE:o + 2 * LANE]
    a_hi = a3.astype(BF16)
    a_lo = (a3 - a_hi.astype(F32)).astype(BF16)
    lane = lax.broadcasted_iota(jnp.int32, a3.shape, 1)
    u = _dot(jnp.where(lane < 2 * gate_rank, a_hi, a_lo), wa2_ref[...]) + ba_ref[...]
    log_sig = jnp.minimum(u, 0.0) - jnp.log1p(jnp.exp(-jnp.abs(u)))
    la_ref[...] = log_sig / GLA_TAU


def _mla_lat(x, g, w_lat, qg, kvg, cs, wa2_split, ba, tm, q_rank, kv_rank, gate_rank):
    t, d = x.shape
    n = w_lat.shape[1]
    gk = wa2_split.shape[1]
    row = lambda i: (i, 0)
    fix = lambda i: (0, 0)
    return pl.pallas_call(
        functools.partial(_mla_lat_kernel, q_rank=q_rank, kv_rank=kv_rank, gate_rank=gate_rank),
        grid=(t // tm,),
        in_specs=[pl.BlockSpec((tm, d), row), pl.BlockSpec((1, d), fix), pl.BlockSpec((d, n), fix),
                  pl.BlockSpec((1, q_rank), fix), pl.BlockSpec((1, kv_rank), fix),
                  pl.BlockSpec((tm, LANE), row), pl.BlockSpec((LANE, gk), fix), pl.BlockSpec((1, gk), fix)],
        out_specs=[pl.BlockSpec((tm, d), row), pl.BlockSpec((tm, q_rank), row), pl.BlockSpec((tm, kv_rank), row),
                   pl.BlockSpec((tm, MLA_ROPE), row), pl.BlockSpec((tm, gk), row)],
        out_shape=[jax.ShapeDtypeStruct((t, d), BF16),
                   jax.ShapeDtypeStruct((t, q_rank), BF16), jax.ShapeDtypeStruct((t, kv_rank), F32),
                   jax.ShapeDtypeStruct((t, MLA_ROPE), F32), jax.ShapeDtypeStruct((t, gk), F32)],
        compiler_params=_params("parallel"),
        name="mla_lat",
    )(x, g, w_lat, qg, kvg, cs, wa2_split, ba)


def _gate_act(z, swish):
    return jax.nn.sigmoid(z) * jnp.where(swish, z, 1.0)


def _proj_kernel(h_ref, w_ref, *refs, swish_tiles):
    o_ref = refs[-1]

    def body():
        z = _dot(h_ref[...], w_ref[...])
        if swish_tiles is not None:
            z = _gate_act(z, pl.program_id(1) < swish_tiles)
        o_ref[...] = z.astype(o_ref.dtype)

    if len(refs) == 1:
        body()
        return
    first_ref = refs[0]

    @pl.when(pl.program_id(0) == 0)
    def _():
        o_ref[...] = first_ref[...]

    pl.when(pl.program_id(0) > 0)(body)


def _proj(h, w, tm, tn, swish_cols=None, first=None):
    t, d = h.shape
    n = w.shape[1]
    in_specs = [pl.BlockSpec((tm, d), lambda i, j: (i, 0))]
    args = [h, w]
    if first is None:
        in_specs.append(pl.BlockSpec((d, tn), lambda i, j: (0, j)))
    else:
        in_specs.append(pl.BlockSpec((d, tn), lambda i, j: (0, jnp.where(i == 0, 0, j))))
        in_specs.append(pl.BlockSpec((tm, tn), lambda i, j: (0, jnp.where(i == 0, j, 0))))
        args.append(first)
    assert swish_cols is None or swish_cols % tn == 0
    return pl.pallas_call(
        functools.partial(_proj_kernel, swish_tiles=None if swish_cols is None else swish_cols // tn),
        grid=(t // tm, n // tn),
        in_specs=in_specs,
        out_specs=pl.BlockSpec((tm, tn), lambda i, j: (i, j)),
        out_shape=jax.ShapeDtypeStruct((t, n), BF16),
        compiler_params=_params("arbitrary", "arbitrary"),
        name="proj",
    )(*args)


def _proj_first_kernel(h_ref, a_ref, b_ref, o_ref, w_ref, *, shift, n_cols, next_col0, swish_tiles):
    j = pl.program_id(0)
    tn = a_ref.shape[1]
    b = b_ref[...]
    col = next_col0 + j * tn + lax.broadcasted_iota(jnp.int32, b.shape, 1)
    src = jnp.concatenate([a_ref[...], jnp.where(col < n_cols, b, jnp.zeros_like(b))], axis=1).astype(F32)
    w = pltpu.roll(src, src.shape[1] - shift, 1)[:, :tn].astype(BF16)
    w_ref[...] = w
    z = _dot(h_ref[...], w)
    if swish_tiles is not None:
        z = _gate_act(z, j < swish_tiles)
    o_ref[...] = z.astype(o_ref.dtype)


def _proj_first(h, w_all, col0, width, tm, tn, swish_cols=None):
    t, d = h.shape
    n_cols = w_all.shape[1]
    base = col0 // LANE * LANE
    shift = col0 - base
    assert shift > 0 and base % tn == 0 and width % tn == 0 and tn % LANE == 0
    assert swish_cols is None or swish_cols % tn == 0
    return pl.pallas_call(
        functools.partial(_proj_first_kernel, shift=shift, n_cols=n_cols, next_col0=base + tn,
                          swish_tiles=None if swish_cols is None else swish_cols // tn),
        grid=(width // tn,),
        in_specs=[pl.BlockSpec((tm, d), lambda j: (0, 0)),
                  pl.BlockSpec((d, tn), lambda j: (0, base // tn + j)),
                  pl.BlockSpec((d, LANE), lambda j: (0, (base + (j + 1) * tn) // LANE))],
        out_specs=[pl.BlockSpec((tm, tn), lambda j: (0, j)), pl.BlockSpec((d, tn), lambda j: (0, j))],
        out_shape=[jax.ShapeDtypeStruct((tm, width), BF16), jax.ShapeDtypeStruct((d, width), BF16)],
        compiler_params=_params("parallel"),
        name="proj_first",
    )(h, w_all, w_all)


def _mla_q_kernel(ql_ref, w_ref, cs_ref, gain_ref, q_ref, *, heads):
    ql = ql_ref[...]
    cs = cs_ref[...]
    gain = gain_ref[...]
    lane = lax.broadcasted_iota(jnp.int32, (1, LANE), 1)
    for hd in range(heads):
        c0 = hd * MLA_HEAD_PAD
        z = _dot(ql, w_ref[:, c0:c0 + MLA_HEAD_PAD])
        nope = z[:, :MLA_NOPE]
        rr = z[:, MLA_NOPE:] * cs
        rot = rr + pltpu.roll(rr, MLA_ROPE, 1)
        ss = (jnp.sum(nope * nope, axis=-1, keepdims=True)
              + jnp.sum(jnp.where(lane < MLA_ROPE, rot * rot, 0.0), axis=-1, keepdims=True))
        inv = lax.rsqrt(ss / MLA_QK_DIM + EPS)
        q_ref[:, c0:c0 + MLA_NOPE] = (nope * inv * gain[:, :MLA_NOPE]).astype(q_ref.dtype)
        q_ref[:, c0 + MLA_NOPE:c0 + MLA_HEAD_PAD] = (rot * inv * gain[:, MLA_NOPE:]).astype(q_ref.dtype)


def _mla_q(qlat, w_uq, cs, gain, tm, heads):
    t, r = qlat.shape
    n = heads * MLA_HEAD_PAD
    return pl.pallas_call(
        functools.partial(_mla_q_kernel, heads=heads),
        grid=(t // tm,),
        in_specs=[pl.BlockSpec((tm, r), lambda i: (i, 0)), pl.BlockSpec((r, n), lambda i: (0, 0)),
                  pl.BlockSpec((tm, LANE), lambda i: (i, 0)), pl.BlockSpec((1, MLA_HEAD_PAD), lambda i: (0, 0))],
        out_specs=pl.BlockSpec((tm, n), lambda i: (i, 0)),
        out_shape=jax.ShapeDtypeStruct((t, n), BF16),
        compiler_params=_params("parallel"),
        name="mla_q",
    )(qlat, w_uq, cs, gain)


def _mla_qt_kernel(ql_ref, w_ref, cs_ref, gain_ref, q_ref, *, heads):
    ql = ql_ref[...]
    cs = cs_ref[...]
    gain = gain_ref[...]
    for hd in range(heads):
        r0 = hd * MLA_HEAD_PAD
        z = _dot_nt(w_ref[r0:r0 + MLA_HEAD_PAD, :], ql)
        nope = z[:MLA_NOPE]
        rr = z[MLA_NOPE:] * cs
        rot = rr[:MLA_ROPE] + rr[MLA_ROPE:]
        ss = jnp.sum(nope * nope, axis=0, keepdims=True) + jnp.sum(rot * rot, axis=0, keepdims=True)
        inv = lax.rsqrt(ss / MLA_QK_DIM + EPS)
        qn = (nope * inv * gain[:MLA_NOPE]).astype(q_ref.dtype)
        qr = (rot * inv * gain[MLA_NOPE:MLA_QK_DIM]).astype(q_ref.dtype)
        for bb in range(q_ref.shape[1]):
            q_ref[hd, bb, :MLA_NOPE, :] = qn[:, bb * QBLK:(bb + 1) * QBLK]
            q_ref[hd, bb, MLA_NOPE:MLA_QK_DIM, :] = qr[:, bb * QBLK:(bb + 1) * QBLK]
            q_ref[hd, bb, MLA_QK_DIM:, :] = jnp.zeros((MLA_HEAD_PAD - MLA_QK_DIM, QBLK), q_ref.dtype)


def _mla_qt(qlat, w_uq_t, cs_t, gain_col, tm, tq, heads):
    t, r = qlat.shape
    n = heads * MLA_HEAD_PAD
    per = tq // tm
    assert tm % QBLK == 0 and tq % tm == 0
    return pl.pallas_call(
        functools.partial(_mla_qt_kernel, heads=heads),
        grid=(t // tm,),
        in_specs=[pl.BlockSpec((tm, r), lambda i: (i, 0)), pl.BlockSpec((n, r), lambda i: (0, 0)),
                  pl.BlockSpec((LANE, tm), lambda i: (0, i)), pl.BlockSpec((MLA_HEAD_PAD, 1), lambda i: (0, 0))],
        out_specs=pl.BlockSpec((heads, None, tm // QBLK, MLA_HEAD_PAD, QBLK),
                               lambda i: (0, i // per, i % per, 0, 0)),
        out_shape=jax.ShapeDtypeStruct((heads, t // tq, tq // QBLK, MLA_HEAD_PAD, QBLK), BF16),
        compiler_params=_params("parallel"),
        name="mla_qt",
    )(qlat, w_uq_t, cs_t, gain_col)


def _mla_kv_kernel(ckv_ref, kr_ref, wk_ref, wvt_ref, k_ref, vt_ref, *, heads):
    c = ckv_ref[...].astype(BF16)
    kr = kr_ref[...]
    kr2 = jnp.sum(kr * kr, axis=-1, keepdims=True)
    kr_pad = jnp.concatenate([kr, jnp.zeros_like(kr)], axis=1)
    for pair in range(heads // 2):
        z = _dot(c, wk_ref[:, pair * 2 * MLA_NOPE:(pair + 1) * 2 * MLA_NOPE])
        for sub in range(2):
            kn = z[:, sub * MLA_NOPE:(sub + 1) * MLA_NOPE]
            inv = lax.rsqrt((jnp.sum(kn * kn, axis=-1, keepdims=True) + kr2) / MLA_QK_DIM + EPS)
            c0 = (2 * pair + sub) * MLA_HEAD_PAD
            k_ref[:, c0:c0 + MLA_NOPE] = (kn * inv).astype(k_ref.dtype)
            k_ref[:, c0 + MLA_NOPE:c0 + MLA_HEAD_PAD] = (kr_pad * inv).astype(k_ref.dtype)
    vt = _dot_nt(wvt_ref[...], c)
    for hd in range(heads):
        vt_ref[hd, :MLA_V, :] = vt[hd * MLA_V:(hd + 1) * MLA_V].astype(vt_ref.dtype)
        vt_ref[hd, MLA_V:, :] = jnp.ones((V_ROWS - MLA_V, vt.shape[1]), vt_ref.dtype)


def _mla_kv(ckv, kr, w_uk, w_uv_t, tm, heads):
    t, r = ckv.shape
    return pl.pallas_call(
        functools.partial(_mla_kv_kernel, heads=heads),
        grid=(t // tm,),
        in_specs=[pl.BlockSpec((tm, r), lambda i: (i, 0)), pl.BlockSpec((tm, MLA_ROPE), lambda i: (i, 0)),
                  pl.BlockSpec((r, heads * MLA_NOPE), lambda i: (0, 0)),
                  pl.BlockSpec((heads * MLA_V, r), lambda i: (0, 0))],
        out_specs=[pl.BlockSpec((tm, heads * MLA_HEAD_PAD), lambda i: (i, 0)),
                   pl.BlockSpec((heads, None, V_ROWS, tm), lambda i: (0, i, 0, 0))],
        out_shape=[jax.ShapeDtypeStruct((t, heads * MLA_HEAD_PAD), BF16),
                   jax.ShapeDtypeStruct((heads, t // tm, V_ROWS, tm), BF16)],
        compiler_params=_params("parallel"),
        name="mla_kv",
    )(ckv, kr, w_uk, w_uv_t)


def _diag_mode(u, c, tk):
    k_lo, k_hi = (u * tk) // CHUNK, ((u + 1) * tk - 1) // CHUNK
    q_lo, q_hi = (c * QBLK) // CHUNK, ((c + 1) * QBLK - 1) // CHUNK
    return "all" if k_hi <= q_lo else "none" if k_lo > q_hi else "some"


def _diag_bias(tk, n_blk):
    index, blocks = {}, []
    for u in range(2):
        for c in range(n_blk):
            if _diag_mode(u, c, tk) == "some":
                k_chunk = (u * tk + np.arange(tk)[:, None]) // CHUNK
                q_chunk = (c * QBLK + np.arange(QBLK)[None, :]) // CHUNK
                index[(u, c)] = len(blocks)
                blocks.append(np.where(k_chunk <= q_chunk, 0.0, -np.inf).astype(np.float32))
    return index, np.stack(blocks)


def _flash_kernel(qt_ref, k_ref, vt_ref, bias_ref, o_ref, m_ref, acc_ref, s_ref, mx_ref, p_ref, al_ref,
                  *, tq, tk, bias_index):
    n_q, n_blk = qt_ref.shape[0], qt_ref.shape[1]
    acc_ref[...] = jnp.zeros_like(acc_ref)


    def diag_mode(u, c):
        return _diag_mode(u, c, tk)

    def score(i, t, slot, c, mode):
        if mode == "none":
            return
        k = k_ref[pl.ds(pl.multiple_of(t * tk, tk), tk), :]
        s = _dot(k, qt_ref[i, c])
        if mode == "some":
            s = s + bias_ref[bias_index[(slot, c)]]
        s_ref[slot, c] = s
        mx_ref[slot, c] = jnp.max(s, axis=0, keepdims=True)

    def soften(slot, c, mode="all", first=False):
        if mode == "none":
            return
        if first:
            m_new = mx_ref[slot, c]
            al_ref[slot, c] = jnp.zeros_like(m_new)
        else:
            m_prev = m_ref[c]
            m_new = jnp.maximum(m_prev, mx_ref[slot, c])
            al_ref[slot, c] = jnp.exp2(m_prev - m_new)
        p_ref[slot, c] = jnp.exp2(s_ref[slot, c] - m_new).astype(BF16)
        m_ref[c] = m_new

    def gather(t, slot, c, mode="all"):
        if mode == "none":
            return
        acc_ref[c] = al_ref[slot, c] * acc_ref[c] + _dot(vt_ref[t], p_ref[slot, c])

    def pair(i, g, diag):
        for c in range(n_blk):
            score(i, g, 0, c, diag_mode(0, c) if diag else "all")
            soften(1, c)
            gather(g - 2, 0, c)
        for c in range(n_blk):
            score(i, g + 1, 1, c, diag_mode(1, c) if diag else "all")
            soften(0, c, diag_mode(0, c) if diag else "all")
            gather(g - 1, 1, c)

    def head(i, diag):
        for c in range(n_blk):
            score(i, 0, 0, c, diag_mode(0, c) if diag else "all")
        for c in range(n_blk):
            score(i, 1, 1, c, diag_mode(1, c) if diag else "all")
            soften(0, c, diag_mode(0, c) if diag else "all", first=True)

    def tail(i):
        for c in range(n_blk):
            soften(1, c, diag_mode(1, c))
            gather(2 * i, 0, c, diag_mode(0, c))
        for c in range(n_blk):
            gather(2 * i + 1, 1, c, diag_mode(1, c))
        for c in range(n_blk):
            acc = acc_ref[c]
            rows = pl.ds(pl.multiple_of(i * tq + c * QBLK, QBLK), QBLK)
            o_ref[rows, :] = (acc[:MLA_V] / acc[MLA_V:MLA_V + 1]).T.astype(o_ref.dtype)

    def middle(i):
        def body(j, carry):
            pair(i, 2 * j, False)
            return carry

        lax.fori_loop(1, i, body, 0)
        pair(i, 2 * i, True)

    head(0, True)
    if n_q > 1:
        tail(0)
        head(1, False)

        def outer(i, carry):
            middle(i)
            tail(i)
            head(i + 1, False)
            return carry

        lax.fori_loop(1, n_q - 1, outer, 0)
        middle(n_q - 1)
    tail(n_q - 1)


def _flash(qt, k, vt, heads):
    t = k.shape[0]
    n_q, n_blk = qt.shape[1], qt.shape[2]
    tq = n_blk * QBLK
    n_kt, tk = vt.shape[1], vt.shape[3]
    assert tq == 2 * tk and tk % CHUNK == 0 and qt.shape[4] == QBLK and n_q * tq == t
    bias_index, bias = _diag_bias(tk, n_blk)
    return pl.pallas_call(
        functools.partial(_flash_kernel, tq=tq, tk=tk, bias_index=bias_index),
        grid=(heads,),
        in_specs=[pl.BlockSpec((None, n_q, n_blk, MLA_HEAD_PAD, QBLK), lambda hd: (hd, 0, 0, 0, 0)),
                  pl.BlockSpec((t, MLA_HEAD_PAD), lambda hd: (0, hd)),
                  pl.BlockSpec((None, n_kt, V_ROWS, tk), lambda hd: (hd, 0, 0, 0)),
                  pl.BlockSpec(bias.shape, lambda hd: (0, 0, 0))],
        out_specs=pl.BlockSpec((t, MLA_V), lambda hd: (0, hd)),
        out_shape=jax.ShapeDtypeStruct((t, heads * MLA_V), BF16),
        scratch_shapes=[pltpu.VMEM((n_blk, 1, QBLK), F32), pltpu.VMEM((n_blk, V_ROWS, QBLK), F32),
                        pltpu.VMEM((2, n_blk, tk, QBLK), F32), pltpu.VMEM((2, n_blk, 1, QBLK), F32),
                        pltpu.VMEM((2, n_blk, tk, QBLK), BF16), pltpu.VMEM((2, n_blk, 1, QBLK), F32)],
        compiler_params=_params("parallel"),
        name="flash",
    )(qt, k, vt, jnp.asarray(bias))


def _sattn_kernel(q_ref, cc_ref, ck_ref, nc_ref, nk_ref, wuk_ref, o_ref,
                  qabs_ref, qr_ref, m_ref, l_ref, acc_ref, *, heads, nq, tk, n_new):
    past, rank = cc_ref.shape
    n_tiles = past // tk
    for hd in range(heads):
        c0 = hd * MLA_HEAD_PAD
        qn = q_ref[:, c0:c0 + MLA_NOPE]
        qabs_ref[hd * nq:(hd + 1) * nq, :] = _dot(
            qn, wuk_ref[hd * MLA_NOPE:(hd + 1) * MLA_NOPE, :]).astype(qabs_ref.dtype)
        qr_ref[hd * nq:(hd + 1) * nq, :] = q_ref[:, c0 + MLA_NOPE:c0 + MLA_NOPE + MLA_ROPE]
    m_ref[...] = jnp.full_like(m_ref, -jnp.inf)
    l_ref[...] = jnp.zeros_like(l_ref)
    acc_ref[...] = jnp.zeros_like(acc_ref)

    def tile(t):
        if t < n_tiles:
            return cc_ref[t * tk:(t + 1) * tk, :].astype(BF16), ck_ref[t * tk:(t + 1) * tk, :]
        pad = LANE - n_new
        c_new = jnp.concatenate([nc_ref[...], jnp.zeros((pad, rank), F32)], axis=0)
        k_new = jnp.concatenate([nk_ref[...], jnp.zeros((pad, MLA_ROPE), F32)], axis=0)
        return c_new.astype(BF16), k_new

    def key_norm(c, kr):
        n_keys = c.shape[0]
        grp = 8 if heads % 8 == 0 else heads
        ss_parts = []
        for g0 in range(0, heads, grp):
            kn_t = _dot_nt(wuk_ref[g0 * MLA_NOPE:(g0 + grp) * MLA_NOPE, :], c)
            ss_parts.append(jnp.sum((kn_t * kn_t).reshape(grp, MLA_NOPE, n_keys), axis=1))
        ss_t = jnp.concatenate(ss_parts, axis=0)
        kr2_t = lax.dot_general(jnp.ones((8, MLA_ROPE), F32), kr * kr, NT_DIMS,
                                preferred_element_type=F32, precision=lax.Precision.HIGHEST)[0:1]
        return lax.rsqrt((ss_t + kr2_t) / MLA_QK_DIM + EPS)

    def attend(c, kr, inv_t, n_valid):
        n_keys = c.shape[0]
        s = _dot_nt(qabs_ref[...], c) + _dot_nt(qr_ref[...], kr.astype(BF16))
        s = s * jnp.broadcast_to(inv_t[:, None, :], (heads, nq, n_keys)).reshape(heads * nq, n_keys)
        if n_valid < n_keys:
            s = jnp.where(lax.broadcasted_iota(jnp.int32, s.shape, 1) < n_valid, s, -jnp.inf)
        m_prev = m_ref[...]
        m_new = jnp.maximum(m_prev, jnp.max(s, axis=-1, keepdims=True))
        alpha = jnp.exp(m_prev - m_new)
        p = jnp.exp(s - m_new)
        l_ref[...] = alpha * l_ref[...] + jnp.sum(p, axis=-1, keepdims=True)
        acc_ref[...] = alpha * acc_ref[...] + _dot(p.astype(BF16), c)
        m_ref[...] = m_new

    c, kr = tile(0)
    inv_t = key_norm(c, kr)
    for t in range(n_tiles + 1):
        if t < n_tiles:
            c_next, kr_next = tile(t + 1)
            inv_next = key_norm(c_next, kr_next)
        attend(c, kr, inv_t, tk if t < n_tiles else n_new)
        if t < n_tiles:
            c, kr, inv_t = c_next, kr_next, inv_next
    o_ref[...] = (acc_ref[...] / l_ref[...]).astype(o_ref.dtype)


def _sattn(q, cache_c, cache_k, new_c, new_k, wuk_t, heads, nq, tk):
    nb, past, rank = cache_c.shape
    return pl.pallas_call(
        functools.partial(_sattn_kernel, heads=heads, nq=nq, tk=tk, n_new=nq),
        grid=(nb,),
        in_specs=[pl.BlockSpec((nq, heads * MLA_HEAD_PAD), lambda b: (b, 0)),
                  pl.BlockSpec((None, past, rank), lambda b: (b, 0, 0)),
                  pl.BlockSpec((None, past, MLA_ROPE), lambda b: (b, 0, 0)),
                  pl.BlockSpec((nq, rank), lambda b: (b, 0)),
                  pl.BlockSpec((nq, MLA_ROPE), lambda b: (b, 0)),
                  pl.BlockSpec(wuk_t.shape, lambda b: (0, 0))],
        out_specs=pl.BlockSpec((None, heads * nq, rank), lambda b: (b, 0, 0)),
        out_shape=jax.ShapeDtypeStruct((nb, heads * nq, rank), BF16),
        scratch_shapes=[pltpu.VMEM((heads * nq, rank), BF16), pltpu.VMEM((heads * nq, MLA_ROPE), BF16),
                        pltpu.VMEM((heads * nq, 1), F32), pltpu.VMEM((heads * nq, 1), F32),
                        pltpu.VMEM((heads * nq, rank), F32)],
        compiler_params=_params("parallel"),
        name="sattn",
    )(q, cache_c, cache_k, new_c, new_k, wuk_t)


def _svup_kernel(ol_ref, w_ref, o_ref, *, nq):
    nb, rows, rank = ol_ref.shape
    for hh in range(rows // nq):
        x = ol_ref[:, hh * nq:(hh + 1) * nq, :].reshape(nb * nq, rank)
        o_ref[:, hh * MLA_V:(hh + 1) * MLA_V] = _dot(x, w_ref[:, hh * MLA_V:(hh + 1) * MLA_V]).astype(o_ref.dtype)


def _svup(o_lat, wuv, heads, nq):
    nb, _, rank = o_lat.shape
    hs = 4 if heads % 4 == 0 else 1
    return pl.pallas_call(
        functools.partial(_svup_kernel, nq=nq),
        grid=(heads // hs,),
        in_specs=[pl.BlockSpec((nb, hs * nq, rank), lambda g: (0, g, 0)),
                  pl.BlockSpec((rank, hs * MLA_V), lambda g: (0, g))],
        out_specs=pl.BlockSpec((nb * nq, hs * MLA_V), lambda g: (0, g)),
        out_shape=jax.ShapeDtypeStruct((nb * nq, heads * MLA_V), BF16),
        compiler_params=_params("parallel"),
        name="svup",
    )(o_lat, wuv)


def _gla_levels(c):
    leaf = min(c, GLA_LEAF)
    levels = [(leaf, leaf // 2 - 1)]
    g = 2 * leaf
    while g <= c:
        levels.append((g, g // 2 - 1))
        g *= 2
    return levels


def _gla_kernel(q_ref, k_ref, v_ref, la_ref, g_ref, o_ref, sout_ref, st_ref, *, heads, dk, dv, c):
    j = pl.program_id(1)
    nj = pl.num_programs(1)

    @pl.when(j == 0)
    def _():
        st_ref[...] = jnp.zeros_like(st_ref)

    levels = _gla_levels(c)
    row = lax.broadcasted_iota(jnp.int32, (c, c), 0)
    col = lax.broadcasted_iota(jnp.int32, (c, c), 1)
    masks = []
    for lv, (g, _) in enumerate(levels):
        same = (row // g) == (col // g)
        if lv == 0:
            masks.append(jnp.logical_and(same, col <= row))
        else:
            half = g // 2
            masks.append(jnp.logical_and(same, jnp.logical_and((row // half) % 2 == 1, (col // half) % 2 == 0)))
    rid = lax.broadcasted_iota(jnp.int32, (c, dk), 0)
    gain = g_ref[...]

    for hd in range(heads):
        la = la_ref[:, hd * dk:(hd + 1) * dk]
        b = la
        sh = 1
        while sh < c:
            b = b + jnp.where(rid >= sh, pltpu.roll(b, sh, 0), 0.0)
            sh *= 2
        b_end = b[c - 1:c, :]
        q = q_ref[:, hd * dk:(hd + 1) * dk].astype(F32) * (dk ** -0.5)
        k = k_ref[:, hd * dk:(hd + 1) * dk].astype(F32)
        v = v_ref[:, hd * dv:(hd + 1) * dv]

        a = jnp.zeros((c, c), F32)
        for lv, (g, r) in enumerate(levels):
            ref_rows = jnp.broadcast_to(b.reshape(c // g, g, dk)[:, r:r + 1, :], (c // g, g, dk)).reshape(c, dk)
            d = b - ref_rows
            if lv == 0:
                fq, fk = jnp.exp(d), jnp.exp(-d)
            else:
                fq = fk = jnp.exp(-jnp.abs(d))
            a_lv = _dot_nt((q * fq).astype(BF16), (k * fk).astype(BF16))
            a = jnp.where(masks[lv], a_lv, a)

        st = st_ref[hd]
        o = _dot(a.astype(BF16), v) + _dot_nt((q * jnp.exp(b)).astype(BF16), st.astype(BF16))
        k_end = (k * jnp.exp(b_end - b)).astype(BF16)
        st_ref[hd] = st * jnp.exp(b_end) + lax.dot_general(v, k_end, TN_DIMS, preferred_element_type=F32)
        o_ref[:, hd * dv:(hd + 1) * dv] = (o * _rms_inv(o, dv) * gain).astype(o_ref.dtype)

    @pl.when(j == nj - 1)
    def _():
        for hd in range(heads):
            sout_ref[hd] = st_ref[hd].T


def _gla(zp, la, gain, nb, c, heads, dk, dv):
    t = la.shape[0]
    nj = t // (nb * c)
    gk, gv = heads * dk, heads * dv
    assert gv % gk == 0
    row = lambda bb, j: (bb * nj + j, 0)
    in_specs = [pl.BlockSpec((c, gk), row),
                pl.BlockSpec((c, gk), lambda bb, j: (bb * nj + j, 1)),
                pl.BlockSpec((c, gv), lambda bb, j: (bb * nj + j, 2 * gk // gv)),
                pl.BlockSpec((c, gk), row),
                pl.BlockSpec((1, dv), lambda bb, j: (0, 0))]
    return pl.pallas_call(
        functools.partial(_gla_kernel, heads=heads, dk=dk, dv=dv, c=c),
        grid=(nb, nj),
        in_specs=in_specs,
        out_specs=[pl.BlockSpec((c, gv), row),
                   pl.BlockSpec((None, heads, dk, dv), lambda bb, j: (bb, 0, 0, 0))],
        out_shape=[jax.ShapeDtypeStruct((t, gv), BF16), jax.ShapeDtypeStruct((nb, heads, dk, dv), F32)],
        scratch_shapes=[pltpu.VMEM((heads, dv, dk), F32)],
        compiler_params=_params("parallel", "arbitrary"),
        name="gla",
    )(zp, zp, zp, la, gain)


def _gla_step_kernel(q_ref, k_ref, v_ref, la_ref, g_ref, s0_ref, o_ref, sout_ref, *, heads, dk, dv):
    c = la_ref.shape[0]
    row = lax.broadcasted_iota(jnp.int32, (c, c), 0)
    col = lax.broadcasted_iota(jnp.int32, (c, c), 1)
    rid = lax.broadcasted_iota(jnp.int32, (c, dk), 0)
    gain = g_ref[...]
    ones = jnp.ones((c, LANE), F32)
    for hd in range(heads):
        la = la_ref[:, hd * dk:(hd + 1) * dk]
        b = la
        sh = 1
        while sh < c:
            b = b + jnp.where(rid >= sh, pltpu.roll(b, sh, 0), 0.0)
            sh *= 2
        q = q_ref[:, hd * dk:(hd + 1) * dk].astype(F32) * (dk ** -0.5)
        k = k_ref[:, hd * dk:(hd + 1) * dk].astype(F32)
        v = v_ref[:, hd * dv:(hd + 1) * dv]
        d = b - b[c // 2 - 1:c // 2, :]
        a = jnp.where(col <= row, _dot_nt((q * jnp.exp(d)).astype(BF16), (k * jnp.exp(-d)).astype(BF16)), 0.0)
        st = s0_ref[hd]
        o = _dot(a.astype(BF16), v) + _dot((q * jnp.exp(b)).astype(BF16), st.astype(BF16))
        k_end = (k * jnp.exp(b[c - 1:c, :] - b)).astype(BF16)
        decay = jnp.exp(lax.dot_general(la, ones, TN_DIMS, preferred_element_type=F32,
                                        precision=lax.Precision.HIGHEST))
        sout_ref[hd] = (st * jnp.tile(decay, (1, dv // LANE))
                        + lax.dot_general(k_end, v, TN_DIMS, preferred_element_type=F32))
        o_ref[:, hd * dv:(hd + 1) * dv] = (o * _rms_inv(o, dv) * gain).astype(o_ref.dtype)


def _gla_step(zp, la, gain, s0, c, heads, dk, dv):
    nb = s0.shape[0]
    gk, gv = heads * dk, heads * dv
    assert la.shape[0] == nb * c and c <= GLA_LEAF and gv % gk == 0
    state = pl.BlockSpec((None, heads, dk, dv), lambda bb: (bb, 0, 0, 0))
    return pl.pallas_call(
        functools.partial(_gla_step_kernel, heads=heads, dk=dk, dv=dv),
        grid=(nb,),
        in_specs=[pl.BlockSpec((c, gk), lambda bb: (bb, 0)), pl.BlockSpec((c, gk), lambda bb: (bb, 1)),
                  pl.BlockSpec((c, gv), lambda bb: (bb, 2 * gk // gv)), pl.BlockSpec((c, gk), lambda bb: (bb, 0)),
                  pl.BlockSpec((1, dv), lambda bb: (0, 0)), state],
        out_specs=[pl.BlockSpec((c, gv), lambda bb: (bb, 0)), state],
        out_shape=[jax.ShapeDtypeStruct((nb * c, gv), BF16), jax.ShapeDtypeStruct((nb, heads, dk, dv), F32)],
        compiler_params=_params("parallel"),
        name="gla_step",
    )(zp, zp, zp, la, gain, s0)


def _mix_kernel(om_ref, og_ref, gate_o_ref, gate_m_ref, gate_g_ref, x_ref, w_ref, g_ref, y_ref, h_ref):
    o_gla = og_ref[...].astype(F32) * gate_o_ref[...].astype(F32)
    mixed = gate_m_ref[...].astype(F32) * om_ref[...].astype(F32) + gate_g_ref[...].astype(F32) * o_gla
    y = x_ref[...] + _dot(mixed.astype(BF16), w_ref[...])
    y_ref[...] = y
    h_ref[...] = (y * _rms_inv(y, y.shape[-1]) * g_ref[...]).astype(h_ref.dtype)


def _mix(o_mla, o_gla, zp, gate_block0, x, w_o, g, tm):
    t, d = x.shape
    row = lambda i: (i, 0)
    return pl.pallas_call(
        _mix_kernel,
        grid=(t // tm,),
        in_specs=[pl.BlockSpec((tm, d), row), pl.BlockSpec((tm, d), row),
                  pl.BlockSpec((tm, d), lambda i: (i, gate_block0)),
                  pl.BlockSpec((tm, d), lambda i: (i, gate_block0 + 1)),
                  pl.BlockSpec((tm, d), lambda i: (i, gate_block0 + 2)),
                  pl.BlockSpec((tm, d), row), pl.BlockSpec((d, d), lambda i: (0, 0)),
                  pl.BlockSpec((1, d), lambda i: (0, 0))],
        out_specs=[pl.BlockSpec((tm, d), row), pl.BlockSpec((tm, d), row)],
        out_shape=[jax.ShapeDtypeStruct((t, d), F32), jax.ShapeDtypeStruct((t, d), BF16)],
        compiler_params=_params("parallel"),
        name="mix",
    )(o_mla, o_gla, zp, zp, zp, x, w_o, g)


def _ffn_kernel(h_ref, y_ref, wu_ref, wd_ref, *refs):
    o_ref = refs[-1]
    i, f = pl.program_id(0), pl.program_id(1)

    def body():
        @pl.when(f == 0)
        def _():
            o_ref[...] = y_ref[...]

        u = jnp.maximum(_dot(h_ref[...], wu_ref[...]), 0.0)
        o_ref[...] += _dot((u * u).astype(BF16), wd_ref[...])

    if len(refs) == 1:
        body()
        return
    first_ref = refs[0]

    @pl.when(jnp.logical_and(i == 0, f == 0))
    def _():
        o_ref[...] = first_ref[...]

    pl.when(i > 0)(body)


def _ffn(h, y, w_up, w_down, tm, tf, first=None):
    t, d = y.shape
    dff = w_up.shape[1]
    row = lambda i, f: (i, 0)
    args = [h, y, w_up, w_down]
    if first is None:
        w_specs = [pl.BlockSpec((d, tf), lambda i, f: (0, f)), pl.BlockSpec((tf, d), lambda i, f: (f, 0))]
    else:
        w_specs = [pl.BlockSpec((d, tf), lambda i, f: (0, jnp.where(i == 0, 0, f))),
                   pl.BlockSpec((tf, d), lambda i, f: (jnp.where(i == 0, 0, f), 0)),
                   pl.BlockSpec((tm, d), lambda i, f: (0, 0))]
        args.append(first)
    return pl.pallas_call(
        _ffn_kernel,
        grid=(t // tm, dff // tf),
        in_specs=[pl.BlockSpec((tm, d), row), pl.BlockSpec((tm, d), row)] + w_specs,
        out_specs=pl.BlockSpec((tm, d), row),
        out_shape=jax.ShapeDtypeStruct((t, d), F32),
        compiler_params=_params("arbitrary", "arbitrary"),
        name="ffn",
    )(*args)


def _ffn_first_kernel(h_ref, y_ref, hs_ref, ys_ref, wu_ref, wd_ref, o_ref, os_ref, wub_ref, wdb_ref):
    f = pl.program_id(0)

    @pl.when(f == 0)
    def _():
        o_ref[...] = y_ref[...]
        os_ref[...] = ys_ref[...]

    wu = wu_ref[...].astype(BF16)
    wd = wd_ref[...].astype(BF16)
    wub_ref[...] = wu
    wdb_ref[...] = wd
    for hh_ref, acc_ref in ((h_ref, o_ref), (hs_ref, os_ref)):
        u = jnp.maximum(_dot(hh_ref[...], wu), 0.0)
        acc_ref[...] += _dot((u * u).astype(BF16), wd)


def _ffn_first(h, y, hs, ys, w_up, w_down, tm, tf):
    d = y.shape[1]
    ts = ys.shape[0]
    dff = w_up.shape[1]
    fix = lambda f: (0, 0)
    return pl.pallas_call(
        _ffn_first_kernel,
        grid=(dff // tf,),
        in_specs=[pl.BlockSpec((tm, d), fix), pl.BlockSpec((tm, d), fix),
                  pl.BlockSpec((ts, d), fix), pl.BlockSpec((ts, d), fix),
                  pl.BlockSpec((d, tf), lambda f: (0, f)), pl.BlockSpec((tf, d), lambda f: (f, 0))],
        out_specs=[pl.BlockSpec((tm, d), fix), pl.BlockSpec((ts, d), fix),
                   pl.BlockSpec((d, tf), lambda f: (0, f)), pl.BlockSpec((tf, d), lambda f: (f, 0))],
        out_shape=[jax.ShapeDtypeStruct((tm, d), F32), jax.ShapeDtypeStruct((ts, d), F32),
                   jax.ShapeDtypeStruct((d, dff), BF16), jax.ShapeDtypeStruct((dff, d), BF16)],
        compiler_params=_params("arbitrary"),
        name="ffn_first",
    )(h, y, hs, ys, w_up, w_down)


def _rope_table(pos):
    half = MLA_ROPE // 2
    freqs = jnp.power(ROPE_THETA, -jnp.arange(half, dtype=F32) / half)
    ang = pos[:, None] * freqs[None, :]
    cos, sin = jnp.cos(ang), jnp.sin(ang)
    return jnp.concatenate([cos, cos, -sin, sin], axis=1)


def _swap_halves(w):
    half = w.shape[-1] // 2
    return jnp.concatenate([w[..., half:], w[..., :half]], axis=-1)


def _pick(t, pref):
    while t % pref:
        pref //= 2
    return pref


def kernel(x_prompt, x_sample, cache_mla_ckv, cache_mla_krope, state_gla, norm_mix_g, w_in, mla_q_norm_g,
           mla_w_uq, mla_kv_norm_g, mla_w_ukv, mla_q_gain_nope, mla_q_gain_rope, mla_k_gain_nope,
           mla_k_gain_rope, gla_w_a2, gla_b_a, gla_norm_g, w_o, norm_ffn_g, ffn_w_up, ffn_w_down):
    depth = w_in.shape[0]
    bp, seq, d = x_prompt.shape
    nb, dec_seq, _ = x_sample.shape
    past = cache_mla_ckv.shape[2]
    q_rank, heads = mla_w_uq.shape[1], mla_w_uq.shape[2]
    kv_rank = mla_w_ukv.shape[1]
    gla_heads, dk, dv = state_gla.shape[2], state_gla.shape[3], state_gla.shape[4]
    gate_rank = gla_w_a2.shape[1]
    gk, gv = gla_heads * dk, gla_heads * dv
    assert bp == 1 and heads * MLA_V == d and gv == d and seq % CHUNK == 0

    cs_p = _rope_table(jnp.arange(seq, dtype=F32))
    cs_s = jnp.tile(_rope_table(past + jnp.arange(dec_seq, dtype=F32)), (nb, 1))

    xp = x_prompt.reshape(seq, d)
    xs = x_sample.reshape(nb * dec_seq, d)
    outs = [[] for _ in range(6)]
    for l in range(depth):
        wi = w_in[l].astype(BF16)
        pts = np.cumsum([q_rank, kv_rank, MLA_ROPE, gk, gk, gv, gate_rank, gv, d]).tolist()
        w_qkv_lat = wi[:, :pts[1]]
        w_kr = wi[:, pts[1]:pts[2]]
        w_alr = wi[:, pts[5]:pts[6]]
        assert 3 * gate_rank <= LANE
        w_lat = jnp.concatenate(
            [w_qkv_lat, w_kr, _swap_halves(w_kr), w_alr, w_alr, w_alr,
             jnp.zeros((d, LANE - 3 * gate_rank), BF16)], axis=1)
        wa2_hi = gla_w_a2[l].astype(BF16)
        wa2_lo = (gla_w_a2[l] - wa2_hi.astype(F32)).astype(BF16)
        wa2_split = jnp.concatenate(
            [wa2_hi, wa2_lo, wa2_hi, jnp.zeros((LANE - 3 * gate_rank, gk), BF16)], axis=0)

        wq = mla_w_uq[l]
        wq_r = wq[..., MLA_NOPE:]
        w_uq = jnp.concatenate([wq[..., :MLA_NOPE], wq_r, _swap_halves(wq_r)], axis=-1)
        w_uq = w_uq.reshape(q_rank, heads * MLA_HEAD_PAD).astype(BF16)
        w_uq_t = w_uq.T
        wuk = mla_w_ukv[l][..., :MLA_NOPE].reshape(kv_rank, heads * MLA_NOPE).astype(BF16)
        wuk_t = wuk.T
        wuv = mla_w_ukv[l][..., MLA_NOPE:].reshape(kv_rank, heads * MLA_V).astype(BF16)
        wuv_t = wuv.T
        gain = jnp.concatenate([mla_q_gain_nope[l] * mla_k_gain_nope[l],
                                jnp.tile(mla_q_gain_rope[l] * mla_k_gain_rope[l], 2),
                                jnp.zeros((MLA_HEAD_PAD - MLA_QK_DIM,), F32)])[None, :] * MLA_SCALE
        gain_col = gain.T * LOG2E
        wo_b = w_o[l].astype(BF16)
        g_mix = norm_mix_g[l][None, :]
        g_q = mla_q_norm_g[l][None, :]
        g_kv = mla_kv_norm_g[l][None, :]
        g_gla = gla_norm_g[l][None, :]
        g_ffn = norm_ffn_g[l][None, :]
        ba = gla_b_a[l][None, :]

        def front(x, cs, tm, proj_w=None):
            h, qlat, ckv, kr, la = _mla_lat(x, g_mix, w_lat, g_q, g_kv, cs, wa2_split, ba, tm, q_rank, kv_rank,
                                            gate_rank)
            pm = _pick(x.shape[0], 1024)
            z_gla = z_gate = None
            if proj_w is None:
                z_gla, w_gla = _proj_first(h, wi, pts[2], pts[5] - pts[2], pm, 1024)
                z_gate, w_gate = _proj_first(h, wi, pts[6], wi.shape[1] - pts[6], pm, 1024, swish_cols=gv)
                proj_w = (w_gla, w_gate)
            if z_gla is None or x.shape[0] > pm:
                z_gla = _proj(h, proj_w[0], pm, 2048, first=z_gla)
                z_gate = _proj(h, proj_w[1], pm, gv, swish_cols=gv, first=z_gate)
            return qlat, ckv, kr, la, z_gla, z_gate, proj_w

        tile = _pick(seq, 512)
        qlat, ckv, kr, la, z_gla, z_gate, proj_w = front(xp, cs_p, tile)
        qt = _mla_qt(qlat, w_uq_t, cs_p.T, gain_col, tile, 2 * tile, heads)
        kcat, vt = _mla_kv(ckv, kr, wuk, wuv_t, tile, heads)
        o_mla = _flash(qt, kcat, vt, heads)
        o_gla, st = _gla(z_gla, la, g_gla, 1, _pick(seq, 128), gla_heads, dk, dv)
        y1_p, h2_p = _mix(o_mla, o_gla, z_gate, 0, xp, wo_b, g_ffn, tile)
        outs[0].append(ckv.reshape(bp, seq, kv_rank))
        outs[1].append(kr.reshape(bp, seq, MLA_ROPE))
        outs[2].append(st)

        qlat, ckv, kr, la, z_gla, z_gate, _ = front(xs, cs_s, nb * dec_seq, proj_w)
        qcat = _mla_q(qlat, w_uq, cs_s, gain, nb * dec_seq, heads)
        o_lat = _sattn(qcat, cache_mla_ckv[l], cache_mla_krope[l], ckv, kr, wuk_t, heads, dec_seq, _pick(past, 512))
        o_mla = _svup(o_lat, wuv, heads, dec_seq)
        o_gla, st = _gla_step(z_gla, la, g_gla, state_gla[l], dec_seq, gla_heads, dk, dv)
        y1_s, h2_s = _mix(o_mla, o_gla, z_gate, 0, xs, wo_b, g_ffn, nb * dec_seq)
        outs[3].append(ckv.reshape(nb, dec_seq, kv_rank))
        outs[4].append(kr.reshape(nb, dec_seq, MLA_ROPE))
        outs[5].append(st)

        xp, xs, wup_b, wdn_b = _ffn_first(h2_p, y1_p, h2_s, y1_s, ffn_w_up[l], ffn_w_down[l], tile, 512)
        if seq > tile:
            xp = _ffn(h2_p, y1_p, wup_b, wdn_b, tile, 1024, first=xp)

    return (xp.reshape(bp, seq, d), xs.reshape(nb, dec_seq, d),
            jnp.stack(outs[0]), jnp.stack(outs[1]), jnp.stack(outs[2]),
            jnp.stack(outs[3]), jnp.stack(outs[4]), jnp.stack(outs[5]))
```
